```python
import math
import jax, jax.numpy as jnp
from jax import lax
import numpy as np

D_MODEL = 1024
BATCH = 4
SEQ = 8192
DEPTH = 4

CTX_LEN = 256
GRID_W = 64
N_EVEN = (DEPTH + 1) // 2
N_ODD = DEPTH // 2
EPS = 1e-6
N_MOD = 6
D_FF = 4 * D_MODEL
C_HYENA = D_MODEL // 2
HYENA_ORDER = 2
HYENA_SHORT_W = 3
N_BANDS = 16
FILTER_EMB = 1 + 2 * N_BANDS
FILTER_HIDDEN = 64
DECAY_MIN = math.log(100.0) / 1.5
DECAY_MAX = math.log(100.0) / 0.3
DIFF_HEAD_DIM = 64
DIFF_V_DIM = 2 * DIFF_HEAD_DIM
DIFF_HEADS = (D_MODEL // 2) // DIFF_V_DIM
C_DIFF = DIFF_HEADS * DIFF_V_DIM
QK_WIDTH = DIFF_HEADS * 2 * DIFF_HEAD_DIM
HY_WIDTH = 3 * C_HYENA
D_EVEN_IN = HY_WIDTH + 2 * QK_WIDTH + C_DIFF
EVEN_SPLITS = (HY_WIDTH, HY_WIDTH + QK_WIDTH, HY_WIDTH + 2 * QK_WIDTH)
KV_START = HY_WIDTH + QK_WIDTH
ROPE_AXIS = DIFF_HEAD_DIM // 2
ROPE_BASE = 10000.0
Q_BLOCK = 128
D_RNN = 1280
RG_BLOCKS = 16
RG_BS = D_RNN // RG_BLOCKS
RG_CONV_W = 4
RG_CONV_LEFT = 2
RG_C = 8.0

kernel_name = 'hybrid_hyena_diffattn_rglru_dit'


def rms_norm(x, g):
    xf = x.astype(jnp.float32)
    y = xf * lax.rsqrt(jnp.mean(xf * xf, axis=-1, keepdims=True) + EPS)
    return (y * g.astype(jnp.float32)).astype(x.dtype)


def modulate(x, g, shift, scale):
    return rms_norm(x, g) * (1.0 + scale) + shift


def ada_mod(cond, w, b, n):
    m = jax.nn.silu(cond) @ w[:, :n * D_MODEL] + b[:n * D_MODEL]
    return jnp.split(m, n, axis=-1)


def dw_conv(x, w, b, left):
    width, L = w.shape[0], x.shape[1]
    xp = jnp.pad(x, ((0, 0), (left, width - 1 - left), (0, 0)))
    return sum(w[j] * xp[:, j:j + L] for j in range(width)) + b


def sq_relu_mlp(x, w1, w2):
    return jnp.square(jax.nn.relu(x @ w1)) @ w2


def hyena_filters(L, w1, b1, w2, b2, w3, b3, freq, decay):
    t = jnp.arange(L, dtype=jnp.float32) / L
    bands = jnp.arange(1, N_BANDS + 1, dtype=jnp.float32)
    ang = 2.0 * math.pi * t[:, None] * bands
    feats = jnp.concatenate([t[:, None], jnp.cos(ang), jnp.sin(ang)], axis=-1)
    h = jnp.sin(freq * (feats @ w1 + b1))
    h = jnp.sin(freq * (h @ w2 + b2))
    h = (h @ w3 + b3).reshape(L, HYENA_ORDER, 2, C_HYENA)
    h = h * jnp.exp(-t[:, None, None, None] * jnp.abs(decay))
    hf, hb = h[:, :, 0], h[:, :, 1]
    return jnp.concatenate([hf, jnp.zeros_like(hf[:1]), hb[:0:-1]], axis=0)


def fft_long_conv(u, k_hat, bias):
    L = u.shape[1]
    U = jnp.fft.rfft(u.astype(jnp.float32), n=2 * L, axis=1)
    y = jnp.fft.irfft(U * k_hat, n=2 * L, axis=1)[:, :L]
    return (y + u * bias).astype(u.dtype)


def hyena_mix(u3, short_w, short_b, hy_bias, fparams):
    L = u3.shape[1]
    u3 = dw_conv(u3, short_w, short_b, 1)
    v, x1, x2 = jnp.split(u3, 3, axis=-1)
    k_hat = jnp.fft.rfft(hyena_filters(L, *fparams), axis=0)
    z = x1 * fft_long_conv(v, k_hat[:, 0], hy_bias[0])
    z = x2 * fft_long_conv(z, k_hat[:, 1], hy_bias[1])
    return z


def axial_rope(x, rows, cols):
    freqs = ROPE_BASE ** (-jnp.arange(0, ROPE_AXIS, 2, dtype=jnp.float32) / ROPE_AXIS)

    def rot(xa, pos):
        ang = pos[:, None] * freqs
        cos = jnp.cos(ang)[:, None, None, :]
        sin = jnp.sin(ang)[:, None, None, :]
        a1, a2 = jnp.split(xa, 2, axis=-1)
        return jnp.concatenate([a1 * cos - a2 * sin, a2 * cos + a1 * sin], axis=-1)

    out = jnp.concatenate([rot(x[..., :ROPE_AXIS], rows), rot(x[..., ROPE_AXIS:], cols)], axis=-1)
    return out.astype(x.dtype)


def diff_attend(q, k, v, lam):
    s = jnp.einsum('bqhjd,bkhjd->bhjqk', q, k).astype(jnp.float32) * (DIFF_HEAD_DIM ** -0.5)
    p = jax.nn.softmax(s, axis=-1)
    a = p[:, :, 0] - lam * p[:, :, 1]
    return jnp.einsum('bhqk,bkhe->bqhe', a.astype(v.dtype), v)


def diff_heads_out(o, subln_g, lam_init):
    B, L = o.shape[:2]
    return (rms_norm(o, subln_g) * (1.0 - lam_init)).reshape(B, L, C_DIFF)


def even_mixer(h_ctx, h_lat, rows, cols, w_in, w_out, short_w, short_b,
               f_w1, f_b1, f_w2, f_b2, f_w3, f_b3, f_freq, f_decay, hy_bias,
               lq1, lk1, lq2, lk2, subln_g, lam_init, ctx_out):
    B, L, _ = h_lat.shape
    Lc = h_ctx.shape[1]
    fparams = (f_w1, f_b1, f_w2, f_b2, f_w3, f_b3, f_freq, f_decay)
    lam = jnp.exp(jnp.sum(lq1 * lk1)) - jnp.exp(jnp.sum(lq2 * lk2)) + lam_init
    qk_shape = (DIFF_HEADS, 2, DIFF_HEAD_DIM)
    u_lat, q_lat, k_lat, v_lat = jnp.split(h_lat @ w_in, EVEN_SPLITS, axis=-1)
    q_lat = axial_rope(q_lat.reshape(B, L, *qk_shape), rows, cols)
    k_lat = axial_rope(k_lat.reshape(B, L, *qk_shape), rows, cols)
    v_lat = v_lat.reshape(B, L, DIFF_HEADS, DIFF_V_DIM)
    if ctx_out:
        u_ctx, q_ctx, k_ctx, v_ctx = jnp.split(h_ctx @ w_in, EVEN_SPLITS, axis=-1)
        q_ctx = q_ctx.reshape(B, Lc, *qk_shape)
    else:
        k_ctx, v_ctx = jnp.split(h_ctx @ w_in[:, KV_START:], 2, axis=-1)
    k_ctx = k_ctx.reshape(B, Lc, *qk_shape)
    v_ctx = v_ctx.reshape(B, Lc, DIFF_HEADS, DIFF_V_DIM)
    k_all = jnp.concatenate([k_lat, k_ctx], axis=1)
    v_all = jnp.concatenate([v_lat, v_ctx], axis=1)
    n_blk = L // Q_BLOCK
    q_blocks = jnp.moveaxis(q_lat.reshape(B, n_blk, Q_BLOCK, *qk_shape), 1, 0)
    o_lat = lax.map(lambda qb: diff_attend(qb, k_all, v_all, lam), q_blocks)
    o_lat = jnp.moveaxis(o_lat, 0, 1).reshape(B, L, DIFF_HEADS, DIFF_V_DIM)
    y_lat = jnp.concatenate([hyena_mix(u_lat, short_w, short_b, hy_bias, fparams),
                             diff_heads_out(o_lat, subln_g, lam_init)], axis=-1) @ w_out
    if not ctx_out:
        return None, y_lat
    o_ctx = diff_attend(q_ctx, k_ctx, v_ctx, lam)
    y_ctx = jnp.concatenate([hyena_mix(u_ctx, short_w, short_b, hy_bias, fparams),
                             diff_heads_out(o_ctx, subln_g, lam_init)], axis=-1) @ w_out
    return y_ctx, y_lat


def rglru_gates(xc, wa, ba, wx, bx, lam):
    B, L, _ = xc.shape
    xb = xc.reshape(B, L, RG_BLOCKS, RG_BS)
    r = jax.nn.sigmoid(jnp.einsum('blnc,ncd->blnd', xb, wa).reshape(B, L, D_RNN) + ba)
    i = jax.nn.sigmoid(jnp.einsum('blnc,ncd->blnd', xb, wx).reshape(B, L, D_RNN) + bx)
    log_a = -RG_C * r.astype(jnp.float32) * jax.nn.softplus(-lam.astype(jnp.float32))
    a = jnp.exp(log_a)
    b = jnp.sqrt(-jnp.expm1(2.0 * log_a)) * (i * xc).astype(jnp.float32)
    return a, b


def _combine(left, right):
    return (left[0] * right[0], right[0] * left[1] + right[1])


def linear_scan(a, b, h0):
    A, Bc = lax.associative_scan(_combine, (a, b), axis=1)
    return Bc if h0 is None else Bc + A * h0[:, None, :]


def odd_mixer(h_ctx, h_lat, w_in, w_out, conv_w, conv_b, wa, ba, wx, bx, lam, ctx_out):
    g_lat, x_lat = jnp.split(h_lat @ w_in, 2, axis=-1)
    if ctx_out:
        g_ctx, x_ctx = jnp.split(h_ctx @ w_in, 2, axis=-1)
    else:
        x_ctx = h_ctx @ w_in[:, D_RNN:]
    x_ctx = dw_conv(x_ctx, conv_w, conv_b, RG_CONV_LEFT)
    x_lat = dw_conv(x_lat, conv_w, conv_b, RG_CONV_LEFT)
    h_lat_dirs, h_ctx_dirs = [], []
    for d in range(2):
        flip = (lambda t: jnp.flip(t, axis=1)) if d == 1 else (lambda t: t)
        a_c, b_c = rglru_gates(flip(x_ctx), wa[d], ba[d], wx[d], bx[d], lam[d])
        h_c = linear_scan(a_c, b_c, None)
        a_l, b_l = rglru_gates(flip(x_lat), wa[d], ba[d], wx[d], bx[d], lam[d])
        h_lat_dirs.append(flip(linear_scan(a_l, b_l, h_c[:, -1])))
        if ctx_out:
            h_ctx_dirs.append(flip(h_c))
    y_lat = ((h_lat_dirs[0] + h_lat_dirs[1]).astype(g_lat.dtype) * jax.nn.gelu(g_lat)) @ w_out
    if not ctx_out:
        return None, y_lat
    y_ctx = ((h_ctx_dirs[0] + h_ctx_dirs[1]).astype(g_ctx.dtype) * jax.nn.gelu(g_ctx)) @ w_out
    return y_ctx, y_lat


def setup_inputs(seed: int = 0) -> dict:
    key = jax.random.key(seed)
    ks = iter(jax.random.split(key, 48))

    def nrm(shape, scale):
        return scale * jax.random.normal(next(ks), shape, jnp.float32)

    D = D_MODEL
    decay0 = jnp.linspace(DECAY_MIN, DECAY_MAX, C_HYENA, dtype=jnp.float32)
    a0 = jax.random.uniform(next(ks), (N_ODD, 2, D_RNN), jnp.float32, 0.9, 0.999)
    s0 = a0 ** (1.0 / RG_C)
    rg_lam = jnp.log(s0) - jnp.log1p(-s0)
    return {
        'x': nrm((BATCH, SEQ, D), 1.0),
        'c': nrm((BATCH, D), 1.0),
        'ctx': nrm((BATCH, CTX_LEN, D), 1.0),
        'c_ctx': nrm((D,), 1.0),
        'ada_w': nrm((DEPTH, D, N_MOD * D), 0.5 * D ** -0.5),
        'ada_b': nrm((DEPTH, N_MOD * D), 0.02),
        'norm1_g': 1.0 + nrm((DEPTH, D), 0.02),
        'norm2_g': 1.0 + nrm((DEPTH, D), 0.02),
        'mlp_w1': nrm((DEPTH, D, D_FF), D ** -0.5),
        'mlp_w2': nrm((DEPTH, D_FF, D), D_FF ** -0.5),
        'final_g': 1.0 + nrm((D,), 0.02),
        'ev_w_in': nrm((N_EVEN, D, D_EVEN_IN), D ** -0.5),
        'ev_w_out': nrm((N_EVEN, C_HYENA + C_DIFF, D), (C_HYENA + C_DIFF) ** -0.5),
        'hy_short_w': nrm((N_EVEN, HYENA_SHORT_W, HY_WIDTH), HYENA_SHORT_W ** -0.5),
        'hy_short_b': nrm((N_EVEN, HY_WIDTH), 0.02),
        'hy_f_w1': nrm((N_EVEN, FILTER_EMB, FILTER_HIDDEN), FILTER_EMB ** -0.5),
        'hy_f_b1': nrm((N_EVEN, FILTER_HIDDEN), 0.02),
        'hy_f_w2': nrm((N_EVEN, FILTER_HIDDEN, FILTER_HIDDEN), FILTER_HIDDEN ** -0.5),
        'hy_f_b2': nrm((N_EVEN, FILTER_HIDDEN), 0.02),
        'hy_f_w3': nrm((N_EVEN, FILTER_HIDDEN, HYENA_ORDER * 2 * C_HYENA), 0.05 * FILTER_HIDDEN ** -0.5),
        'hy_f_b3': nrm((N_EVEN, HYENA_ORDER * 2 * C_HYENA), 0.005),
        'hy_f_freq': 1.0 + nrm((N_EVEN, FILTER_HIDDEN), 0.1),
        'hy_f_decay': decay0 + nrm((N_EVEN, HYENA_ORDER, 2, C_HYENA), 0.1),
        'hy_bias': nrm((N_EVEN, HYENA_ORDER, C_HYENA), 1.0),
        'df_lq1': nrm((N_EVEN, DIFF_HEAD_DIM), 0.1),
        'df_lk1': nrm((N_EVEN, DIFF_HEAD_DIM), 0.1),
        'df_lq2': nrm((N_EVEN, DIFF_HEAD_DIM), 0.1),
        'df_lk2': nrm((N_EVEN, DIFF_HEAD_DIM), 0.1),
        'df_subln_g': 1.0 + nrm((N_EVEN, DIFF_V_DIM), 0.02),
        'od_w_in': nrm((N_ODD, D, 2 * D_RNN), D ** -0.5),
        'od_w_out': nrm((N_ODD, D_RNN, D), D_RNN ** -0.5),
        'rg_conv_w': nrm((N_ODD, RG_CONV_W, D_RNN), RG_CONV_W ** -0.5),
        'rg_conv_b': nrm((N_ODD, D_RNN), 0.02),
        'rg_wa': nrm((N_ODD, 2, RG_BLOCKS, RG_BS, RG_BS), RG_BS ** -0.5),
        'rg_ba': nrm((N_ODD, 2, D_RNN), 0.02),
        'rg_wx': nrm((N_ODD, 2, RG_BLOCKS, RG_BS, RG_BS), RG_BS ** -0.5),
        'rg_bx': nrm((N_ODD, 2, D_RNN), 0.02),
        'rg_lam': rg_lam,
    }


def reference(x, c, ctx, c_ctx, ada_w, ada_b, norm1_g, norm2_g, mlp_w1, mlp_w2, final_g,
              ev_w_in, ev_w_out, hy_short_w, hy_short_b, hy_f_w1, hy_f_b1, hy_f_w2, hy_f_b2,
              hy_f_w3, hy_f_b3, hy_f_freq, hy_f_decay, hy_bias, df_lq1, df_lk1, df_lq2, df_lk2,
              df_subln_g, od_w_in, od_w_out, rg_conv_w, rg_conv_b, rg_wa, rg_ba, rg_wx, rg_bx, rg_lam):
    B, L, _ = x.shape
    ROWS = L // GRID_W
    rows = jnp.repeat(jnp.arange(ROWS, dtype=jnp.float32), GRID_W)
    cols = jnp.tile(jnp.arange(GRID_W, dtype=jnp.float32), ROWS)
    c_lat = c[:, None, :]
    h_lat, h_ctx = x, ctx
    for i in range(DEPTH):
        last = i == DEPTH - 1
        j = i // 2
        sh1, sc1, g1, sh2, sc2, g2 = ada_mod(c_lat, ada_w[i], ada_b[i], N_MOD)
        if last:
            csh1, csc1 = ada_mod(c_ctx, ada_w[i], ada_b[i], 2)
        else:
            csh1, csc1, cg1, csh2, csc2, cg2 = ada_mod(c_ctx, ada_w[i], ada_b[i], N_MOD)
        n_lat = modulate(h_lat, norm1_g[i], sh1, sc1)
        n_ctx = modulate(h_ctx, norm1_g[i], csh1, csc1)
        if i % 2 == 0:
            y_ctx, y_lat = even_mixer(
                n_ctx, n_lat, rows, cols, ev_w_in[j], ev_w_out[j], hy_short_w[j], hy_short_b[j],
                hy_f_w1[j], hy_f_b1[j], hy_f_w2[j], hy_f_b2[j], hy_f_w3[j], hy_f_b3[j], hy_f_freq[j],
                hy_f_decay[j], hy_bias[j], df_lq1[j], df_lk1[j], df_lq2[j], df_lk2[j], df_subln_g[j],
                0.8 - 0.6 * math.exp(-0.3 * i), not last)
        else:
            y_ctx, y_lat = odd_mixer(
                n_ctx, n_lat, od_w_in[j], od_w_out[j], rg_conv_w[j], rg_conv_b[j],
                rg_wa[j], rg_ba[j], rg_wx[j], rg_bx[j], rg_lam[j], not last)
        h_lat = h_lat + g1 * y_lat
        h_lat = h_lat + g2 * sq_relu_mlp(modulate(h_lat, norm2_g[i], sh2, sc2), mlp_w1[i], mlp_w2[i])
        if not last:
            h_ctx = h_ctx + cg1 * y_ctx
            h_ctx = h_ctx + cg2 * sq_relu_mlp(modulate(h_ctx, norm2_g[i], csh2, csc2), mlp_w1[i], mlp_w2[i])
    return rms_norm(h_lat, final_g)
```

```python
import functools
import math

import numpy as np
import jax
import jax.numpy as jnp
from jax import lax
from jax.experimental import pallas as pl
from jax.experimental.pallas import tpu as pltpu

F32 = jnp.float32
BF16 = jnp.bfloat16
HIGHEST = lax.Precision.HIGHEST

EPS = 1e-6
GRID_W = 64
ROPE_BASE = 10000.0
N_BANDS = 16
RG_C = 8.0
RG_CONV_LEFT = 2
HY_SHORT_LEFT = 1

TT = 256
LANES = 128
SUBLANES = 8
VMEM_LIMIT = 56 * 1024 * 1024
FF_CHUNK = 1024
FFT_N2_TILE = 8
FFT_K1_TILE = 8


def _cp(*sem):
    return pltpu.CompilerParams(dimension_semantics=sem, vmem_limit_bytes=VMEM_LIMIT)


def _dot(a, b, **kw):
    return jnp.dot(a, b, preferred_element_type=F32, **kw)


def _sigmoid(x):
    return 1.0 / (1.0 + jnp.exp(-x))


def _rms_mod(x, g, shift, scale):
    y = x * lax.rsqrt(jnp.mean(x * x, axis=-1, keepdims=True) + EPS)
    return (y * g) * (1.0 + scale) + shift


def _ada_kernel(c_ref, w_ref, b_ref, o_ref):
    c = c_ref[...]
    s = c * _sigmoid(c)
    o_ref[0] = _dot(s, w_ref[0], precision=HIGHEST) + b_ref[0]


def _ada_mods(cond, ada_w, ada_b):
    depth, d, n = ada_w.shape
    rows = cond.shape[0]
    tn = min(n, 1536)
    return pl.pallas_call(
        _ada_kernel,
        grid=(depth, n // tn),
        in_specs=[
            pl.BlockSpec((rows, d), lambda l, j: (0, 0)),
            pl.BlockSpec((1, d, tn), lambda l, j: (l, 0, j)),
            pl.BlockSpec((1, 1, tn), lambda l, j: (l, 0, j)),
        ],
        out_specs=pl.BlockSpec((1, rows, tn), lambda l, j: (l, 0, j)),
        out_shape=jax.ShapeDtypeStruct((depth, rows, n), F32),
        compiler_params=_cp("parallel", "parallel"),
        name="ada_mods",
    )(cond, ada_w, ada_b.reshape(depth, 1, n))


def _mod_spec(layer, rows, nb, n_lat_tiles, n6):
    def imap(b, i):
        return (layer * rows + jnp.where(i >= n_lat_tiles, nb, b), 0, 0)
    return pl.BlockSpec((1, 1, n6), imap)


def _ev_inproj_kernel(h_ref, mod_ref, g_ref, w_ref, cos_ref, sin_ref,
                      u_ref, q_ref, k_ref, v_ref, *, d, hy3, qk, qscale):
    m = mod_ref[0]
    xn = _rms_mod(h_ref[0], g_ref[...], m[:, 0:d], m[:, d:2 * d]).astype(BF16)
    y = _dot(xn, w_ref[...])
    u_ref[0] = y[:, :hy3]
    cos = cos_ref[...]
    sin = sin_ref[...]
    lane = lax.broadcasted_iota(jnp.int32, cos.shape, 1)
    first = (lane % 32) < 16

    def rope(z):
        sw = jnp.where(first, pltpu.roll(z, LANES - 16, 1), pltpu.roll(z, 16, 1))
        return z * cos + sw * sin

    for c in range(qk // LANES):
        lo = c * LANES
        q_ref[0, :, lo:lo + LANES] = (rope(y[:, hy3 + lo:hy3 + lo + LANES]) * qscale).astype(BF16)
        k_ref[0, :, lo:lo + LANES] = rope(y[:, hy3 + qk + lo:hy3 + qk + lo + LANES]).astype(BF16)
    v_ref[0] = y[:, hy3 + 2 * qk:].astype(BF16)


def _ev_inproj(h, mods, rows, layer, g, w_in, cos_t, sin_t, *, n_lat_tiles, hy3, qk, head_dim):
    nb, s, d = h.shape
    n_in = w_in.shape[1]
    kern = functools.partial(_ev_inproj_kernel, d=d, hy3=hy3, qk=qk, qscale=head_dim ** -0.5)
    return pl.pallas_call(
        kern,
        grid=(nb, s // TT),
        in_specs=[
            pl.BlockSpec((1, TT, d), lambda b, i: (b, i, 0)),
            _mod_spec(layer, rows, nb, n_lat_tiles, mods.shape[-1]),
            pl.BlockSpec((1, d), lambda b, i: (0, 0)),
            pl.BlockSpec((d, n_in), lambda b, i: (0, 0)),
            pl.BlockSpec((TT, LANES), lambda b, i: (i, 0)),
            pl.BlockSpec((TT, LANES), lambda b, i: (i, 0)),
        ],
        out_specs=[
            pl.BlockSpec((1, TT, hy3), lambda b, i: (b, i, 0)),
            pl.BlockSpec((1, TT, qk), lambda b, i: (b, i, 0)),
            pl.BlockSpec((1, TT, qk), lambda b, i: (b, i, 0)),
            pl.BlockSpec((1, TT, qk), lambda b, i: (b, i, 0)),
        ],
        out_shape=[
            jax.ShapeDtypeStruct((nb, s, hy3), F32),
            jax.ShapeDtypeStruct((nb, s, qk), BF16),
            jax.ShapeDtypeStruct((nb, s, qk), BF16),
            jax.ShapeDtypeStruct((nb, s, qk), BF16),
        ],
        compiler_params=_cp("parallel", "parallel"),
        name="ev_inproj",
    )(h, mods, g, w_in, cos_t, sin_t)


def _attn_kernel(q_ref, k_ref, v_ref, lq1_ref, lk1_ref, lq2_ref, lk2_ref, sg_ref, o_ref,
                 *, n_lat_tiles, n_tiles, lam_init, half):
    i = pl.program_id(2)
    q = q_ref[0]
    lane = lax.broadcasted_iota(jnp.int32, q.shape, 1)
    zero = jnp.zeros_like(q)
    qs = (jnp.where(lane < half, q, zero), jnp.where(lane >= half, q, zero))
    lam = (jnp.exp(jnp.sum(lq1_ref[...] * lk1_ref[...], axis=-1, keepdims=True))
           - jnp.exp(jnp.sum(lq2_ref[...] * lk2_ref[...], axis=-1, keepdims=True)) + lam_init)
    rows, dv = q.shape[0], v_ref.shape[-1]
    start = jnp.where(i >= n_lat_tiles, n_lat_tiles, 0)

    def body(c, carry):
        off = pl.multiple_of(c * TT, TT)
        kc = k_ref[0, pl.ds(off, TT), :]
        vc = v_ref[0, pl.ds(off, TT), :]
        new = []
        for j in range(2):
            m, l, acc = carry[j]
            s = lax.dot_general(qs[j], kc, (((1,), (1,)), ((), ())), preferred_element_type=F32)
            m_new = jnp.maximum(m, jnp.max(s, axis=-1, keepdims=True))
            alpha = jnp.exp(m - m_new)
            p = jnp.exp(s - m_new)
            l = alpha * l + jnp.sum(p, axis=-1, keepdims=True)
            acc = alpha * acc + _dot(p.astype(BF16), vc)
            new.append((m_new, l, acc))
        return tuple(new)

    init = tuple((jnp.full((rows, 1), -jnp.inf, F32), jnp.zeros((rows, 1), F32),
                  jnp.zeros((rows, dv), F32)) for _ in range(2))
    (_, l0, a0), (_, l1, a1) = lax.fori_loop(start, n_tiles, body, init)
    o = a0 / l0 - lam * (a1 / l1)
    on = o * lax.rsqrt(jnp.mean(o * o, axis=-1, keepdims=True) + EPS)
    o_ref[0] = (on * sg_ref[...] * (1.0 - lam_init)).astype(o_ref.dtype)


def _diff_attention(q, k, v, lq1, lk1, lq2, lk2, subln_g, *, n_lat_tiles, lam_init, head_dim):
    nb, s, qk = q.shape
    dv = 2 * head_dim
    heads = qk // dv
    n_tiles = s // TT
    kern = functools.partial(_attn_kernel, n_lat_tiles=n_lat_tiles, n_tiles=n_tiles,
                             lam_init=lam_init, half=head_dim)
    vec = lambda n: pl.BlockSpec((1, n), lambda b, h, i: (0, 0))
    return pl.pallas_call(
        kern,
        grid=(nb, heads, n_tiles),
        in_specs=[
            pl.BlockSpec((1, TT, dv), lambda b, h, i: (b, i, h)),
            pl.BlockSpec((1, s, dv), lambda b, h, i: (b, 0, h)),
            pl.BlockSpec((1, s, dv), lambda b, h, i: (b, 0, h)),
            vec(head_dim), vec(head_dim), vec(head_dim), vec(head_dim), vec(dv),
        ],
        out_specs=pl.BlockSpec((1, TT, dv), lambda b, h, i: (b, i, h)),
        out_shape=jax.ShapeDtypeStruct((nb, s, qk), BF16),
        compiler_params=_cp("parallel", "parallel", "parallel"),
        name="diff_attn",
    )(q, k, v, lq1, lk1, lq2, lk2, subln_g)


def _halo_fill(xp_ref, prev_ref, x_ref, next_ref, has_prev, has_next):
    x = x_ref[0]
    xp_ref[SUBLANES:SUBLANES + TT, :] = x
    zero = jnp.zeros((SUBLANES, x.shape[-1]), x.dtype)
    xp_ref[0:SUBLANES, :] = jnp.where(has_prev, prev_ref[0, 0], zero)
    xp_ref[SUBLANES + TT:2 * SUBLANES + TT, :] = jnp.where(has_next, next_ref[0, 0], zero)


def _conv_taps(xp_ref, w, bias, left):
    width = w.shape[0]
    acc = bias
    for j in range(width):
        lo = SUBLANES + j - left
        acc = acc + w[j:j + 1, :] * xp_ref[lo:lo + TT, :]
    return acc


def _halo_specs(width, tile0, n_seg_tiles, n8):
    per = TT // SUBLANES
    prev = pl.BlockSpec((1, 1, SUBLANES, width),
                        lambda b, i: (b, jnp.maximum((tile0 + i) * per - 1, 0), 0, 0))
    nxt = pl.BlockSpec((1, 1, SUBLANES, width),
                       lambda b, i: (b, jnp.minimum((tile0 + i + 1) * per, n8 - 1), 0, 0))
    return prev, nxt


def _shortconv_kernel(prev_ref, x_ref, next_ref, w_ref, b_ref, v_ref, x1_ref, x2_ref, xp_ref,
                      *, n_seg_tiles, hy):
    i = pl.program_id(1)
    _halo_fill(xp_ref, prev_ref, x_ref, next_ref, i > 0, i < n_seg_tiles - 1)
    y = _conv_taps(xp_ref, w_ref[...], b_ref[...], HY_SHORT_LEFT)
    v_ref[0] = y[:, :hy]
    x1_ref[0] = y[:, hy:2 * hy]
    x2_ref[0] = y[:, 2 * hy:]


def _shortconv(u, w, b, *, tile0, n_seg_tiles):
    nb, s, hy3 = u.shape
    hy = hy3 // 3
    n8 = s // SUBLANES
    u8 = u.reshape(nb, n8, SUBLANES, hy3)
    prev, nxt = _halo_specs(hy3, tile0, n_seg_tiles, n8)
    out = jax.ShapeDtypeStruct((nb, n_seg_tiles * TT, hy), F32)
    ospec = pl.BlockSpec((1, TT, hy), lambda b, i: (b, i, 0))
    return pl.pallas_call(
        functools.partial(_shortconv_kernel, n_seg_tiles=n_seg_tiles, hy=hy),
        grid=(nb, n_seg_tiles),
        in_specs=[
            prev,
            pl.BlockSpec((1, TT, hy3), lambda b, i: (b, tile0 + i, 0)),
            nxt,
            pl.BlockSpec(w.shape, lambda b, i: (0, 0)),
            pl.BlockSpec((1, hy3), lambda b, i: (0, 0)),
        ],
        out_specs=[ospec, ospec, ospec],
        out_shape=[out, out, out],
        scratch_shapes=[pltpu.VMEM((TT + 2 * SUBLANES, hy3), F32)],
        compiler_params=_cp("parallel", "parallel"),
        name="hy_shortconv",
    )(u8, u, u8, w, b.reshape(1, hy3))


def _filter_feats(lh):
    n = np.arange(2 * lh)
    lag = np.where(n < lh, n, 2 * lh - n).astype(np.float64)
    t = (lag / lh).astype(np.float32).astype(np.float64)
    bands = np.arange(1, N_BANDS + 1, dtype=np.float64)
    ang = 2.0 * math.pi * t[:, None] * bands
    feats = np.concatenate([t[:, None], np.cos(ang), np.sin(ang)], axis=-1)
    pad = (-feats.shape[1]) % SUBLANES
    return np.pad(feats, ((0, 0), (0, pad)))


def _filt_kernel(ft_ref, w1_ref, b1_ref, w2_ref, b2_ref, w3_ref, b3_ref, fr_ref, dec_ref, o_ref,
                 *, lh, c, tr):
    ft = ft_ref[...]
    freq = fr_ref[...]
    h = jnp.sin(freq * (_dot(ft, w1_ref[...], precision=HIGHEST) + b1_ref[...]))
    h = jnp.sin(freq * (_dot(h, w2_ref[...], precision=HIGHEST) + b2_ref[...]))
    h = _dot(h, w3_ref[...], precision=HIGHEST) + b3_ref[...]
    t = ft[:, 0:1]
    h = h * jnp.exp(-t * jnp.abs(dec_ref[...]))
    row = pl.program_id(0) * tr + lax.broadcasted_iota(jnp.int32, (tr, 1), 0)
    for o in range(2):
        fwd = h[:, (2 * o) * c:(2 * o + 1) * c]
        bwd = h[:, (2 * o + 1) * c:(2 * o + 2) * c]
        o_ref[o] = jnp.where(row < lh, fwd, jnp.where(row > lh, bwd, jnp.zeros_like(bwd)))


def _hyena_filter(lh, w1, b1, w2, b2, w3, b3, freq, decay):
    c = decay.shape[-1]
    feats = jnp.asarray(_filter_feats(lh), F32)
    fe = feats.shape[1]
    w1p = jnp.pad(w1, ((0, fe - w1.shape[0]), (0, 0)))
    hid = w1.shape[1]
    tr = min(2 * lh, 512)
    full = lambda a: pl.BlockSpec(a.shape, lambda i: (0,) * a.ndim)
    args = (w1p, b1.reshape(1, hid), w2, b2.reshape(1, hid), w3, b3.reshape(1, -1),
            freq.reshape(1, hid), decay.reshape(1, -1))
    return pl.pallas_call(
        functools.partial(_filt_kernel, lh=lh, c=c, tr=tr),
        grid=(2 * lh // tr,),
        in_specs=[pl.BlockSpec((tr, fe), lambda i: (i, 0))] + [full(a) for a in args],
        out_specs=pl.BlockSpec((2, tr, c), lambda i: (0, i, 0)),
        out_shape=jax.ShapeDtypeStruct((2, 2 * lh, c), F32),
        compiler_params=_cp("parallel"),
        name="hy_filter",
    )(feats, *args)


@functools.lru_cache(maxsize=None)
def _dft_tables(n):
    nn = n * n
    h = n // 2
    k = np.arange(n)
    th = 2.0 * math.pi * np.outer(k, k) / n
    c, s = np.cos(th), np.sin(th)
    f1_data = np.block([[c[:, :h], s[:, :h]], [-s[:, :h], c[:, :h]]])
    f1_real = np.concatenate([c, -s], axis=0)
    idx = (k[None, None, :] * (k[:, None, None] + n * k[None, :, None])) % nn
    phi = 2.0 * math.pi * idx / nn
    cp, sp = np.cos(phi), np.sin(phi)
    g = np.concatenate([np.concatenate([cp, sp], axis=2), np.concatenate([-sp, cp], axis=2)], axis=1)
    hmat = np.transpose(g, (0, 2, 1)) / nn
    ci, si = c[:h, :], s[:h, :]
    f3 = np.zeros((n, 2 * n))
    f3[:h, 0::2], f3[:h, 1::2] = ci, -si
    f3[h:, 0::2], f3[h:, 1::2] = si, ci
    return f1_data, f1_real, g, hmat, f3


def _fft_s1_kernel(x_ref, f_ref, o_ref):
    o_ref[0] = _dot(f_ref[...], x_ref[0].astype(BF16)).astype(o_ref.dtype)


def _fft_s1(x, f1, n, c):
    p = x.shape[0]
    tc = FFT_N2_TILE * c
    return pl.pallas_call(
        _fft_s1_kernel,
        grid=(p, n * c // tc),
        in_specs=[pl.BlockSpec((1, n, tc), lambda q, j: (q, 0, j)),
                  pl.BlockSpec(f1.shape, lambda q, j: (0, 0))],
        out_specs=pl.BlockSpec((1, 2 * n, tc), lambda q, j: (q, 0, j)),
        out_shape=jax.ShapeDtypeStruct((p, 2 * n, n * c), BF16),
        compiler_params=_cp("parallel", "parallel"),
        name="fft_s1",
    )(x, f1)


def _fft_spec_kernel(a_ref, g_ref, o_ref, *, tk):
    for j in range(tk):
        x = jnp.concatenate([a_ref[0, 0, j], a_ref[0, 1, j]], axis=0)
        o_ref[0, j] = _dot(g_ref[j], x)


def _fft_mid_kernel(a_ref, g_ref, h_ref, kh_ref, o_ref, *, tk, n):
    for j in range(tk):
        x = jnp.concatenate([a_ref[0, 0, j], a_ref[0, 1, j]], axis=0)
        t = _dot(g_ref[j], x)
        tr, ti = t[:n], t[n:]
        kr, ki = kh_ref[0, j, :n], kh_ref[0, j, n:]
        y = jnp.concatenate([tr * kr - ti * ki, tr * ki + ti * kr], axis=0).astype(BF16)
        o_ref[0, j] = _dot(h_ref[j], y).astype(o_ref.dtype)


def _fft_spectrum(a, g, n, c):
    p = a.shape[0]
    tk = FFT_K1_TILE
    a5 = a.reshape(p, 2, n, n, c)
    return pl.pallas_call(
        functools.partial(_fft_spec_kernel, tk=tk),
        grid=(p, n // tk),
        in_specs=[pl.BlockSpec((1, 2, tk, n, c), lambda q, j: (q, 0, j, 0, 0)),
                  pl.BlockSpec((tk, 2 * n, 2 * n), lambda q, j: (j, 0, 0))],
        out_specs=pl.BlockSpec((1, tk, 2 * n, c), lambda q, j: (q, j, 0, 0)),
        out_shape=jax.ShapeDtypeStruct((p, n, 2 * n, c), F32),
        compiler_params=_cp("parallel", "parallel"),
        name="fft_spectrum",
    )(a5, g)


def _fft_mid(a, g, hm, khat, order, n, c):
    p = a.shape[0]
    tk = FFT_K1_TILE
    a5 = a.reshape(p, 2, n, n, c)
    return pl.pallas_call(
        functools.partial(_fft_mid_kernel, tk=tk, n=n),
        grid=(p, n // tk),
        in_specs=[pl.BlockSpec((1, 2, tk, n, c), lambda q, j: (q, 0, j, 0, 0)),
                  pl.BlockSpec((tk, 2 * n, 2 * n), lambda q, j: (j, 0, 0)),
                  pl.BlockSpec((tk, 2 * n, 2 * n), lambda q, j: (j, 0, 0)),
                  pl.BlockSpec((1, tk, 2 * n, c), lambda q, j: (order, j, 0, 0))],
        out_specs=pl.BlockSpec((1, tk, 2 * n, c), lambda q, j: (q, j, 0, 0)),
        out_shape=jax.ShapeDtypeStruct((p, n, 2 * n, c), BF16),
        compiler_params=_cp("parallel", "parallel"),
        name="fft_mid",
    )(a5, g, hm, khat)


def _fft_s3_kernel(c_ref, f_ref, v_ref, x_ref, b_ref, o_ref):
    y = _dot(f_ref[...], c_ref[0])
    v = v_ref[0]
    o_ref[0] = (x_ref[0] * (y + v * b_ref[...])).astype(o_ref.dtype)


def _fft_s3(cm, f3, vin, xg, bias, n, c, out_dtype):
    p = cm.shape[0]
    tc = FFT_N2_TILE * c
    c2 = cm.reshape(p, 2 * n, n * c)
    bias_t = jnp.tile(bias.reshape(1, c), (1, FFT_N2_TILE))
    blk = pl.BlockSpec((1, n, tc), lambda q, j: (q, 0, j))
    return pl.pallas_call(
        _fft_s3_kernel,
        grid=(p, n * c // tc),
        in_specs=[pl.BlockSpec((1, 2 * n, tc), lambda q, j: (q, 0, j)),
                  pl.BlockSpec(f3.shape, lambda q, j: (0, 0)),
                  blk, blk,
                  pl.BlockSpec((1, tc), lambda q, j: (0, 0))],
        out_specs=blk,
        out_shape=jax.ShapeDtypeStruct((p, n, n * c), out_dtype),
        compiler_params=_cp("parallel", "parallel"),
        name="fft_s3",
    )(c2, f3, vin, xg, bias_t)


def _hyena_long(v, x1, x2, kk, hy_bias):
    nb, l, c = v.shape
    n = math.isqrt(2 * l)
    assert n * n == 2 * l and nb % 2 == 0
    p = nb // 2
    f1d, f1r, g, hm, f3 = (jnp.asarray(t, F32).astype(BF16) for t in _dft_tables(n))
    pair = lambda a: a.reshape(p, n, n * c)
    khat = _fft_spectrum(_fft_s1(kk.reshape(2, n, n * c), f1r, n, c), g, n, c)
    z = pair(v)
    for order, (xg, dt) in enumerate(((x1, F32), (x2, BF16))):
        a = _fft_s1(z, f1d, n, c)
        cm = _fft_mid(a, g, hm, khat, order, n, c)
        z = _fft_s3(cm, f3, z, pair(xg), hy_bias[order], n, c, dt)
    return z.reshape(nb, l, c)


@functools.lru_cache(maxsize=None)
def _ctx_dft_tables(lc):
    m = 2 * lc
    k = np.arange(m)
    th = 2.0 * math.pi * np.outer(k, k) / m
    c, s = np.cos(th), np.sin(th)
    f_data = np.block([[c[:, :lc], s[:, :lc]], [-s[:, :lc], c[:, :lc]]])
    f_real = np.concatenate([c, -s], axis=0)
    ci, si = c[:lc, :], s[:lc, :]
    f_inv = np.block([[ci, -si], [si, ci]]) / m
    return f_data, f_real, f_inv


def _ctxconv_kernel(v_ref, x1_ref, x2_ref, kk_ref, fd_ref, fr_ref, fi_ref, b_ref, o_ref, *, m):
    def conv(u, order):
        kh = _dot(fr_ref[...], kk_ref[order].astype(BF16))
        t = _dot(fd_ref[...], u.astype(BF16))
        tr, ti, kr, ki = t[:m], t[m:], kh[:m], kh[m:]
        y = jnp.concatenate([tr * kr - ti * ki, tr * ki + ti * kr], axis=0).astype(BF16)
        return _dot(fi_ref[...], y)

    v = v_ref[0]
    z1 = x1_ref[0] * (conv(v, 0) + v * b_ref[0:1, :])
    o_ref[0] = (x2_ref[0] * (conv(z1, 1) + z1 * b_ref[1:2, :])).astype(o_ref.dtype)


def _hyena_ctx(v, x1, x2, kk, hy_bias):
    nb, lc, c = v.shape
    p, m = nb // 2, 2 * lc
    fd, fr, fi = (jnp.asarray(t, F32).astype(BF16) for t in _ctx_dft_tables(lc))
    pair = lambda a: a.reshape(p, m, c)
    blk = pl.BlockSpec((1, m, c), lambda q: (q, 0, 0))
    full = lambda a: pl.BlockSpec(a.shape, lambda q: (0,) * a.ndim)
    z = pl.pallas_call(
        functools.partial(_ctxconv_kernel, m=m),
        grid=(p,),
        in_specs=[blk, blk, blk, full(kk), full(fd), full(fr), full(fi), full(hy_bias)],
        out_specs=blk,
        out_shape=jax.ShapeDtypeStruct((p, m, c), BF16),
        compiler_params=_cp("parallel"),
        name="hy_ctx",
    )(pair(v), pair(x1), pair(x2), kk, fd, fr, fi, hy_bias)
    return z.reshape(nb, lc, c)


def _od_inproj_kernel(h_ref, mod_ref, g_ref, w_ref, gate_ref, x_ref, *, d, r):
    m = mod_ref[0]
    xn = _rms_mod(h_ref[0], g_ref[...], m[:, 0:d], m[:, d:2 * d]).astype(BF16)
    y = _dot(xn, w_ref[...])
    gate_ref[0] = y[:, :r].astype(gate_ref.dtype)
    x_ref[0] = y[:, r:]


def _od_inproj(h, mods, rows, layer, g, w_in, *, n_lat_tiles):
    nb, s, d = h.shape
    r = w_in.shape[1] // 2
    ospec = pl.BlockSpec((1, TT, r), lambda b, i: (b, i, 0))
    return pl.pallas_call(
        functools.partial(_od_inproj_kernel, d=d, r=r),
        grid=(nb, s // TT),
        in_specs=[
            pl.BlockSpec((1, TT, d), lambda b, i: (b, i, 0)),
            _mod_spec(layer, rows, nb, n_lat_tiles, mods.shape[-1]),
            pl.BlockSpec((1, d), lambda b, i: (0, 0)),
            pl.BlockSpec(w_in.shape, lambda b, i: (0, 0)),
        ],
        out_specs=[ospec, ospec],
        out_shape=[jax.ShapeDtypeStruct((nb, s, r), BF16), jax.ShapeDtypeStruct((nb, s, r), F32)],
        compiler_params=_cp("parallel", "parallel"),
        name="od_inproj",
    )(h, mods, g, w_in)


def _rglru_kernel(prev_ref, x_ref, next_ref, cw_ref, cb_ref, w_ref, b_ref, lam_ref, o_ref,
                  xp_ref, a_ref, bb_ref, carry_ref, *, n_lat_tiles, n_tiles, r):
    dr = pl.program_id(1)
    i = pl.program_id(2)
    tile = _scan_tile(dr, i, n_lat_tiles, n_tiles)
    seg_first = jnp.logical_or(tile == 0, tile == n_lat_tiles)
    seg_last = jnp.logical_or(tile == n_lat_tiles - 1, tile == n_tiles - 1)
    _halo_fill(xp_ref, prev_ref, x_ref, next_ref, jnp.logical_not(seg_first),
               jnp.logical_not(seg_last))
    xc = _conv_taps(xp_ref, cw_ref[...], cb_ref[...], RG_CONV_LEFT)
    pre = _dot(xc.astype(BF16), w_ref[0]) + b_ref[0]
    rg = _sigmoid(pre[:, :r])
    ig = _sigmoid(pre[:, r:])
    nl = -lam_ref[0]
    softplus = jnp.maximum(nl, 0.0) + jnp.log1p(jnp.exp(-jnp.abs(nl)))
    log_a = -RG_C * rg * softplus
    a = jnp.exp(log_a)
    a_ref[...] = a
    bb_ref[...] = jnp.sqrt(-jnp.tanh(log_a) * (a * a + 1.0)) * (ig * xc)

    @pl.when(i == 0)
    def _():
        carry_ref[...] = jnp.zeros_like(carry_ref)

    def step(t, h):
        idx = jnp.where(dr == 0, t, TT - 1 - t)
        h = a_ref[pl.ds(idx, 1), :] * h + bb_ref[pl.ds(idx, 1), :]
        o_ref[0, 0, pl.ds(idx, 1), :] = h
        return h

    carry_ref[0:1, :] = lax.fori_loop(0, TT, step, carry_ref[0:1, :], unroll=8)


def _scan_tile(dr, i, n_lat_tiles, n_tiles):
    fwd = jnp.where(i == 0, n_lat_tiles, i - 1)
    bwd = jnp.where(i == 0, n_lat_tiles, n_lat_tiles - i)
    return jnp.where(dr == 0, fwd, bwd)


def _rglru(x, conv_w, conv_b, wcat, bcat, lam, *, n_lat_tiles):
    nb, s, r = x.shape
    n_tiles = s // TT
    assert n_tiles == n_lat_tiles + 1
    n8 = s // SUBLANES
    per = TT // SUBLANES
    x8 = x.reshape(nb, n8, SUBLANES, r)
    tile = lambda d, i: _scan_tile(d, i, n_lat_tiles, n_tiles)
    return pl.pallas_call(
        functools.partial(_rglru_kernel, n_lat_tiles=n_lat_tiles, n_tiles=n_tiles, r=r),
        grid=(nb, 2, n_tiles),
        in_specs=[
            pl.BlockSpec((1, 1, SUBLANES, r),
                         lambda b, d, i: (b, jnp.maximum(tile(d, i) * per - 1, 0), 0, 0)),
            pl.BlockSpec((1, TT, r), lambda b, d, i: (b, tile(d, i), 0)),
            pl.BlockSpec((1, 1, SUBLANES, r),
                         lambda b, d, i: (b, jnp.minimum((tile(d, i) + 1) * per, n8 - 1), 0, 0)),
            pl.BlockSpec(conv_w.shape, lambda b, d, i: (0, 0)),
            pl.BlockSpec((1, r), lambda b, d, i: (0, 0)),
            pl.BlockSpec((1, r, 2 * r), lambda b, d, i: (d, 0, 0)),
            pl.BlockSpec((1, 1, 2 * r), lambda b, d, i: (d, 0, 0)),
            pl.BlockSpec((1, 1, r), lambda b, d, i: (d, 0, 0)),
        ],
        out_specs=pl.BlockSpec((1, 1, TT, r), lambda b, d, i: (d, b, tile(d, i), 0)),
        out_shape=jax.ShapeDtypeStruct((2, nb, s, r), F32),
        scratch_shapes=[pltpu.VMEM((TT + 2 * SUBLANES, r), F32), pltpu.VMEM((TT, r), F32),
                        pltpu.VMEM((TT, r), F32), pltpu.VMEM((SUBLANES, r), F32)],
        compiler_params=_cp("parallel", "parallel", "arbitrary"),
        name="rglru",
    )(x8, x, x8, conv_w, conv_b.reshape(1, r), wcat, bcat, lam.reshape(2, 1, r))


def _block_diag(w):
    n, bs, _ = w.shape
    eye = jnp.eye(n, dtype=w.dtype)
    return (eye[:, None, :, None] * w[:, :, None, :]).reshape(n * bs, n * bs)


def _gelu_tanh(x):
    return 0.5 * x * (1.0 + jnp.tanh(math.sqrt(2.0 / math.pi) * (x + 0.044715 * (x * x * x))))


def _post_kernel(*refs, kind, final, d, n_lat_tiles):
    if kind == "even":
        h_ref, zl_ref, zc_ref, o_ref, mod_ref, g2_ref, wo_ref, w1_ref, w2_ref = refs[:9]
        rest = refs[9:]
        is_ctx = pl.program_id(1) >= n_lat_tiles
        z = jnp.where(is_ctx, zc_ref[0], zl_ref[0])
        half = z.shape[-1]
        y = _dot(z, wo_ref[:half, :]) + _dot(o_ref[0], wo_ref[half:, :])
    else:
        h_ref, hd_ref, gate_ref, mod_ref, g2_ref, wo_ref, w1_ref, w2_ref = refs[:8]
        rest = refs[8:]
        mix = (hd_ref[0, 0] + hd_ref[1, 0]) * _gelu_tanh(gate_ref[0].astype(F32))
        y = _dot(mix.astype(BF16), wo_ref[...])
    out_ref = rest[-1]
    m = mod_ref[0]
    h1 = h_ref[0] + m[:, 2 * d:3 * d] * y
    xn = _rms_mod(h1, g2_ref[...], m[:, 3 * d:4 * d], m[:, 4 * d:5 * d]).astype(BF16)
    acc = jnp.zeros_like(h1)
    dff = w1_ref.shape[1]
    for c0 in range(0, dff, FF_CHUNK):
        a = jnp.maximum(_dot(xn, w1_ref[:, c0:c0 + FF_CHUNK]), 0.0)
        acc = acc + _dot((a * a).astype(BF16), w2_ref[c0:c0 + FF_CHUNK, :])
    h2 = h1 + m[:, 5 * d:6 * d] * acc
    if final:
        fg_ref = rest[0]
        h2 = h2 * lax.rsqrt(jnp.mean(h2 * h2, axis=-1, keepdims=True) + EPS) * fg_ref[...]
    out_ref[0] = h2


def _post(kind, h, mix_args, mods, rows, layer, g2, w_out, w1, w2, final_g, *, n_lat_tiles, final):
    nb, s, d = h.shape
    n_tiles = n_lat_tiles if final else s // TT
    tok = lambda w: pl.BlockSpec((1, TT, w), lambda b, i: (b, i, 0))
    full = lambda a: pl.BlockSpec(a.shape, lambda b, i: (0,) * a.ndim)
    if kind == "even":
        z_lat, z_ctx, o = mix_args
        hy = z_lat.shape[-1]
        mix_specs = [
            pl.BlockSpec((1, TT, hy), lambda b, i: (b, jnp.minimum(i, n_lat_tiles - 1), 0)),
            pl.BlockSpec((1, TT, hy), lambda b, i: (b, 0, 0)),
            tok(o.shape[-1]),
        ]
    else:
        hd, gate = mix_args
        r = gate.shape[-1]
        mix_specs = [pl.BlockSpec((2, 1, TT, r), lambda b, i: (0, b, i, 0)), tok(r)]
    in_specs = [tok(d)] + mix_specs + [
        _mod_spec(layer, rows, nb, n_lat_tiles, mods.shape[-1]),
        pl.BlockSpec((1, d), lambda b, i: (0, 0)), full(w_out), full(w1), full(w2)]
    args = [h, *mix_args, mods, g2, w_out, w1, w2]
    if final:
        in_specs.append(pl.BlockSpec((1, d), lambda b, i: (0, 0)))
        args.append(final_g)
    return pl.pallas_call(
        functools.partial(_post_kernel, kind=kind, final=final, d=d, n_lat_tiles=n_lat_tiles),
        grid=(nb, n_tiles),
        in_specs=in_specs,
        out_specs=tok(d),
        out_shape=jax.ShapeDtypeStruct((nb, n_tiles * TT, d), F32),
        compiler_params=_cp("parallel", "parallel"),
        name="post_" + kind,
    )(*args)


@functools.lru_cache(maxsize=None)
def _rope_tables(l, lc, head_dim):
    axis = head_dim // 2
    freqs = ROPE_BASE ** (-np.arange(0, axis, 2, dtype=np.float64) / axis)
    freqs = freqs.astype(np.float32).astype(np.float64)
    t = np.arange(l)
    ang_r = (t // GRID_W)[:, None] * freqs
    ang_c = (t % GRID_W)[:, None] * freqs
    cos = np.concatenate([np.cos(ang_r)] * 2 + [np.cos(ang_c)] * 2, axis=-1)
    sin = np.concatenate([-np.sin(ang_r), np.sin(ang_r), -np.sin(ang_c), np.sin(ang_c)], axis=-1)
    cos = np.concatenate([cos, np.ones((lc, head_dim))], axis=0)
    sin = np.concatenate([sin, np.zeros((lc, head_dim))], axis=0)
    rep = LANES // head_dim
    return np.tile(cos, (1, rep)), np.tile(sin, (1, rep))


def kernel(x, c, ctx, c_ctx, ada_w, ada_b, norm1_g, norm2_g, mlp_w1, mlp_w2, final_g, ev_w_in, ev_w_out, hy_short_w, hy_short_b, hy_f_w1, hy_f_b1, hy_f_w2, hy_f_b2, hy_f_w3, hy_f_b3, hy_f_freq, hy_f_decay, hy_bias, df_lq1, df_lk1, df_lq2, df_lk2, df_subln_g, od_w_in, od_w_out, rg_conv_w, rg_conv_b, rg_wa, rg_ba, rg_wx, rg_bx, rg_lam):
    nb, l, d = x.shape
    lc = ctx.shape[1]
    depth = ada_w.shape[0]
    hy = hy_bias.shape[-1]
    head_dim = df_lq1.shape[-1]
    qk = (ev_w_in.shape[-1] - 3 * hy) // 3
    r = rg_lam.shape[-1]
    assert l % TT == 0 and lc == TT and l % GRID_W == 0
    n_lat_tiles = l // TT

    rows = -(-(nb + 1) // SUBLANES) * SUBLANES
    cond = jnp.zeros((rows, d), F32).at[:nb].set(c).at[nb].set(c_ctx)
    mods = _ada_mods(cond, ada_w, ada_b).reshape(depth * rows, 1, ada_w.shape[-1])

    cos_t, sin_t = (jnp.asarray(t, F32) for t in _rope_tables(l, lc, head_dim))
    h = jnp.concatenate([x, ctx], axis=1)
    w1_bf, w2_bf = mlp_w1.astype(BF16), mlp_w2.astype(BF16)

    for i in range(depth):
        j = i // 2
        final = i == depth - 1
        g1, g2 = norm1_g[i].reshape(1, d), norm2_g[i].reshape(1, d)
        if i % 2 == 0:
            lam_init = 0.8 - 0.6 * math.exp(-0.3 * i)
            u, q, k, v = _ev_inproj(h, mods, rows, i, g1, ev_w_in[j].astype(BF16), cos_t, sin_t,
                                    n_lat_tiles=n_lat_tiles, hy3=3 * hy, qk=qk, head_dim=head_dim)
            o = _diff_attention(q, k, v, df_lq1[j].reshape(1, -1), df_lk1[j].reshape(1, -1),
                                df_lq2[j].reshape(1, -1), df_lk2[j].reshape(1, -1),
                                df_subln_g[j].reshape(1, -1), n_lat_tiles=n_lat_tiles,
                                lam_init=lam_init, head_dim=head_dim)
            fparams = (hy_f_w1[j], hy_f_b1[j], hy_f_w2[j], hy_f_b2[j], hy_f_w3[j], hy_f_b3[j],
                       hy_f_freq[j], hy_f_decay[j])
            vl, x1l, x2l = _shortconv(u, hy_short_w[j], hy_short_b[j], tile0=0,
                                      n_seg_tiles=n_lat_tiles)
            vc, x1c, x2c = _shortconv(u, hy_short_w[j], hy_short_b[j], tile0=n_lat_tiles,
                                      n_seg_tiles=lc // TT)
            z_lat = _hyena_long(vl, x1l, x2l, _hyena_filter(l, *fparams), hy_bias[j])
            z_ctx = _hyena_ctx(vc, x1c, x2c, _hyena_filter(lc, *fparams), hy_bias[j])
            h = _post("even", h, (z_lat, z_ctx, o), mods, rows, i, g2, ev_w_out[j].astype(BF16),
                      w1_bf[i], w2_bf[i], final_g.reshape(1, d), n_lat_tiles=n_lat_tiles,
                      final=final)
        else:
            gate, xr = _od_inproj(h, mods, rows, i, g1, od_w_in[j].astype(BF16), n_lat_tiles=n_lat_tiles)
            wcat = jnp.stack([jnp.concatenate([_block_diag(rg_wa[j, dd]), _block_diag(rg_wx[j, dd])],
                                              axis=1) for dd in range(2)]).astype(BF16)
            bcat = jnp.concatenate([rg_ba[j], rg_bx[j]], axis=-1).reshape(2, 1, 2 * r)
            hd = _rglru(xr, rg_conv_w[j], rg_conv_b[j], wcat, bcat, rg_lam[j],
                        n_lat_tiles=n_lat_tiles)
            h = _post("odd", h, (hd, gate), mods, rows, i, g2, od_w_out[j].astype(BF16),
                      w1_bf[i], w2_bf[i], final_g.reshape(1, d), n_lat_tiles=n_lat_tiles,
                      final=final)
    return h[:, :l] if h.shape[1] != l else h
```

```python
import functools
import math

import numpy as np
import jax
import jax.numpy as jnp
from jax import lax
from jax.experimental import pallas as pl
from jax.experimental.pallas import tpu as pltpu

F32 = jnp.float32
BF16 = jnp.bfloat16
HIGHEST = lax.Precision.HIGHEST

EPS = 1e-6
GRID_W = 64
ROPE_BASE = 10000.0
N_BANDS = 16
RG_C = 8.0
RG_CONV_LEFT = 2
HY_SHORT_LEFT = 1

TT = 256
LANES = 128
SUBLANES = 8
VMEM_LIMIT = 56 * 1024 * 1024
FF_CHUNK = 1024
FFT_N2_TILE = 8
FFT_K1_TILE = 8


def _cp(*sem):
    return pltpu.CompilerParams(dimension_semantics=sem, vmem_limit_bytes=VMEM_LIMIT)


def _dot(a, b, **kw):
    return jnp.dot(a, b, preferred_element_type=F32, **kw)


def _sigmoid(x):
    return 1.0 / (1.0 + jnp.exp(-x))


def _sigmoid_tanh(x):
    return 0.5 * jnp.tanh(0.5 * x) + 0.5


def _rms_mod(x, g, shift, scale):
    y = x * lax.rsqrt(jnp.mean(x * x, axis=-1, keepdims=True) + EPS)
    return (y * g) * (1.0 + scale) + shift


def _ada_kernel(c_ref, w_ref, b_ref, o_ref):
    c = c_ref[...]
    s = c * _sigmoid(c)
    o_ref[0] = _dot(s, w_ref[0], precision=HIGHEST) + b_ref[0]


def _ada_mods(cond, ada_w, ada_b):
    depth, d, n = ada_w.shape
    rows = cond.shape[0]
    tn = min(n, 1536)
    return pl.pallas_call(
        _ada_kernel,
        grid=(depth, n // tn),
        in_specs=[
            pl.BlockSpec((rows, d), lambda l, j: (0, 0)),
            pl.BlockSpec((1, d, tn), lambda l, j: (l, 0, j)),
            pl.BlockSpec((1, 1, tn), lambda l, j: (l, 0, j)),
        ],
        out_specs=pl.BlockSpec((1, rows, tn), lambda l, j: (l, 0, j)),
        out_shape=jax.ShapeDtypeStruct((depth, rows, n), F32),
        compiler_params=_cp("parallel", "parallel"),
        name="ada_mods",
    )(cond, ada_w, ada_b.reshape(depth, 1, n))


def _mod_spec(layer, rows, nb, n_lat_tiles, n6):
    def imap(b, i):
        return (layer * rows + jnp.where(i >= n_lat_tiles, nb, b), 0, 0)
    return pl.BlockSpec((1, 1, n6), imap)


def _ev_inproj_kernel(h_ref, mod_ref, g_ref, w_ref, cos_ref, sin_ref,
                      u_ref, q_ref, k_ref, v_ref, *, d, hy3, qk, qscale):
    m = mod_ref[0]
    xn = _rms_mod(h_ref[0], g_ref[...], m[:, 0:d], m[:, d:2 * d]).astype(BF16)
    y = _dot(xn, w_ref[...])
    u_ref[0] = y[:, :hy3].astype(u_ref.dtype)
    cos = cos_ref[...]
    sin = sin_ref[...]
    lane = lax.broadcasted_iota(jnp.int32, cos.shape, 1)
    first = (lane % 32) < 16

    def rope(z):
        sw = jnp.where(first, pltpu.roll(z, LANES - 16, 1), pltpu.roll(z, 16, 1))
        return z * cos + sw * sin

    for c in range(qk // LANES):
        lo = c * LANES
        q_ref[0, :, lo:lo + LANES] = (rope(y[:, hy3 + lo:hy3 + lo + LANES]) * qscale).astype(BF16)
        k_ref[0, :, lo:lo + LANES] = rope(y[:, hy3 + qk + lo:hy3 + qk + lo + LANES]).astype(BF16)
    one_hot = jnp.where(lane == 0, 1.0, 0.0).astype(BF16)
    for c in range(qk // LANES):
        lo = hy3 + 2 * qk + c * LANES
        v_ref[0, :, 2 * c * LANES:(2 * c + 1) * LANES] = y[:, lo:lo + LANES].astype(BF16)
        v_ref[0, :, (2 * c + 1) * LANES:(2 * c + 2) * LANES] = one_hot


def _ev_inproj(h, mods, rows, layer, g, w_in, cos_t, sin_t, *, n_lat_tiles, hy3, qk, head_dim):
    nb, s, d = h.shape
    n_in = w_in.shape[1]
    assert 2 * head_dim == LANES
    kern = functools.partial(_ev_inproj_kernel, d=d, hy3=hy3, qk=qk,
                             qscale=head_dim ** -0.5 * math.log2(math.e))
    return pl.pallas_call(
        kern,
        grid=(nb, s // TT),
        in_specs=[
            pl.BlockSpec((1, TT, d), lambda b, i: (b, i, 0)),
            _mod_spec(layer, rows, nb, n_lat_tiles, mods.shape[-1]),
            pl.BlockSpec((1, d), lambda b, i: (0, 0)),
            pl.BlockSpec((d, n_in), lambda b, i: (0, 0)),
            pl.BlockSpec((TT, LANES), lambda b, i: (i, 0)),
            pl.BlockSpec((TT, LANES), lambda b, i: (i, 0)),
        ],
        out_specs=[
            pl.BlockSpec((1, TT, hy3), lambda b, i: (b, i, 0)),
            pl.BlockSpec((1, TT, qk), lambda b, i: (b, i, 0)),
            pl.BlockSpec((1, TT, qk), lambda b, i: (b, i, 0)),
            pl.BlockSpec((1, TT, 2 * qk), lambda b, i: (b, i, 0)),
        ],
        out_shape=[
            jax.ShapeDtypeStruct((nb, s, hy3), BF16),
            jax.ShapeDtypeStruct((nb, s, qk), BF16),
            jax.ShapeDtypeStruct((nb, s, qk), BF16),
            jax.ShapeDtypeStruct((nb, s, 2 * qk), BF16),
        ],
        compiler_params=_cp("parallel", "parallel"),
        name="ev_inproj",
    )(h, mods, g, w_in, cos_t, sin_t)


def _attn_kernel(q_ref, k_ref, v_ref, lq1_ref, lk1_ref, lq2_ref, lk2_ref, sg_ref, o_ref,
                 s_ref, p_ref, acc_ref, m_ref, alpha_ref,
                 *, n_lat_tiles, s_len, tk, lam_init, half):
    i = pl.program_id(2)
    q = q_ref[0]
    lane = lax.broadcasted_iota(jnp.int32, q.shape, 1)
    zero = jnp.zeros_like(q)
    qs = (jnp.where(lane < half, q, zero), jnp.where(lane >= half, q, zero))
    dv = 2 * half

    def scores(j, off, size):
        return lax.dot_general(qs[j], k_ref[0, pl.ds(off, size), :], (((1,), (1,)), ((), ())),
                               preferred_element_type=F32)

    def soft(m, sc):
        m_new = jnp.maximum(m, jnp.max(sc, axis=-1, keepdims=True))
        return m_new, jnp.exp2(sc - m_new).astype(BF16), jnp.exp2(m - m_new)

    def pv(acc, alpha, p, off, size):
        return alpha * acc + _dot(p, v_ref[0, pl.ds(off, size), :])

    def finish(a0, a1):
        lam = (jnp.exp(jnp.sum(lq1_ref[...] * lk1_ref[...], axis=-1, keepdims=True))
               - jnp.exp(jnp.sum(lq2_ref[...] * lk2_ref[...], axis=-1, keepdims=True)) + lam_init)
        o = a0[:, :dv] / a0[:, dv:dv + 1] - lam * (a1[:, :dv] / a1[:, dv:dv + 1])
        on = o * lax.rsqrt(jnp.mean(o * o, axis=-1, keepdims=True) + EPS)
        o_ref[0] = (on * sg_ref[...] * (1.0 - lam_init)).astype(o_ref.dtype)

    def put_scores(slot, c):
        for j in range(2):
            s_ref[slot, j] = scores(j, pl.multiple_of(c * tk, tk), tk)

    def put_probs(slot):
        for j in range(2):
            m_new, p, alpha = soft(m_ref[j], s_ref[slot, j])
            m_ref[j] = m_new
            alpha_ref[slot, j] = alpha
            p_ref[slot, j] = p

    def add_pv(slot, c):
        for j in range(2):
            acc_ref[j] = pv(acc_ref[j], alpha_ref[slot, j], p_ref[slot, j],
                            pl.multiple_of(c * tk, tk), tk)

    def stage(slot, c):
        put_scores(slot, c + 2)
        put_probs(1 - slot)
        add_pv(slot, c)

    n_chunks = s_len // tk

    @pl.when(i < n_lat_tiles)
    def _():
        m_ref[...] = jnp.full(m_ref.shape, -jnp.inf, F32)
        acc_ref[...] = jnp.zeros(acc_ref.shape, F32)
        put_scores(0, 0)
        put_probs(0)
        put_scores(1, 1)
        n_steady = n_chunks - 2

        def pair(u, carry):
            stage(0, 2 * u)
            stage(1, 2 * u + 1)
            return carry

        lax.fori_loop(0, n_steady // 2, pair, 0)
        if n_steady % 2:
            stage(0, n_steady - 1)
        last = (n_chunks - 1) % 2
        put_probs(last)
        add_pv(1 - last, n_chunks - 2)
        add_pv(last, n_chunks - 1)
        finish(acc_ref[0], acc_ref[1])

    @pl.when(i >= n_lat_tiles)
    def _():
        off, size = n_lat_tiles * TT, s_len - n_lat_tiles * TT
        acc = []
        for j in range(2):
            m0 = jnp.full((q.shape[0], 1), -jnp.inf, F32)
            _, p, alpha = soft(m0, scores(j, off, size))
            acc.append(pv(jnp.zeros((q.shape[0], v_ref.shape[-1]), F32), alpha, p, off, size))
        finish(*acc)


def _diff_attention(q, k, v, lq1, lk1, lq2, lk2, subln_g, *, n_lat_tiles, lam_init, head_dim):
    nb, s, qk = q.shape
    dv = 2 * head_dim
    heads = qk // dv
    tk = next(t for t in (3 * TT, 2 * TT, TT) if s % t == 0 and s // t >= 2)
    kern = functools.partial(_attn_kernel, n_lat_tiles=n_lat_tiles, s_len=s, tk=tk,
                             lam_init=lam_init, half=head_dim)
    vec = lambda n: pl.BlockSpec((1, n), lambda b, h, i: (0, 0))
    return pl.pallas_call(
        kern,
        grid=(nb, heads, s // TT),
        in_specs=[
            pl.BlockSpec((1, TT, dv), lambda b, h, i: (b, i, h)),
            pl.BlockSpec((1, s, dv), lambda b, h, i: (b, 0, h)),
            pl.BlockSpec((1, s, 2 * dv), lambda b, h, i: (b, 0, h)),
            vec(head_dim), vec(head_dim), vec(head_dim), vec(head_dim), vec(dv),
        ],
        out_specs=pl.BlockSpec((1, TT, dv), lambda b, h, i: (b, i, h)),
        out_shape=jax.ShapeDtypeStruct((nb, s, qk), BF16),
        scratch_shapes=[pltpu.VMEM((2, 2, TT, tk), F32), pltpu.VMEM((2, 2, TT, tk), BF16),
                        pltpu.VMEM((2, TT, 2 * dv), F32), pltpu.VMEM((2, TT, 1), F32),
                        pltpu.VMEM((2, 2, TT, 1), F32)],
        compiler_params=_cp("parallel", "parallel", "parallel"),
        name="diff_attn",
    )(q, k, v, lq1, lk1, lq2, lk2, subln_g)


def _halo_rows(dtype):
    return SUBLANES * 4 // jnp.dtype(dtype).itemsize


def _halo_fill(xp_ref, prev_ref, x_ref, next_ref, has_prev, has_next):
    hr = prev_ref.shape[2]
    zero = jnp.zeros((hr, xp_ref.shape[-1]), F32)
    xp_ref[hr:hr + TT, :] = x_ref[0].astype(F32)
    xp_ref[0:hr, :] = jnp.where(has_prev, prev_ref[0, 0].astype(F32), zero)
    xp_ref[hr + TT:2 * hr + TT, :] = jnp.where(has_next, next_ref[0, 0].astype(F32), zero)


def _conv_taps(xp_ref, w, bias, left):
    hr = (xp_ref.shape[0] - TT) // 2
    acc = bias
    for j in range(w.shape[0]):
        lo = hr + j - left
        acc = acc + w[j:j + 1, :] * xp_ref[lo:lo + TT, :]
    return acc


def _shortconv_kernel(prev_ref, x_ref, next_ref, w_ref, b_ref, v_ref, x1_ref, x2_ref, xp_ref,
                      *, n_seg_tiles, hy):
    i = pl.program_id(1)
    _halo_fill(xp_ref, prev_ref, x_ref, next_ref, i > 0, i < n_seg_tiles - 1)
    y = _conv_taps(xp_ref, w_ref[...], b_ref[...], HY_SHORT_LEFT)
    v_ref[0] = y[:, :hy].astype(v_ref.dtype)
    x1_ref[0] = y[:, hy:2 * hy].astype(x1_ref.dtype)
    x2_ref[0] = y[:, 2 * hy:].astype(x2_ref.dtype)


def _shortconv(u, w, b, *, tile0, n_seg_tiles):
    nb, s, hy3 = u.shape
    hy = hy3 // 3
    hr = _halo_rows(u.dtype)
    nh, per = s // hr, TT // hr
    uh = u.reshape(nb, nh, hr, hy3)
    prev = pl.BlockSpec((1, 1, hr, hy3),
                        lambda b, i: (b, jnp.maximum((tile0 + i) * per - 1, 0), 0, 0))
    nxt = pl.BlockSpec((1, 1, hr, hy3),
                       lambda b, i: (b, jnp.minimum((tile0 + i + 1) * per, nh - 1), 0, 0))
    out = jax.ShapeDtypeStruct((nb, n_seg_tiles * TT, hy), BF16)
    ospec = pl.BlockSpec((1, TT, hy), lambda b, i: (b, i, 0))
    return pl.pallas_call(
        functools.partial(_shortconv_kernel, n_seg_tiles=n_seg_tiles, hy=hy),
        grid=(nb, n_seg_tiles),
        in_specs=[
            prev,
            pl.BlockSpec((1, TT, hy3), lambda b, i: (b, tile0 + i, 0)),
            nxt,
            pl.BlockSpec(w.shape, lambda b, i: (0, 0)),
            pl.BlockSpec((1, hy3), lambda b, i: (0, 0)),
        ],
        out_specs=[ospec, ospec, ospec],
        out_shape=[out, out, out],
        scratch_shapes=[pltpu.VMEM((TT + 2 * hr, hy3), F32)],
        compiler_params=_cp("parallel", "parallel"),
        name="hy_shortconv",
    )(uh, u, uh, w, b.reshape(1, hy3))


def _filter_feats(lh, n_cols):
    n = np.arange(2 * lh).reshape(-1, n_cols).T.reshape(-1)
    lag = np.where(n < lh, n, 2 * lh - n).astype(np.float64)
    t = (lag / lh).astype(np.float32).astype(np.float64)
    bands = np.arange(1, N_BANDS + 1, dtype=np.float64)
    ang = 2.0 * math.pi * t[:, None] * bands
    feats = np.concatenate([t[:, None], np.cos(ang), np.sin(ang)], axis=-1)
    pad = (-(feats.shape[1] + 1)) % SUBLANES
    return np.concatenate([feats, np.zeros((2 * lh, pad)), n[:, None].astype(np.float64)], axis=-1)


def _filt_kernel(ft_ref, w1_ref, b1_ref, w2_ref, b2_ref, w3_ref, b3_ref, fr_ref, dec_ref, o_ref,
                 *, lh, c):
    ft = ft_ref[...]
    freq = fr_ref[...]
    h = jnp.sin(freq * (_dot(ft, w1_ref[...], precision=HIGHEST) + b1_ref[...]))
    h = jnp.sin(freq * (_dot(h, w2_ref[...], precision=HIGHEST) + b2_ref[...]))
    h = _dot(h.astype(BF16), w3_ref[...]) + b3_ref[...]
    t = ft[:, 0:1]
    h = h * jnp.exp(-t * jnp.abs(dec_ref[...]))
    row = ft[:, ft.shape[1] - 1:]
    for o in range(2):
        fwd = h[:, (2 * o) * c:(2 * o + 1) * c]
        bwd = h[:, (2 * o + 1) * c:(2 * o + 2) * c]
        o_ref[o] = jnp.where(row < lh, fwd, jnp.where(row > lh, bwd, jnp.zeros_like(bwd)))


def _hyena_filter(lh, n_cols, w1, b1, w2, b2, w3, b3, freq, decay):
    c = decay.shape[-1]
    feats = jnp.asarray(_filter_feats(lh, n_cols), F32)
    fe = feats.shape[1]
    w1p = jnp.pad(w1, ((0, fe - w1.shape[0]), (0, 0)))
    hid = w1.shape[1]
    tr = 2 * lh // n_cols
    full = lambda a: pl.BlockSpec(a.shape, lambda i: (0,) * a.ndim)
    args = (w1p, b1.reshape(1, hid), w2, b2.reshape(1, hid), w3.astype(BF16), b3.reshape(1, -1),
            freq.reshape(1, hid), decay.reshape(1, -1))
    return pl.pallas_call(
        functools.partial(_filt_kernel, lh=lh, c=c),
        grid=(n_cols,),
        in_specs=[pl.BlockSpec((tr, fe), lambda i: (i, 0))] + [full(a) for a in args],
        out_specs=pl.BlockSpec((2, tr, c), lambda i: (0, 0, i)),
        out_shape=jax.ShapeDtypeStruct((2, tr, n_cols * c), F32),
        compiler_params=_cp("parallel"),
        name="hy_filter",
    )(feats, *args)


@functools.lru_cache(maxsize=None)
def _dft_tables(n):
    nn = n * n
    h = n // 2
    k = np.arange(n)
    th = 2.0 * math.pi * np.outer(k, k) / n
    c, s = np.cos(th), np.sin(th)
    f1_data = np.block([[c[:, :h], s[:, :h]], [-s[:, :h], c[:, :h]]])
    f1_real = np.concatenate([c, -s], axis=0)
    idx = (k[None, None, :] * (k[:, None, None] + n * k[None, :, None])) % nn
    phi = 2.0 * math.pi * idx / nn
    cp, sp = np.cos(phi), np.sin(phi)
    g = np.concatenate([np.concatenate([cp, sp], axis=2), np.concatenate([-sp, cp], axis=2)], axis=1)
    hmat = np.transpose(g, (0, 2, 1)) / nn
    ci, si = c[:h, :], s[:h, :]
    f3 = np.zeros((n, 2 * n))
    f3[:h, 0::2], f3[:h, 1::2] = ci, -si
    f3[h:, 0::2], f3[h:, 1::2] = si, ci
    return f1_data, f1_real, g, hmat, f3


def _fft_s1_kernel(x_ref, f_ref, o_ref):
    o_ref[0] = _dot(f_ref[...], x_ref[0].astype(BF16)).astype(o_ref.dtype)


def _fft_s1(x, f1, n, c):
    p = x.shape[0]
    tc = FFT_N2_TILE * c
    return pl.pallas_call(
        _fft_s1_kernel,
        grid=(p, n * c // tc),
        in_specs=[pl.BlockSpec((1, n, tc), lambda q, j: (q, 0, j)),
                  pl.BlockSpec(f1.shape, lambda q, j: (0, 0))],
        out_specs=pl.BlockSpec((1, 2 * n, tc), lambda q, j: (q, 0, j)),
        out_shape=jax.ShapeDtypeStruct((p, 2 * n, n * c), BF16),
        compiler_params=_cp("parallel", "parallel"),
        name="fft_s1",
    )(x, f1)


def _fft_spec_kernel(a_ref, g_ref, o_ref, *, tk):
    for j in range(tk):
        x = jnp.concatenate([a_ref[0, 0, j], a_ref[0, 1, j]], axis=0)
        o_ref[0, j] = _dot(g_ref[j], x)


def _fft_mid_kernel(a_ref, g_ref, h_ref, kh_ref, o_ref, *, tk, n):
    for j in range(tk):
        x = jnp.concatenate([a_ref[0, 0, j], a_ref[0, 1, j]], axis=0)
        t = _dot(g_ref[j], x)
        tr, ti = t[:n], t[n:]
        kr, ki = kh_ref[0, j, :n], kh_ref[0, j, n:]
        y = jnp.concatenate([tr * kr - ti * ki, tr * ki + ti * kr], axis=0).astype(BF16)
        o_ref[0, j] = _dot(h_ref[j], y).astype(o_ref.dtype)


def _fft_spectrum(a, g, n, c):
    p = a.shape[0]
    tk = FFT_K1_TILE
    a5 = a.reshape(p, 2, n, n, c)
    return pl.pallas_call(
        functools.partial(_fft_spec_kernel, tk=tk),
        grid=(p, n // tk),
        in_specs=[pl.BlockSpec((1, 2, tk, n, c), lambda q, j: (q, 0, j, 0, 0)),
                  pl.BlockSpec((tk, 2 * n, 2 * n), lambda q, j: (j, 0, 0))],
        out_specs=pl.BlockSpec((1, tk, 2 * n, c), lambda q, j: (q, j, 0, 0)),
        out_shape=jax.ShapeDtypeStruct((p, n, 2 * n, c), F32),
        compiler_params=_cp("parallel", "parallel"),
        name="fft_spectrum",
    )(a5, g)


def _fft_mid(a, g, hm, khat, order, n, c):
    p = a.shape[0]
    tk = FFT_K1_TILE
    a5 = a.reshape(p, 2, n, n, c)
    return pl.pallas_call(
        functools.partial(_fft_mid_kernel, tk=tk, n=n),
        grid=(p, n // tk),
        in_specs=[pl.BlockSpec((1, 2, tk, n, c), lambda q, j: (q, 0, j, 0, 0)),
                  pl.BlockSpec((tk, 2 * n, 2 * n), lambda q, j: (j, 0, 0)),
                  pl.BlockSpec((tk, 2 * n, 2 * n), lambda q, j: (j, 0, 0)),
                  pl.BlockSpec((1, tk, 2 * n, c), lambda q, j: (order, j, 0, 0))],
        out_specs=pl.BlockSpec((1, tk, 2 * n, c), lambda q, j: (q, j, 0, 0)),
        out_shape=jax.ShapeDtypeStruct((p, n, 2 * n, c), BF16),
        compiler_params=_cp("parallel", "parallel"),
        name="fft_mid",
    )(a5, g, hm, khat)


def _fft_s3_kernel(c_ref, f_ref, v_ref, x_ref, b_ref, o_ref):
    y = _dot(f_ref[...], c_ref[0])
    v = v_ref[0].astype(F32)
    o_ref[0] = (x_ref[0].astype(F32) * (y + v * b_ref[...])).astype(o_ref.dtype)


def _fft_s3(cm, f3, vin, xg, bias, n, c, out_dtype):
    p = cm.shape[0]
    tc = FFT_N2_TILE * c
    c2 = cm.reshape(p, 2 * n, n * c)
    bias_t = jnp.tile(bias.reshape(1, c), (1, FFT_N2_TILE))
    blk = pl.BlockSpec((1, n, tc), lambda q, j: (q, 0, j))
    return pl.pallas_call(
        _fft_s3_kernel,
        grid=(p, n * c // tc),
        in_specs=[pl.BlockSpec((1, 2 * n, tc), lambda q, j: (q, 0, j)),
                  pl.BlockSpec(f3.shape, lambda q, j: (0, 0)),
                  blk, blk,
                  pl.BlockSpec((1, tc), lambda q, j: (0, 0))],
        out_specs=blk,
        out_shape=jax.ShapeDtypeStruct((p, n, n * c), out_dtype),
        compiler_params=_cp("parallel", "parallel"),
        name="fft_s3",
    )(c2, f3, vin, xg, bias_t)


def _hyena_long(v, x1, x2, kk, hy_bias):
    nb, l, c = v.shape
    n = math.isqrt(2 * l)
    assert n * n == 2 * l and nb % 2 == 0
    p = nb // 2
    f1d, f1r, g, hm, f3 = (jnp.asarray(t, F32).astype(BF16) for t in _dft_tables(n))
    pair = lambda a: a.reshape(p, n, n * c)
    khat = _fft_spectrum(_fft_s1(kk, f1r, n, c), g, n, c)
    z = pair(v)
    for order, xg in enumerate((x1, x2)):
        a = _fft_s1(z, f1d, n, c)
        cm = _fft_mid(a, g, hm, khat, order, n, c)
        z = _fft_s3(cm, f3, z, pair(xg), hy_bias[order], n, c, BF16)
    return z.reshape(nb, l, c)


@functools.lru_cache(maxsize=None)
def _ctx_dft_tables(lc):
    m = 2 * lc
    k = np.arange(m)
    th = 2.0 * math.pi * np.outer(k, k) / m
    c, s = np.cos(th), np.sin(th)
    f_data = np.block([[c[:, :lc], s[:, :lc]], [-s[:, :lc], c[:, :lc]]])
    f_real = np.concatenate([c, -s], axis=0)
    ci, si = c[:lc, :], s[:lc, :]
    f_inv = np.block([[ci, -si], [si, ci]]) / m
    return f_data, f_real, f_inv


def _ctxconv_kernel(v_ref, x1_ref, x2_ref, kk_ref, fd_ref, fr_ref, fi_ref, b_ref, o_ref, *, m):
    def conv(u, order):
        kh = _dot(fr_ref[...], kk_ref[order].astype(BF16))
        t = _dot(fd_ref[...], u.astype(BF16))
        tr, ti, kr, ki = t[:m], t[m:], kh[:m], kh[m:]
        y = jnp.concatenate([tr * kr - ti * ki, tr * ki + ti * kr], axis=0).astype(BF16)
        return _dot(fi_ref[...], y)

    v = v_ref[0].astype(F32)
    z1 = x1_ref[0].astype(F32) * (conv(v, 0) + v * b_ref[0:1, :])
    o_ref[0] = (x2_ref[0].astype(F32) * (conv(z1, 1) + z1 * b_ref[1:2, :])).astype(o_ref.dtype)


def _hyena_ctx(v, x1, x2, kk, hy_bias):
    nb, lc, c = v.shape
    p, m = nb // 2, 2 * lc
    fd, fr, fi = (jnp.asarray(t, F32).astype(BF16) for t in _ctx_dft_tables(lc))
    pair = lambda a: a.reshape(p, m, c)
    blk = pl.BlockSpec((1, m, c), lambda q: (q, 0, 0))
    full = lambda a: pl.BlockSpec(a.shape, lambda q: (0,) * a.ndim)
    z = pl.pallas_call(
        functools.partial(_ctxconv_kernel, m=m),
        grid=(p,),
        in_specs=[blk, blk, blk, full(kk), full(fd), full(fr), full(fi), full(hy_bias)],
        out_specs=blk,
        out_shape=jax.ShapeDtypeStruct((p, m, c), BF16),
        compiler_params=_cp("parallel"),
        name="hy_ctx",
    )(pair(v), pair(x1), pair(x2), kk, fd, fr, fi, hy_bias)
    return z.reshape(nb, lc, c)


def _od_inproj_kernel(h_ref, mod_ref, g_ref, w_ref, gate_ref, x_ref, *, d, r):
    m = mod_ref[0]
    xn = _rms_mod(h_ref[0], g_ref[...], m[:, 0:d], m[:, d:2 * d]).astype(BF16)
    y = _dot(xn, w_ref[...])
    gate_ref[0] = y[:, :r].astype(gate_ref.dtype)
    x_ref[0] = y[:, r:]


def _od_inproj(h, mods, rows, layer, g, w_in, *, n_lat_tiles):
    nb, s, d = h.shape
    r = w_in.shape[1] // 2
    ospec = pl.BlockSpec((1, TT, r), lambda b, i: (b, i, 0))
    return pl.pallas_call(
        functools.partial(_od_inproj_kernel, d=d, r=r),
        grid=(nb, s // TT),
        in_specs=[
            pl.BlockSpec((1, TT, d), lambda b, i: (b, i, 0)),
            _mod_spec(layer, rows, nb, n_lat_tiles, mods.shape[-1]),
            pl.BlockSpec((1, d), lambda b, i: (0, 0)),
            pl.BlockSpec(w_in.shape, lambda b, i: (0, 0)),
        ],
        out_specs=[ospec, ospec],
        out_shape=[jax.ShapeDtypeStruct((nb, s, r), BF16), jax.ShapeDtypeStruct((nb, s, r), F32)],
        compiler_params=_cp("parallel", "parallel"),
        name="od_inproj",
    )(h, mods, g, w_in)


def _rglru_kernel(prev_ref, x_ref, next_ref, cw_ref, cb_ref, w_ref, b_ref, lam_ref, o_ref,
                  xp_ref, a_ref, bb_ref, carry_ref, *, n_lat_tiles, n_tiles, r, windows):
    dr = pl.program_id(1)
    i = pl.program_id(2)
    tile = _scan_tile(dr, i, n_lat_tiles, n_tiles)
    seg_first = jnp.logical_or(tile == 0, tile == n_lat_tiles)
    seg_last = jnp.logical_or(tile == n_lat_tiles - 1, tile == n_tiles - 1)
    _halo_fill(xp_ref, prev_ref, x_ref, next_ref, jnp.logical_not(seg_first),
               jnp.logical_not(seg_last))
    xc = _conv_taps(xp_ref, cw_ref[...], cb_ref[...], RG_CONV_LEFT)
    xb = xc.astype(BF16)
    nl = -lam_ref[0]
    softplus = jnp.maximum(nl, 0.0) + jnp.log1p(jnp.exp(-jnp.abs(nl)))
    for c0, c1, k0, k1 in windows:
        xk = xb[:, k0:k1]
        rg = _sigmoid_tanh(_dot(xk, w_ref[0, k0:k1, c0:c1]) + b_ref[0, :, c0:c1])
        ig = _sigmoid_tanh(_dot(xk, w_ref[0, k0:k1, r + c0:r + c1]) + b_ref[0, :, r + c0:r + c1])
        log_a = -RG_C * rg * softplus[:, c0:c1]
        a = jnp.exp(log_a)
        a_ref[:, c0:c1] = a
        bb_ref[:, c0:c1] = jnp.sqrt(-jnp.tanh(log_a) * (a * a + 1.0)) * (ig * xc[:, c0:c1])

    @pl.when(i == 0)
    def _():
        carry_ref[...] = jnp.zeros_like(carry_ref)

    def step(t, h):
        idx = jnp.where(dr == 0, t, TT - 1 - t)
        h = a_ref[pl.ds(idx, 1), :] * h + bb_ref[pl.ds(idx, 1), :]
        o_ref[0, 0, pl.ds(idx, 1), :] = h
        return h

    carry_ref[0:1, :] = lax.fori_loop(0, TT, step, carry_ref[0:1, :], unroll=8)


def _scan_tile(dr, i, n_lat_tiles, n_tiles):
    fwd = jnp.where(i == 0, n_lat_tiles, i - 1)
    bwd = jnp.where(i == 0, n_lat_tiles, n_lat_tiles - i)
    return jnp.where(dr == 0, fwd, bwd)


def _gate_windows(r, bs):
    out = []
    for c0 in range(0, r, 2 * LANES):
        c1 = min(c0 + 2 * LANES, r)
        k0 = (c0 // bs) * bs // LANES * LANES
        k1 = min(-(-(((c1 - 1) // bs + 1) * bs) // LANES) * LANES, r)
        out.append((c0, c1, k0, k1))
    return tuple(out)


def _rglru(x, conv_w, conv_b, wcat, bcat, lam, *, n_lat_tiles, block_size):
    nb, s, r = x.shape
    n_tiles = s // TT
    assert n_tiles == n_lat_tiles + 1
    n8 = s // SUBLANES
    per = TT // SUBLANES
    x8 = x.reshape(nb, n8, SUBLANES, r)
    tile = lambda d, i: _scan_tile(d, i, n_lat_tiles, n_tiles)
    return pl.pallas_call(
        functools.partial(_rglru_kernel, n_lat_tiles=n_lat_tiles, n_tiles=n_tiles, r=r,
                          windows=_gate_windows(r, block_size)),
        grid=(nb, 2, n_tiles),
        in_specs=[
            pl.BlockSpec((1, 1, SUBLANES, r),
                         lambda b, d, i: (b, jnp.maximum(tile(d, i) * per - 1, 0), 0, 0)),
            pl.BlockSpec((1, TT, r), lambda b, d, i: (b, tile(d, i), 0)),
            pl.BlockSpec((1, 1, SUBLANES, r),
                         lambda b, d, i: (b, jnp.minimum((tile(d, i) + 1) * per, n8 - 1), 0, 0)),
            pl.BlockSpec(conv_w.shape, lambda b, d, i: (0, 0)),
            pl.BlockSpec((1, r), lambda b, d, i: (0, 0)),
            pl.BlockSpec((1, r, 2 * r), lambda b, d, i: (d, 0, 0)),
            pl.BlockSpec((1, 1, 2 * r), lambda b, d, i: (d, 0, 0)),
            pl.BlockSpec((1, 1, r), lambda b, d, i: (d, 0, 0)),
        ],
        out_specs=pl.BlockSpec((1, 1, TT, r), lambda b, d, i: (d, b, tile(d, i), 0)),
        out_shape=jax.ShapeDtypeStruct((2, nb, s, r), F32),
        scratch_shapes=[pltpu.VMEM((TT + 2 * SUBLANES, r), F32), pltpu.VMEM((TT, r), F32),
                        pltpu.VMEM((TT, r), F32), pltpu.VMEM((SUBLANES, r), F32)],
        compiler_params=_cp("parallel", "parallel", "arbitrary"),
        name="rglru",
    )(x8, x, x8, conv_w, conv_b.reshape(1, r), wcat, bcat, lam.reshape(2, 1, r))


def _block_diag(w):
    n, bs, _ = w.shape
    eye = jnp.eye(n, dtype=w.dtype)
    return (eye[:, None, :, None] * w[:, :, None, :]).reshape(n * bs, n * bs)


def _gelu_tanh(x):
    return 0.5 * x * (1.0 + jnp.tanh(math.sqrt(2.0 / math.pi) * (x + 0.044715 * (x * x * x))))


def _post_kernel(*refs, kind, final, d, n_lat_tiles):
    if kind == "even":
        h_ref, zl_ref, zc_ref, o_ref, mod_ref, g2_ref, wo_ref, w1_ref, w2_ref = refs[:9]
        rest = refs[9:]
        is_ctx = pl.program_id(1) >= n_lat_tiles
        z = jnp.where(is_ctx, zc_ref[0], zl_ref[0])
        half = z.shape[-1]
        y = _dot(z, wo_ref[:half, :]) + _dot(o_ref[0], wo_ref[half:, :])
    else:
        h_ref, hd_ref, gate_ref, mod_ref, g2_ref, wo_ref, w1_ref, w2_ref = refs[:8]
        rest = refs[8:]
        mix = (hd_ref[0, 0] + hd_ref[1, 0]) * _gelu_tanh(gate_ref[0].astype(F32))
        y = _dot(mix.astype(BF16), wo_ref[...])
    out_ref = rest[-1]
    m = mod_ref[0]
    h1 = h_ref[0] + m[:, 2 * d:3 * d] * y
    xn = _rms_mod(h1, g2_ref[...], m[:, 3 * d:4 * d], m[:, 4 * d:5 * d]).astype(BF16)
    acc = jnp.zeros_like(h1)
    dff = w1_ref.shape[1]
    for c0 in range(0, dff, FF_CHUNK):
        a = jnp.maximum(_dot(xn, w1_ref[:, c0:c0 + FF_CHUNK]), 0.0)
        acc = acc + _dot((a * a).astype(BF16), w2_ref[c0:c0 + FF_CHUNK, :])
    h2 = h1 + m[:, 5 * d:6 * d] * acc
    if final:
        fg_ref = rest[0]
        h2 = h2 * lax.rsqrt(jnp.mean(h2 * h2, axis=-1, keepdims=True) + EPS) * fg_ref[...]
    out_ref[0] = h2


def _post(kind, h, mix_args, mods, rows, layer, g2, w_out, w1, w2, final_g, *, n_lat_tiles, final):
    nb, s, d = h.shape
    n_tiles = n_lat_tiles if final else s // TT
    tok = lambda w: pl.BlockSpec((1, TT, w), lambda b, i: (b, i, 0))
    full = lambda a: pl.BlockSpec(a.shape, lambda b, i: (0,) * a.ndim)
    if kind == "even":
        z_lat, z_ctx, o = mix_args
        hy = z_lat.shape[-1]
        mix_specs = [
            pl.BlockSpec((1, TT, hy), lambda b, i: (b, jnp.minimum(i, n_lat_tiles - 1), 0)),
            pl.BlockSpec((1, TT, hy), lambda b, i: (b, 0, 0)),
            tok(o.shape[-1]),
        ]
    else:
        hd, gate = mix_args
        r = gate.shape[-1]
        mix_specs = [pl.BlockSpec((2, 1, TT, r), lambda b, i: (0, b, i, 0)), tok(r)]
    in_specs = [tok(d)] + mix_specs + [
        _mod_spec(layer, rows, nb, n_lat_tiles, mods.shape[-1]),
        pl.BlockSpec((1, d), lambda b, i: (0, 0)), full(w_out), full(w1), full(w2)]
    args = [h, *mix_args, mods, g2, w_out, w1, w2]
    if final:
        in_specs.append(pl.BlockSpec((1, d), lambda b, i: (0, 0)))
        args.append(final_g)
    return pl.pallas_call(
        functools.partial(_post_kernel, kind=kind, final=final, d=d, n_lat_tiles=n_lat_tiles),
        grid=(nb, n_tiles),
        in_specs=in_specs,
        out_specs=tok(d),
        out_shape=jax.ShapeDtypeStruct((nb, n_tiles * TT, d), F32),
        compiler_params=_cp("parallel", "parallel"),
        name="post_" + kind,
    )(*args)


@functools.lru_cache(maxsize=None)
def _rope_tables(l, lc, head_dim):
    axis = head_dim // 2
    freqs = ROPE_BASE ** (-np.arange(0, axis, 2, dtype=np.float64) / axis)
    freqs = freqs.astype(np.float32).astype(np.float64)
    t = np.arange(l)
    ang_r = (t // GRID_W)[:, None] * freqs
    ang_c = (t % GRID_W)[:, None] * freqs
    cos = np.concatenate([np.cos(ang_r)] * 2 + [np.cos(ang_c)] * 2, axis=-1)
    sin = np.concatenate([-np.sin(ang_r), np.sin(ang_r), -np.sin(ang_c), np.sin(ang_c)], axis=-1)
    cos = np.concatenate([cos, np.ones((lc, head_dim))], axis=0)
    sin = np.concatenate([sin, np.zeros((lc, head_dim))], axis=0)
    rep = LANES // head_dim
    return np.tile(cos, (1, rep)), np.tile(sin, (1, rep))


def kernel(x, c, ctx, c_ctx, ada_w, ada_b, norm1_g, norm2_g, mlp_w1, mlp_w2, final_g, ev_w_in, ev_w_out, hy_short_w, hy_short_b, hy_f_w1, hy_f_b1, hy_f_w2, hy_f_b2, hy_f_w3, hy_f_b3, hy_f_freq, hy_f_decay, hy_bias, df_lq1, df_lk1, df_lq2, df_lk2, df_subln_g, od_w_in, od_w_out, rg_conv_w, rg_conv_b, rg_wa, rg_ba, rg_wx, rg_bx, rg_lam):
    nb, l, d = x.shape
    lc = ctx.shape[1]
    depth = ada_w.shape[0]
    hy = hy_bias.shape[-1]
    head_dim = df_lq1.shape[-1]
    qk = (ev_w_in.shape[-1] - 3 * hy) // 3
    r = rg_lam.shape[-1]
    assert l % TT == 0 and lc == TT and l % GRID_W == 0
    n_lat_tiles = l // TT

    rows = -(-(nb + 1) // SUBLANES) * SUBLANES
    cond = jnp.zeros((rows, d), F32).at[:nb].set(c).at[nb].set(c_ctx)
    mods = _ada_mods(cond, ada_w, ada_b).reshape(depth * rows, 1, ada_w.shape[-1])

    cos_t, sin_t = (jnp.asarray(t, F32) for t in _rope_tables(l, lc, head_dim))
    h = jnp.concatenate([x, ctx], axis=1)
    w1_bf, w2_bf = mlp_w1.astype(BF16), mlp_w2.astype(BF16)

    for i in range(depth):
        j = i // 2
        final = i == depth - 1
        g1, g2 = norm1_g[i].reshape(1, d), norm2_g[i].reshape(1, d)
        if i % 2 == 0:
            lam_init = 0.8 - 0.6 * math.exp(-0.3 * i)
            u, q, k, v = _ev_inproj(h, mods, rows, i, g1, ev_w_in[j].astype(BF16), cos_t, sin_t,
                                    n_lat_tiles=n_lat_tiles, hy3=3 * hy, qk=qk, head_dim=head_dim)
            o = _diff_attention(q, k, v, df_lq1[j].reshape(1, -1), df_lk1[j].reshape(1, -1),
                                df_lq2[j].reshape(1, -1), df_lk2[j].reshape(1, -1),
                                df_subln_g[j].reshape(1, -1), n_lat_tiles=n_lat_tiles,
                                lam_init=lam_init, head_dim=head_dim)
            fparams = (hy_f_w1[j], hy_f_b1[j], hy_f_w2[j], hy_f_b2[j], hy_f_w3[j], hy_f_b3[j],
                       hy_f_freq[j], hy_f_decay[j])
            vl, x1l, x2l = _shortconv(u, hy_short_w[j], hy_short_b[j], tile0=0,
                                      n_seg_tiles=n_lat_tiles)
            vc, x1c, x2c = _shortconv(u, hy_short_w[j], hy_short_b[j], tile0=n_lat_tiles,
                                      n_seg_tiles=lc // TT)
            z_lat = _hyena_long(vl, x1l, x2l, _hyena_filter(l, math.isqrt(2 * l), *fparams),
                                hy_bias[j])
            z_ctx = _hyena_ctx(vc, x1c, x2c, _hyena_filter(lc, 1, *fparams), hy_bias[j])
            h = _post("even", h, (z_lat, z_ctx, o), mods, rows, i, g2, ev_w_out[j].astype(BF16),
                      w1_bf[i], w2_bf[i], final_g.reshape(1, d), n_lat_tiles=n_lat_tiles,
                      final=final)
        else:
            gate, xr = _od_inproj(h, mods, rows, i, g1, od_w_in[j].astype(BF16), n_lat_tiles=n_lat_tiles)
            wcat = jnp.stack([jnp.concatenate([_block_diag(rg_wa[j, dd]), _block_diag(rg_wx[j, dd])],
                                              axis=1) for dd in range(2)]).astype(BF16)
            bcat = jnp.concatenate([rg_ba[j], rg_bx[j]], axis=-1).reshape(2, 1, 2 * r)
            hd = _rglru(xr, rg_conv_w[j], rg_conv_b[j], wcat, bcat, rg_lam[j],
                        n_lat_tiles=n_lat_tiles, block_size=rg_wa.shape[-1])
            h = _post("odd", h, (hd, gate), mods, rows, i, g2, od_w_out[j].astype(BF16),
                      w1_bf[i], w2_bf[i], final_g.reshape(1, d), n_lat_tiles=n_lat_tiles,
                      final=final)
    return h[:, :l] if h.shape[1] != l else h
```

```python
import functools
import math

import numpy as np
import jax
import jax.numpy as jnp
from jax import lax
from jax.experimental import pallas as pl
from jax.experimental.pallas import tpu as pltpu

F32 = jnp.float32
BF16 = jnp.bfloat16
HIGHEST = lax.Precision.HIGHEST

EPS = 1e-6
GRID_W = 64
ROPE_BASE = 10000.0
N_BANDS = 16
RG_C = 8.0
RG_CONV_LEFT = 2
HY_SHORT_LEFT = 1

TT = 256
LANES = 128
SUBLANES = 8
VMEM_LIMIT = 56 * 1024 * 1024
FF_CHUNK = 1024
FFT_N2_TILE = 8
FFT_K1_TILE = 8
VT_ROWS = LANES + 16


def _cp(*sem):
    return pltpu.CompilerParams(dimension_semantics=sem, vmem_limit_bytes=VMEM_LIMIT)


def _dot(a, b, **kw):
    return jnp.dot(a, b, preferred_element_type=F32, **kw)


def _sigmoid(x):
    return 1.0 / (1.0 + jnp.exp(-x))


def _rms_mod(x, g, shift, scale):
    y = x * lax.rsqrt(jnp.mean(x * x, axis=-1, keepdims=True) + EPS)
    return (y * g) * (1.0 + scale) + shift


def _ada_kernel(c_ref, w_ref, b_ref, o_ref):
    c = c_ref[...]
    s = c * _sigmoid(c)
    o_ref[0] = _dot(s, w_ref[0], precision=HIGHEST) + b_ref[0]


def _ada_mods(cond, ada_w, ada_b):
    depth, d, n = ada_w.shape
    rows = cond.shape[0]
    tn = min(n, 1536)
    return pl.pallas_call(
        _ada_kernel,
        grid=(depth, n // tn),
        in_specs=[
            pl.BlockSpec((rows, d), lambda l, j: (0, 0)),
            pl.BlockSpec((1, d, tn), lambda l, j: (l, 0, j)),
            pl.BlockSpec((1, 1, tn), lambda l, j: (l, 0, j)),
        ],
        out_specs=pl.BlockSpec((1, rows, tn), lambda l, j: (l, 0, j)),
        out_shape=jax.ShapeDtypeStruct((depth, rows, n), F32),
        compiler_params=_cp("parallel", "parallel"),
        name="ada_mods",
    )(cond, ada_w, ada_b.reshape(depth, 1, n))


def _mod_spec(layer, rows, nb, n_lat_tiles, n6):
    def imap(b, i):
        return (layer * rows + jnp.where(i >= n_lat_tiles, nb, b), 0, 0)
    return pl.BlockSpec((1, 1, n6), imap)


def _ev_inproj_kernel(h_ref, mod_ref, g_ref, w_ref, cos_ref, sin_ref,
                      u_ref, qt_ref, k_ref, vt_ref, *, d, hy3, qk, qscale):
    m = mod_ref[0]
    xn = _rms_mod(h_ref[0], g_ref[...], m[:, 0:d], m[:, d:2 * d]).astype(BF16)
    y = _dot(xn, w_ref[...])
    u_ref[0] = y[:, :hy3].astype(u_ref.dtype)
    cos = cos_ref[...]
    sin = sin_ref[...]
    lane = lax.broadcasted_iota(jnp.int32, cos.shape, 1)
    first = (lane % 32) < 16

    def rope(z):
        sw = jnp.where(first, pltpu.roll(z, LANES - 16, 1), pltpu.roll(z, 16, 1))
        return z * cos + sw * sin

    extra = vt_ref.shape[3] - LANES
    ones_row = jnp.where(lax.broadcasted_iota(jnp.int32, (extra, cos.shape[0]), 0) == 0,
                         1.0, 0.0).astype(BF16)
    for c in range(qk // LANES):
        lo = c * LANES
        qt_ref[0, c] = (rope(y[:, hy3 + lo:hy3 + lo + LANES]) * qscale).T.astype(BF16)
        k_ref[0, :, lo:lo + LANES] = rope(y[:, hy3 + qk + lo:hy3 + qk + lo + LANES]).astype(BF16)
        vt_ref[0, c, 0, 0:LANES, :] = y[:, hy3 + 2 * qk + lo:hy3 + 2 * qk + lo + LANES].T.astype(BF16)
        vt_ref[0, c, 0, LANES:, :] = ones_row


def _ev_inproj(h, mods, rows, layer, g, w_in, cos_t, sin_t, *, n_lat_tiles, hy3, qk, head_dim, tk):
    nb, s, d = h.shape
    n_in = w_in.shape[1]
    assert 2 * head_dim == LANES
    heads, per = qk // LANES, tk // TT
    kern = functools.partial(_ev_inproj_kernel, d=d, hy3=hy3, qk=qk,
                             qscale=head_dim ** -0.5 * math.log2(math.e))
    return pl.pallas_call(
        kern,
        grid=(nb, s // TT),
        in_specs=[
            pl.BlockSpec((1, TT, d), lambda b, i: (b, i, 0)),
            _mod_spec(layer, rows, nb, n_lat_tiles, mods.shape[-1]),
            pl.BlockSpec((1, d), lambda b, i: (0, 0)),
            pl.BlockSpec((d, n_in), lambda b, i: (0, 0)),
            pl.BlockSpec((TT, LANES), lambda b, i: (i, 0)),
            pl.BlockSpec((TT, LANES), lambda b, i: (i, 0)),
        ],
        out_specs=[
            pl.BlockSpec((1, TT, hy3), lambda b, i: (b, i, 0)),
            pl.BlockSpec((1, heads, LANES, TT), lambda b, i: (b, 0, 0, i)),
            pl.BlockSpec((1, TT, qk), lambda b, i: (b, i, 0)),
            pl.BlockSpec((1, heads, 1, VT_ROWS, TT), lambda b, i: (b, 0, i // per, 0, i % per)),
        ],
        out_shape=[
            jax.ShapeDtypeStruct((nb, s, hy3), BF16),
            jax.ShapeDtypeStruct((nb, heads, LANES, s), BF16),
            jax.ShapeDtypeStruct((nb, s, qk), BF16),
            jax.ShapeDtypeStruct((nb, heads, s // tk, VT_ROWS, tk), BF16),
        ],
        compiler_params=_cp("parallel", "parallel"),
        name="ev_inproj",
    )(h, mods, g, w_in, cos_t, sin_t)


def _split_maps(qt, half):
    row = lax.broadcasted_iota(jnp.int32, qt.shape, 0)
    zero = jnp.zeros_like(qt)
    return jnp.where(row < half, qt, zero), jnp.where(row >= half, qt, zero)


def _attn_out(a0, a1, lam_refs, sg_ref, lam_init, dv):
    lq1_ref, lk1_ref, lq2_ref, lk2_ref = lam_refs
    lam = (jnp.exp(jnp.sum(lq1_ref[...] * lk1_ref[...], axis=-1, keepdims=True))
           - jnp.exp(jnp.sum(lq2_ref[...] * lk2_ref[...], axis=-1, keepdims=True)) + lam_init)
    o = (a0[:dv] / a0[dv:dv + 1] - lam * (a1[:dv] / a1[dv:dv + 1])).T
    on = o * lax.rsqrt(jnp.mean(o * o, axis=-1, keepdims=True) + EPS)
    return on * sg_ref[...] * (1.0 - lam_init)


def _col_max8(sc):
    part = sc[0:SUBLANES]
    for g in range(1, sc.shape[0] // SUBLANES):
        part = jnp.maximum(part, sc[g * SUBLANES:(g + 1) * SUBLANES])
    return part


def _attn_kernel(qt_ref, qnt_ref, k_ref, vt_ref, lq1_ref, lk1_ref, lq2_ref, lk2_ref, sg_ref, o_ref,
                 s_ref, mp_ref, p_ref, acc_ref, m_ref, alpha_ref, *, n_chunks, tk, lam_init, half):
    i = pl.program_id(2)
    dv = 2 * half
    lam_refs = (lq1_ref, lk1_ref, lq2_ref, lk2_ref)
    q_sub = (_split_maps(qt_ref[0, 0, :, 0:TT], half), _split_maps(qt_ref[0, 0, :, TT:2 * TT], half))
    q_next = _split_maps(qnt_ref[0, 0], half)

    def put_scores(slot, qts, c):
        kc = k_ref[0, pl.ds(pl.multiple_of(c * tk, tk), tk), :]
        for j in range(2):
            sc = _dot(kc, qts[j])
            s_ref[slot, j] = sc
            mp_ref[slot, j] = _col_max8(sc)

    def put_probs(slot, sub, first):
        for j in range(2):
            col_max = jnp.max(mp_ref[slot, j], axis=0, keepdims=True)
            if first:
                m_new = col_max
            else:
                m_old = m_ref[sub, j]
                m_new = jnp.maximum(m_old, col_max)
                alpha_ref[slot, j] = jnp.exp2(m_old - m_new)
            m_ref[sub, j] = m_new
            p_ref[slot, j] = jnp.exp2(s_ref[slot, j] - m_new).astype(BF16)

    def add_pv(slot, sub, c, first):
        for j in range(2):
            pv = _dot(vt_ref[0, 0, c], p_ref[slot, j])
            acc_ref[sub, j] = pv if first else alpha_ref[slot, j] * acc_ref[sub, j] + pv

    def sub_tile(sub, parity, qts, qts_after, sub_after):
        slot = lambda c: (c + parity) % 2
        put_scores(slot(0), qts, 2)
        put_probs(slot(1), sub, False)
        add_pv(slot(0), sub, 0, True)
        n_uniform = n_chunks - 3

        def pair(u, carry):
            c = 1 + 2 * u
            put_scores(slot(1), qts, c + 2)
            put_probs(slot(0), sub, False)
            add_pv(slot(1), sub, c, False)
            put_scores(slot(0), qts, c + 3)
            put_probs(slot(1), sub, False)
            add_pv(slot(0), sub, c + 1, False)
            return carry

        lax.fori_loop(0, n_uniform // 2, pair, 0)
        c = n_chunks - 2
        put_scores(slot(c), qts_after, 0)
        put_probs(slot(c + 1), sub, False)
        add_pv(slot(c), sub, c, False)
        c = n_chunks - 1
        put_scores(slot(c), qts_after, 1)
        put_probs(slot(c + 1), sub_after, True)
        add_pv(slot(c), sub, c, False)
        o_ref[0, sub * TT:(sub + 1) * TT, :] = _attn_out(
            acc_ref[sub, 0], acc_ref[sub, 1], lam_refs, sg_ref, lam_init, dv).astype(o_ref.dtype)

    @pl.when(i == 0)
    def _():
        put_scores(0, q_sub[0], 0)
        put_probs(0, 0, True)
        put_scores(1, q_sub[0], 1)

    sub_tile(0, 0, q_sub[0], q_sub[1], 1)
    sub_tile(1, 1, q_sub[1], q_next, 0)


def _ctx_attn_kernel(qt_ref, k_ref, vt_ref, lq1_ref, lk1_ref, lq2_ref, lk2_ref, sg_ref, o_ref,
                     *, lam_init, half):
    acc = []
    for qtj in _split_maps(qt_ref[0, 0], half):
        sc = _dot(k_ref[0], qtj)
        col_max = jnp.max(_col_max8(sc), axis=0, keepdims=True)
        acc.append(_dot(vt_ref[0, 0, 0], jnp.exp2(sc - col_max).astype(BF16)))
    o_ref[0] = _attn_out(acc[0], acc[1], (lq1_ref, lk1_ref, lq2_ref, lk2_ref), sg_ref, lam_init,
                         2 * half).astype(o_ref.dtype)


def _attn_chunk(s):
    return next(t for t in (3 * TT, TT) if s % t == 0 and (s // t) % 2 == 1 and s // t >= 5)


def _diff_attention(qt, k, vt, lam_vecs, subln_g, *, n_lat_tiles, lam_init, head_dim):
    nb, s, qk = k.shape
    dv = 2 * head_dim
    heads = qk // dv
    l = n_lat_tiles * TT
    n_chunks, tk = vt.shape[2], vt.shape[4]
    assert n_lat_tiles % 2 == 0 and n_chunks % 2 == 1 and n_chunks >= 5 and s - l == TT
    n_steps = n_lat_tiles // 2
    vec = lambda n: pl.BlockSpec((1, n), lambda b, h, i: (0, 0))
    vecs = [vec(head_dim)] * 4 + [vec(dv)]
    o_lat = pl.pallas_call(
        functools.partial(_attn_kernel, n_chunks=n_chunks, tk=tk, lam_init=lam_init, half=head_dim),
        grid=(nb, heads, n_steps),
        in_specs=[
            pl.BlockSpec((1, 1, dv, 2 * TT), lambda b, h, i: (b, h, 0, i)),
            pl.BlockSpec((1, 1, dv, TT),
                         lambda b, h, i: (b, h, 0, jnp.minimum(2 * i + 2, n_lat_tiles - 1))),
            pl.BlockSpec((1, s, dv), lambda b, h, i: (b, 0, h)),
            pl.BlockSpec((1, 1, n_chunks, VT_ROWS, tk), lambda b, h, i: (b, h, 0, 0, 0)),
        ] + vecs,
        out_specs=pl.BlockSpec((1, 2 * TT, dv), lambda b, h, i: (b, i, h)),
        out_shape=jax.ShapeDtypeStruct((nb, l, qk), BF16),
        scratch_shapes=[pltpu.VMEM((2, 2, tk, TT), F32), pltpu.VMEM((2, 2, SUBLANES, TT), F32),
                        pltpu.VMEM((2, 2, tk, TT), BF16), pltpu.VMEM((2, 2, VT_ROWS, TT), F32),
                        pltpu.VMEM((2, 2, 1, TT), F32), pltpu.VMEM((2, 2, 1, TT), F32)],
        compiler_params=_cp("parallel", "parallel", "arbitrary"),
        name="diff_attn",
    )(qt, qt, k, vt, *lam_vecs, subln_g)
    per = tk // TT
    vec2 = lambda n: pl.BlockSpec((1, n), lambda b, h: (0, 0))
    o_ctx = pl.pallas_call(
        functools.partial(_ctx_attn_kernel, lam_init=lam_init, half=head_dim),
        grid=(nb, heads),
        in_specs=[pl.BlockSpec((1, 1, dv, TT), lambda b, h: (b, h, 0, n_lat_tiles)),
                  pl.BlockSpec((1, TT, dv), lambda b, h: (b, n_lat_tiles, h)),
                  pl.BlockSpec((1, 1, 1, VT_ROWS, TT),
                               lambda b, h: (b, h, n_lat_tiles // per, 0, n_lat_tiles % per)),
                  ] + [vec2(head_dim)] * 4 + [vec2(dv)],
        out_specs=pl.BlockSpec((1, TT, dv), lambda b, h: (b, 0, h)),
        out_shape=jax.ShapeDtypeStruct((nb, TT, qk), BF16),
        compiler_params=_cp("parallel", "parallel"),
        name="ctx_attn",
    )(qt, k, vt, *lam_vecs, subln_g)
    return o_lat, o_ctx


def _halo_rows(dtype):
    return SUBLANES * 4 // jnp.dtype(dtype).itemsize


def _halo_fill(xp_ref, prev_ref, x_ref, next_ref, has_prev, has_next):
    hr = prev_ref.shape[2]
    zero = jnp.zeros((hr, xp_ref.shape[-1]), F32)
    xp_ref[hr:hr + TT, :] = x_ref[0].astype(F32)
    xp_ref[0:hr, :] = jnp.where(has_prev, prev_ref[0, 0].astype(F32), zero)
    xp_ref[hr + TT:2 * hr + TT, :] = jnp.where(has_next, next_ref[0, 0].astype(F32), zero)


def _conv_taps(xp_ref, w, bias, left):
    rows = xp_ref.shape[0]
    hr = (rows - TT) // 2
    xp = xp_ref[...]
    before = None
    for j in range(left):
        z = w[j:j + 1, :] * xp
        before = pltpu.roll(z if before is None else before + z, 1, 0)
    after = None
    for j in range(w.shape[0] - 1, left, -1):
        z = w[j:j + 1, :] * xp
        after = pltpu.roll(z if after is None else after + z, rows - 1, 0)
    acc = w[left:left + 1, :] * xp + bias
    for part in (before, after):
        if part is not None:
            acc = acc + part
    return acc[hr:hr + TT, :]


def _shortconv_kernel(prev_ref, x_ref, next_ref, w_ref, b_ref, v_ref, x1_ref, x2_ref, xp_ref,
                      *, n_seg_tiles, hy):
    i = pl.program_id(1)
    _halo_fill(xp_ref, prev_ref, x_ref, next_ref, i > 0, i < n_seg_tiles - 1)
    y = _conv_taps(xp_ref, w_ref[...], b_ref[...], HY_SHORT_LEFT)
    v_ref[0] = y[:, :hy].astype(v_ref.dtype)
    x1_ref[0] = y[:, hy:2 * hy].astype(x1_ref.dtype)
    x2_ref[0] = y[:, 2 * hy:].astype(x2_ref.dtype)


def _shortconv(u, w, b, *, tile0, n_seg_tiles):
    nb, s, hy3 = u.shape
    hy = hy3 // 3
    hr = _halo_rows(u.dtype)
    nh, per = s // hr, TT // hr
    uh = u.reshape(nb, nh, hr, hy3)
    prev = pl.BlockSpec((1, 1, hr, hy3),
                        lambda b, i: (b, jnp.maximum((tile0 + i) * per - 1, 0), 0, 0))
    nxt = pl.BlockSpec((1, 1, hr, hy3),
                       lambda b, i: (b, jnp.minimum((tile0 + i + 1) * per, nh - 1), 0, 0))
    out = jax.ShapeDtypeStruct((nb, n_seg_tiles * TT, hy), BF16)
    ospec = pl.BlockSpec((1, TT, hy), lambda b, i: (b, i, 0))
    return pl.pallas_call(
        functools.partial(_shortconv_kernel, n_seg_tiles=n_seg_tiles, hy=hy),
        grid=(nb, n_seg_tiles),
        in_specs=[
            prev,
            pl.BlockSpec((1, TT, hy3), lambda b, i: (b, tile0 + i, 0)),
            nxt,
            pl.BlockSpec(w.shape, lambda b, i: (0, 0)),
            pl.BlockSpec((1, hy3), lambda b, i: (0, 0)),
        ],
        out_specs=[ospec, ospec, ospec],
        out_shape=[out, out, out],
        scratch_shapes=[pltpu.VMEM((TT + 2 * hr, hy3), F32)],
        compiler_params=_cp("parallel", "parallel"),
        name="hy_shortconv",
    )(uh, u, uh, w, b.reshape(1, hy3))


def _filter_feats(lh, n_cols):
    n = np.arange(2 * lh).reshape(-1, n_cols).T.reshape(-1)
    lag = np.where(n < lh, n, 2 * lh - n).astype(np.float64)
    t = (lag / lh).astype(np.float32).astype(np.float64)
    bands = np.arange(1, N_BANDS + 1, dtype=np.float64)
    ang = 2.0 * math.pi * t[:, None] * bands
    feats = np.concatenate([t[:, None], np.cos(ang), np.sin(ang)], axis=-1)
    pad = (-(feats.shape[1] + 1)) % SUBLANES
    return np.concatenate([feats, np.zeros((2 * lh, pad)), n[:, None].astype(np.float64)], axis=-1)


def _filt_kernel(ft_ref, w1_ref, b1_ref, w2_ref, b2_ref, w3_ref, b3_ref, fr_ref, dec_ref, o_ref,
                 *, lh, c):
    ft = ft_ref[...]
    freq = fr_ref[...]
    h = jnp.sin(freq * (_dot(ft, w1_ref[...], precision=HIGHEST) + b1_ref[...]))
    h = jnp.sin(freq * (_dot(h, w2_ref[...], precision=HIGHEST) + b2_ref[...]))
    h = _dot(h.astype(BF16), w3_ref[...]) + b3_ref[...]
    t = ft[:, 0:1]
    h = h * jnp.exp(-t * jnp.abs(dec_ref[...]))
    row = ft[:, ft.shape[1] - 1:]
    for o in range(2):
        fwd = h[:, (2 * o) * c:(2 * o + 1) * c]
        bwd = h[:, (2 * o + 1) * c:(2 * o + 2) * c]
        o_ref[o] = jnp.where(row < lh, fwd, jnp.where(row > lh, bwd, jnp.zeros_like(bwd)))


def _hyena_filter(lh, n_cols, w1, b1, w2, b2, w3, b3, freq, decay):
    c = decay.shape[-1]
    feats = jnp.asarray(_filter_feats(lh, n_cols), F32)
    fe = feats.shape[1]
    w1p = jnp.pad(w1, ((0, fe - w1.shape[0]), (0, 0)))
    hid = w1.shape[1]
    tr = 2 * lh // n_cols
    full = lambda a: pl.BlockSpec(a.shape, lambda i: (0,) * a.ndim)
    args = (w1p, b1.reshape(1, hid), w2, b2.reshape(1, hid), w3.astype(BF16), b3.reshape(1, -1),
            freq.reshape(1, hid), decay.reshape(1, -1))
    return pl.pallas_call(
        functools.partial(_filt_kernel, lh=lh, c=c),
        grid=(n_cols,),
        in_specs=[pl.BlockSpec((tr, fe), lambda i: (i, 0))] + [full(a) for a in args],
        out_specs=pl.BlockSpec((2, tr, c), lambda i: (0, 0, i)),
        out_shape=jax.ShapeDtypeStruct((2, tr, n_cols * c), F32),
        compiler_params=_cp("parallel"),
        name="hy_filter",
    )(feats, *args)


@functools.lru_cache(maxsize=None)
def _dft_tables(n):
    nn = n * n
    h = n // 2
    k = np.arange(n)
    th = 2.0 * math.pi * np.outer(k, k) / n
    c, s = np.cos(th), np.sin(th)
    f1_data = np.block([[c[:, :h], s[:, :h]], [-s[:, :h], c[:, :h]]])
    f1_real = np.concatenate([c, -s], axis=0)
    idx = (k[None, None, :] * (k[:, None, None] + n * k[None, :, None])) % nn
    phi = 2.0 * math.pi * idx / nn
    cp, sp = np.cos(phi), np.sin(phi)
    g = np.concatenate([np.concatenate([cp, sp], axis=2), np.concatenate([-sp, cp], axis=2)], axis=1)
    hmat = np.transpose(g, (0, 2, 1)) / nn
    ci, si = c[:h, :], s[:h, :]
    f3 = np.zeros((n, 2 * n))
    f3[:h, 0::2], f3[:h, 1::2] = ci, -si
    f3[h:, 0::2], f3[h:, 1::2] = si, ci
    return f1_data, f1_real, g, hmat, f3


def _fft_s1_kernel(x_ref, f_ref, o_ref):
    o_ref[0] = _dot(f_ref[...], x_ref[0].astype(BF16)).astype(o_ref.dtype)


def _fft_s1(x, f1, n, c):
    p = x.shape[0]
    tc = FFT_N2_TILE * c
    return pl.pallas_call(
        _fft_s1_kernel,
        grid=(p, n * c // tc),
        in_specs=[pl.BlockSpec((1, n, tc), lambda q, j: (q, 0, j)),
                  pl.BlockSpec(f1.shape, lambda q, j: (0, 0))],
        out_specs=pl.BlockSpec((1, 2 * n, tc), lambda q, j: (q, 0, j)),
        out_shape=jax.ShapeDtypeStruct((p, 2 * n, n * c), BF16),
        compiler_params=_cp("parallel", "parallel"),
        name="fft_s1",
    )(x, f1)


def _fft_spec_kernel(a_ref, g_ref, o_ref, *, tk):
    for j in range(tk):
        x = jnp.concatenate([a_ref[0, 0, j], a_ref[0, 1, j]], axis=0)
        o_ref[0, j] = _dot(g_ref[j], x)


def _fft_mid_kernel(a_ref, g_ref, h_ref, kh_ref, o_ref, *, tk, n):
    for j in range(tk):
        x = jnp.concatenate([a_ref[0, 0, j], a_ref[0, 1, j]], axis=0)
        t = _dot(g_ref[j], x)
        tr, ti = t[:n], t[n:]
        kr, ki = kh_ref[0, j, :n], kh_ref[0, j, n:]
        y = jnp.concatenate([tr * kr - ti * ki, tr * ki + ti * kr], axis=0).astype(BF16)
        o_ref[0, j] = _dot(h_ref[j], y).astype(o_ref.dtype)


def _fft_spectrum(a, g, n, c):
    p = a.shape[0]
    tk = FFT_K1_TILE
    a5 = a.reshape(p, 2, n, n, c)
    return pl.pallas_call(
        functools.partial(_fft_spec_kernel, tk=tk),
        grid=(p, n // tk),
        in_specs=[pl.BlockSpec((1, 2, tk, n, c), lambda q, j: (q, 0, j, 0, 0)),
                  pl.BlockSpec((tk, 2 * n, 2 * n), lambda q, j: (j, 0, 0))],
        out_specs=pl.BlockSpec((1, tk, 2 * n, c), lambda q, j: (q, j, 0, 0)),
        out_shape=jax.ShapeDtypeStruct((p, n, 2 * n, c), F32),
        compiler_params=_cp("parallel", "parallel"),
        name="fft_spectrum",
    )(a5, g)


def _fft_mid(a, g, hm, khat, order, n, c):
    p = a.shape[0]
    tk = FFT_K1_TILE
    a5 = a.reshape(p, 2, n, n, c)
    return pl.pallas_call(
        functools.partial(_fft_mid_kernel, tk=tk, n=n),
        grid=(p, n // tk),
        in_specs=[pl.BlockSpec((1, 2, tk, n, c), lambda q, j: (q, 0, j, 0, 0)),
                  pl.BlockSpec((tk, 2 * n, 2 * n), lambda q, j: (j, 0, 0)),
                  pl.BlockSpec((tk, 2 * n, 2 * n), lambda q, j: (j, 0, 0)),
                  pl.BlockSpec((1, tk, 2 * n, c), lambda q, j: (order, j, 0, 0))],
        out_specs=pl.BlockSpec((1, tk, 2 * n, c), lambda q, j: (q, j, 0, 0)),
        out_shape=jax.ShapeDtypeStruct((p, n, 2 * n, c), BF16),
        compiler_params=_cp("parallel", "parallel"),
        name="fft_mid",
    )(a5, g, hm, khat)


def _fft_s3_kernel(c_ref, f_ref, v_ref, x_ref, b_ref, o_ref):
    y = _dot(f_ref[...], c_ref[0])
    v = v_ref[0].astype(F32)
    o_ref[0] = (x_ref[0].astype(F32) * (y + v * b_ref[...])).astype(o_ref.dtype)


def _fft_s3(cm, f3, vin, xg, bias, n, c, out_dtype):
    p = cm.shape[0]
    tc = FFT_N2_TILE * c
    c2 = cm.reshape(p, 2 * n, n * c)
    bias_t = jnp.tile(bias.reshape(1, c), (1, FFT_N2_TILE))
    blk = pl.BlockSpec((1, n, tc), lambda q, j: (q, 0, j))
    return pl.pallas_call(
        _fft_s3_kernel,
        grid=(p, n * c // tc),
        in_specs=[pl.BlockSpec((1, 2 * n, tc), lambda q, j: (q, 0, j)),
                  pl.BlockSpec(f3.shape, lambda q, j: (0, 0)),
                  blk, blk,
                  pl.BlockSpec((1, tc), lambda q, j: (0, 0))],
        out_specs=blk,
        out_shape=jax.ShapeDtypeStruct((p, n, n * c), out_dtype),
        compiler_params=_cp("parallel", "parallel"),
        name="fft_s3",
    )(c2, f3, vin, xg, bias_t)


def _hyena_long(v, x1, x2, kk, hy_bias):
    nb, l, c = v.shape
    n = math.isqrt(2 * l)
    assert n * n == 2 * l and nb % 2 == 0
    p = nb // 2
    f1d, f1r, g, hm, f3 = (jnp.asarray(t, F32).astype(BF16) for t in _dft_tables(n))
    pair = lambda a: a.reshape(p, n, n * c)
    khat = _fft_spectrum(_fft_s1(kk, f1r, n, c), g, n, c)
    z = pair(v)
    for order, xg in enumerate((x1, x2)):
        a = _fft_s1(z, f1d, n, c)
        cm = _fft_mid(a, g, hm, khat, order, n, c)
        z = _fft_s3(cm, f3, z, pair(xg), hy_bias[order], n, c, BF16)
    return z.reshape(nb, l, c)


@functools.lru_cache(maxsize=None)
def _ctx_dft_tables(lc):
    m = 2 * lc
    k = np.arange(m)
    th = 2.0 * math.pi * np.outer(k, k) / m
    c, s = np.cos(th), np.sin(th)
    f_data = np.block([[c[:, :lc], s[:, :lc]], [-s[:, :lc], c[:, :lc]]])
    f_real = np.concatenate([c, -s], axis=0)
    ci, si = c[:lc, :], s[:lc, :]
    f_inv = np.block([[ci, -si], [si, ci]]) / m
    return f_data, f_real, f_inv


def _ctxconv_kernel(v_ref, x1_ref, x2_ref, kk_ref, fd_ref, fr_ref, fi_ref, b_ref, o_ref, *, m):
    def conv(u, order):
        kh = _dot(fr_ref[...], kk_ref[order].astype(BF16))
        t = _dot(fd_ref[...], u.astype(BF16))
        tr, ti, kr, ki = t[:m], t[m:], kh[:m], kh[m:]
        y = jnp.concatenate([tr * kr - ti * ki, tr * ki + ti * kr], axis=0).astype(BF16)
        return _dot(fi_ref[...], y)

    v = v_ref[0].astype(F32)
    z1 = x1_ref[0].astype(F32) * (conv(v, 0) + v * b_ref[0:1, :])
    o_ref[0] = (x2_ref[0].astype(F32) * (conv(z1, 1) + z1 * b_ref[1:2, :])).astype(o_ref.dtype)


def _hyena_ctx(v, x1, x2, kk, hy_bias):
    nb, lc, c = v.shape
    p, m = nb // 2, 2 * lc
    fd, fr, fi = (jnp.asarray(t, F32).astype(BF16) for t in _ctx_dft_tables(lc))
    pair = lambda a: a.reshape(p, m, c)
    blk = pl.BlockSpec((1, m, c), lambda q: (q, 0, 0))
    full = lambda a: pl.BlockSpec(a.shape, lambda q: (0,) * a.ndim)
    z = pl.pallas_call(
        functools.partial(_ctxconv_kernel, m=m),
        grid=(p,),
        in_specs=[blk, blk, blk, full(kk), full(fd), full(fr), full(fi), full(hy_bias)],
        out_specs=blk,
        out_shape=jax.ShapeDtypeStruct((p, m, c), BF16),
        compiler_params=_cp("parallel"),
        name="hy_ctx",
    )(pair(v), pair(x1), pair(x2), kk, fd, fr, fi, hy_bias)
    return z.reshape(nb, lc, c)


def _od_inproj_kernel(h_ref, mod_ref, g_ref, w_ref, gate_ref, x_ref, *, d, r):
    m = mod_ref[0]
    xn = _rms_mod(h_ref[0], g_ref[...], m[:, 0:d], m[:, d:2 * d]).astype(BF16)
    y = _dot(xn, w_ref[...])
    gate_ref[0] = y[:, :r].astype(gate_ref.dtype)
    x_ref[0] = y[:, r:]


def _od_inproj(h, mods, rows, layer, g, w_in, *, n_lat_tiles):
    nb, s, d = h.shape
    r = w_in.shape[1] // 2
    ospec = pl.BlockSpec((1, TT, r), lambda b, i: (b, i, 0))
    return pl.pallas_call(
        functools.partial(_od_inproj_kernel, d=d, r=r),
        grid=(nb, s // TT),
        in_specs=[
            pl.BlockSpec((1, TT, d), lambda b, i: (b, i, 0)),
            _mod_spec(layer, rows, nb, n_lat_tiles, mods.shape[-1]),
            pl.BlockSpec((1, d), lambda b, i: (0, 0)),
            pl.BlockSpec(w_in.shape, lambda b, i: (0, 0)),
        ],
        out_specs=[ospec, ospec],
        out_shape=[jax.ShapeDtypeStruct((nb, s, r), BF16), jax.ShapeDtypeStruct((nb, s, r), F32)],
        compiler_params=_cp("parallel", "parallel"),
        name="od_inproj",
    )(h, mods, g, w_in)


def _rglru_kernel(prev_ref, x_ref, next_ref, cw_ref, cb_ref, w_ref, b_ref, lam_ref, o_ref,
                  xp_ref, a_ref, bb_ref, carry_ref, *, n_lat_tiles, n_tiles, r, windows):
    dr = pl.program_id(1)
    i = pl.program_id(2)
    tile = _scan_tile(dr, i, n_lat_tiles, n_tiles)
    seg_first = jnp.logical_or(tile == 0, tile == n_lat_tiles)
    seg_last = jnp.logical_or(tile == n_lat_tiles - 1, tile == n_tiles - 1)
    _halo_fill(xp_ref, prev_ref, x_ref, next_ref, jnp.logical_not(seg_first),
               jnp.logical_not(seg_last))
    xc = _conv_taps(xp_ref, cw_ref[...], cb_ref[...], RG_CONV_LEFT)
    xb = xc.astype(BF16)
    half_x = 0.5 * xc
    nl = -lam_ref[0]
    softplus = jnp.maximum(nl, 0.0) + jnp.log1p(jnp.exp(-jnp.abs(nl)))
    neg_rate = (0.5 * RG_C) * softplus
    exp2_rate = (-0.5 * RG_C * math.log2(math.e)) * softplus
    for c0, c1, k0, k1 in windows:
        xk = xb[:, k0:k1]
        tr = jnp.tanh(_dot(xk, w_ref[0, k0:k1, c0:c1]) + b_ref[0, :, c0:c1]) + 1.0
        ti = jnp.tanh(_dot(xk, w_ref[0, k0:k1, r + c0:r + c1]) + b_ref[0, :, r + c0:r + c1]) + 1.0
        a = jnp.exp2(exp2_rate[:, c0:c1] * tr)
        a_ref[:, c0:c1] = a
        e = jnp.tanh(neg_rate[:, c0:c1] * tr) * (a * a + 1.0)
        root = jnp.where(e > 0.0, e * lax.rsqrt(e), 0.0)
        bb_ref[:, c0:c1] = root * (ti * half_x[:, c0:c1])

    @pl.when(i == 0)
    def _():
        carry_ref[...] = jnp.zeros_like(carry_ref)

    def step(t, h):
        idx = jnp.where(dr == 0, t, TT - 1 - t)
        h = a_ref[pl.ds(idx, 1), :] * h + bb_ref[pl.ds(idx, 1), :]
        o_ref[0, 0, pl.ds(idx, 1), :] = h
        return h

    carry_ref[0:1, :] = lax.fori_loop(0, TT, step, carry_ref[0:1, :], unroll=8)


def _scan_tile(dr, i, n_lat_tiles, n_tiles):
    fwd = jnp.where(i == 0, n_lat_tiles, i - 1)
    bwd = jnp.where(i == 0, n_lat_tiles, n_lat_tiles - i)
    return jnp.where(dr == 0, fwd, bwd)


def _gate_windows(r, bs):
    out = []
    for c0 in range(0, r, 2 * LANES):
        c1 = min(c0 + 2 * LANES, r)
        k0 = (c0 // bs) * bs // LANES * LANES
        k1 = min(-(-(((c1 - 1) // bs + 1) * bs) // LANES) * LANES, r)
        out.append((c0, c1, k0, k1))
    return tuple(out)


def _rglru(x, conv_w, conv_b, wcat, bcat, lam, *, n_lat_tiles, block_size):
    nb, s, r = x.shape
    n_tiles = s // TT
    assert n_tiles == n_lat_tiles + 1
    n8 = s // SUBLANES
    per = TT // SUBLANES
    x8 = x.reshape(nb, n8, SUBLANES, r)
    tile = lambda d, i: _scan_tile(d, i, n_lat_tiles, n_tiles)
    return pl.pallas_call(
        functools.partial(_rglru_kernel, n_lat_tiles=n_lat_tiles, n_tiles=n_tiles, r=r,
                          windows=_gate_windows(r, block_size)),
        grid=(nb, 2, n_tiles),
        in_specs=[
            pl.BlockSpec((1, 1, SUBLANES, r),
                         lambda b, d, i: (b, jnp.maximum(tile(d, i) * per - 1, 0), 0, 0)),
            pl.BlockSpec((1, TT, r), lambda b, d, i: (b, tile(d, i), 0)),
            pl.BlockSpec((1, 1, SUBLANES, r),
                         lambda b, d, i: (b, jnp.minimum((tile(d, i) + 1) * per, n8 - 1), 0, 0)),
            pl.BlockSpec(conv_w.shape, lambda b, d, i: (0, 0)),
            pl.BlockSpec((1, r), lambda b, d, i: (0, 0)),
            pl.BlockSpec((1, r, 2 * r), lambda b, d, i: (d, 0, 0)),
            pl.BlockSpec((1, 1, 2 * r), lambda b, d, i: (d, 0, 0)),
            pl.BlockSpec((1, 1, r), lambda b, d, i: (d, 0, 0)),
        ],
        out_specs=pl.BlockSpec((1, 1, TT, r), lambda b, d, i: (d, b, tile(d, i), 0)),
        out_shape=jax.ShapeDtypeStruct((2, nb, s, r), F32),
        scratch_shapes=[pltpu.VMEM((TT + 2 * SUBLANES, r), F32), pltpu.VMEM((TT, r), F32),
                        pltpu.VMEM((TT, r), F32), pltpu.VMEM((SUBLANES, r), F32)],
        compiler_params=_cp("parallel", "parallel", "arbitrary"),
        name="rglru",
    )(x8, x, x8, conv_w, conv_b.reshape(1, r), wcat, bcat, lam.reshape(2, 1, r))


def _block_diag(w):
    n, bs, _ = w.shape
    eye = jnp.eye(n, dtype=w.dtype)
    return (eye[:, None, :, None] * w[:, :, None, :]).reshape(n * bs, n * bs)


def _gelu_tanh(x):
    return 0.5 * x * (1.0 + jnp.tanh(math.sqrt(2.0 / math.pi) * (x + 0.044715 * (x * x * x))))


def _post_kernel(*refs, kind, final, d, n_lat_tiles):
    if kind == "even":
        h_ref, zl_ref, zc_ref, ol_ref, oc_ref, mod_ref, g2_ref, wo_ref, w1_ref, w2_ref = refs[:10]
        rest = refs[10:]
        is_ctx = pl.program_id(1) >= n_lat_tiles
        z = jnp.where(is_ctx, zc_ref[0], zl_ref[0])
        o = jnp.where(is_ctx, oc_ref[0], ol_ref[0])
        half = z.shape[-1]
        y = _dot(z, wo_ref[:half, :]) + _dot(o, wo_ref[half:, :])
    else:
        h_ref, hd_ref, gate_ref, mod_ref, g2_ref, wo_ref, w1_ref, w2_ref = refs[:8]
        rest = refs[8:]
        mix = (hd_ref[0, 0] + hd_ref[1, 0]) * _gelu_tanh(gate_ref[0].astype(F32))
        y = _dot(mix.astype(BF16), wo_ref[...])
    out_ref = rest[-1]
    m = mod_ref[0]
    h1 = h_ref[0] + m[:, 2 * d:3 * d] * y
    xn = _rms_mod(h1, g2_ref[...], m[:, 3 * d:4 * d], m[:, 4 * d:5 * d]).astype(BF16)
    acc = jnp.zeros_like(h1)
    dff = w1_ref.shape[1]
    for c0 in range(0, dff, FF_CHUNK):
        a = jnp.maximum(_dot(xn, w1_ref[:, c0:c0 + FF_CHUNK]), 0.0)
        acc = acc + _dot((a * a).astype(BF16), w2_ref[c0:c0 + FF_CHUNK, :])
    h2 = h1 + m[:, 5 * d:6 * d] * acc
    if final:
        fg_ref = rest[0]
        h2 = h2 * lax.rsqrt(jnp.mean(h2 * h2, axis=-1, keepdims=True) + EPS) * fg_ref[...]
    out_ref[0] = h2


def _post(kind, h, mix_args, mods, rows, layer, g2, w_out, w1, w2, final_g, *, n_lat_tiles, final):
    nb, s, d = h.shape
    n_tiles = n_lat_tiles if final else s // TT
    tok = lambda w: pl.BlockSpec((1, TT, w), lambda b, i: (b, i, 0))
    full = lambda a: pl.BlockSpec(a.shape, lambda b, i: (0,) * a.ndim)
    if kind == "even":
        lat = lambda w: pl.BlockSpec((1, TT, w), lambda b, i: (b, jnp.minimum(i, n_lat_tiles - 1), 0))
        ctx = lambda w: pl.BlockSpec((1, TT, w), lambda b, i: (b, 0, 0))
        z_lat, z_ctx, o_lat, o_ctx = mix_args
        mix_specs = [lat(z_lat.shape[-1]), ctx(z_ctx.shape[-1]), lat(o_lat.shape[-1]),
                     ctx(o_ctx.shape[-1])]
    else:
        hd, gate = mix_args
        r = gate.shape[-1]
        mix_specs = [pl.BlockSpec((2, 1, TT, r), lambda b, i: (0, b, i, 0)), tok(r)]
    in_specs = [tok(d)] + mix_specs + [
        _mod_spec(layer, rows, nb, n_lat_tiles, mods.shape[-1]),
        pl.BlockSpec((1, d), lambda b, i: (0, 0)), full(w_out), full(w1), full(w2)]
    args = [h, *mix_args, mods, g2, w_out, w1, w2]
    if final:
        in_specs.append(pl.BlockSpec((1, d), lambda b, i: (0, 0)))
        args.append(final_g)
    return pl.pallas_call(
        functools.partial(_post_kernel, kind=kind, final=final, d=d, n_lat_tiles=n_lat_tiles),
        grid=(nb, n_tiles),
        in_specs=in_specs,
        out_specs=tok(d),
        out_shape=jax.ShapeDtypeStruct((nb, n_tiles * TT, d), F32),
        compiler_params=_cp("parallel", "parallel"),
        name="post_" + kind,
    )(*args)


@functools.lru_cache(maxsize=None)
def _rope_tables(l, lc, head_dim):
    axis = head_dim // 2
    freqs = ROPE_BASE ** (-np.arange(0, axis, 2, dtype=np.float64) / axis)
    freqs = freqs.astype(np.float32).astype(np.float64)
    t = np.arange(l)
    ang_r = (t // GRID_W)[:, None] * freqs
    ang_c = (t % GRID_W)[:, None] * freqs
    cos = np.concatenate([np.cos(ang_r)] * 2 + [np.cos(ang_c)] * 2, axis=-1)
    sin = np.concatenate([-np.sin(ang_r), np.sin(ang_r), -np.sin(ang_c), np.sin(ang_c)], axis=-1)
    cos = np.concatenate([cos, np.ones((lc, head_dim))], axis=0)
    sin = np.concatenate([sin, np.zeros((lc, head_dim))], axis=0)
    rep = LANES // head_dim
    return np.tile(cos, (1, rep)), np.tile(sin, (1, rep))


def kernel(x, c, ctx, c_ctx, ada_w, ada_b, norm1_g, norm2_g, mlp_w1, mlp_w2, final_g, ev_w_in, ev_w_out, hy_short_w, hy_short_b, hy_f_w1, hy_f_b1, hy_f_w2, hy_f_b2, hy_f_w3, hy_f_b3, hy_f_freq, hy_f_decay, hy_bias, df_lq1, df_lk1, df_lq2, df_lk2, df_subln_g, od_w_in, od_w_out, rg_conv_w, rg_conv_b, rg_wa, rg_ba, rg_wx, rg_bx, rg_lam):
    nb, l, d = x.shape
    lc = ctx.shape[1]
    depth = ada_w.shape[0]
    hy = hy_bias.shape[-1]
    head_dim = df_lq1.shape[-1]
    qk = (ev_w_in.shape[-1] - 3 * hy) // 3
    r = rg_lam.shape[-1]
    assert l % TT == 0 and lc == TT and l % GRID_W == 0
    n_lat_tiles = l // TT

    rows = -(-(nb + 1) // SUBLANES) * SUBLANES
    cond = jnp.zeros((rows, d), F32).at[:nb].set(c).at[nb].set(c_ctx)
    mods = _ada_mods(cond, ada_w, ada_b).reshape(depth * rows, 1, ada_w.shape[-1])

    cos_t, sin_t = (jnp.asarray(t, F32) for t in _rope_tables(l, lc, head_dim))
    h = jnp.concatenate([x, ctx], axis=1)
    w1_bf, w2_bf = mlp_w1.astype(BF16), mlp_w2.astype(BF16)

    for i in range(depth):
        j = i // 2
        final = i == depth - 1
        g1, g2 = norm1_g[i].reshape(1, d), norm2_g[i].reshape(1, d)
        if i % 2 == 0:
            lam_init = 0.8 - 0.6 * math.exp(-0.3 * i)
            u, qt, k, vt = _ev_inproj(h, mods, rows, i, g1, ev_w_in[j].astype(BF16), cos_t, sin_t,
                                      n_lat_tiles=n_lat_tiles, hy3=3 * hy, qk=qk, head_dim=head_dim,
                                      tk=_attn_chunk(l + lc))
            lam_vecs = [a[j].reshape(1, -1) for a in (df_lq1, df_lk1, df_lq2, df_lk2)]
            o_lat, o_ctx = _diff_attention(qt, k, vt, lam_vecs, df_subln_g[j].reshape(1, -1),
                                           n_lat_tiles=n_lat_tiles, lam_init=lam_init,
                                           head_dim=head_dim)
            fparams = (hy_f_w1[j], hy_f_b1[j], hy_f_w2[j], hy_f_b2[j], hy_f_w3[j], hy_f_b3[j],
                       hy_f_freq[j], hy_f_decay[j])
            vl, x1l, x2l = _shortconv(u, hy_short_w[j], hy_short_b[j], tile0=0,
                                      n_seg_tiles=n_lat_tiles)
            vc, x1c, x2c = _shortconv(u, hy_short_w[j], hy_short_b[j], tile0=n_lat_tiles,
                                      n_seg_tiles=lc // TT)
            z_lat = _hyena_long(vl, x1l, x2l, _hyena_filter(l, math.isqrt(2 * l), *fparams),
                                hy_bias[j])
            z_ctx = _hyena_ctx(vc, x1c, x2c, _hyena_filter(lc, 1, *fparams), hy_bias[j])
            h = _post("even", h, (z_lat, z_ctx, o_lat, o_ctx), mods, rows, i, g2, ev_w_out[j].astype(BF16),
                      w1_bf[i], w2_bf[i], final_g.reshape(1, d), n_lat_tiles=n_lat_tiles,
                      final=final)
        else:
            gate, xr = _od_inproj(h, mods, rows, i, g1, od_w_in[j].astype(BF16), n_lat_tiles=n_lat_tiles)
            wcat = (0.5 * jnp.stack(
                [jnp.concatenate([_block_diag(rg_wa[j, dd]), _block_diag(rg_wx[j, dd])], axis=1)
                 for dd in range(2)])).astype(BF16)
            bcat = 0.5 * jnp.concatenate([rg_ba[j], rg_bx[j]], axis=-1).reshape(2, 1, 2 * r)
            hd = _rglru(xr, rg_conv_w[j], rg_conv_b[j], wcat, bcat, rg_lam[j],
                        n_lat_tiles=n_lat_tiles, block_size=rg_wa.shape[-1])
            h = _post("odd", h, (hd, gate), mods, rows, i, g2, od_w_out[j].astype(BF16),
                      w1_bf[i], w2_bf[i], final_g.reshape(1, d), n_lat_tiles=n_lat_tiles,
                      final=final)
    return h[:, :l] if h.shape[1] != l else h
```

```python
import functools
import math

import numpy as np
import jax
import jax.numpy as jnp
from jax import lax
from jax.experimental import pallas as pl
from jax.experimental.pallas import tpu as pltpu

F32 = jnp.float32
BF16 = jnp.bfloat16
HIGHEST = lax.Precision.HIGHEST

EPS = 1e-6
GRID_W = 64
ROPE_BASE = 10000.0
N_BANDS = 16
RG_C = 8.0
RG_CONV_LEFT = 2
HY_SHORT_LEFT = 1

TT = 256
LANES = 128
SUBLANES = 8
VMEM_LIMIT = 56 * 1024 * 1024
FF_CHUNK = 1024
FFT_N2_TILE = 8
FFT_K1_TILE = 8
ATTN_SUB_Q = 512
FILT_COL_BLOCKS = 4
VT_ROWS = LANES + 16


def _cp(*sem):
    return pltpu.CompilerParams(dimension_semantics=sem, vmem_limit_bytes=VMEM_LIMIT)


def _dot(a, b, **kw):
    return jnp.dot(a, b, preferred_element_type=F32, **kw)


def _sigmoid(x):
    return 1.0 / (1.0 + jnp.exp(-x))


def _rms_mod(x, g, shift, scale):
    y = x * lax.rsqrt(jnp.mean(x * x, axis=-1, keepdims=True) + EPS)
    return (y * g) * (1.0 + scale) + shift


def _ada_kernel(c_ref, w_ref, b_ref, o_ref):
    c = c_ref[...]
    s = c * _sigmoid(c)
    o_ref[0] = _dot(s.astype(BF16), w_ref[0].astype(BF16)) + b_ref[0]


def _ada_mods(cond, ada_w, ada_b):
    depth, d, n = ada_w.shape
    rows = cond.shape[0]
    tn = min(n, 1536)
    return pl.pallas_call(
        _ada_kernel,
        grid=(depth, n // tn),
        in_specs=[
            pl.BlockSpec((rows, d), lambda l, j: (0, 0)),
            pl.BlockSpec((1, d, tn), lambda l, j: (l, 0, j)),
            pl.BlockSpec((1, 1, tn), lambda l, j: (l, 0, j)),
        ],
        out_specs=pl.BlockSpec((1, rows, tn), lambda l, j: (l, 0, j)),
        out_shape=jax.ShapeDtypeStruct((depth, rows, n), F32),
        compiler_params=_cp("parallel", "parallel"),
        name="ada_mods",
    )(cond, ada_w, ada_b.reshape(depth, 1, n))


def _mod_spec(layer, rows, nb, n_lat_tiles, n6):
    def imap(b, i):
        return (layer * rows + jnp.where(i >= n_lat_tiles, nb, b), 0, 0)
    return pl.BlockSpec((1, 1, n6), imap)


def _ev_inproj_kernel(h_ref, mod_ref, g_ref, w_ref, cos_ref, sin_ref,
                      u_ref, qt_ref, k_ref, vt_ref, *, d, hy3, qk, qscale):
    m = mod_ref[0]
    xn = _rms_mod(h_ref[0], g_ref[...], m[:, 0:d], m[:, d:2 * d]).astype(BF16)
    y = _dot(xn, w_ref[...])
    u_ref[0] = y[:, :hy3].astype(u_ref.dtype)
    cos = cos_ref[...]
    sin = sin_ref[...]
    lane = lax.broadcasted_iota(jnp.int32, cos.shape, 1)
    first = (lane % 32) < 16

    def rope(z):
        sw = jnp.where(first, pltpu.roll(z, LANES - 16, 1), pltpu.roll(z, 16, 1))
        return z * cos + sw * sin

    extra = vt_ref.shape[3] - LANES
    ones_row = jnp.where(lax.broadcasted_iota(jnp.int32, (extra, cos.shape[0]), 0) == 0,
                         1.0, 0.0).astype(BF16)
    for c in range(qk // LANES):
        lo = c * LANES
        qt_ref[0, c] = (rope(y[:, hy3 + lo:hy3 + lo + LANES]) * qscale).T.astype(BF16)
        k_ref[0, :, lo:lo + LANES] = rope(y[:, hy3 + qk + lo:hy3 + qk + lo + LANES]).astype(BF16)
        vt_ref[0, c, 0, 0:LANES, :] = y[:, hy3 + 2 * qk + lo:hy3 + 2 * qk + lo + LANES].T.astype(BF16)
        vt_ref[0, c, 0, LANES:, :] = ones_row


def _ev_inproj(h, mods, rows, layer, g, w_in, cos_t, sin_t, *, n_lat_tiles, hy3, qk, head_dim, tk):
    nb, s, d = h.shape
    n_in = w_in.shape[1]
    assert 2 * head_dim == LANES
    heads, per = qk // LANES, tk // TT
    kern = functools.partial(_ev_inproj_kernel, d=d, hy3=hy3, qk=qk,
                             qscale=head_dim ** -0.5 * math.log2(math.e))
    return pl.pallas_call(
        kern,
        grid=(nb, s // TT),
        in_specs=[
            pl.BlockSpec((1, TT, d), lambda b, i: (b, i, 0)),
            _mod_spec(layer, rows, nb, n_lat_tiles, mods.shape[-1]),
            pl.BlockSpec((1, d), lambda b, i: (0, 0)),
            pl.BlockSpec((d, n_in), lambda b, i: (0, 0)),
            pl.BlockSpec((TT, LANES), lambda b, i: (i, 0)),
            pl.BlockSpec((TT, LANES), lambda b, i: (i, 0)),
        ],
        out_specs=[
            pl.BlockSpec((1, TT, hy3), lambda b, i: (b, i, 0)),
            pl.BlockSpec((1, heads, LANES, TT), lambda b, i: (b, 0, 0, i)),
            pl.BlockSpec((1, TT, qk), lambda b, i: (b, i, 0)),
            pl.BlockSpec((1, heads, 1, VT_ROWS, TT), lambda b, i: (b, 0, i // per, 0, i % per)),
        ],
        out_shape=[
            jax.ShapeDtypeStruct((nb, s, hy3), BF16),
            jax.ShapeDtypeStruct((nb, heads, LANES, s), BF16),
            jax.ShapeDtypeStruct((nb, s, qk), BF16),
            jax.ShapeDtypeStruct((nb, heads, s // tk, VT_ROWS, tk), BF16),
        ],
        compiler_params=_cp("parallel", "parallel"),
        name="ev_inproj",
    )(h, mods, g, w_in, cos_t, sin_t)


def _split_maps(qt, half):
    row = lax.broadcasted_iota(jnp.int32, qt.shape, 0)
    zero = jnp.zeros_like(qt)
    return jnp.where(row < half, qt, zero), jnp.where(row >= half, qt, zero)


def _attn_out(a0, a1, lam_refs, sg_ref, lam_init, dv):
    lq1_ref, lk1_ref, lq2_ref, lk2_ref = lam_refs
    lam = (jnp.exp(jnp.sum(lq1_ref[...] * lk1_ref[...], axis=-1, keepdims=True))
           - jnp.exp(jnp.sum(lq2_ref[...] * lk2_ref[...], axis=-1, keepdims=True)) + lam_init)
    o = (a0[:dv] / a0[dv:dv + 1] - lam * (a1[:dv] / a1[dv:dv + 1])).T
    on = o * lax.rsqrt(jnp.mean(o * o, axis=-1, keepdims=True) + EPS)
    return on * sg_ref[...] * (1.0 - lam_init)


def _col_max8(sc):
    part = sc[0:SUBLANES]
    for g in range(1, sc.shape[0] // SUBLANES):
        part = jnp.maximum(part, sc[g * SUBLANES:(g + 1) * SUBLANES])
    return part


def _attn_kernel(qt_ref, qnt_ref, k_ref, vt_ref, lq1_ref, lk1_ref, lq2_ref, lk2_ref, sg_ref, o_ref,
                 s_ref, mp_ref, p_ref, acc_ref, m_ref, alpha_ref, *, n_chunks, tk, lam_init, half):
    i = pl.program_id(2)
    dv = 2 * half
    lam_refs = (lq1_ref, lk1_ref, lq2_ref, lk2_ref)
    tq = qnt_ref.shape[-1]
    q_sub = (_split_maps(qt_ref[0, 0, :, 0:tq], half), _split_maps(qt_ref[0, 0, :, tq:2 * tq], half))
    q_next = _split_maps(qnt_ref[0, 0], half)

    def put_scores(slot, qts, c):
        kc = k_ref[0, pl.ds(pl.multiple_of(c * tk, tk), tk), :]
        for j in range(2):
            sc = _dot(kc, qts[j])
            s_ref[slot, j] = sc
            mp_ref[slot, j] = _col_max8(sc)

    def put_probs(slot, sub, first):
        for j in range(2):
            col_max = jnp.max(mp_ref[slot, j], axis=0, keepdims=True)
            if first:
                m_new = col_max
            else:
                m_old = m_ref[sub, j]
                m_new = jnp.maximum(m_old, col_max)
                alpha_ref[slot, j] = jnp.exp2(m_old - m_new)
            m_ref[sub, j] = m_new
            p_ref[slot, j] = jnp.exp2(s_ref[slot, j] - m_new).astype(BF16)

    def add_pv(slot, sub, c, first):
        for j in range(2):
            pv = _dot(vt_ref[0, 0, c], p_ref[slot, j])
            acc_ref[sub, j] = pv if first else alpha_ref[slot, j] * acc_ref[sub, j] + pv

    def sub_tile(sub, parity, qts, qts_after, sub_after):
        slot = lambda c: (c + parity) % 2
        put_scores(slot(0), qts, 2)
        put_probs(slot(1), sub, False)
        add_pv(slot(0), sub, 0, True)
        n_uniform = n_chunks - 3

        def pair(u, carry):
            c = 1 + 2 * u
            put_scores(slot(1), qts, c + 2)
            put_probs(slot(0), sub, False)
            add_pv(slot(1), sub, c, False)
            put_scores(slot(0), qts, c + 3)
            put_probs(slot(1), sub, False)
            add_pv(slot(0), sub, c + 1, False)
            return carry

        lax.fori_loop(0, n_uniform // 2, pair, 0)
        c = n_chunks - 2
        put_scores(slot(c), qts_after, 0)
        put_probs(slot(c + 1), sub, False)
        add_pv(slot(c), sub, c, False)
        c = n_chunks - 1
        put_scores(slot(c), qts_after, 1)
        put_probs(slot(c + 1), sub_after, True)
        add_pv(slot(c), sub, c, False)
        o_ref[0, sub * tq:(sub + 1) * tq, :] = _attn_out(
            acc_ref[sub, 0], acc_ref[sub, 1], lam_refs, sg_ref, lam_init, dv).astype(o_ref.dtype)

    @pl.when(i == 0)
    def _():
        put_scores(0, q_sub[0], 0)
        put_probs(0, 0, True)
        put_scores(1, q_sub[0], 1)

    sub_tile(0, 0, q_sub[0], q_sub[1], 1)
    sub_tile(1, 1, q_sub[1], q_next, 0)


def _ctx_attn_kernel(qt_ref, k_ref, vt_ref, lq1_ref, lk1_ref, lq2_ref, lk2_ref, sg_ref, o_ref,
                     *, lam_init, half):
    acc = []
    for qtj in _split_maps(qt_ref[0, 0], half):
        sc = _dot(k_ref[0], qtj)
        col_max = jnp.max(_col_max8(sc), axis=0, keepdims=True)
        acc.append(_dot(vt_ref[0, 0, 0], jnp.exp2(sc - col_max).astype(BF16)))
    o_ref[0] = _attn_out(acc[0], acc[1], (lq1_ref, lk1_ref, lq2_ref, lk2_ref), sg_ref, lam_init,
                         2 * half).astype(o_ref.dtype)


def _attn_chunk(s):
    return next(t for t in (3 * TT, TT) if s % t == 0 and (s // t) % 2 == 1 and s // t >= 5)


def _diff_attention(qt, k, vt, lam_vecs, subln_g, *, n_lat_tiles, lam_init, head_dim):
    nb, s, qk = k.shape
    dv = 2 * head_dim
    heads = qk // dv
    l = n_lat_tiles * TT
    n_chunks, tk = vt.shape[2], vt.shape[4]
    tq = ATTN_SUB_Q if l % (2 * ATTN_SUB_Q) == 0 else TT
    assert l % (2 * tq) == 0 and n_chunks % 2 == 1 and n_chunks >= 5 and s - l == TT
    n_steps = l // (2 * tq)
    vec = lambda n: pl.BlockSpec((1, n), lambda b, h, i: (0, 0))
    vecs = [vec(head_dim)] * 4 + [vec(dv)]
    o_lat = pl.pallas_call(
        functools.partial(_attn_kernel, n_chunks=n_chunks, tk=tk, lam_init=lam_init, half=head_dim),
        grid=(nb, heads, n_steps),
        in_specs=[
            pl.BlockSpec((1, 1, dv, 2 * tq), lambda b, h, i: (b, h, 0, i)),
            pl.BlockSpec((1, 1, dv, tq),
                         lambda b, h, i: (b, h, 0, jnp.minimum(2 * i + 2, 2 * n_steps - 1))),
            pl.BlockSpec((1, s, dv), lambda b, h, i: (b, 0, h)),
            pl.BlockSpec((1, 1, n_chunks, VT_ROWS, tk), lambda b, h, i: (b, h, 0, 0, 0)),
        ] + vecs,
        out_specs=pl.BlockSpec((1, 2 * tq, dv), lambda b, h, i: (b, i, h)),
        out_shape=jax.ShapeDtypeStruct((nb, l, qk), BF16),
        scratch_shapes=[pltpu.VMEM((2, 2, tk, tq), F32), pltpu.VMEM((2, 2, SUBLANES, tq), F32),
                        pltpu.VMEM((2, 2, tk, tq), BF16), pltpu.VMEM((2, 2, VT_ROWS, tq), F32),
                        pltpu.VMEM((2, 2, 1, tq), F32), pltpu.VMEM((2, 2, 1, tq), F32)],
        compiler_params=_cp("parallel", "parallel", "arbitrary"),
        name="diff_attn",
    )(qt, qt, k, vt, *lam_vecs, subln_g)
    per = tk // TT
    vec2 = lambda n: pl.BlockSpec((1, n), lambda b, h: (0, 0))
    o_ctx = pl.pallas_call(
        functools.partial(_ctx_attn_kernel, lam_init=lam_init, half=head_dim),
        grid=(nb, heads),
        in_specs=[pl.BlockSpec((1, 1, dv, TT), lambda b, h: (b, h, 0, n_lat_tiles)),
                  pl.BlockSpec((1, TT, dv), lambda b, h: (b, n_lat_tiles, h)),
                  pl.BlockSpec((1, 1, 1, VT_ROWS, TT),
                               lambda b, h: (b, h, n_lat_tiles // per, 0, n_lat_tiles % per)),
                  ] + [vec2(head_dim)] * 4 + [vec2(dv)],
        out_specs=pl.BlockSpec((1, TT, dv), lambda b, h: (b, 0, h)),
        out_shape=jax.ShapeDtypeStruct((nb, TT, qk), BF16),
        compiler_params=_cp("parallel", "parallel"),
        name="ctx_attn",
    )(qt, k, vt, *lam_vecs, subln_g)
    return o_lat, o_ctx


def _halo_rows(dtype):
    return SUBLANES * 4 // jnp.dtype(dtype).itemsize


def _halo_fill(xp_ref, prev_ref, x_ref, next_ref, has_prev, has_next):
    hr = prev_ref.shape[2]
    zero = jnp.zeros((hr, xp_ref.shape[-1]), F32)
    xp_ref[hr:hr + TT, :] = x_ref[0].astype(F32)
    xp_ref[0:hr, :] = jnp.where(has_prev, prev_ref[0, 0].astype(F32), zero)
    xp_ref[hr + TT:2 * hr + TT, :] = jnp.where(has_next, next_ref[0, 0].astype(F32), zero)


def _conv_taps(xp, w, bias, left):
    rows = xp.shape[0]
    hr = (rows - TT) // 2
    before = None
    for j in range(left):
        z = w[j:j + 1, :] * xp
        before = pltpu.roll(z if before is None else before + z, 1, 0)
    after = None
    for j in range(w.shape[0] - 1, left, -1):
        z = w[j:j + 1, :] * xp
        after = pltpu.roll(z if after is None else after + z, rows - 1, 0)
    acc = w[left:left + 1, :] * xp + bias
    for part in (before, after):
        if part is not None:
            acc = acc + part
    return acc[hr:hr + TT, :]


def _shortconv_kernel(prev_ref, x_ref, next_ref, w_ref, b_ref, v_ref, x1_ref, x2_ref, xp_ref,
                      *, n_seg_tiles, hy):
    i = pl.program_id(1)
    _halo_fill(xp_ref, prev_ref, x_ref, next_ref, i > 0, i < n_seg_tiles - 1)
    y = _conv_taps(xp_ref[...], w_ref[...], b_ref[...], HY_SHORT_LEFT)
    v_ref[0] = y[:, :hy].astype(v_ref.dtype)
    x1_ref[0] = y[:, hy:2 * hy].astype(x1_ref.dtype)
    x2_ref[0] = y[:, 2 * hy:].astype(x2_ref.dtype)


def _shortconv(u, w, b, *, tile0, n_seg_tiles):
    nb, s, hy3 = u.shape
    hy = hy3 // 3
    hr = _halo_rows(u.dtype)
    nh, per = s // hr, TT // hr
    uh = u.reshape(nb, nh, hr, hy3)
    prev = pl.BlockSpec((1, 1, hr, hy3),
                        lambda b, i: (b, jnp.maximum((tile0 + i) * per - 1, 0), 0, 0))
    nxt = pl.BlockSpec((1, 1, hr, hy3),
                       lambda b, i: (b, jnp.minimum((tile0 + i + 1) * per, nh - 1), 0, 0))
    out = jax.ShapeDtypeStruct((nb, n_seg_tiles * TT, hy), BF16)
    ospec = pl.BlockSpec((1, TT, hy), lambda b, i: (b, i, 0))
    return pl.pallas_call(
        functools.partial(_shortconv_kernel, n_seg_tiles=n_seg_tiles, hy=hy),
        grid=(nb, n_seg_tiles),
        in_specs=[
            prev,
            pl.BlockSpec((1, TT, hy3), lambda b, i: (b, tile0 + i, 0)),
            nxt,
            pl.BlockSpec(w.shape, lambda b, i: (0, 0)),
            pl.BlockSpec((1, hy3), lambda b, i: (0, 0)),
        ],
        out_specs=[ospec, ospec, ospec],
        out_shape=[out, out, out],
        scratch_shapes=[pltpu.VMEM((TT + 2 * hr, hy3), F32)],
        compiler_params=_cp("parallel", "parallel"),
        name="hy_shortconv",
    )(uh, u, uh, w, b.reshape(1, hy3))


def _filter_feats(lh, n_cols):
    n = np.arange(2 * lh).reshape(-1, n_cols).T.reshape(-1)
    lag = np.where(n < lh, n, 2 * lh - n).astype(np.float64)
    t = (lag / lh).astype(np.float32).astype(np.float64)
    bands = np.arange(1, N_BANDS + 1, dtype=np.float64)
    ang = 2.0 * math.pi * t[:, None] * bands
    feats = np.concatenate([t[:, None], np.cos(ang), np.sin(ang)], axis=-1)
    pad = (-(feats.shape[1] + 1)) % SUBLANES
    return np.concatenate([feats, np.zeros((2 * lh, pad)), n[:, None].astype(np.float64)], axis=-1)


def _filt_kernel(ft_ref, w1_ref, b1_ref, w2_ref, b2_ref, w3_ref, b3_ref, fr_ref, dec_ref, o_ref,
                 *, lh, c):
    ft = ft_ref[...]
    freq = fr_ref[...]
    h = jnp.sin(freq * (_dot(ft, w1_ref[...], precision=HIGHEST) + b1_ref[...]))
    h = jnp.sin(freq * (_dot(h, w2_ref[...], precision=HIGHEST) + b2_ref[...]))
    h = _dot(h.astype(BF16), w3_ref[...]) + b3_ref[...]
    t = ft[:, 0:1]
    h = h * jnp.exp(-t * jnp.abs(dec_ref[...]))
    row = ft[:, ft.shape[1] - 1:]
    tr = o_ref.shape[1]
    for o in range(2):
        fwd = h[:, (2 * o) * c:(2 * o + 1) * c]
        bwd = h[:, (2 * o + 1) * c:(2 * o + 2) * c]
        sel = jnp.where(row < lh, fwd, jnp.where(row > lh, bwd, jnp.zeros_like(bwd)))
        for q in range(ft.shape[0] // tr):
            o_ref[o, :, q * c:(q + 1) * c] = sel[q * tr:(q + 1) * tr]


def _hyena_filter(lh, n_cols, w1, b1, w2, b2, w3, b3, freq, decay):
    c = decay.shape[-1]
    feats = jnp.asarray(_filter_feats(lh, n_cols), F32)
    fe = feats.shape[1]
    w1p = jnp.pad(w1, ((0, fe - w1.shape[0]), (0, 0)))
    hid = w1.shape[1]
    tr = 2 * lh // n_cols
    cb = math.gcd(n_cols, FILT_COL_BLOCKS)
    full = lambda a: pl.BlockSpec(a.shape, lambda i: (0,) * a.ndim)
    args = (w1p, b1.reshape(1, hid), w2, b2.reshape(1, hid), w3.astype(BF16), b3.reshape(1, -1),
            freq.reshape(1, hid), decay.reshape(1, -1))
    return pl.pallas_call(
        functools.partial(_filt_kernel, lh=lh, c=c),
        grid=(n_cols // cb,),
        in_specs=[pl.BlockSpec((cb * tr, fe), lambda i: (i, 0))] + [full(a) for a in args],
        out_specs=pl.BlockSpec((2, tr, cb * c), lambda i: (0, 0, i)),
        out_shape=jax.ShapeDtypeStruct((2, tr, n_cols * c), F32),
        compiler_params=_cp("parallel"),
        name="hy_filter",
    )(feats, *args)


@functools.lru_cache(maxsize=None)
def _dft_tables(n):
    nn = n * n
    h = n // 2
    k = np.arange(n)
    th = 2.0 * math.pi * np.outer(k, k) / n
    c, s = np.cos(th), np.sin(th)
    f1_data = np.block([[c[:, :h], s[:, :h]], [-s[:, :h], c[:, :h]]])
    f1_real = np.concatenate([c, -s], axis=0)
    idx = (k[None, None, :] * (k[:, None, None] + n * k[None, :, None])) % nn
    phi = 2.0 * math.pi * idx / nn
    cp, sp = np.cos(phi), np.sin(phi)
    g = np.concatenate([np.concatenate([cp, sp], axis=2), np.concatenate([-sp, cp], axis=2)], axis=1)
    hmat = np.transpose(g, (0, 2, 1)) / nn
    ci, si = c[:h, :], s[:h, :]
    f3 = np.zeros((n, 2 * n))
    f3[:h, 0::2], f3[:h, 1::2] = ci, -si
    f3[h:, 0::2], f3[h:, 1::2] = si, ci
    return f1_data, f1_real, g, hmat, f3


def _fft_s1_kernel(x_ref, f_ref, o_ref):
    o_ref[0] = _dot(f_ref[...], x_ref[0].astype(BF16)).astype(o_ref.dtype)


def _fft_s1(x, f1, n, c):
    p = x.shape[0]
    tc = FFT_N2_TILE * c
    return pl.pallas_call(
        _fft_s1_kernel,
        grid=(p, n * c // tc),
        in_specs=[pl.BlockSpec((1, n, tc), lambda q, j: (q, 0, j)),
                  pl.BlockSpec(f1.shape, lambda q, j: (0, 0))],
        out_specs=pl.BlockSpec((1, 2 * n, tc), lambda q, j: (q, 0, j)),
        out_shape=jax.ShapeDtypeStruct((p, 2 * n, n * c), BF16),
        compiler_params=_cp("parallel", "parallel"),
        name="fft_s1",
    )(x, f1)


def _fft_spec_kernel(a_ref, g_ref, o_ref, *, tk):
    for j in range(tk):
        x = jnp.concatenate([a_ref[0, 0, j], a_ref[0, 1, j]], axis=0)
        o_ref[0, j] = _dot(g_ref[j], x).astype(o_ref.dtype)


def _fft_mid_kernel(a_ref, g_ref, h_ref, kh_ref, o_ref, *, tk, n):
    for j in range(tk):
        x = jnp.concatenate([a_ref[0, 0, j], a_ref[0, 1, j]], axis=0)
        t = _dot(g_ref[j], x)
        tr, ti = t[:n], t[n:]
        kr, ki = kh_ref[0, j, :n].astype(F32), kh_ref[0, j, n:].astype(F32)
        y = jnp.concatenate([tr * kr - ti * ki, tr * ki + ti * kr], axis=0).astype(BF16)
        o_ref[0, j] = _dot(h_ref[j], y).astype(o_ref.dtype)


def _fft_spectrum(a, g, n, c):
    p = a.shape[0]
    tk = FFT_K1_TILE
    a5 = a.reshape(p, 2, n, n, c)
    return pl.pallas_call(
        functools.partial(_fft_spec_kernel, tk=tk),
        grid=(p, n // tk),
        in_specs=[pl.BlockSpec((1, 2, tk, n, c), lambda q, j: (q, 0, j, 0, 0)),
                  pl.BlockSpec((tk, 2 * n, 2 * n), lambda q, j: (j, 0, 0))],
        out_specs=pl.BlockSpec((1, tk, 2 * n, c), lambda q, j: (q, j, 0, 0)),
        out_shape=jax.ShapeDtypeStruct((p, n, 2 * n, c), BF16),
        compiler_params=_cp("parallel", "parallel"),
        name="fft_spectrum",
    )(a5, g)


def _fft_mid(a, g, hm, khat, order, n, c):
    p = a.shape[0]
    tk = FFT_K1_TILE
    a5 = a.reshape(p, 2, n, n, c)
    return pl.pallas_call(
        functools.partial(_fft_mid_kernel, tk=tk, n=n),
        grid=(p, n // tk),
        in_specs=[pl.BlockSpec((1, 2, tk, n, c), lambda q, j: (q, 0, j, 0, 0)),
                  pl.BlockSpec((tk, 2 * n, 2 * n), lambda q, j: (j, 0, 0)),
                  pl.BlockSpec((tk, 2 * n, 2 * n), lambda q, j: (j, 0, 0)),
                  pl.BlockSpec((1, tk, 2 * n, c), lambda q, j: (order, j, 0, 0))],
        out_specs=pl.BlockSpec((1, tk, 2 * n, c), lambda q, j: (q, j, 0, 0)),
        out_shape=jax.ShapeDtypeStruct((p, n, 2 * n, c), BF16),
        compiler_params=_cp("parallel", "parallel"),
        name="fft_mid",
    )(a5, g, hm, khat)


def _fft_s3_kernel(c_ref, f_ref, v_ref, x_ref, b_ref, o_ref):
    y = _dot(f_ref[...], c_ref[0])
    v = v_ref[0].astype(F32)
    o_ref[0] = (x_ref[0].astype(F32) * (y + v * b_ref[...])).astype(o_ref.dtype)


def _fft_s3(cm, f3, vin, xg, bias, n, c, out_dtype):
    p = cm.shape[0]
    tc = FFT_N2_TILE * c
    c2 = cm.reshape(p, 2 * n, n * c)
    bias_t = jnp.tile(bias.reshape(1, c), (1, FFT_N2_TILE))
    blk = pl.BlockSpec((1, n, tc), lambda q, j: (q, 0, j))
    return pl.pallas_call(
        _fft_s3_kernel,
        grid=(p, n * c // tc),
        in_specs=[pl.BlockSpec((1, 2 * n, tc), lambda q, j: (q, 0, j)),
                  pl.BlockSpec(f3.shape, lambda q, j: (0, 0)),
                  blk, blk,
                  pl.BlockSpec((1, tc), lambda q, j: (0, 0))],
        out_specs=blk,
        out_shape=jax.ShapeDtypeStruct((p, n, n * c), out_dtype),
        compiler_params=_cp("parallel", "parallel"),
        name="fft_s3",
    )(c2, f3, vin, xg, bias_t)


def _hyena_long(v, x1, x2, kk, hy_bias):
    nb, l, c = v.shape
    n = math.isqrt(2 * l)
    assert n * n == 2 * l and nb % 2 == 0
    p = nb // 2
    f1d, f1r, g, hm, f3 = (jnp.asarray(t, F32).astype(BF16) for t in _dft_tables(n))
    pair = lambda a: a.reshape(p, n, n * c)
    khat = _fft_spectrum(_fft_s1(kk, f1r, n, c), g, n, c)
    z = pair(v)
    for order, xg in enumerate((x1, x2)):
        a = _fft_s1(z, f1d, n, c)
        cm = _fft_mid(a, g, hm, khat, order, n, c)
        z = _fft_s3(cm, f3, z, pair(xg), hy_bias[order], n, c, BF16)
    return z.reshape(nb, l, c)


@functools.lru_cache(maxsize=None)
def _ctx_dft_tables(lc):
    m = 2 * lc
    k = np.arange(m)
    th = 2.0 * math.pi * np.outer(k, k) / m
    c, s = np.cos(th), np.sin(th)
    f_data = np.block([[c[:, :lc], s[:, :lc]], [-s[:, :lc], c[:, :lc]]])
    f_real = np.concatenate([c, -s], axis=0)
    ci, si = c[:lc, :], s[:lc, :]
    f_inv = np.block([[ci, -si], [si, ci]]) / m
    return f_data, f_real, f_inv


def _ctxconv_kernel(v_ref, x1_ref, x2_ref, kk_ref, fd_ref, fr_ref, fi_ref, b_ref, o_ref, *, m):
    def conv(u, order):
        kh = _dot(fr_ref[...], kk_ref[order].astype(BF16))
        t = _dot(fd_ref[...], u.astype(BF16))
        tr, ti, kr, ki = t[:m], t[m:], kh[:m], kh[m:]
        y = jnp.concatenate([tr * kr - ti * ki, tr * ki + ti * kr], axis=0).astype(BF16)
        return _dot(fi_ref[...], y)

    v = v_ref[0].astype(F32)
    z1 = x1_ref[0].astype(F32) * (conv(v, 0) + v * b_ref[0:1, :])
    o_ref[0] = (x2_ref[0].astype(F32) * (conv(z1, 1) + z1 * b_ref[1:2, :])).astype(o_ref.dtype)


def _hyena_ctx(v, x1, x2, kk, hy_bias):
    nb, lc, c = v.shape
    p, m = nb // 2, 2 * lc
    fd, fr, fi = (jnp.asarray(t, F32).astype(BF16) for t in _ctx_dft_tables(lc))
    pair = lambda a: a.reshape(p, m, c)
    blk = pl.BlockSpec((1, m, c), lambda q: (q, 0, 0))
    full = lambda a: pl.BlockSpec(a.shape, lambda q: (0,) * a.ndim)
    z = pl.pallas_call(
        functools.partial(_ctxconv_kernel, m=m),
        grid=(p,),
        in_specs=[blk, blk, blk, full(kk), full(fd), full(fr), full(fi), full(hy_bias)],
        out_specs=blk,
        out_shape=jax.ShapeDtypeStruct((p, m, c), BF16),
        compiler_params=_cp("parallel"),
        name="hy_ctx",
    )(pair(v), pair(x1), pair(x2), kk, fd, fr, fi, hy_bias)
    return z.reshape(nb, lc, c)


def _od_inproj_kernel(prev_ref, h_ref, next_ref, mod_ref, g_ref, w_ref, cw_ref, cb_ref,
                      gate_ref, xc_ref, *, d, r, n_lat_tiles, n_tiles):
    i = pl.program_id(1)
    seg_first = jnp.logical_or(i == 0, i == n_lat_tiles)
    seg_last = jnp.logical_or(i == n_lat_tiles - 1, i == n_tiles - 1)
    m = mod_ref[0]
    rows = jnp.concatenate([prev_ref[0, 0], h_ref[0], next_ref[0, 0]], axis=0)
    xn = _rms_mod(rows, g_ref[...], m[:, 0:d], m[:, d:2 * d]).astype(BF16)
    y = _dot(xn, w_ref[...])
    gate_ref[0] = y[SUBLANES:SUBLANES + TT, :r].astype(gate_ref.dtype)
    x = y[:, r:]
    row = lax.broadcasted_iota(jnp.int32, (x.shape[0], 1), 0)
    outside = jnp.logical_or(jnp.logical_and(row < SUBLANES, seg_first),
                             jnp.logical_and(row >= SUBLANES + TT, seg_last))
    x = jnp.where(outside, 0.0, x)
    xc_ref[0] = _conv_taps(x, cw_ref[...], cb_ref[...], RG_CONV_LEFT)


def _od_inproj(h, mods, rows, layer, g, w_in, conv_w, conv_b, *, n_lat_tiles):
    nb, s, d = h.shape
    r = w_in.shape[1] // 2
    n_tiles = s // TT
    n8, per = s // SUBLANES, TT // SUBLANES
    h8 = h.reshape(nb, n8, SUBLANES, d)
    ospec = pl.BlockSpec((1, TT, r), lambda b, i: (b, i, 0))
    return pl.pallas_call(
        functools.partial(_od_inproj_kernel, d=d, r=r, n_lat_tiles=n_lat_tiles, n_tiles=n_tiles),
        grid=(nb, n_tiles),
        in_specs=[
            pl.BlockSpec((1, 1, SUBLANES, d), lambda b, i: (b, jnp.maximum(i * per - 1, 0), 0, 0)),
            pl.BlockSpec((1, TT, d), lambda b, i: (b, i, 0)),
            pl.BlockSpec((1, 1, SUBLANES, d),
                         lambda b, i: (b, jnp.minimum((i + 1) * per, n8 - 1), 0, 0)),
            _mod_spec(layer, rows, nb, n_lat_tiles, mods.shape[-1]),
            pl.BlockSpec((1, d), lambda b, i: (0, 0)),
            pl.BlockSpec(w_in.shape, lambda b, i: (0, 0)),
            pl.BlockSpec(conv_w.shape, lambda b, i: (0, 0)),
            pl.BlockSpec((1, r), lambda b, i: (0, 0)),
        ],
        out_specs=[ospec, ospec],
        out_shape=[jax.ShapeDtypeStruct((nb, s, r), BF16), jax.ShapeDtypeStruct((nb, s, r), F32)],
        compiler_params=_cp("parallel", "parallel"),
        name="od_inproj",
    )(h8, h, h8, mods, g, w_in, conv_w, conv_b.reshape(1, r))


def _rglru_kernel(x_ref, w_ref, b_ref, lam_ref, o_ref, a_ref, bb_ref, carry_ref, *, r, windows):
    dr = pl.program_id(1)
    i = pl.program_id(2)
    xc = x_ref[0]
    xb = xc.astype(BF16)
    half_x = 0.5 * xc
    nl = -lam_ref[0]
    softplus = jnp.maximum(nl, 0.0) + jnp.log1p(jnp.exp(-jnp.abs(nl)))
    neg_rate = (0.5 * RG_C) * softplus
    exp2_rate = (-0.5 * RG_C * math.log2(math.e)) * softplus
    for c0, c1, k0, k1 in windows:
        xk = xb[:, k0:k1]
        tr = jnp.tanh(_dot(xk, w_ref[0, k0:k1, c0:c1]) + b_ref[0, :, c0:c1]) + 1.0
        ti = jnp.tanh(_dot(xk, w_ref[0, k0:k1, r + c0:r + c1]) + b_ref[0, :, r + c0:r + c1]) + 1.0
        a = jnp.exp2(exp2_rate[:, c0:c1] * tr)
        a_ref[:, c0:c1] = a
        e = jnp.tanh(neg_rate[:, c0:c1] * tr) * (a * a + 1.0)
        root = jnp.where(e > 0.0, e * lax.rsqrt(e), 0.0)
        bb_ref[:, c0:c1] = root * (ti * half_x[:, c0:c1])

    @pl.when(i == 0)
    def _():
        carry_ref[...] = jnp.zeros_like(carry_ref)

    def step(t, h):
        idx = jnp.where(dr == 0, t, TT - 1 - t)
        h = a_ref[pl.ds(idx, 1), :] * h + bb_ref[pl.ds(idx, 1), :]
        o_ref[0, 0, pl.ds(idx, 1), :] = h
        return h

    carry_ref[0:1, :] = lax.fori_loop(0, TT, step, carry_ref[0:1, :], unroll=8)


def _scan_tile(dr, i, n_lat_tiles, n_tiles):
    fwd = jnp.where(i == 0, n_lat_tiles, i - 1)
    bwd = jnp.where(i == 0, n_lat_tiles, n_lat_tiles - i)
    return jnp.where(dr == 0, fwd, bwd)


def _gate_windows(r, bs):
    out = []
    for c0 in range(0, r, 2 * LANES):
        c1 = min(c0 + 2 * LANES, r)
        k0 = (c0 // bs) * bs // LANES * LANES
        k1 = min(-(-(((c1 - 1) // bs + 1) * bs) // LANES) * LANES, r)
        out.append((c0, c1, k0, k1))
    return tuple(out)


def _rglru(xc, wcat, bcat, lam, *, n_lat_tiles, block_size):
    nb, s, r = xc.shape
    n_tiles = s // TT
    assert n_tiles == n_lat_tiles + 1
    tile = lambda d, i: _scan_tile(d, i, n_lat_tiles, n_tiles)
    return pl.pallas_call(
        functools.partial(_rglru_kernel, r=r, windows=_gate_windows(r, block_size)),
        grid=(nb, 2, n_tiles),
        in_specs=[
            pl.BlockSpec((1, TT, r), lambda b, d, i: (b, tile(d, i), 0)),
            pl.BlockSpec((1, r, 2 * r), lambda b, d, i: (d, 0, 0)),
            pl.BlockSpec((1, 1, 2 * r), lambda b, d, i: (d, 0, 0)),
            pl.BlockSpec((1, 1, r), lambda b, d, i: (d, 0, 0)),
        ],
        out_specs=pl.BlockSpec((1, 1, TT, r), lambda b, d, i: (d, b, tile(d, i), 0)),
        out_shape=jax.ShapeDtypeStruct((2, nb, s, r), F32),
        scratch_shapes=[pltpu.VMEM((TT, r), F32), pltpu.VMEM((TT, r), F32),
                        pltpu.VMEM((SUBLANES, r), F32)],
        compiler_params=_cp("parallel", "parallel", "arbitrary"),
        name="rglru",
    )(xc, wcat, bcat, lam.reshape(2, 1, r))


def _block_diag(w):
    n, bs, _ = w.shape
    eye = jnp.eye(n, dtype=w.dtype)
    return (eye[:, None, :, None] * w[:, :, None, :]).reshape(n * bs, n * bs)


def _gelu_tanh(x):
    return 0.5 * x * (1.0 + jnp.tanh(math.sqrt(2.0 / math.pi) * (x + 0.044715 * (x * x * x))))


def _post_kernel(*refs, kind, final, d, n_lat_tiles):
    if kind == "even":
        h_ref, zl_ref, zc_ref, ol_ref, oc_ref, mod_ref, g2_ref, wo_ref, w1_ref, w2_ref = refs[:10]
        rest = refs[10:]
        is_ctx = pl.program_id(1) >= n_lat_tiles
        z = jnp.where(is_ctx, zc_ref[0], zl_ref[0])
        o = jnp.where(is_ctx, oc_ref[0], ol_ref[0])
        half = z.shape[-1]
        y = _dot(z, wo_ref[:half, :]) + _dot(o, wo_ref[half:, :])
    else:
        h_ref, hd_ref, gate_ref, mod_ref, g2_ref, wo_ref, w1_ref, w2_ref = refs[:8]
        rest = refs[8:]
        mix = (hd_ref[0, 0] + hd_ref[1, 0]) * _gelu_tanh(gate_ref[0].astype(F32))
        y = _dot(mix.astype(BF16), wo_ref[...])
    out_ref = rest[-1]
    m = mod_ref[0]
    h1 = h_ref[0] + m[:, 2 * d:3 * d] * y
    xn = _rms_mod(h1, g2_ref[...], m[:, 3 * d:4 * d], m[:, 4 * d:5 * d]).astype(BF16)
    acc = jnp.zeros_like(h1)
    dff = w1_ref.shape[1]
    for c0 in range(0, dff, FF_CHUNK):
        a = jnp.maximum(_dot(xn, w1_ref[:, c0:c0 + FF_CHUNK]), 0.0)
        acc = acc + _dot((a * a).astype(BF16), w2_ref[c0:c0 + FF_CHUNK, :])
    h2 = h1 + m[:, 5 * d:6 * d] * acc
    if final:
        fg_ref = rest[0]
        h2 = h2 * lax.rsqrt(jnp.mean(h2 * h2, axis=-1, keepdims=True) + EPS) * fg_ref[...]
    out_ref[0] = h2


def _post(kind, h, mix_args, mods, rows, layer, g2, w_out, w1, w2, final_g, *, n_lat_tiles, final):
    nb, s, d = h.shape
    n_tiles = n_lat_tiles if final else s // TT
    tok = lambda w: pl.BlockSpec((1, TT, w), lambda b, i: (b, i, 0))
    full = lambda a: pl.BlockSpec(a.shape, lambda b, i: (0,) * a.ndim)
    if kind == "even":
        lat = lambda w: pl.BlockSpec((1, TT, w), lambda b, i: (b, jnp.minimum(i, n_lat_tiles - 1), 0))
        ctx = lambda w: pl.BlockSpec((1, TT, w), lambda b, i: (b, 0, 0))
        z_lat, z_ctx, o_lat, o_ctx = mix_args
        mix_specs = [lat(z_lat.shape[-1]), ctx(z_ctx.shape[-1]), lat(o_lat.shape[-1]),
                     ctx(o_ctx.shape[-1])]
    else:
        hd, gate = mix_args
        r = gate.shape[-1]
        mix_specs = [pl.BlockSpec((2, 1, TT, r), lambda b, i: (0, b, i, 0)), tok(r)]
    in_specs = [tok(d)] + mix_specs + [
        _mod_spec(layer, rows, nb, n_lat_tiles, mods.shape[-1]),
        pl.BlockSpec((1, d), lambda b, i: (0, 0)), full(w_out), full(w1), full(w2)]
    args = [h, *mix_args, mods, g2, w_out, w1, w2]
    if final:
        in_specs.append(pl.BlockSpec((1, d), lambda b, i: (0, 0)))
        args.append(final_g)
    return pl.pallas_call(
        functools.partial(_post_kernel, kind=kind, final=final, d=d, n_lat_tiles=n_lat_tiles),
        grid=(nb, n_tiles),
        in_specs=in_specs,
        out_specs=tok(d),
        out_shape=jax.ShapeDtypeStruct((nb, n_tiles * TT, d), F32),
        compiler_params=_cp("parallel", "parallel"),
        name="post_" + kind,
    )(*args)


@functools.lru_cache(maxsize=None)
def _rope_tables(l, lc, head_dim):
    axis = head_dim // 2
    freqs = ROPE_BASE ** (-np.arange(0, axis, 2, dtype=np.float64) / axis)
    freqs = freqs.astype(np.float32).astype(np.float64)
    t = np.arange(l)
    ang_r = (t // GRID_W)[:, None] * freqs
    ang_c = (t % GRID_W)[:, None] * freqs
    cos = np.concatenate([np.cos(ang_r)] * 2 + [np.cos(ang_c)] * 2, axis=-1)
    sin = np.concatenate([-np.sin(ang_r), np.sin(ang_r), -np.sin(ang_c), np.sin(ang_c)], axis=-1)
    cos = np.concatenate([cos, np.ones((lc, head_dim))], axis=0)
    sin = np.concatenate([sin, np.zeros((lc, head_dim))], axis=0)
    rep = LANES // head_dim
    return np.tile(cos, (1, rep)), np.tile(sin, (1, rep))


def kernel(x, c, ctx, c_ctx, ada_w, ada_b, norm1_g, norm2_g, mlp_w1, mlp_w2, final_g, ev_w_in, ev_w_out, hy_short_w, hy_short_b, hy_f_w1, hy_f_b1, hy_f_w2, hy_f_b2, hy_f_w3, hy_f_b3, hy_f_freq, hy_f_decay, hy_bias, df_lq1, df_lk1, df_lq2, df_lk2, df_subln_g, od_w_in, od_w_out, rg_conv_w, rg_conv_b, rg_wa, rg_ba, rg_wx, rg_bx, rg_lam):
    nb, l, d = x.shape
    lc = ctx.shape[1]
    depth = ada_w.shape[0]
    hy = hy_bias.shape[-1]
    head_dim = df_lq1.shape[-1]
    qk = (ev_w_in.shape[-1] - 3 * hy) // 3
    r = rg_lam.shape[-1]
    assert l % TT == 0 and lc == TT and l % GRID_W == 0
    n_lat_tiles = l // TT

    rows = -(-(nb + 1) // SUBLANES) * SUBLANES
    cond = jnp.zeros((rows, d), F32).at[:nb].set(c).at[nb].set(c_ctx)
    mods = _ada_mods(cond, ada_w, ada_b).reshape(depth * rows, 1, ada_w.shape[-1])

    cos_t, sin_t = (jnp.asarray(t, F32) for t in _rope_tables(l, lc, head_dim))
    h = jnp.concatenate([x, ctx], axis=1)
    w1_bf, w2_bf = mlp_w1.astype(BF16), mlp_w2.astype(BF16)

    for i in range(depth):
        j = i // 2
        final = i == depth - 1
        g1, g2 = norm1_g[i].reshape(1, d), norm2_g[i].reshape(1, d)
        if i % 2 == 0:
            lam_init = 0.8 - 0.6 * math.exp(-0.3 * i)
            u, qt, k, vt = _ev_inproj(h, mods, rows, i, g1, ev_w_in[j].astype(BF16), cos_t, sin_t,
                                      n_lat_tiles=n_lat_tiles, hy3=3 * hy, qk=qk, head_dim=head_dim,
                                      tk=_attn_chunk(l + lc))
            lam_vecs = [a[j].reshape(1, -1) for a in (df_lq1, df_lk1, df_lq2, df_lk2)]
            o_lat, o_ctx = _diff_attention(qt, k, vt, lam_vecs, df_subln_g[j].reshape(1, -1),
                                           n_lat_tiles=n_lat_tiles, lam_init=lam_init,
                                           head_dim=head_dim)
            fparams = (hy_f_w1[j], hy_f_b1[j], hy_f_w2[j], hy_f_b2[j], hy_f_w3[j], hy_f_b3[j],
                       hy_f_freq[j], hy_f_decay[j])
            vl, x1l, x2l = _shortconv(u, hy_short_w[j], hy_short_b[j], tile0=0,
                                      n_seg_tiles=n_lat_tiles)
            vc, x1c, x2c = _shortconv(u, hy_short_w[j], hy_short_b[j], tile0=n_lat_tiles,
                                      n_seg_tiles=lc // TT)
            z_lat = _hyena_long(vl, x1l, x2l, _hyena_filter(l, math.isqrt(2 * l), *fparams),
                                hy_bias[j])
            z_ctx = _hyena_ctx(vc, x1c, x2c, _hyena_filter(lc, 1, *fparams), hy_bias[j])
            h = _post("even", h, (z_lat, z_ctx, o_lat, o_ctx), mods, rows, i, g2, ev_w_out[j].astype(BF16),
                      w1_bf[i], w2_bf[i], final_g.reshape(1, d), n_lat_tiles=n_lat_tiles,
                      final=final)
        else:
            gate, xc = _od_inproj(h, mods, rows, i, g1, od_w_in[j].astype(BF16), rg_conv_w[j],
                                  rg_conv_b[j], n_lat_tiles=n_lat_tiles)
            wcat = (0.5 * jnp.stack(
                [jnp.concatenate([_block_diag(rg_wa[j, dd]), _block_diag(rg_wx[j, dd])], axis=1)
                 for dd in range(2)])).astype(BF16)
            bcat = 0.5 * jnp.concatenate([rg_ba[j], rg_bx[j]], axis=-1).reshape(2, 1, 2 * r)
            hd = _rglru(xc, wcat, bcat, rg_lam[j], n_lat_tiles=n_lat_tiles,
                        block_size=rg_wa.shape[-1])
            h = _post("odd", h, (hd, gate), mods, rows, i, g2, od_w_out[j].astype(BF16),
                      w1_bf[i], w2_bf[i], final_g.reshape(1, d), n_lat_tiles=n_lat_tiles,
                      final=final)
    return h[:, :l] if h.shape[1] != l else h
```

```python
import functools
import math

import numpy as np
import jax
import jax.numpy as jnp
from jax import lax
from jax.experimental import pallas as pl
from jax.experimental.pallas import tpu as pltpu

F32 = jnp.float32
BF16 = jnp.bfloat16
HIGHEST = lax.Precision.HIGHEST

EPS = 1e-6
GRID_W = 64
ROPE_BASE = 10000.0
N_BANDS = 16
RG_C = 8.0
RG_CONV_LEFT = 2
HY_SHORT_LEFT = 1

TT = 256
LANES = 128
SUBLANES = 8
VMEM_LIMIT = 56 * 1024 * 1024
FF_CHUNK = 1024
FFT_N2_TILE = 8
FFT_K1_TILE = 8
ADA_K_TILE = 256
ATTN_SUB_Q = 512
FILT_COL_BLOCKS = 4
VT_ROWS = LANES + 16


def _cp(*sem):
    return pltpu.CompilerParams(dimension_semantics=sem, vmem_limit_bytes=VMEM_LIMIT)


def _dot(a, b, **kw):
    return jnp.dot(a, b, preferred_element_type=F32, **kw)


def _sigmoid(x):
    return 1.0 / (1.0 + jnp.exp(-x))


def _rms_mod(x, g, shift, scale):
    y = x * lax.rsqrt(jnp.mean(x * x, axis=-1, keepdims=True) + EPS)
    return (y * g) * (1.0 + scale) + shift


def _ada_kernel(c_ref, w_ref, b_ref, o_ref):
    k = pl.program_id(1)
    c = c_ref[k]
    part = _dot((c * _sigmoid(c)).astype(BF16), w_ref[0].astype(BF16))

    @pl.when(k == 0)
    def _():
        o_ref[0] = part + b_ref[0]

    @pl.when(k > 0)
    def _():
        o_ref[0] += part


def _ada_mods(cond, ada_w, ada_b):
    depth, d, n = ada_w.shape
    rows = cond.shape[0]
    tk = min(d, ADA_K_TILE)
    cond_k = cond.reshape(rows, d // tk, tk).transpose(1, 0, 2)
    return pl.pallas_call(
        _ada_kernel,
        grid=(depth, d // tk),
        in_specs=[
            pl.BlockSpec(cond_k.shape, lambda l, k: (0, 0, 0)),
            pl.BlockSpec((1, tk, n), lambda l, k: (l, k, 0)),
            pl.BlockSpec((1, 1, n), lambda l, k: (l, 0, 0)),
        ],
        out_specs=pl.BlockSpec((1, rows, n), lambda l, k: (l, 0, 0)),
        out_shape=jax.ShapeDtypeStruct((depth, rows, n), F32),
        compiler_params=_cp("parallel", "arbitrary"),
        name="ada_mods",
    )(cond_k, ada_w, ada_b.reshape(depth, 1, n))


def _stream_specs(h, n_lat_tiles):
    separate = isinstance(h, tuple)
    lat_arr, ctx_arr = h if separate else (h, h)
    ctx_blk = 0 if separate else n_lat_tiles
    d = lat_arr.shape[-1]
    lat = pl.BlockSpec((1, TT, d), lambda b, i: (b, jnp.minimum(i, n_lat_tiles - 1), 0))
    ctx = pl.BlockSpec((1, TT, d), lambda b, i: (b, ctx_blk, 0))
    return [lat, ctx], [lat_arr, ctx_arr]


def _stream_tile(hl_ref, hc_ref, n_lat_tiles):
    return jnp.where(pl.program_id(1) >= n_lat_tiles, hc_ref[0], hl_ref[0])


def _mod_spec(layer, rows, nb, n_lat_tiles, n6):
    def imap(b, i):
        return (layer * rows + jnp.where(i >= n_lat_tiles, nb, b), 0, 0)
    return pl.BlockSpec((1, 1, n6), imap)


def _ev_inproj_kernel(hl_ref, hc_ref, mod_ref, g_ref, w_ref, cos_ref, sin_ref,
                      u_ref, qt_ref, k_ref, vt_ref, *, d, hy3, qk, qscale, n_lat_tiles):
    m = mod_ref[0]
    x = _stream_tile(hl_ref, hc_ref, n_lat_tiles)
    xn = _rms_mod(x, g_ref[...], m[:, 0:d], m[:, d:2 * d]).astype(BF16)
    y = _dot(xn, w_ref[...])
    u_ref[0] = y[:, :hy3].astype(u_ref.dtype)
    cos = cos_ref[...]
    sin = sin_ref[...]
    lane = lax.broadcasted_iota(jnp.int32, cos.shape, 1)
    first = (lane % 32) < 16

    def rope(z):
        sw = jnp.where(first, pltpu.roll(z, LANES - 16, 1), pltpu.roll(z, 16, 1))
        return z * cos + sw * sin

    extra = vt_ref.shape[3] - LANES
    ones_row = jnp.where(lax.broadcasted_iota(jnp.int32, (extra, cos.shape[0]), 0) == 0,
                         1.0, 0.0).astype(BF16)
    for c in range(qk // LANES):
        lo = c * LANES
        qt_ref[0, c] = (rope(y[:, hy3 + lo:hy3 + lo + LANES]) * qscale).T.astype(BF16)
        k_ref[0, :, lo:lo + LANES] = rope(y[:, hy3 + qk + lo:hy3 + qk + lo + LANES]).astype(BF16)
        vt_ref[0, c, 0, 0:LANES, :] = y[:, hy3 + 2 * qk + lo:hy3 + 2 * qk + lo + LANES].T.astype(BF16)
        vt_ref[0, c, 0, LANES:, :] = ones_row


def _ev_inproj(h, mods, rows, layer, g, w_in, cos_t, sin_t, *, n_lat_tiles, hy3, qk, head_dim, tk):
    h_specs, h_args = _stream_specs(h, n_lat_tiles)
    nb, _, d = h_args[0].shape
    s = (n_lat_tiles + 1) * TT
    n_in = w_in.shape[1]
    assert 2 * head_dim == LANES
    heads, per = qk // LANES, tk // TT
    kern = functools.partial(_ev_inproj_kernel, d=d, hy3=hy3, qk=qk, n_lat_tiles=n_lat_tiles,
                             qscale=head_dim ** -0.5 * math.log2(math.e))
    return pl.pallas_call(
        kern,
        grid=(nb, s // TT),
        in_specs=h_specs + [
            _mod_spec(layer, rows, nb, n_lat_tiles, mods.shape[-1]),
            pl.BlockSpec((1, d), lambda b, i: (0, 0)),
            pl.BlockSpec((d, n_in), lambda b, i: (0, 0)),
            pl.BlockSpec((TT, LANES), lambda b, i: (i, 0)),
            pl.BlockSpec((TT, LANES), lambda b, i: (i, 0)),
        ],
        out_specs=[
            pl.BlockSpec((1, TT, hy3), lambda b, i: (b, i, 0)),
            pl.BlockSpec((1, heads, LANES, TT), lambda b, i: (b, 0, 0, i)),
            pl.BlockSpec((1, TT, qk), lambda b, i: (b, i, 0)),
            pl.BlockSpec((1, heads, 1, VT_ROWS, TT), lambda b, i: (b, 0, i // per, 0, i % per)),
        ],
        out_shape=[
            jax.ShapeDtypeStruct((nb, s, hy3), BF16),
            jax.ShapeDtypeStruct((nb, heads, LANES, s), BF16),
            jax.ShapeDtypeStruct((nb, s, qk), BF16),
            jax.ShapeDtypeStruct((nb, heads, s // tk, VT_ROWS, tk), BF16),
        ],
        compiler_params=_cp("parallel", "parallel"),
        name="ev_inproj",
    )(*h_args, mods, g, w_in, cos_t, sin_t)


def _split_maps(qt, half):
    row = lax.broadcasted_iota(jnp.int32, qt.shape, 0)
    zero = jnp.zeros_like(qt)
    return jnp.where(row < half, qt, zero), jnp.where(row >= half, qt, zero)


def _attn_out(a0, a1, lam_refs, sg_ref, lam_init, dv):
    lq1_ref, lk1_ref, lq2_ref, lk2_ref = lam_refs
    lam = (jnp.exp(jnp.sum(lq1_ref[...] * lk1_ref[...], axis=-1, keepdims=True))
           - jnp.exp(jnp.sum(lq2_ref[...] * lk2_ref[...], axis=-1, keepdims=True)) + lam_init)
    o = (a0[:dv] / a0[dv:dv + 1] - lam * (a1[:dv] / a1[dv:dv + 1])).T
    on = o * lax.rsqrt(jnp.mean(o * o, axis=-1, keepdims=True) + EPS)
    return on * sg_ref[...] * (1.0 - lam_init)


def _col_max8(sc):
    part = sc[0:SUBLANES]
    for g in range(1, sc.shape[0] // SUBLANES):
        part = jnp.maximum(part, sc[g * SUBLANES:(g + 1) * SUBLANES])
    return part


def _attn_kernel(qt_ref, qnt_ref, k_ref, vt_ref, lq1_ref, lk1_ref, lq2_ref, lk2_ref, sg_ref, o_ref,
                 s_ref, mp_ref, p_ref, acc_ref, m_ref, alpha_ref, *, n_chunks, tk, lam_init, half):
    i = pl.program_id(2)
    dv = 2 * half
    lam_refs = (lq1_ref, lk1_ref, lq2_ref, lk2_ref)
    tq = qnt_ref.shape[-1]
    q_sub = (_split_maps(qt_ref[0, 0, :, 0:tq], half), _split_maps(qt_ref[0, 0, :, tq:2 * tq], half))
    q_next = _split_maps(qnt_ref[0, 0], half)

    def put_scores(slot, qts, c):
        kc = k_ref[0, pl.ds(pl.multiple_of(c * tk, tk), tk), :]
        for j in range(2):
            sc = _dot(kc, qts[j])
            s_ref[slot, j] = sc
            mp_ref[slot, j] = _col_max8(sc)

    def put_probs(slot, sub, first):
        for j in range(2):
            col_max = jnp.max(mp_ref[slot, j], axis=0, keepdims=True)
            if first:
                m_new = col_max
            else:
                m_old = m_ref[sub, j]
                m_new = jnp.maximum(m_old, col_max)
                alpha_ref[slot, j] = jnp.exp2(m_old - m_new)
            m_ref[sub, j] = m_new
            p_ref[slot, j] = jnp.exp2(s_ref[slot, j] - m_new).astype(BF16)

    def add_pv(slot, sub, c, first):
        for j in range(2):
            pv = _dot(vt_ref[0, 0, c], p_ref[slot, j])
            acc_ref[sub, j] = pv if first else alpha_ref[slot, j] * acc_ref[sub, j] + pv

    def sub_tile(sub, parity, qts, qts_after, sub_after):
        slot = lambda c: (c + parity) % 2
        put_scores(slot(0), qts, 2)
        put_probs(slot(1), sub, False)
        add_pv(slot(0), sub, 0, True)
        n_uniform = n_chunks - 3

        def pair(u, carry):
            c = 1 + 2 * u
            put_scores(slot(1), qts, c + 2)
            put_probs(slot(0), sub, False)
            add_pv(slot(1), sub, c, False)
            put_scores(slot(0), qts, c + 3)
            put_probs(slot(1), sub, False)
            add_pv(slot(0), sub, c + 1, False)
            return carry

        lax.fori_loop(0, n_uniform // 2, pair, 0)
        c = n_chunks - 2
        put_scores(slot(c), qts_after, 0)
        put_probs(slot(c + 1), sub, False)
        add_pv(slot(c), sub, c, False)
        c = n_chunks - 1
        put_scores(slot(c), qts_after, 1)
        put_probs(slot(c + 1), sub_after, True)
        add_pv(slot(c), sub, c, False)
        o_ref[0, sub * tq:(sub + 1) * tq, :] = _attn_out(
            acc_ref[sub, 0], acc_ref[sub, 1], lam_refs, sg_ref, lam_init, dv).astype(o_ref.dtype)

    @pl.when(i == 0)
    def _():
        put_scores(0, q_sub[0], 0)
        put_probs(0, 0, True)
        put_scores(1, q_sub[0], 1)

    sub_tile(0, 0, q_sub[0], q_sub[1], 1)
    sub_tile(1, 1, q_sub[1], q_next, 0)


def _ctx_attn_kernel(qt_ref, k_ref, vt_ref, lq1_ref, lk1_ref, lq2_ref, lk2_ref, sg_ref, o_ref,
                     *, lam_init, half):
    acc = []
    for qtj in _split_maps(qt_ref[0, 0], half):
        sc = _dot(k_ref[0], qtj)
        col_max = jnp.max(_col_max8(sc), axis=0, keepdims=True)
        acc.append(_dot(vt_ref[0, 0, 0], jnp.exp2(sc - col_max).astype(BF16)))
    o_ref[0] = _attn_out(acc[0], acc[1], (lq1_ref, lk1_ref, lq2_ref, lk2_ref), sg_ref, lam_init,
                         2 * half).astype(o_ref.dtype)


def _attn_chunk(s):
    return next(t for t in (3 * TT, TT) if s % t == 0 and (s // t) % 2 == 1 and s // t >= 5)


def _diff_attention(qt, k, vt, lam_vecs, subln_g, *, n_lat_tiles, lam_init, head_dim):
    nb, s, qk = k.shape
    dv = 2 * head_dim
    heads = qk // dv
    l = n_lat_tiles * TT
    n_chunks, tk = vt.shape[2], vt.shape[4]
    tq = ATTN_SUB_Q if l % (2 * ATTN_SUB_Q) == 0 else TT
    assert l % (2 * tq) == 0 and n_chunks % 2 == 1 and n_chunks >= 5 and s - l == TT
    n_steps = l // (2 * tq)
    vec = lambda n: pl.BlockSpec((1, n), lambda b, h, i: (0, 0))
    vecs = [vec(head_dim)] * 4 + [vec(dv)]
    o_lat = pl.pallas_call(
        functools.partial(_attn_kernel, n_chunks=n_chunks, tk=tk, lam_init=lam_init, half=head_dim),
        grid=(nb, heads, n_steps),
        in_specs=[
            pl.BlockSpec((1, 1, dv, 2 * tq), lambda b, h, i: (b, h, 0, i)),
            pl.BlockSpec((1, 1, dv, tq),
                         lambda b, h, i: (b, h, 0, jnp.minimum(2 * i + 2, 2 * n_steps - 1))),
            pl.BlockSpec((1, s, dv), lambda b, h, i: (b, 0, h)),
            pl.BlockSpec((1, 1, n_chunks, VT_ROWS, tk), lambda b, h, i: (b, h, 0, 0, 0)),
        ] + vecs,
        out_specs=pl.BlockSpec((1, 2 * tq, dv), lambda b, h, i: (b, i, h)),
        out_shape=jax.ShapeDtypeStruct((nb, l, qk), BF16),
        scratch_shapes=[pltpu.VMEM((2, 2, tk, tq), F32), pltpu.VMEM((2, 2, SUBLANES, tq), F32),
                        pltpu.VMEM((2, 2, tk, tq), BF16), pltpu.VMEM((2, 2, VT_ROWS, tq), F32),
                        pltpu.VMEM((2, 2, 1, tq), F32), pltpu.VMEM((2, 2, 1, tq), F32)],
        compiler_params=_cp("parallel", "parallel", "arbitrary"),
        name="diff_attn",
    )(qt, qt, k, vt, *lam_vecs, subln_g)
    per = tk // TT
    vec2 = lambda n: pl.BlockSpec((1, n), lambda b, h: (0, 0))
    o_ctx = pl.pallas_call(
        functools.partial(_ctx_attn_kernel, lam_init=lam_init, half=head_dim),
        grid=(nb, heads),
        in_specs=[pl.BlockSpec((1, 1, dv, TT), lambda b, h: (b, h, 0, n_lat_tiles)),
                  pl.BlockSpec((1, TT, dv), lambda b, h: (b, n_lat_tiles, h)),
                  pl.BlockSpec((1, 1, 1, VT_ROWS, TT),
                               lambda b, h: (b, h, n_lat_tiles // per, 0, n_lat_tiles % per)),
                  ] + [vec2(head_dim)] * 4 + [vec2(dv)],
        out_specs=pl.BlockSpec((1, TT, dv), lambda b, h: (b, 0, h)),
        out_shape=jax.ShapeDtypeStruct((nb, TT, qk), BF16),
        compiler_params=_cp("parallel", "parallel"),
        name="ctx_attn",
    )(qt, k, vt, *lam_vecs, subln_g)
    return o_lat, o_ctx


def _halo_rows(dtype):
    return SUBLANES * 4 // jnp.dtype(dtype).itemsize


def _halo_fill(xp_ref, prev_ref, x_ref, next_ref, has_prev, has_next):
    hr = prev_ref.shape[2]
    zero = jnp.zeros((hr, xp_ref.shape[-1]), F32)
    xp_ref[hr:hr + TT, :] = x_ref[0].astype(F32)
    xp_ref[0:hr, :] = jnp.where(has_prev, prev_ref[0, 0].astype(F32), zero)
    xp_ref[hr + TT:2 * hr + TT, :] = jnp.where(has_next, next_ref[0, 0].astype(F32), zero)


def _conv_taps(xp, w, bias, left):
    rows = xp.shape[0]
    hr = (rows - TT) // 2
    before = None
    for j in range(left):
        z = w[j:j + 1, :] * xp
        before = pltpu.roll(z if before is None else before + z, 1, 0)
    after = None
    for j in range(w.shape[0] - 1, left, -1):
        z = w[j:j + 1, :] * xp
        after = pltpu.roll(z if after is None else after + z, rows - 1, 0)
    acc = w[left:left + 1, :] * xp + bias
    for part in (before, after):
        if part is not None:
            acc = acc + part
    return acc[hr:hr + TT, :]


def _shortconv_kernel(prev_ref, x_ref, next_ref, w_ref, b_ref, v_ref, x1_ref, x2_ref, xp_ref,
                      *, n_seg_tiles, hy):
    i = pl.program_id(1)
    _halo_fill(xp_ref, prev_ref, x_ref, next_ref, i > 0, i < n_seg_tiles - 1)
    y = _conv_taps(xp_ref[...], w_ref[...], b_ref[...], HY_SHORT_LEFT)
    v_ref[0] = y[:, :hy].astype(v_ref.dtype)
    x1_ref[0] = y[:, hy:2 * hy].astype(x1_ref.dtype)
    x2_ref[0] = y[:, 2 * hy:].astype(x2_ref.dtype)


def _shortconv(u, w, b, *, tile0, n_seg_tiles):
    nb, s, hy3 = u.shape
    hy = hy3 // 3
    hr = _halo_rows(u.dtype)
    nh, per = s // hr, TT // hr
    uh = u.reshape(nb, nh, hr, hy3)
    prev = pl.BlockSpec((1, 1, hr, hy3),
                        lambda b, i: (b, jnp.maximum((tile0 + i) * per - 1, 0), 0, 0))
    nxt = pl.BlockSpec((1, 1, hr, hy3),
                       lambda b, i: (b, jnp.minimum((tile0 + i + 1) * per, nh - 1), 0, 0))
    out = jax.ShapeDtypeStruct((nb, n_seg_tiles * TT, hy), BF16)
    ospec = pl.BlockSpec((1, TT, hy), lambda b, i: (b, i, 0))
    return pl.pallas_call(
        functools.partial(_shortconv_kernel, n_seg_tiles=n_seg_tiles, hy=hy),
        grid=(nb, n_seg_tiles),
        in_specs=[
            prev,
            pl.BlockSpec((1, TT, hy3), lambda b, i: (b, tile0 + i, 0)),
            nxt,
            pl.BlockSpec(w.shape, lambda b, i: (0, 0)),
            pl.BlockSpec((1, hy3), lambda b, i: (0, 0)),
        ],
        out_specs=[ospec, ospec, ospec],
        out_shape=[out, out, out],
        scratch_shapes=[pltpu.VMEM((TT + 2 * hr, hy3), F32)],
        compiler_params=_cp("parallel", "parallel"),
        name="hy_shortconv",
    )(uh, u, uh, w, b.reshape(1, hy3))


def _filter_feats(lh, n_cols):
    n = np.arange(2 * lh).reshape(-1, n_cols).T.reshape(-1)
    lag = np.where(n < lh, n, 2 * lh - n).astype(np.float64)
    t = (lag / lh).astype(np.float32).astype(np.float64)
    bands = np.arange(1, N_BANDS + 1, dtype=np.float64)
    ang = 2.0 * math.pi * t[:, None] * bands
    feats = np.concatenate([t[:, None], np.cos(ang), np.sin(ang)], axis=-1)
    pad = (-(feats.shape[1] + 1)) % SUBLANES
    return np.concatenate([feats, np.zeros((2 * lh, pad)), n[:, None].astype(np.float64)], axis=-1)


def _filt_kernel(ft_ref, w1_ref, b1_ref, w2_ref, b2_ref, w3_ref, b3_ref, fr_ref, dec_ref, o_ref,
                 *, lh, c):
    ft = ft_ref[...]
    freq = fr_ref[...]
    h = jnp.sin(freq * (_dot(ft, w1_ref[...], precision=HIGHEST) + b1_ref[...]))
    h = jnp.sin(freq * (_dot(h, w2_ref[...], precision=HIGHEST) + b2_ref[...]))
    h = _dot(h.astype(BF16), w3_ref[...]) + b3_ref[...]
    t = ft[:, 0:1]
    h = h * jnp.exp(-t * jnp.abs(dec_ref[...]))
    row = ft[:, ft.shape[1] - 1:]
    tr = o_ref.shape[1]
    for o in range(2):
        fwd = h[:, (2 * o) * c:(2 * o + 1) * c]
        bwd = h[:, (2 * o + 1) * c:(2 * o + 2) * c]
        sel = jnp.where(row < lh, fwd, jnp.where(row > lh, bwd, jnp.zeros_like(bwd)))
        for q in range(ft.shape[0] // tr):
            o_ref[o, :, q * c:(q + 1) * c] = sel[q * tr:(q + 1) * tr]


def _hyena_filter(lh, n_cols, w1, b1, w2, b2, w3, b3, freq, decay):
    c = decay.shape[-1]
    feats = jnp.asarray(_filter_feats(lh, n_cols), F32)
    fe = feats.shape[1]
    w1p = jnp.pad(w1, ((0, fe - w1.shape[0]), (0, 0)))
    hid = w1.shape[1]
    tr = 2 * lh // n_cols
    cb = math.gcd(n_cols, FILT_COL_BLOCKS)
    full = lambda a: pl.BlockSpec(a.shape, lambda i: (0,) * a.ndim)
    args = (w1p, b1.reshape(1, hid), w2, b2.reshape(1, hid), w3.astype(BF16), b3.reshape(1, -1),
            freq.reshape(1, hid), decay.reshape(1, -1))
    return pl.pallas_call(
        functools.partial(_filt_kernel, lh=lh, c=c),
        grid=(n_cols // cb,),
        in_specs=[pl.BlockSpec((cb * tr, fe), lambda i: (i, 0))] + [full(a) for a in args],
        out_specs=pl.BlockSpec((2, tr, cb * c), lambda i: (0, 0, i)),
        out_shape=jax.ShapeDtypeStruct((2, tr, n_cols * c), F32),
        compiler_params=_cp("parallel"),
        name="hy_filter",
    )(feats, *args)


@functools.lru_cache(maxsize=None)
def _dft_tables(n):
    nn = n * n
    h = n // 2
    k = np.arange(n)
    th = 2.0 * math.pi * np.outer(k, k) / n
    c, s = np.cos(th), np.sin(th)
    f1_data = np.block([[c[:, :h], s[:, :h]], [-s[:, :h], c[:, :h]]])
    f1_real = np.concatenate([c, -s], axis=0)
    idx = (k[None, None, :] * (k[:, None, None] + n * k[None, :, None])) % nn
    phi = 2.0 * math.pi * idx / nn
    cp, sp = np.cos(phi), np.sin(phi)
    g = np.concatenate([np.concatenate([cp, sp], axis=2), np.concatenate([-sp, cp], axis=2)], axis=1)
    hmat = np.transpose(g, (0, 2, 1)) / nn
    ci, si = c[:h, :], s[:h, :]
    f3 = np.zeros((n, 2 * n))
    f3[:h, 0::2], f3[:h, 1::2] = ci, -si
    f3[h:, 0::2], f3[h:, 1::2] = si, ci
    return f1_data, f1_real, g, hmat, f3


def _fft_s1_kernel(x_ref, f_ref, o_ref):
    o_ref[0] = _dot(f_ref[...], x_ref[0].astype(BF16)).astype(o_ref.dtype)


def _fft_s1(x, f1, n, c):
    p = x.shape[0]
    tc = FFT_N2_TILE * c
    return pl.pallas_call(
        _fft_s1_kernel,
        grid=(p, n * c // tc),
        in_specs=[pl.BlockSpec((1, n, tc), lambda q, j: (q, 0, j)),
                  pl.BlockSpec(f1.shape, lambda q, j: (0, 0))],
        out_specs=pl.BlockSpec((1, 2 * n, tc), lambda q, j: (q, 0, j)),
        out_shape=jax.ShapeDtypeStruct((p, 2 * n, n * c), BF16),
        compiler_params=_cp("parallel", "parallel"),
        name="fft_s1",
    )(x, f1)


def _fft_spec_kernel(a_ref, g_ref, o_ref, *, tk):
    for j in range(tk):
        x = jnp.concatenate([a_ref[0, 0, j], a_ref[0, 1, j]], axis=0)
        o_ref[0, j] = _dot(g_ref[j], x).astype(o_ref.dtype)


def _fft_mid_kernel(a_ref, g_ref, h_ref, kh_ref, o_ref, *, tk, n):
    for j in range(tk):
        x = jnp.concatenate([a_ref[0, 0, j], a_ref[0, 1, j]], axis=0)
        t = _dot(g_ref[j], x)
        tr, ti = t[:n], t[n:]
        kr, ki = kh_ref[0, j, :n].astype(F32), kh_ref[0, j, n:].astype(F32)
        y = jnp.concatenate([tr * kr - ti * ki, tr * ki + ti * kr], axis=0).astype(BF16)
        o_ref[0, j] = _dot(h_ref[j], y).astype(o_ref.dtype)


def _fft_spectrum(a, g, n, c):
    p = a.shape[0]
    tk = FFT_K1_TILE
    a5 = a.reshape(p, 2, n, n, c)
    return pl.pallas_call(
        functools.partial(_fft_spec_kernel, tk=tk),
        grid=(p, n // tk),
        in_specs=[pl.BlockSpec((1, 2, tk, n, c), lambda q, j: (q, 0, j, 0, 0)),
                  pl.BlockSpec((tk, 2 * n, 2 * n), lambda q, j: (j, 0, 0))],
        out_specs=pl.BlockSpec((1, tk, 2 * n, c), lambda q, j: (q, j, 0, 0)),
        out_shape=jax.ShapeDtypeStruct((p, n, 2 * n, c), BF16),
        compiler_params=_cp("parallel", "parallel"),
        name="fft_spectrum",
    )(a5, g)


def _fft_mid(a, g, hm, khat, order, n, c):
    p = a.shape[0]
    tk = FFT_K1_TILE
    a5 = a.reshape(p, 2, n, n, c)
    return pl.pallas_call(
        functools.partial(_fft_mid_kernel, tk=tk, n=n),
        grid=(p, n // tk),
        in_specs=[pl.BlockSpec((1, 2, tk, n, c), lambda q, j: (q, 0, j, 0, 0)),
                  pl.BlockSpec((tk, 2 * n, 2 * n), lambda q, j: (j, 0, 0)),
                  pl.BlockSpec((tk, 2 * n, 2 * n), lambda q, j: (j, 0, 0)),
                  pl.BlockSpec((1, tk, 2 * n, c), lambda q, j: (order, j, 0, 0))],
        out_specs=pl.BlockSpec((1, tk, 2 * n, c), lambda q, j: (q, j, 0, 0)),
        out_shape=jax.ShapeDtypeStruct((p, n, 2 * n, c), BF16),
        compiler_params=_cp("parallel", "parallel"),
        name="fft_mid",
    )(a5, g, hm, khat)


def _fft_s3_kernel(*refs, gated, chained):
    c_ref, f_ref, v_ref, b_ref = refs[:4]
    rest = list(refs[4:])
    y = _dot(f_ref[...], c_ref[0]) + v_ref[0].astype(F32) * b_ref[...]
    if gated:
        y = rest.pop(0)[0].astype(F32) * y
    f1_ref = rest.pop(0) if chained else None
    o_ref = rest.pop(0)
    z = y.astype(o_ref.dtype)
    o_ref[0] = z
    if chained:
        a_ref = rest.pop(0)
        a_ref[0] = _dot(f1_ref[...], z).astype(a_ref.dtype)


def _fft_s3(cm, f3, vin, xg, bias, n, c, f1_next=None):
    p = cm.shape[0]
    tc = FFT_N2_TILE * c
    c2 = cm.reshape(p, 2 * n, n * c)
    bias_t = jnp.tile(bias.reshape(1, c), (1, FFT_N2_TILE))
    blk = pl.BlockSpec((1, n, tc), lambda q, j: (q, 0, j))
    full = lambda a: pl.BlockSpec(a.shape, lambda q, j: (0, 0))
    in_specs = [pl.BlockSpec((1, 2 * n, tc), lambda q, j: (q, 0, j)), full(f3), blk,
                pl.BlockSpec((1, tc), lambda q, j: (0, 0))]
    args = [c2, f3, vin, bias_t]
    out_specs, out_shape = [blk], [jax.ShapeDtypeStruct((p, n, n * c), BF16)]
    if xg is not None:
        in_specs.append(blk)
        args.append(xg)
    if f1_next is not None:
        in_specs.append(full(f1_next))
        args.append(f1_next)
        out_specs.append(pl.BlockSpec((1, 2 * n, tc), lambda q, j: (q, 0, j)))
        out_shape.append(jax.ShapeDtypeStruct((p, 2 * n, n * c), BF16))
    return pl.pallas_call(
        functools.partial(_fft_s3_kernel, gated=xg is not None, chained=f1_next is not None),
        grid=(p, n * c // tc),
        in_specs=in_specs,
        out_specs=out_specs,
        out_shape=out_shape,
        compiler_params=_cp("parallel", "parallel"),
        name="fft_s3",
    )(*args)


def _hyena_long(v, x1, kk, hy_bias):
    nb, l, c = v.shape
    n = math.isqrt(2 * l)
    assert n * n == 2 * l and nb % 2 == 0
    p = nb // 2
    f1d, f1r, g, hm, f3 = (jnp.asarray(t, F32).astype(BF16) for t in _dft_tables(n))
    pair = lambda a: a.reshape(p, n, n * c)
    khat = _fft_spectrum(_fft_s1(kk, f1r, n, c), g, n, c)
    v = pair(v)
    cm = _fft_mid(_fft_s1(v, f1d, n, c), g, hm, khat, 0, n, c)
    z1, a = _fft_s3(cm, f3, v, pair(x1), hy_bias[0], n, c, f1_next=f1d)
    cm = _fft_mid(a, g, hm, khat, 1, n, c)
    (w2,) = _fft_s3(cm, f3, z1, None, hy_bias[1], n, c)
    return w2.reshape(nb, l, c)


@functools.lru_cache(maxsize=None)
def _ctx_dft_tables(lc):
    m = 2 * lc
    k = np.arange(m)
    th = 2.0 * math.pi * np.outer(k, k) / m
    c, s = np.cos(th), np.sin(th)
    f_data = np.block([[c[:, :lc], s[:, :lc]], [-s[:, :lc], c[:, :lc]]])
    f_real = np.concatenate([c, -s], axis=0)
    ci, si = c[:lc, :], s[:lc, :]
    f_inv = np.block([[ci, -si], [si, ci]]) / m
    return f_data, f_real, f_inv


def _ctxconv_kernel(v_ref, x1_ref, x2_ref, kk_ref, fd_ref, fr_ref, fi_ref, b_ref, o_ref, *, m):
    def conv(u, order):
        kh = _dot(fr_ref[...], kk_ref[order].astype(BF16))
        t = _dot(fd_ref[...], u.astype(BF16))
        tr, ti, kr, ki = t[:m], t[m:], kh[:m], kh[m:]
        y = jnp.concatenate([tr * kr - ti * ki, tr * ki + ti * kr], axis=0).astype(BF16)
        return _dot(fi_ref[...], y)

    v = v_ref[0].astype(F32)
    z1 = x1_ref[0].astype(F32) * (conv(v, 0) + v * b_ref[0:1, :])
    o_ref[0] = (x2_ref[0].astype(F32) * (conv(z1, 1) + z1 * b_ref[1:2, :])).astype(o_ref.dtype)


def _hyena_ctx(v, x1, x2, kk, hy_bias):
    nb, lc, c = v.shape
    p, m = nb // 2, 2 * lc
    fd, fr, fi = (jnp.asarray(t, F32).astype(BF16) for t in _ctx_dft_tables(lc))
    pair = lambda a: a.reshape(p, m, c)
    blk = pl.BlockSpec((1, m, c), lambda q: (q, 0, 0))
    full = lambda a: pl.BlockSpec(a.shape, lambda q: (0,) * a.ndim)
    z = pl.pallas_call(
        functools.partial(_ctxconv_kernel, m=m),
        grid=(p,),
        in_specs=[blk, blk, blk, full(kk), full(fd), full(fr), full(fi), full(hy_bias)],
        out_specs=blk,
        out_shape=jax.ShapeDtypeStruct((p, m, c), BF16),
        compiler_params=_cp("parallel"),
        name="hy_ctx",
    )(pair(v), pair(x1), pair(x2), kk, fd, fr, fi, hy_bias)
    return z.reshape(nb, lc, c)


def _od_inproj_kernel(prev_ref, h_ref, next_ref, mod_ref, g_ref, w_ref, cw_ref, cb_ref,
                      gate_ref, xc_ref, *, d, r, n_lat_tiles, n_tiles):
    i = pl.program_id(1)
    seg_first = jnp.logical_or(i == 0, i == n_lat_tiles)
    seg_last = jnp.logical_or(i == n_lat_tiles - 1, i == n_tiles - 1)
    m = mod_ref[0]
    rows = jnp.concatenate([prev_ref[0, 0], h_ref[0], next_ref[0, 0]], axis=0)
    xn = _rms_mod(rows, g_ref[...], m[:, 0:d], m[:, d:2 * d]).astype(BF16)
    y = _dot(xn, w_ref[...])
    gate_ref[0] = y[SUBLANES:SUBLANES + TT, :r].astype(gate_ref.dtype)
    x = y[:, r:]
    row = lax.broadcasted_iota(jnp.int32, (x.shape[0], 1), 0)
    outside = jnp.logical_or(jnp.logical_and(row < SUBLANES, seg_first),
                             jnp.logical_and(row >= SUBLANES + TT, seg_last))
    x = jnp.where(outside, 0.0, x)
    xc_ref[0] = _conv_taps(x, cw_ref[...], cb_ref[...], RG_CONV_LEFT)


def _od_inproj(h, mods, rows, layer, g, w_in, conv_w, conv_b, *, n_lat_tiles):
    nb, s, d = h.shape
    r = w_in.shape[1] // 2
    n_tiles = s // TT
    n8, per = s // SUBLANES, TT // SUBLANES
    h8 = h.reshape(nb, n8, SUBLANES, d)
    ospec = pl.BlockSpec((1, TT, r), lambda b, i: (b, i, 0))
    return pl.pallas_call(
        functools.partial(_od_inproj_kernel, d=d, r=r, n_lat_tiles=n_lat_tiles, n_tiles=n_tiles),
        grid=(nb, n_tiles),
        in_specs=[
            pl.BlockSpec((1, 1, SUBLANES, d), lambda b, i: (b, jnp.maximum(i * per - 1, 0), 0, 0)),
            pl.BlockSpec((1, TT, d), lambda b, i: (b, i, 0)),
            pl.BlockSpec((1, 1, SUBLANES, d),
                         lambda b, i: (b, jnp.minimum((i + 1) * per, n8 - 1), 0, 0)),
            _mod_spec(layer, rows, nb, n_lat_tiles, mods.shape[-1]),
            pl.BlockSpec((1, d), lambda b, i: (0, 0)),
            pl.BlockSpec(w_in.shape, lambda b, i: (0, 0)),
            pl.BlockSpec(conv_w.shape, lambda b, i: (0, 0)),
            pl.BlockSpec((1, r), lambda b, i: (0, 0)),
        ],
        out_specs=[ospec, ospec],
        out_shape=[jax.ShapeDtypeStruct((nb, s, r), BF16), jax.ShapeDtypeStruct((nb, s, r), F32)],
        compiler_params=_cp("parallel", "parallel"),
        name="od_inproj",
    )(h8, h, h8, mods, g, w_in, conv_w, conv_b.reshape(1, r))


def _rglru_kernel(x_ref, w_ref, b_ref, lam_ref, o_ref, a_ref, bb_ref, carry_ref, *, r, windows):
    dr = pl.program_id(1)
    i = pl.program_id(2)
    xc = x_ref[0]
    xb = xc.astype(BF16)
    half_x = 0.5 * xc
    nl = -lam_ref[0]
    softplus = jnp.maximum(nl, 0.0) + jnp.log1p(jnp.exp(-jnp.abs(nl)))
    neg_rate = (0.5 * RG_C) * softplus
    exp2_rate = (-0.5 * RG_C * math.log2(math.e)) * softplus
    for c0, c1, k0, k1 in windows:
        xk = xb[:, k0:k1]
        tr = jnp.tanh(_dot(xk, w_ref[0, k0:k1, c0:c1]) + b_ref[0, :, c0:c1]) + 1.0
        ti = jnp.tanh(_dot(xk, w_ref[0, k0:k1, r + c0:r + c1]) + b_ref[0, :, r + c0:r + c1]) + 1.0
        a = jnp.exp2(exp2_rate[:, c0:c1] * tr)
        a_ref[:, c0:c1] = a
        e = jnp.tanh(neg_rate[:, c0:c1] * tr) * (a * a + 1.0)
        root = jnp.where(e > 0.0, e * lax.rsqrt(e), 0.0)
        bb_ref[:, c0:c1] = root * (ti * half_x[:, c0:c1])

    @pl.when(i == 0)
    def _():
        carry_ref[...] = jnp.zeros_like(carry_ref)

    def step(t, h):
        idx = jnp.where(dr == 0, t, TT - 1 - t)
        h = a_ref[pl.ds(idx, 1), :] * h + bb_ref[pl.ds(idx, 1), :]
        o_ref[0, 0, pl.ds(idx, 1), :] = h
        return h

    carry_ref[0:1, :] = lax.fori_loop(0, TT, step, carry_ref[0:1, :], unroll=8)


def _scan_tile(dr, i, n_lat_tiles, n_tiles):
    fwd = jnp.where(i == 0, n_lat_tiles, i - 1)
    bwd = jnp.where(i == 0, n_lat_tiles, n_lat_tiles - i)
    return jnp.where(dr == 0, fwd, bwd)


def _gate_windows(r, bs):
    out = []
    for c0 in range(0, r, 2 * LANES):
        c1 = min(c0 + 2 * LANES, r)
        k0 = (c0 // bs) * bs // LANES * LANES
        k1 = min(-(-(((c1 - 1) // bs + 1) * bs) // LANES) * LANES, r)
        out.append((c0, c1, k0, k1))
    return tuple(out)


def _rglru(xc, wcat, bcat, lam, *, n_lat_tiles, block_size):
    nb, s, r = xc.shape
    n_tiles = s // TT
    assert n_tiles == n_lat_tiles + 1
    tile = lambda d, i: _scan_tile(d, i, n_lat_tiles, n_tiles)
    return pl.pallas_call(
        functools.partial(_rglru_kernel, r=r, windows=_gate_windows(r, block_size)),
        grid=(nb, 2, n_tiles),
        in_specs=[
            pl.BlockSpec((1, TT, r), lambda b, d, i: (b, tile(d, i), 0)),
            pl.BlockSpec((1, r, 2 * r), lambda b, d, i: (d, 0, 0)),
            pl.BlockSpec((1, 1, 2 * r), lambda b, d, i: (d, 0, 0)),
            pl.BlockSpec((1, 1, r), lambda b, d, i: (d, 0, 0)),
        ],
        out_specs=pl.BlockSpec((1, 1, TT, r), lambda b, d, i: (d, b, tile(d, i), 0)),
        out_shape=jax.ShapeDtypeStruct((2, nb, s, r), F32),
        scratch_shapes=[pltpu.VMEM((TT, r), F32), pltpu.VMEM((TT, r), F32),
                        pltpu.VMEM((SUBLANES, r), F32)],
        compiler_params=_cp("parallel", "parallel", "arbitrary"),
        name="rglru",
    )(xc, wcat, bcat, lam.reshape(2, 1, r))


def _block_diag(w):
    n, bs, _ = w.shape
    eye = jnp.eye(n, dtype=w.dtype)
    return (eye[:, None, :, None] * w[:, :, None, :]).reshape(n * bs, n * bs)


def _gelu_tanh(x):
    return 0.5 * x * (1.0 + jnp.tanh(math.sqrt(2.0 / math.pi) * (x + 0.044715 * (x * x * x))))


def _post_kernel(*refs, kind, final, d, n_lat_tiles):
    if kind == "even":
        (hl_ref, hc_ref, x2_ref, wl_ref, zc_ref, ol_ref, oc_ref, mod_ref, g2_ref, wo_ref, w1_ref,
         w2_ref) = refs[:12]
        rest = refs[12:]
        is_ctx = pl.program_id(1) >= n_lat_tiles
        z_lat = (x2_ref[0].astype(F32) * wl_ref[0].astype(F32)).astype(BF16)
        z = jnp.where(is_ctx, zc_ref[0], z_lat)
        o = jnp.where(is_ctx, oc_ref[0], ol_ref[0])
        half = z.shape[-1]
        y = _dot(z, wo_ref[:half, :]) + _dot(o, wo_ref[half:, :])
    else:
        hl_ref, hc_ref, hd_ref, gate_ref, mod_ref, g2_ref, wo_ref, w1_ref, w2_ref = refs[:9]
        rest = refs[9:]
        mix = (hd_ref[0, 0] + hd_ref[1, 0]) * _gelu_tanh(gate_ref[0].astype(F32))
        y = _dot(mix.astype(BF16), wo_ref[...])
    out_ref = rest[-1]
    m = mod_ref[0]
    h1 = _stream_tile(hl_ref, hc_ref, n_lat_tiles) + m[:, 2 * d:3 * d] * y
    xn = _rms_mod(h1, g2_ref[...], m[:, 3 * d:4 * d], m[:, 4 * d:5 * d]).astype(BF16)
    acc = jnp.zeros_like(h1)
    dff = w1_ref.shape[1]
    for c0 in range(0, dff, FF_CHUNK):
        a = jnp.maximum(_dot(xn, w1_ref[:, c0:c0 + FF_CHUNK]), 0.0)
        acc = acc + _dot((a * a).astype(BF16), w2_ref[c0:c0 + FF_CHUNK, :])
    h2 = h1 + m[:, 5 * d:6 * d] * acc
    if final:
        fg_ref = rest[0]
        h2 = h2 * lax.rsqrt(jnp.mean(h2 * h2, axis=-1, keepdims=True) + EPS) * fg_ref[...]
    out_ref[0] = h2


def _post(kind, h, mix_args, mods, rows, layer, g2, w_out, w1, w2, final_g, *, n_lat_tiles, final):
    h_specs, h_args = _stream_specs(h, n_lat_tiles)
    nb, _, d = h_args[0].shape
    n_tiles = n_lat_tiles if final else n_lat_tiles + 1
    tok = lambda w: pl.BlockSpec((1, TT, w), lambda b, i: (b, i, 0))
    full = lambda a: pl.BlockSpec(a.shape, lambda b, i: (0,) * a.ndim)
    if kind == "even":
        lat = lambda w: pl.BlockSpec((1, TT, w), lambda b, i: (b, jnp.minimum(i, n_lat_tiles - 1), 0))
        ctx = lambda w: pl.BlockSpec((1, TT, w), lambda b, i: (b, 0, 0))
        x2_lat, w_lat, z_ctx, o_lat, o_ctx = mix_args
        mix_specs = [lat(x2_lat.shape[-1]), lat(w_lat.shape[-1]), ctx(z_ctx.shape[-1]),
                     lat(o_lat.shape[-1]), ctx(o_ctx.shape[-1])]
    else:
        hd, gate = mix_args
        r = gate.shape[-1]
        mix_specs = [pl.BlockSpec((2, 1, TT, r), lambda b, i: (0, b, i, 0)), tok(r)]
    in_specs = h_specs + mix_specs + [
        _mod_spec(layer, rows, nb, n_lat_tiles, mods.shape[-1]),
        pl.BlockSpec((1, d), lambda b, i: (0, 0)), full(w_out), full(w1), full(w2)]
    args = [*h_args, *mix_args, mods, g2, w_out, w1, w2]
    if final:
        in_specs.append(pl.BlockSpec((1, d), lambda b, i: (0, 0)))
        args.append(final_g)
    return pl.pallas_call(
        functools.partial(_post_kernel, kind=kind, final=final, d=d, n_lat_tiles=n_lat_tiles),
        grid=(nb, n_tiles),
        in_specs=in_specs,
        out_specs=tok(d),
        out_shape=jax.ShapeDtypeStruct((nb, n_tiles * TT, d), F32),
        compiler_params=_cp("parallel", "parallel"),
        name="post_" + kind,
    )(*args)


@functools.lru_cache(maxsize=None)
def _rope_tables(l, lc, head_dim):
    axis = head_dim // 2
    freqs = ROPE_BASE ** (-np.arange(0, axis, 2, dtype=np.float64) / axis)
    freqs = freqs.astype(np.float32).astype(np.float64)
    t = np.arange(l)
    ang_r = (t // GRID_W)[:, None] * freqs
    ang_c = (t % GRID_W)[:, None] * freqs
    cos = np.concatenate([np.cos(ang_r)] * 2 + [np.cos(ang_c)] * 2, axis=-1)
    sin = np.concatenate([-np.sin(ang_r), np.sin(ang_r), -np.sin(ang_c), np.sin(ang_c)], axis=-1)
    cos = np.concatenate([cos, np.ones((lc, head_dim))], axis=0)
    sin = np.concatenate([sin, np.zeros((lc, head_dim))], axis=0)
    rep = LANES // head_dim
    return np.tile(cos, (1, rep)), np.tile(sin, (1, rep))


def kernel(x, c, ctx, c_ctx, ada_w, ada_b, norm1_g, norm2_g, mlp_w1, mlp_w2, final_g, ev_w_in, ev_w_out, hy_short_w, hy_short_b, hy_f_w1, hy_f_b1, hy_f_w2, hy_f_b2, hy_f_w3, hy_f_b3, hy_f_freq, hy_f_decay, hy_bias, df_lq1, df_lk1, df_lq2, df_lk2, df_subln_g, od_w_in, od_w_out, rg_conv_w, rg_conv_b, rg_wa, rg_ba, rg_wx, rg_bx, rg_lam):
    nb, l, d = x.shape
    lc = ctx.shape[1]
    depth = ada_w.shape[0]
    hy = hy_bias.shape[-1]
    head_dim = df_lq1.shape[-1]
    qk = (ev_w_in.shape[-1] - 3 * hy) // 3
    r = rg_lam.shape[-1]
    assert l % TT == 0 and lc == TT and l % GRID_W == 0
    n_lat_tiles = l // TT

    rows = -(-(nb + 1) // SUBLANES) * SUBLANES
    cond = jnp.zeros((rows, d), F32).at[:nb].set(c).at[nb].set(c_ctx)
    mods = _ada_mods(cond, ada_w, ada_b).reshape(depth * rows, 1, ada_w.shape[-1])

    cos_t, sin_t = (jnp.asarray(t, F32) for t in _rope_tables(l, lc, head_dim))
    h = (x, ctx)
    w1_bf, w2_bf = mlp_w1.astype(BF16), mlp_w2.astype(BF16)

    for i in range(depth):
        j = i // 2
        final = i == depth - 1
        g1, g2 = norm1_g[i].reshape(1, d), norm2_g[i].reshape(1, d)
        if i % 2 == 0:
            lam_init = 0.8 - 0.6 * math.exp(-0.3 * i)
            u, qt, k, vt = _ev_inproj(h, mods, rows, i, g1, ev_w_in[j].astype(BF16), cos_t, sin_t,
                                      n_lat_tiles=n_lat_tiles, hy3=3 * hy, qk=qk, head_dim=head_dim,
                                      tk=_attn_chunk(l + lc))
            lam_vecs = [a[j].reshape(1, -1) for a in (df_lq1, df_lk1, df_lq2, df_lk2)]
            o_lat, o_ctx = _diff_attention(qt, k, vt, lam_vecs, df_subln_g[j].reshape(1, -1),
                                           n_lat_tiles=n_lat_tiles, lam_init=lam_init,
                                           head_dim=head_dim)
            fparams = (hy_f_w1[j], hy_f_b1[j], hy_f_w2[j], hy_f_b2[j], hy_f_w3[j], hy_f_b3[j],
                       hy_f_freq[j], hy_f_decay[j])
            vl, x1l, x2l = _shortconv(u, hy_short_w[j], hy_short_b[j], tile0=0,
                                      n_seg_tiles=n_lat_tiles)
            vc, x1c, x2c = _shortconv(u, hy_short_w[j], hy_short_b[j], tile0=n_lat_tiles,
                                      n_seg_tiles=lc // TT)
            w_lat = _hyena_long(vl, x1l, _hyena_filter(l, math.isqrt(2 * l), *fparams), hy_bias[j])
            z_ctx = _hyena_ctx(vc, x1c, x2c, _hyena_filter(lc, 1, *fparams), hy_bias[j])
            h = _post("even", h, (x2l, w_lat, z_ctx, o_lat, o_ctx), mods, rows, i, g2, ev_w_out[j].astype(BF16),
                      w1_bf[i], w2_bf[i], final_g.reshape(1, d), n_lat_tiles=n_lat_tiles,
                      final=final)
        else:
            gate, xc = _od_inproj(h, mods, rows, i, g1, od_w_in[j].astype(BF16), rg_conv_w[j],
                                  rg_conv_b[j], n_lat_tiles=n_lat_tiles)
            wcat = (0.5 * jnp.stack(
                [jnp.concatenate([_block_diag(rg_wa[j, dd]), _block_diag(rg_wx[j, dd])], axis=1)
                 for dd in range(2)])).astype(BF16)
            bcat = 0.5 * jnp.concatenate([rg_ba[j], rg_bx[j]], axis=-1).reshape(2, 1, 2 * r)
            hd = _rglru(xc, wcat, bcat, rg_lam[j], n_lat_tiles=n_lat_tiles,
                        block_size=rg_wa.shape[-1])
            h = _post("odd", h, (hd, gate), mods, rows, i, g2, od_w_out[j].astype(BF16),
                      w1_bf[i], w2_bf[i], final_g.reshape(1, d), n_lat_tiles=n_lat_tiles,
                      final=final)
    return h
```

```python
import functools
import math

import numpy as np
import jax
import jax.numpy as jnp
from jax import lax
from jax.experimental import pallas as pl
from jax.experimental.pallas import tpu as pltpu

F32 = jnp.float32
BF16 = jnp.bfloat16
HIGHEST = lax.Precision.HIGHEST

EPS = 1e-6
GRID_W = 64
ROPE_BASE = 10000.0
N_BANDS = 16
RG_C = 8.0
RG_CONV_LEFT = 2
HY_SHORT_LEFT = 1

TT = 256
LANES = 128
SUBLANES = 8
VMEM_LIMIT = 56 * 1024 * 1024
FF_CHUNK = 1024
FFT_N2_TILE = 8
FFT_K1_TILE = 8
ADA_K_TILE = 256
ATTN_SUB_Q = 1024
FILT_COL_BLOCKS = 4
VT_ROWS = LANES + 16


def _cp(*sem):
    return pltpu.CompilerParams(dimension_semantics=sem, vmem_limit_bytes=VMEM_LIMIT)


def _dot(a, b, **kw):
    return jnp.dot(a, b, preferred_element_type=F32, **kw)


def _sigmoid(x):
    return 1.0 / (1.0 + jnp.exp(-x))


def _rms_mod(x, g, shift, scale):
    y = x * lax.rsqrt(jnp.mean(x * x, axis=-1, keepdims=True) + EPS)
    return (y * g) * (1.0 + scale) + shift


def _ada_kernel(c_ref, w_ref, b_ref, o_ref):
    k = pl.program_id(1)
    c = c_ref[k]
    part = _dot((c * _sigmoid(c)).astype(BF16), w_ref[0].astype(BF16))

    @pl.when(k == 0)
    def _():
        o_ref[0] = part + b_ref[0]

    @pl.when(k > 0)
    def _():
        o_ref[0] += part


def _ada_mods(cond, ada_w, ada_b):
    depth, d, n = ada_w.shape
    rows = cond.shape[0]
    tk = min(d, ADA_K_TILE)
    cond_k = cond.reshape(rows, d // tk, tk).transpose(1, 0, 2)
    return pl.pallas_call(
        _ada_kernel,
        grid=(depth, d // tk),
        in_specs=[
            pl.BlockSpec(cond_k.shape, lambda l, k: (0, 0, 0)),
            pl.BlockSpec((1, tk, n), lambda l, k: (l, k, 0)),
            pl.BlockSpec((1, 1, n), lambda l, k: (l, 0, 0)),
        ],
        out_specs=pl.BlockSpec((1, rows, n), lambda l, k: (l, 0, 0)),
        out_shape=jax.ShapeDtypeStruct((depth, rows, n), F32),
        compiler_params=_cp("parallel", "arbitrary"),
        name="ada_mods",
    )(cond_k, ada_w, ada_b.reshape(depth, 1, n))


def _stream_specs(h, n_lat_tiles):
    separate = isinstance(h, tuple)
    lat_arr, ctx_arr = h if separate else (h, h)
    ctx_blk = 0 if separate else n_lat_tiles
    d = lat_arr.shape[-1]
    lat = pl.BlockSpec((1, TT, d), lambda b, i: (b, jnp.minimum(i, n_lat_tiles - 1), 0))
    ctx = pl.BlockSpec((1, TT, d), lambda b, i: (b, ctx_blk, 0))
    return [lat, ctx], [lat_arr, ctx_arr]


def _stream_tile(hl_ref, hc_ref, n_lat_tiles):
    return jnp.where(pl.program_id(1) >= n_lat_tiles, hc_ref[0], hl_ref[0])


def _mod_spec(layer, rows, nb, n_lat_tiles, n6):
    def imap(b, i):
        return (layer * rows + jnp.where(i >= n_lat_tiles, nb, b), 0, 0)
    return pl.BlockSpec((1, 1, n6), imap)


def _ev_inproj_kernel(hl_ref, hc_ref, mod_ref, g_ref, w_ref, cos_ref, sin_ref,
                      u_ref, qt_ref, k_ref, vt_ref, *, d, hy3, qk, qscale, n_lat_tiles):
    m = mod_ref[0]
    x = _stream_tile(hl_ref, hc_ref, n_lat_tiles)
    xn = _rms_mod(x, g_ref[...], m[:, 0:d], m[:, d:2 * d]).astype(BF16)
    y = _dot(xn, w_ref[...])
    u_ref[0] = y[:, :hy3].astype(u_ref.dtype)
    cos = cos_ref[...]
    sin = sin_ref[...]
    lane = lax.broadcasted_iota(jnp.int32, cos.shape, 1)
    first = (lane % 32) < 16

    def rope(z):
        sw = jnp.where(first, pltpu.roll(z, LANES - 16, 1), pltpu.roll(z, 16, 1))
        return z * cos + sw * sin

    extra = vt_ref.shape[3] - LANES
    ones_row = jnp.where(lax.broadcasted_iota(jnp.int32, (extra, cos.shape[0]), 0) == 0,
                         1.0, 0.0).astype(BF16)
    for c in range(qk // LANES):
        lo = c * LANES
        qt_ref[0, c] = (rope(y[:, hy3 + lo:hy3 + lo + LANES]) * qscale).T.astype(BF16)
        k_ref[0, :, lo:lo + LANES] = rope(y[:, hy3 + qk + lo:hy3 + qk + lo + LANES]).astype(BF16)
        vt_ref[0, c, 0, 0:LANES, :] = y[:, hy3 + 2 * qk + lo:hy3 + 2 * qk + lo + LANES].T.astype(BF16)
        vt_ref[0, c, 0, LANES:, :] = ones_row


def _ev_inproj(h, mods, rows, layer, g, w_in, cos_t, sin_t, *, n_lat_tiles, hy3, qk, head_dim, tk):
    h_specs, h_args = _stream_specs(h, n_lat_tiles)
    nb, _, d = h_args[0].shape
    s = (n_lat_tiles + 1) * TT
    n_in = w_in.shape[1]
    assert 2 * head_dim == LANES
    heads, per = qk // LANES, tk // TT
    kern = functools.partial(_ev_inproj_kernel, d=d, hy3=hy3, qk=qk, n_lat_tiles=n_lat_tiles,
                             qscale=head_dim ** -0.5 * math.log2(math.e))
    return pl.pallas_call(
        kern,
        grid=(nb, s // TT),
        in_specs=h_specs + [
            _mod_spec(layer, rows, nb, n_lat_tiles, mods.shape[-1]),
            pl.BlockSpec((1, d), lambda b, i: (0, 0)),
            pl.BlockSpec((d, n_in), lambda b, i: (0, 0)),
            pl.BlockSpec((TT, LANES), lambda b, i: (i, 0)),
            pl.BlockSpec((TT, LANES), lambda b, i: (i, 0)),
        ],
        out_specs=[
            pl.BlockSpec((1, TT, hy3), lambda b, i: (b, i, 0)),
            pl.BlockSpec((1, heads, LANES, TT), lambda b, i: (b, 0, 0, i)),
            pl.BlockSpec((1, TT, qk), lambda b, i: (b, i, 0)),
            pl.BlockSpec((1, heads, 1, VT_ROWS, TT), lambda b, i: (b, 0, i // per, 0, i % per)),
        ],
        out_shape=[
            jax.ShapeDtypeStruct((nb, s, hy3), BF16),
            jax.ShapeDtypeStruct((nb, heads, LANES, s), BF16),
            jax.ShapeDtypeStruct((nb, s, qk), BF16),
            jax.ShapeDtypeStruct((nb, heads, s // tk, VT_ROWS, tk), BF16),
        ],
        compiler_params=_cp("parallel", "parallel"),
        name="ev_inproj",
    )(*h_args, mods, g, w_in, cos_t, sin_t)


def _split_maps(qt, half):
    row = lax.broadcasted_iota(jnp.int32, qt.shape, 0)
    zero = jnp.zeros_like(qt)
    return jnp.where(row < half, qt, zero), jnp.where(row >= half, qt, zero)


def _attn_out(a0, a1, lam_refs, sg_ref, lam_init, dv):
    lq1_ref, lk1_ref, lq2_ref, lk2_ref = lam_refs
    lam = (jnp.exp(jnp.sum(lq1_ref[...] * lk1_ref[...], axis=-1, keepdims=True))
           - jnp.exp(jnp.sum(lq2_ref[...] * lk2_ref[...], axis=-1, keepdims=True)) + lam_init)
    o = (a0[:dv] / a0[dv:dv + 1] - lam * (a1[:dv] / a1[dv:dv + 1])).T
    on = o * lax.rsqrt(jnp.mean(o * o, axis=-1, keepdims=True) + EPS)
    return on * sg_ref[...] * (1.0 - lam_init)


def _col_max8(sc):
    part = sc[0:SUBLANES]
    for g in range(1, sc.shape[0] // SUBLANES):
        part = jnp.maximum(part, sc[g * SUBLANES:(g + 1) * SUBLANES])
    return part


def _attn_kernel(qt_ref, qnt_ref, k_ref, vt_ref, lq1_ref, lk1_ref, lq2_ref, lk2_ref, sg_ref, o_ref,
                 s_ref, mp_ref, p_ref, acc_ref, m_ref, alpha_ref, *, n_chunks, tk, lam_init, half):
    i = pl.program_id(2)
    dv = 2 * half
    lam_refs = (lq1_ref, lk1_ref, lq2_ref, lk2_ref)
    tq = qnt_ref.shape[-1]
    q_sub = (_split_maps(qt_ref[0, 0, :, 0:tq], half), _split_maps(qt_ref[0, 0, :, tq:2 * tq], half))
    q_next = _split_maps(qnt_ref[0, 0], half)

    def put_scores(slot, qts, c):
        kc = k_ref[0, pl.ds(pl.multiple_of(c * tk, tk), tk), :]
        for j in range(2):
            sc = _dot(kc, qts[j])
            s_ref[slot, j] = sc
            mp_ref[slot, j] = _col_max8(sc)

    def put_probs(slot, sub, first):
        for j in range(2):
            col_max = jnp.max(mp_ref[slot, j], axis=0, keepdims=True)
            if first:
                m_new = col_max
            else:
                m_old = m_ref[sub, j]
                m_new = jnp.maximum(m_old, col_max)
                alpha_ref[slot, j] = jnp.exp2(m_old - m_new)
            m_ref[sub, j] = m_new
            p_ref[slot, j] = jnp.exp2(s_ref[slot, j] - m_new).astype(BF16)

    def add_pv(slot, sub, c, first):
        for j in range(2):
            pv = _dot(vt_ref[0, 0, c], p_ref[slot, j])
            acc_ref[sub, j] = pv if first else alpha_ref[slot, j] * acc_ref[sub, j] + pv

    def sub_tile(sub, parity, qts, qts_after, sub_after):
        slot = lambda c: (c + parity) % 2
        put_scores(slot(0), qts, 2)
        put_probs(slot(1), sub, False)
        add_pv(slot(0), sub, 0, True)
        n_uniform = n_chunks - 3

        unroll = 2

        def group(u, carry):
            c = 1 + unroll * u
            for k in range(unroll):
                put_scores(slot(1 + k), qts, c + k + 2)
                put_probs(slot(k), sub, False)
                add_pv(slot(1 + k), sub, c + k, False)
            return carry

        lax.fori_loop(0, n_uniform // unroll, group, 0)
        c = n_chunks - 2
        put_scores(slot(c), qts_after, 0)
        put_probs(slot(c + 1), sub, False)
        add_pv(slot(c), sub, c, False)
        c = n_chunks - 1
        put_scores(slot(c), qts_after, 1)
        put_probs(slot(c + 1), sub_after, True)
        add_pv(slot(c), sub, c, False)
        o_ref[0, sub * tq:(sub + 1) * tq, :] = _attn_out(
            acc_ref[sub, 0], acc_ref[sub, 1], lam_refs, sg_ref, lam_init, dv).astype(o_ref.dtype)

    @pl.when(i == 0)
    def _():
        put_scores(0, q_sub[0], 0)
        put_probs(0, 0, True)
        put_scores(1, q_sub[0], 1)

    sub_tile(0, 0, q_sub[0], q_sub[1], 1)
    sub_tile(1, 1, q_sub[1], q_next, 0)


def _ctx_attn_kernel(qt_ref, k_ref, vt_ref, lq1_ref, lk1_ref, lq2_ref, lk2_ref, sg_ref, o_ref,
                     *, lam_init, half):
    acc = []
    for qtj in _split_maps(qt_ref[0, 0], half):
        sc = _dot(k_ref[0], qtj)
        col_max = jnp.max(_col_max8(sc), axis=0, keepdims=True)
        acc.append(_dot(vt_ref[0, 0, 0], jnp.exp2(sc - col_max).astype(BF16)))
    o_ref[0] = _attn_out(acc[0], acc[1], (lq1_ref, lk1_ref, lq2_ref, lk2_ref), sg_ref, lam_init,
                         2 * half).astype(o_ref.dtype)


def _attn_chunk(s):
    return next(t for t in (3 * TT, TT) if s % t == 0 and (s // t) % 2 == 1 and s // t >= 5)


def _diff_attention(qt, k, vt, lam_vecs, subln_g, *, n_lat_tiles, lam_init, head_dim):
    nb, s, qk = k.shape
    dv = 2 * head_dim
    heads = qk // dv
    l = n_lat_tiles * TT
    n_chunks, tk = vt.shape[2], vt.shape[4]
    tq = ATTN_SUB_Q if l % (2 * ATTN_SUB_Q) == 0 else TT
    assert l % (2 * tq) == 0 and n_chunks % 2 == 1 and n_chunks >= 5 and s - l == TT
    n_steps = l // (2 * tq)
    vec = lambda n: pl.BlockSpec((1, n), lambda b, h, i: (0, 0))
    vecs = [vec(head_dim)] * 4 + [vec(dv)]
    o_lat = pl.pallas_call(
        functools.partial(_attn_kernel, n_chunks=n_chunks, tk=tk, lam_init=lam_init, half=head_dim),
        grid=(nb, heads, n_steps),
        in_specs=[
            pl.BlockSpec((1, 1, dv, 2 * tq), lambda b, h, i: (b, h, 0, i)),
            pl.BlockSpec((1, 1, dv, tq),
                         lambda b, h, i: (b, h, 0, jnp.minimum(2 * i + 2, 2 * n_steps - 1))),
            pl.BlockSpec((1, s, dv), lambda b, h, i: (b, 0, h)),
            pl.BlockSpec((1, 1, n_chunks, VT_ROWS, tk), lambda b, h, i: (b, h, 0, 0, 0)),
        ] + vecs,
        out_specs=pl.BlockSpec((1, 2 * tq, dv), lambda b, h, i: (b, i, h)),
        out_shape=jax.ShapeDtypeStruct((nb, l, qk), BF16),
        scratch_shapes=[pltpu.VMEM((2, 2, tk, tq), F32), pltpu.VMEM((2, 2, SUBLANES, tq), F32),
                        pltpu.VMEM((2, 2, tk, tq), BF16), pltpu.VMEM((2, 2, VT_ROWS, tq), F32),
                        pltpu.VMEM((2, 2, 1, tq), F32), pltpu.VMEM((2, 2, 1, tq), F32)],
        compiler_params=_cp("parallel", "parallel", "arbitrary"),
        name="diff_attn",
    )(qt, qt, k, vt, *lam_vecs, subln_g)
    per = tk // TT
    vec2 = lambda n: pl.BlockSpec((1, n), lambda b, h: (0, 0))
    o_ctx = pl.pallas_call(
        functools.partial(_ctx_attn_kernel, lam_init=lam_init, half=head_dim),
        grid=(nb, heads),
        in_specs=[pl.BlockSpec((1, 1, dv, TT), lambda b, h: (b, h, 0, n_lat_tiles)),
                  pl.BlockSpec((1, TT, dv), lambda b, h: (b, n_lat_tiles, h)),
                  pl.BlockSpec((1, 1, 1, VT_ROWS, TT),
                               lambda b, h: (b, h, n_lat_tiles // per, 0, n_lat_tiles % per)),
                  ] + [vec2(head_dim)] * 4 + [vec2(dv)],
        out_specs=pl.BlockSpec((1, TT, dv), lambda b, h: (b, 0, h)),
        out_shape=jax.ShapeDtypeStruct((nb, TT, qk), BF16),
        compiler_params=_cp("parallel", "parallel"),
        name="ctx_attn",
    )(qt, k, vt, *lam_vecs, subln_g)
    return o_lat, o_ctx


def _halo_rows(dtype):
    return SUBLANES * 4 // jnp.dtype(dtype).itemsize


def _halo_fill(xp_ref, prev_ref, x_ref, next_ref, has_prev, has_next):
    hr = prev_ref.shape[2]
    zero = jnp.zeros((hr, xp_ref.shape[-1]), F32)
    xp_ref[hr:hr + TT, :] = x_ref[0].astype(F32)
    xp_ref[0:hr, :] = jnp.where(has_prev, prev_ref[0, 0].astype(F32), zero)
    xp_ref[hr + TT:2 * hr + TT, :] = jnp.where(has_next, next_ref[0, 0].astype(F32), zero)


def _conv_taps(xp, w, bias, left):
    rows = xp.shape[0]
    hr = (rows - TT) // 2
    before = None
    for j in range(left):
        z = w[j:j + 1, :] * xp
        before = pltpu.roll(z if before is None else before + z, 1, 0)
    after = None
    for j in range(w.shape[0] - 1, left, -1):
        z = w[j:j + 1, :] * xp
        after = pltpu.roll(z if after is None else after + z, rows - 1, 0)
    acc = w[left:left + 1, :] * xp + bias
    for part in (before, after):
        if part is not None:
            acc = acc + part
    return acc[hr:hr + TT, :]


def _shortconv_kernel(prev_ref, x_ref, next_ref, w_ref, b_ref, v_ref, x1_ref, x2_ref, xp_ref,
                      *, n_seg_tiles, hy):
    i = pl.program_id(1)
    _halo_fill(xp_ref, prev_ref, x_ref, next_ref, i > 0, i < n_seg_tiles - 1)
    y = _conv_taps(xp_ref[...], w_ref[...], b_ref[...], HY_SHORT_LEFT)
    v_ref[0] = y[:, :hy].astype(v_ref.dtype)
    x1_ref[0] = y[:, hy:2 * hy].astype(x1_ref.dtype)
    x2_ref[0] = y[:, 2 * hy:].astype(x2_ref.dtype)


def _shortconv(u, w, b, *, tile0, n_seg_tiles):
    nb, s, hy3 = u.shape
    hy = hy3 // 3
    hr = _halo_rows(u.dtype)
    nh, per = s // hr, TT // hr
    uh = u.reshape(nb, nh, hr, hy3)
    prev = pl.BlockSpec((1, 1, hr, hy3),
                        lambda b, i: (b, jnp.maximum((tile0 + i) * per - 1, 0), 0, 0))
    nxt = pl.BlockSpec((1, 1, hr, hy3),
                       lambda b, i: (b, jnp.minimum((tile0 + i + 1) * per, nh - 1), 0, 0))
    out = jax.ShapeDtypeStruct((nb, n_seg_tiles * TT, hy), BF16)
    ospec = pl.BlockSpec((1, TT, hy), lambda b, i: (b, i, 0))
    return pl.pallas_call(
        functools.partial(_shortconv_kernel, n_seg_tiles=n_seg_tiles, hy=hy),
        grid=(nb, n_seg_tiles),
        in_specs=[
            prev,
            pl.BlockSpec((1, TT, hy3), lambda b, i: (b, tile0 + i, 0)),
            nxt,
            pl.BlockSpec(w.shape, lambda b, i: (0, 0)),
            pl.BlockSpec((1, hy3), lambda b, i: (0, 0)),
        ],
        out_specs=[ospec, ospec, ospec],
        out_shape=[out, out, out],
        scratch_shapes=[pltpu.VMEM((TT + 2 * hr, hy3), F32)],
        compiler_params=_cp("parallel", "parallel"),
        name="hy_shortconv",
    )(uh, u, uh, w, b.reshape(1, hy3))


def _filter_feats(lh, n_cols):
    n = np.arange(2 * lh).reshape(-1, n_cols).T.reshape(-1)
    lag = np.where(n < lh, n, 2 * lh - n).astype(np.float64)
    t = (lag / lh).astype(np.float32).astype(np.float64)
    bands = np.arange(1, N_BANDS + 1, dtype=np.float64)
    ang = 2.0 * math.pi * t[:, None] * bands
    feats = np.concatenate([t[:, None], np.cos(ang), np.sin(ang)], axis=-1)
    pad = (-(feats.shape[1] + 1)) % SUBLANES
    return np.concatenate([feats, np.zeros((2 * lh, pad)), n[:, None].astype(np.float64)], axis=-1)


def _filt_kernel(ft_ref, w1_ref, b1_ref, w2_ref, b2_ref, w3_ref, b3_ref, fr_ref, dec_ref, o_ref,
                 *, lh, c):
    ft = ft_ref[...]
    freq = fr_ref[...]
    h = jnp.sin(freq * (_dot(ft, w1_ref[...], precision=HIGHEST) + b1_ref[...]))
    h = jnp.sin(freq * (_dot(h, w2_ref[...], precision=HIGHEST) + b2_ref[...]))
    h = _dot(h.astype(BF16), w3_ref[...]) + b3_ref[...]
    t = ft[:, 0:1]
    h = h * jnp.exp(-t * jnp.abs(dec_ref[...]))
    row = ft[:, ft.shape[1] - 1:]
    tr = o_ref.shape[1]
    for o in range(2):
        fwd = h[:, (2 * o) * c:(2 * o + 1) * c]
        bwd = h[:, (2 * o + 1) * c:(2 * o + 2) * c]
        sel = jnp.where(row < lh, fwd, jnp.where(row > lh, bwd, jnp.zeros_like(bwd)))
        for q in range(ft.shape[0] // tr):
            o_ref[o, :, q * c:(q + 1) * c] = sel[q * tr:(q + 1) * tr]


def _hyena_filter(lh, n_cols, w1, b1, w2, b2, w3, b3, freq, decay):
    c = decay.shape[-1]
    feats = jnp.asarray(_filter_feats(lh, n_cols), F32)
    fe = feats.shape[1]
    w1p = jnp.pad(w1, ((0, fe - w1.shape[0]), (0, 0)))
    hid = w1.shape[1]
    tr = 2 * lh // n_cols
    cb = math.gcd(n_cols, FILT_COL_BLOCKS)
    full = lambda a: pl.BlockSpec(a.shape, lambda i: (0,) * a.ndim)
    args = (w1p, b1.reshape(1, hid), w2, b2.reshape(1, hid), w3.astype(BF16), b3.reshape(1, -1),
            freq.reshape(1, hid), decay.reshape(1, -1))
    return pl.pallas_call(
        functools.partial(_filt_kernel, lh=lh, c=c),
        grid=(n_cols // cb,),
        in_specs=[pl.BlockSpec((cb * tr, fe), lambda i: (i, 0))] + [full(a) for a in args],
        out_specs=pl.BlockSpec((2, tr, cb * c), lambda i: (0, 0, i)),
        out_shape=jax.ShapeDtypeStruct((2, tr, n_cols * c), F32),
        compiler_params=_cp("parallel"),
        name="hy_filter",
    )(feats, *args)


@functools.lru_cache(maxsize=None)
def _dft_tables(n):
    nn = n * n
    h = n // 2
    k = np.arange(n)
    th = 2.0 * math.pi * np.outer(k, k) / n
    c, s = np.cos(th), np.sin(th)
    f1_data = np.block([[c[:, :h], s[:, :h]], [-s[:, :h], c[:, :h]]])
    f1_real = np.concatenate([c, -s], axis=0)
    idx = (k[None, None, :] * (k[:, None, None] + n * k[None, :, None])) % nn
    phi = 2.0 * math.pi * idx / nn
    cp, sp = np.cos(phi), np.sin(phi)
    g = np.concatenate([np.concatenate([cp, sp], axis=2), np.concatenate([-sp, cp], axis=2)], axis=1)
    hmat = np.transpose(g, (0, 2, 1)) / nn
    ci, si = c[:h, :], s[:h, :]
    f3 = np.zeros((n, 2 * n))
    f3[:h, 0::2], f3[:h, 1::2] = ci, -si
    f3[h:, 0::2], f3[h:, 1::2] = si, ci
    return f1_data, f1_real, g, hmat, f3


def _fft_s1_kernel(x_ref, f_ref, o_ref):
    o_ref[0] = _dot(f_ref[...], x_ref[0].astype(BF16)).astype(o_ref.dtype)


def _fft_s1(x, f1, n, c):
    p = x.shape[0]
    tc = FFT_N2_TILE * c
    return pl.pallas_call(
        _fft_s1_kernel,
        grid=(p, n * c // tc),
        in_specs=[pl.BlockSpec((1, n, tc), lambda q, j: (q, 0, j)),
                  pl.BlockSpec(f1.shape, lambda q, j: (0, 0))],
        out_specs=pl.BlockSpec((1, 2 * n, tc), lambda q, j: (q, 0, j)),
        out_shape=jax.ShapeDtypeStruct((p, 2 * n, n * c), BF16),
        compiler_params=_cp("parallel", "parallel"),
        name="fft_s1",
    )(x, f1)


def _fft_spec_kernel(a_ref, g_ref, o_ref, *, tk):
    for j in range(tk):
        x = jnp.concatenate([a_ref[0, 0, j], a_ref[0, 1, j]], axis=0)
        o_ref[0, j] = _dot(g_ref[j], x).astype(o_ref.dtype)


def _fft_mid_kernel(a_ref, g_ref, h_ref, kh_ref, o_ref, *, tk, n):
    for j in range(tk):
        x = jnp.concatenate([a_ref[0, 0, j], a_ref[0, 1, j]], axis=0)
        t = _dot(g_ref[j], x)
        tr, ti = t[:n], t[n:]
        kr, ki = kh_ref[0, j, :n].astype(F32), kh_ref[0, j, n:].astype(F32)
        y = jnp.concatenate([tr * kr - ti * ki, tr * ki + ti * kr], axis=0).astype(BF16)
        o_ref[0, j] = _dot(h_ref[j], y).astype(o_ref.dtype)


def _fft_spectrum(a, g, n, c):
    p = a.shape[0]
    tk = FFT_K1_TILE
    a5 = a.reshape(p, 2, n, n, c)
    return pl.pallas_call(
        functools.partial(_fft_spec_kernel, tk=tk),
        grid=(p, n // tk),
        in_specs=[pl.BlockSpec((1, 2, tk, n, c), lambda q, j: (q, 0, j, 0, 0)),
                  pl.BlockSpec((tk, 2 * n, 2 * n), lambda q, j: (j, 0, 0))],
        out_specs=pl.BlockSpec((1, tk, 2 * n, c), lambda q, j: (q, j, 0, 0)),
        out_shape=jax.ShapeDtypeStruct((p, n, 2 * n, c), BF16),
        compiler_params=_cp("parallel", "parallel"),
        name="fft_spectrum",
    )(a5, g)


def _fft_mid(a, g, hm, khat, order, n, c):
    p = a.shape[0]
    tk = FFT_K1_TILE
    a5 = a.reshape(p, 2, n, n, c)
    return pl.pallas_call(
        functools.partial(_fft_mid_kernel, tk=tk, n=n),
        grid=(p, n // tk),
        in_specs=[pl.BlockSpec((1, 2, tk, n, c), lambda q, j: (q, 0, j, 0, 0)),
                  pl.BlockSpec((tk, 2 * n, 2 * n), lambda q, j: (j, 0, 0)),
                  pl.BlockSpec((tk, 2 * n, 2 * n), lambda q, j: (j, 0, 0)),
                  pl.BlockSpec((1, tk, 2 * n, c), lambda q, j: (order, j, 0, 0))],
        out_specs=pl.BlockSpec((1, tk, 2 * n, c), lambda q, j: (q, j, 0, 0)),
        out_shape=jax.ShapeDtypeStruct((p, n, 2 * n, c), BF16),
        compiler_params=_cp("parallel", "parallel"),
        name="fft_mid",
    )(a5, g, hm, khat)


def _fft_s3_kernel(*refs, gated, chained):
    c_ref, f_ref, v_ref, b_ref = refs[:4]
    rest = list(refs[4:])
    y = _dot(f_ref[...], c_ref[0]) + v_ref[0].astype(F32) * b_ref[...]
    if gated:
        y = rest.pop(0)[0].astype(F32) * y
    f1_ref = rest.pop(0) if chained else None
    o_ref = rest.pop(0)
    z = y.astype(o_ref.dtype)
    o_ref[0] = z
    if chained:
        a_ref = rest.pop(0)
        a_ref[0] = _dot(f1_ref[...], z).astype(a_ref.dtype)


def _fft_s3(cm, f3, vin, xg, bias, n, c, f1_next=None):
    p = cm.shape[0]
    tc = FFT_N2_TILE * c
    c2 = cm.reshape(p, 2 * n, n * c)
    bias_t = jnp.tile(bias.reshape(1, c), (1, FFT_N2_TILE))
    blk = pl.BlockSpec((1, n, tc), lambda q, j: (q, 0, j))
    full = lambda a: pl.BlockSpec(a.shape, lambda q, j: (0, 0))
    in_specs = [pl.BlockSpec((1, 2 * n, tc), lambda q, j: (q, 0, j)), full(f3), blk,
                pl.BlockSpec((1, tc), lambda q, j: (0, 0))]
    args = [c2, f3, vin, bias_t]
    out_specs, out_shape = [blk], [jax.ShapeDtypeStruct((p, n, n * c), BF16)]
    if xg is not None:
        in_specs.append(blk)
        args.append(xg)
    if f1_next is not None:
        in_specs.append(full(f1_next))
        args.append(f1_next)
        out_specs.append(pl.BlockSpec((1, 2 * n, tc), lambda q, j: (q, 0, j)))
        out_shape.append(jax.ShapeDtypeStruct((p, 2 * n, n * c), BF16))
    return pl.pallas_call(
        functools.partial(_fft_s3_kernel, gated=xg is not None, chained=f1_next is not None),
        grid=(p, n * c // tc),
        in_specs=in_specs,
        out_specs=out_specs,
        out_shape=out_shape,
        compiler_params=_cp("parallel", "parallel"),
        name="fft_s3",
    )(*args)


def _hyena_long(v, x1, kk, hy_bias):
    nb, l, c = v.shape
    n = math.isqrt(2 * l)
    assert n * n == 2 * l and nb % 2 == 0
    p = nb // 2
    f1d, f1r, g, hm, f3 = (jnp.asarray(t, F32).astype(BF16) for t in _dft_tables(n))
    pair = lambda a: a.reshape(p, n, n * c)
    khat = _fft_spectrum(_fft_s1(kk, f1r, n, c), g, n, c)
    v = pair(v)
    cm = _fft_mid(_fft_s1(v, f1d, n, c), g, hm, khat, 0, n, c)
    z1, a = _fft_s3(cm, f3, v, pair(x1), hy_bias[0], n, c, f1_next=f1d)
    cm = _fft_mid(a, g, hm, khat, 1, n, c)
    (w2,) = _fft_s3(cm, f3, z1, None, hy_bias[1], n, c)
    return w2.reshape(nb, l, c)


@functools.lru_cache(maxsize=None)
def _ctx_dft_tables(lc):
    m = 2 * lc
    k = np.arange(m)
    th = 2.0 * math.pi * np.outer(k, k) / m
    c, s = np.cos(th), np.sin(th)
    f_data = np.block([[c[:, :lc], s[:, :lc]], [-s[:, :lc], c[:, :lc]]])
    f_real = np.concatenate([c, -s], axis=0)
    ci, si = c[:lc, :], s[:lc, :]
    f_inv = np.block([[ci, -si], [si, ci]]) / m
    return f_data, f_real, f_inv


def _ctxconv_kernel(v_ref, x1_ref, x2_ref, kk_ref, fd_ref, fr_ref, fi_ref, b_ref, o_ref, *, m):
    def conv(u, order):
        kh = _dot(fr_ref[...], kk_ref[order].astype(BF16))
        t = _dot(fd_ref[...], u.astype(BF16))
        tr, ti, kr, ki = t[:m], t[m:], kh[:m], kh[m:]
        y = jnp.concatenate([tr * kr - ti * ki, tr * ki + ti * kr], axis=0).astype(BF16)
        return _dot(fi_ref[...], y)

    v = v_ref[0].astype(F32)
    z1 = x1_ref[0].astype(F32) * (conv(v, 0) + v * b_ref[0:1, :])
    o_ref[0] = (x2_ref[0].astype(F32) * (conv(z1, 1) + z1 * b_ref[1:2, :])).astype(o_ref.dtype)


def _hyena_ctx(v, x1, x2, kk, hy_bias):
    nb, lc, c = v.shape
    p, m = nb // 2, 2 * lc
    fd, fr, fi = (jnp.asarray(t, F32).astype(BF16) for t in _ctx_dft_tables(lc))
    pair = lambda a: a.reshape(p, m, c)
    blk = pl.BlockSpec((1, m, c), lambda q: (q, 0, 0))
    full = lambda a: pl.BlockSpec(a.shape, lambda q: (0,) * a.ndim)
    z = pl.pallas_call(
        functools.partial(_ctxconv_kernel, m=m),
        grid=(p,),
        in_specs=[blk, blk, blk, full(kk), full(fd), full(fr), full(fi), full(hy_bias)],
        out_specs=blk,
        out_shape=jax.ShapeDtypeStruct((p, m, c), BF16),
        compiler_params=_cp("parallel"),
        name="hy_ctx",
    )(pair(v), pair(x1), pair(x2), kk, fd, fr, fi, hy_bias)
    return z.reshape(nb, lc, c)


def _od_inproj_kernel(prev_ref, h_ref, next_ref, mod_ref, g_ref, w_ref, cw_ref, cb_ref,
                      gate_ref, xc_ref, *, d, r, n_lat_tiles, n_tiles):
    i = pl.program_id(1)
    seg_first = jnp.logical_or(i == 0, i == n_lat_tiles)
    seg_last = jnp.logical_or(i == n_lat_tiles - 1, i == n_tiles - 1)
    m = mod_ref[0]
    rows = jnp.concatenate([prev_ref[0, 0], h_ref[0], next_ref[0, 0]], axis=0)
    xn = _rms_mod(rows, g_ref[...], m[:, 0:d], m[:, d:2 * d]).astype(BF16)
    y = _dot(xn, w_ref[...])
    gate_ref[0] = y[SUBLANES:SUBLANES + TT, :r].astype(gate_ref.dtype)
    x = y[:, r:]
    row = lax.broadcasted_iota(jnp.int32, (x.shape[0], 1), 0)
    outside = jnp.logical_or(jnp.logical_and(row < SUBLANES, seg_first),
                             jnp.logical_and(row >= SUBLANES + TT, seg_last))
    x = jnp.where(outside, 0.0, x)
    xc_ref[0] = _conv_taps(x, cw_ref[...], cb_ref[...], RG_CONV_LEFT)


def _od_inproj(h, mods, rows, layer, g, w_in, conv_w, conv_b, *, n_lat_tiles):
    nb, s, d = h.shape
    r = w_in.shape[1] // 2
    n_tiles = s // TT
    n8, per = s // SUBLANES, TT // SUBLANES
    h8 = h.reshape(nb, n8, SUBLANES, d)
    ospec = pl.BlockSpec((1, TT, r), lambda b, i: (b, i, 0))
    return pl.pallas_call(
        functools.partial(_od_inproj_kernel, d=d, r=r, n_lat_tiles=n_lat_tiles, n_tiles=n_tiles),
        grid=(nb, n_tiles),
        in_specs=[
            pl.BlockSpec((1, 1, SUBLANES, d), lambda b, i: (b, jnp.maximum(i * per - 1, 0), 0, 0)),
            pl.BlockSpec((1, TT, d), lambda b, i: (b, i, 0)),
            pl.BlockSpec((1, 1, SUBLANES, d),
                         lambda b, i: (b, jnp.minimum((i + 1) * per, n8 - 1), 0, 0)),
            _mod_spec(layer, rows, nb, n_lat_tiles, mods.shape[-1]),
            pl.BlockSpec((1, d), lambda b, i: (0, 0)),
            pl.BlockSpec(w_in.shape, lambda b, i: (0, 0)),
            pl.BlockSpec(conv_w.shape, lambda b, i: (0, 0)),
            pl.BlockSpec((1, r), lambda b, i: (0, 0)),
        ],
        out_specs=[ospec, ospec],
        out_shape=[jax.ShapeDtypeStruct((nb, s, r), BF16), jax.ShapeDtypeStruct((nb, s, r), F32)],
        compiler_params=_cp("parallel", "parallel"),
        name="od_inproj",
    )(h8, h, h8, mods, g, w_in, conv_w, conv_b.reshape(1, r))


def _rglru_kernel(x_ref, w_ref, b_ref, lam_ref, o_ref, a_ref, bb_ref, carry_ref, *, r, windows):
    dr = pl.program_id(1)
    i = pl.program_id(2)
    xc = x_ref[0]
    xb = xc.astype(BF16)
    half_x = 0.5 * xc
    nl = -lam_ref[0]
    softplus = jnp.maximum(nl, 0.0) + jnp.log1p(jnp.exp(-jnp.abs(nl)))
    neg_rate = (0.5 * RG_C) * softplus
    exp2_rate = (-0.5 * RG_C * math.log2(math.e)) * softplus
    for c0, c1, k0, k1 in windows:
        xk = xb[:, k0:k1]
        tr = jnp.tanh(_dot(xk, w_ref[0, k0:k1, c0:c1]) + b_ref[0, :, c0:c1]) + 1.0
        ti = jnp.tanh(_dot(xk, w_ref[0, k0:k1, r + c0:r + c1]) + b_ref[0, :, r + c0:r + c1]) + 1.0
        a = jnp.exp2(exp2_rate[:, c0:c1] * tr)
        a_ref[:, c0:c1] = a
        e = jnp.tanh(neg_rate[:, c0:c1] * tr) * (a * a + 1.0)
        root = jnp.where(e > 0.0, e * lax.rsqrt(e), 0.0)
        bb_ref[:, c0:c1] = root * (ti * half_x[:, c0:c1])

    @pl.when(i == 0)
    def _():
        carry_ref[...] = jnp.zeros_like(carry_ref)

    def scan(reverse):
        def block(g, h):
            base = pl.multiple_of((TT // SUBLANES - 1 - g if reverse else g) * SUBLANES, SUBLANES)
            for k in range(SUBLANES):
                row = base + (SUBLANES - 1 - k if reverse else k)
                h = a_ref[pl.ds(row, 1), :] * h + bb_ref[pl.ds(row, 1), :]
                o_ref[0, 0, pl.ds(row, 1), :] = h
            return h

        carry_ref[0:1, :] = lax.fori_loop(0, TT // SUBLANES, block, carry_ref[0:1, :])

    pl.when(dr == 0)(lambda: scan(False))
    pl.when(dr == 1)(lambda: scan(True))


def _scan_tile(dr, i, n_lat_tiles, n_tiles):
    fwd = jnp.where(i == 0, n_lat_tiles, i - 1)
    bwd = jnp.where(i == 0, n_lat_tiles, n_lat_tiles - i)
    return jnp.where(dr == 0, fwd, bwd)


def _gate_windows(r, bs):
    out = []
    for c0 in range(0, r, 2 * LANES):
        c1 = min(c0 + 2 * LANES, r)
        k0 = (c0 // bs) * bs // LANES * LANES
        k1 = min(-(-(((c1 - 1) // bs + 1) * bs) // LANES) * LANES, r)
        out.append((c0, c1, k0, k1))
    return tuple(out)


def _rglru(xc, wcat, bcat, lam, *, n_lat_tiles, block_size):
    nb, s, r = xc.shape
    n_tiles = s // TT
    assert n_tiles == n_lat_tiles + 1
    tile = lambda d, i: _scan_tile(d, i, n_lat_tiles, n_tiles)
    return pl.pallas_call(
        functools.partial(_rglru_kernel, r=r, windows=_gate_windows(r, block_size)),
        grid=(nb, 2, n_tiles),
        in_specs=[
            pl.BlockSpec((1, TT, r), lambda b, d, i: (b, tile(d, i), 0)),
            pl.BlockSpec((1, r, 2 * r), lambda b, d, i: (d, 0, 0)),
            pl.BlockSpec((1, 1, 2 * r), lambda b, d, i: (d, 0, 0)),
            pl.BlockSpec((1, 1, r), lambda b, d, i: (d, 0, 0)),
        ],
        out_specs=pl.BlockSpec((1, 1, TT, r), lambda b, d, i: (d, b, tile(d, i), 0)),
        out_shape=jax.ShapeDtypeStruct((2, nb, s, r), F32),
        scratch_shapes=[pltpu.VMEM((TT, r), F32), pltpu.VMEM((TT, r), F32),
                        pltpu.VMEM((SUBLANES, r), F32)],
        compiler_params=_cp("parallel", "parallel", "arbitrary"),
        name="rglru",
    )(xc, wcat, bcat, lam.reshape(2, 1, r))


def _block_diag(w):
    n, bs, _ = w.shape
    eye = jnp.eye(n, dtype=w.dtype)
    return (eye[:, None, :, None] * w[:, :, None, :]).reshape(n * bs, n * bs)


def _gelu_tanh(x):
    return 0.5 * x * (1.0 + jnp.tanh(math.sqrt(2.0 / math.pi) * (x + 0.044715 * (x * x * x))))


def _post_kernel(*refs, kind, final, d, n_lat_tiles):
    if kind == "even":
        (hl_ref, hc_ref, x2_ref, wl_ref, zc_ref, ol_ref, oc_ref, mod_ref, g2_ref, wo_ref, w1_ref,
         w2_ref) = refs[:12]
        rest = refs[12:]
        is_ctx = pl.program_id(1) >= n_lat_tiles
        z_lat = (x2_ref[0].astype(F32) * wl_ref[0].astype(F32)).astype(BF16)
        z = jnp.where(is_ctx, zc_ref[0], z_lat)
        o = jnp.where(is_ctx, oc_ref[0], ol_ref[0])
        half = z.shape[-1]
        y = _dot(z, wo_ref[:half, :]) + _dot(o, wo_ref[half:, :])
    else:
        hl_ref, hc_ref, hd_ref, gate_ref, mod_ref, g2_ref, wo_ref, w1_ref, w2_ref = refs[:9]
        rest = refs[9:]
        mix = (hd_ref[0, 0] + hd_ref[1, 0]) * _gelu_tanh(gate_ref[0].astype(F32))
        y = _dot(mix.astype(BF16), wo_ref[...])
    out_ref = rest[-1]
    m = mod_ref[0]
    h1 = _stream_tile(hl_ref, hc_ref, n_lat_tiles) + m[:, 2 * d:3 * d] * y
    xn = _rms_mod(h1, g2_ref[...], m[:, 3 * d:4 * d], m[:, 4 * d:5 * d]).astype(BF16)
    acc = jnp.zeros_like(h1)
    dff = w1_ref.shape[1]
    for c0 in range(0, dff, FF_CHUNK):
        a = jnp.maximum(_dot(xn, w1_ref[:, c0:c0 + FF_CHUNK]), 0.0)
        acc = acc + _dot((a * a).astype(BF16), w2_ref[c0:c0 + FF_CHUNK, :])
    h2 = h1 + m[:, 5 * d:6 * d] * acc
    if final:
        fg_ref = rest[0]
        h2 = h2 * lax.rsqrt(jnp.mean(h2 * h2, axis=-1, keepdims=True) + EPS) * fg_ref[...]
    out_ref[0] = h2


def _post(kind, h, mix_args, mods, rows, layer, g2, w_out, w1, w2, final_g, *, n_lat_tiles, final):
    h_specs, h_args = _stream_specs(h, n_lat_tiles)
    nb, _, d = h_args[0].shape
    n_tiles = n_lat_tiles if final else n_lat_tiles + 1
    tok = lambda w: pl.BlockSpec((1, TT, w), lambda b, i: (b, i, 0))
    full = lambda a: pl.BlockSpec(a.shape, lambda b, i: (0,) * a.ndim)
    if kind == "even":
        lat = lambda w: pl.BlockSpec((1, TT, w), lambda b, i: (b, jnp.minimum(i, n_lat_tiles - 1), 0))
        ctx = lambda w: pl.BlockSpec((1, TT, w), lambda b, i: (b, 0, 0))
        x2_lat, w_lat, z_ctx, o_lat, o_ctx = mix_args
        mix_specs = [lat(x2_lat.shape[-1]), lat(w_lat.shape[-1]), ctx(z_ctx.shape[-1]),
                     lat(o_lat.shape[-1]), ctx(o_ctx.shape[-1])]
    else:
        hd, gate = mix_args
        r = gate.shape[-1]
        mix_specs = [pl.BlockSpec((2, 1, TT, r), lambda b, i: (0, b, i, 0)), tok(r)]
    in_specs = h_specs + mix_specs + [
        _mod_spec(layer, rows, nb, n_lat_tiles, mods.shape[-1]),
        pl.BlockSpec((1, d), lambda b, i: (0, 0)), full(w_out), full(w1), full(w2)]
    args = [*h_args, *mix_args, mods, g2, w_out, w1, w2]
    if final:
        in_specs.append(pl.BlockSpec((1, d), lambda b, i: (0, 0)))
        args.append(final_g)
    return pl.pallas_call(
        functools.partial(_post_kernel, kind=kind, final=final, d=d, n_lat_tiles=n_lat_tiles),
        grid=(nb, n_tiles),
        in_specs=in_specs,
        out_specs=tok(d),
        out_shape=jax.ShapeDtypeStruct((nb, n_tiles * TT, d), F32),
        compiler_params=_cp("parallel", "parallel"),
        name="post_" + kind,
    )(*args)


@functools.lru_cache(maxsize=None)
def _rope_tables(l, lc, head_dim):
    axis = head_dim // 2
    freqs = ROPE_BASE ** (-np.arange(0, axis, 2, dtype=np.float64) / axis)
    freqs = freqs.astype(np.float32).astype(np.float64)
    t = np.arange(l)
    ang_r = (t // GRID_W)[:, None] * freqs
    ang_c = (t % GRID_W)[:, None] * freqs
    cos = np.concatenate([np.cos(ang_r)] * 2 + [np.cos(ang_c)] * 2, axis=-1)
    sin = np.concatenate([-np.sin(ang_r), np.sin(ang_r), -np.sin(ang_c), np.sin(ang_c)], axis=-1)
    cos = np.concatenate([cos, np.ones((lc, head_dim))], axis=0)
    sin = np.concatenate([sin, np.zeros((lc, head_dim))], axis=0)
    rep = LANES // head_dim
    return np.tile(cos, (1, rep)), np.tile(sin, (1, rep))


def kernel(x, c, ctx, c_ctx, ada_w, ada_b, norm1_g, norm2_g, mlp_w1, mlp_w2, final_g, ev_w_in, ev_w_out, hy_short_w, hy_short_b, hy_f_w1, hy_f_b1, hy_f_w2, hy_f_b2, hy_f_w3, hy_f_b3, hy_f_freq, hy_f_decay, hy_bias, df_lq1, df_lk1, df_lq2, df_lk2, df_subln_g, od_w_in, od_w_out, rg_conv_w, rg_conv_b, rg_wa, rg_ba, rg_wx, rg_bx, rg_lam):
    nb, l, d = x.shape
    lc = ctx.shape[1]
    depth = ada_w.shape[0]
    hy = hy_bias.shape[-1]
    head_dim = df_lq1.shape[-1]
    qk = (ev_w_in.shape[-1] - 3 * hy) // 3
    r = rg_lam.shape[-1]
    assert l % TT == 0 and lc == TT and l % GRID_W == 0
    n_lat_tiles = l // TT

    rows = -(-(nb + 1) // SUBLANES) * SUBLANES
    cond = jnp.zeros((rows, d), F32).at[:nb].set(c).at[nb].set(c_ctx)
    mods = _ada_mods(cond, ada_w, ada_b).reshape(depth * rows, 1, ada_w.shape[-1])

    cos_t, sin_t = (jnp.asarray(t, F32) for t in _rope_tables(l, lc, head_dim))
    h = (x, ctx)
    w1_bf, w2_bf = mlp_w1.astype(BF16), mlp_w2.astype(BF16)

    for i in range(depth):
        j = i // 2
        final = i == depth - 1
        g1, g2 = norm1_g[i].reshape(1, d), norm2_g[i].reshape(1, d)
        if i % 2 == 0:
            lam_init = 0.8 - 0.6 * math.exp(-0.3 * i)
            u, qt, k, vt = _ev_inproj(h, mods, rows, i, g1, ev_w_in[j].astype(BF16), cos_t, sin_t,
                                      n_lat_tiles=n_lat_tiles, hy3=3 * hy, qk=qk, head_dim=head_dim,
                                      tk=_attn_chunk(l + lc))
            lam_vecs = [a[j].reshape(1, -1) for a in (df_lq1, df_lk1, df_lq2, df_lk2)]
            o_lat, o_ctx = _diff_attention(qt, k, vt, lam_vecs, df_subln_g[j].reshape(1, -1),
                                           n_lat_tiles=n_lat_tiles, lam_init=lam_init,
                                           head_dim=head_dim)
            fparams = (hy_f_w1[j], hy_f_b1[j], hy_f_w2[j], hy_f_b2[j], hy_f_w3[j], hy_f_b3[j],
                       hy_f_freq[j], hy_f_decay[j])
            vl, x1l, x2l = _shortconv(u, hy_short_w[j], hy_short_b[j], tile0=0,
                                      n_seg_tiles=n_lat_tiles)
            vc, x1c, x2c = _shortconv(u, hy_short_w[j], hy_short_b[j], tile0=n_lat_tiles,
                                      n_seg_tiles=lc // TT)
            w_lat = _hyena_long(vl, x1l, _hyena_filter(l, math.isqrt(2 * l), *fparams), hy_bias[j])
            z_ctx = _hyena_ctx(vc, x1c, x2c, _hyena_filter(lc, 1, *fparams), hy_bias[j])
            h = _post("even", h, (x2l, w_lat, z_ctx, o_lat, o_ctx), mods, rows, i, g2, ev_w_out[j].astype(BF16),
                      w1_bf[i], w2_bf[i], final_g.reshape(1, d), n_lat_tiles=n_lat_tiles,
                      final=final)
        else:
            gate, xc = _od_inproj(h, mods, rows, i, g1, od_w_in[j].astype(BF16), rg_conv_w[j],
                                  rg_conv_b[j], n_lat_tiles=n_lat_tiles)
            wcat = (0.5 * jnp.stack(
                [jnp.concatenate([_block_diag(rg_wa[j, dd]), _block_diag(rg_wx[j, dd])], axis=1)
                 for dd in range(2)])).astype(BF16)
            bcat = 0.5 * jnp.concatenate([rg_ba[j], rg_bx[j]], axis=-1).reshape(2, 1, 2 * r)
            hd = _rglru(xc, wcat, bcat, rg_lam[j], n_lat_tiles=n_lat_tiles,
                        block_size=rg_wa.shape[-1])
            h = _post("odd", h, (hd, gate), mods, rows, i, g2, od_w_out[j].astype(BF16),
                      w1_bf[i], w2_bf[i], final_g.reshape(1, d), n_lat_tiles=n_lat_tiles,
                      final=final)
    return h
```

```python
import functools
import math

import numpy as np
import jax
import jax.numpy as jnp
from jax import lax
from jax.experimental import pallas as pl
from jax.experimental.pallas import tpu as pltpu

F32 = jnp.float32
BF16 = jnp.bfloat16
HIGHEST = lax.Precision.HIGHEST

EPS = 1e-6
GRID_W = 64
ROPE_BASE = 10000.0
N_BANDS = 16
RG_C = 8.0
RG_CONV_LEFT = 2
HY_SHORT_LEFT = 1

TT = 256
LANES = 128
SUBLANES = 8
VMEM_LIMIT = 56 * 1024 * 1024
FF_CHUNK = 1024
FFT_N2_TILE = 8
FFT_K1_TILE = 8
ADA_K_TILE = 256
ATTN_SUB_Q = 512
FILT_COL_BLOCKS = 4
VT_ROWS = LANES + 16


def _cp(*sem):
    return pltpu.CompilerParams(dimension_semantics=sem, vmem_limit_bytes=VMEM_LIMIT)


def _dot(a, b, **kw):
    return jnp.dot(a, b, preferred_element_type=F32, **kw)


def _sigmoid(x):
    return 1.0 / (1.0 + jnp.exp(-x))


def _rms_mod(x, g, shift, scale):
    y = x * lax.rsqrt(jnp.mean(x * x, axis=-1, keepdims=True) + EPS)
    return (y * g) * (1.0 + scale) + shift


def _ada_kernel(c_ref, w_ref, b_ref, o_ref):
    k = pl.program_id(1)
    c = c_ref[k]
    part = _dot((c * _sigmoid(c)).astype(BF16), w_ref[0].astype(BF16))

    @pl.when(k == 0)
    def _():
        o_ref[0] = part + b_ref[0]

    @pl.when(k > 0)
    def _():
        o_ref[0] += part


def _ada_mods(cond, ada_w, ada_b):
    depth, d, n = ada_w.shape
    rows = cond.shape[0]
    tk = min(d, ADA_K_TILE)
    cond_k = cond.reshape(rows, d // tk, tk).transpose(1, 0, 2)
    return pl.pallas_call(
        _ada_kernel,
        grid=(depth, d // tk),
        in_specs=[
            pl.BlockSpec(cond_k.shape, lambda l, k: (0, 0, 0)),
            pl.BlockSpec((1, tk, n), lambda l, k: (l, k, 0)),
            pl.BlockSpec((1, 1, n), lambda l, k: (l, 0, 0)),
        ],
        out_specs=pl.BlockSpec((1, rows, n), lambda l, k: (l, 0, 0)),
        out_shape=jax.ShapeDtypeStruct((depth, rows, n), F32),
        compiler_params=_cp("parallel", "arbitrary"),
        name="ada_mods",
    )(cond_k, ada_w, ada_b.reshape(depth, 1, n))


def _stream_specs(h, n_lat_tiles):
    separate = isinstance(h, tuple)
    lat_arr, ctx_arr = h if separate else (h, h)
    ctx_blk = 0 if separate else n_lat_tiles
    d = lat_arr.shape[-1]
    lat = pl.BlockSpec((1, TT, d), lambda b, i: (b, jnp.minimum(i, n_lat_tiles - 1), 0))
    ctx = pl.BlockSpec((1, TT, d), lambda b, i: (b, ctx_blk, 0))
    return [lat, ctx], [lat_arr, ctx_arr]


def _stream_tile(hl_ref, hc_ref, n_lat_tiles):
    return jnp.where(pl.program_id(1) >= n_lat_tiles, hc_ref[0], hl_ref[0])


def _mod_spec(layer, rows, nb, n_lat_tiles, n6):
    def imap(b, i):
        return (layer * rows + jnp.where(i >= n_lat_tiles, nb, b), 0, 0)
    return pl.BlockSpec((1, 1, n6), imap)


def _ev_inproj_kernel(hl_ref, hc_ref, mod_ref, g_ref, w_ref, cos_ref, sin_ref,
                      u_ref, qt_ref, k_ref, vt_ref, *, d, hy3, qk, qscale, n_lat_tiles):
    m = mod_ref[0]
    x = _stream_tile(hl_ref, hc_ref, n_lat_tiles)
    xn = _rms_mod(x, g_ref[...], m[:, 0:d], m[:, d:2 * d]).astype(BF16)
    y = _dot(xn, w_ref[...])
    u_ref[0] = y[:, :hy3].astype(u_ref.dtype)
    cos = cos_ref[...]
    sin = sin_ref[...]
    lane = lax.broadcasted_iota(jnp.int32, cos.shape, 1)
    first = (lane % 32) < 16

    def rope(z):
        sw = jnp.where(first, pltpu.roll(z, LANES - 16, 1), pltpu.roll(z, 16, 1))
        return z * cos + sw * sin

    extra = vt_ref.shape[3] - LANES
    ones_row = jnp.where(lax.broadcasted_iota(jnp.int32, (extra, cos.shape[0]), 0) == 0,
                         1.0, 0.0).astype(BF16)
    for c in range(qk // LANES):
        lo = c * LANES
        qt_ref[0, c] = (rope(y[:, hy3 + lo:hy3 + lo + LANES]) * qscale).T.astype(BF16)
        k_ref[0, :, lo:lo + LANES] = rope(y[:, hy3 + qk + lo:hy3 + qk + lo + LANES]).astype(BF16)
        vt_ref[0, c, 0, 0:LANES, :] = y[:, hy3 + 2 * qk + lo:hy3 + 2 * qk + lo + LANES].T.astype(BF16)
        vt_ref[0, c, 0, LANES:, :] = ones_row


def _ev_inproj(h, mods, rows, layer, g, w_in, cos_t, sin_t, *, n_lat_tiles, hy3, qk, head_dim, tk):
    h_specs, h_args = _stream_specs(h, n_lat_tiles)
    nb, _, d = h_args[0].shape
    s = (n_lat_tiles + 1) * TT
    n_in = w_in.shape[1]
    assert 2 * head_dim == LANES
    heads, per = qk // LANES, tk // TT
    kern = functools.partial(_ev_inproj_kernel, d=d, hy3=hy3, qk=qk, n_lat_tiles=n_lat_tiles,
                             qscale=head_dim ** -0.5 * math.log2(math.e))
    return pl.pallas_call(
        kern,
        grid=(nb, s // TT),
        in_specs=h_specs + [
            _mod_spec(layer, rows, nb, n_lat_tiles, mods.shape[-1]),
            pl.BlockSpec((1, d), lambda b, i: (0, 0)),
            pl.BlockSpec((d, n_in), lambda b, i: (0, 0)),
            pl.BlockSpec((TT, LANES), lambda b, i: (i, 0)),
            pl.BlockSpec((TT, LANES), lambda b, i: (i, 0)),
        ],
        out_specs=[
            pl.BlockSpec((1, TT, hy3), lambda b, i: (b, i, 0)),
            pl.BlockSpec((1, heads, LANES, TT), lambda b, i: (b, 0, 0, i)),
            pl.BlockSpec((1, TT, qk), lambda b, i: (b, i, 0)),
            pl.BlockSpec((1, heads, 1, VT_ROWS, TT), lambda b, i: (b, 0, i // per, 0, i % per)),
        ],
        out_shape=[
            jax.ShapeDtypeStruct((nb, s, hy3), BF16),
            jax.ShapeDtypeStruct((nb, heads, LANES, s), BF16),
            jax.ShapeDtypeStruct((nb, s, qk), BF16),
            jax.ShapeDtypeStruct((nb, heads, s // tk, VT_ROWS, tk), BF16),
        ],
        compiler_params=_cp("parallel", "parallel"),
        name="ev_inproj",
    )(*h_args, mods, g, w_in, cos_t, sin_t)


def _split_maps(qt, half):
    row = lax.broadcasted_iota(jnp.int32, qt.shape, 0)
    zero = jnp.zeros_like(qt)
    return jnp.where(row < half, qt, zero), jnp.where(row >= half, qt, zero)


def _attn_out(a0, a1, lam_refs, sg_ref, lam_init, dv):
    lq1_ref, lk1_ref, lq2_ref, lk2_ref = lam_refs
    lam = (jnp.exp(jnp.sum(lq1_ref[...] * lk1_ref[...], axis=-1, keepdims=True))
           - jnp.exp(jnp.sum(lq2_ref[...] * lk2_ref[...], axis=-1, keepdims=True)) + lam_init)
    o = (a0[:dv] / a0[dv:dv + 1] - lam * (a1[:dv] / a1[dv:dv + 1])).T
    on = o * lax.rsqrt(jnp.mean(o * o, axis=-1, keepdims=True) + EPS)
    return on * sg_ref[...] * (1.0 - lam_init)


def _col_max8(sc):
    part = sc[0:SUBLANES]
    for g in range(1, sc.shape[0] // SUBLANES):
        part = jnp.maximum(part, sc[g * SUBLANES:(g + 1) * SUBLANES])
    return part


def _attn_kernel(qt_ref, qnt_ref, k_ref, vt_ref, lq1_ref, lk1_ref, lq2_ref, lk2_ref, sg_ref, o_ref,
                 s_ref, mp_ref, p_ref, acc_ref, m_ref, alpha_ref, *, n_chunks, tk, lam_init, half):
    i = pl.program_id(2)
    dv = 2 * half
    lam_refs = (lq1_ref, lk1_ref, lq2_ref, lk2_ref)
    tq = qnt_ref.shape[-1]
    q_sub = (_split_maps(qt_ref[0, 0, :, 0:tq], half), _split_maps(qt_ref[0, 0, :, tq:2 * tq], half))
    q_next = _split_maps(qnt_ref[0, 0], half)

    def put_scores(slot, qts, c):
        kc = k_ref[0, pl.ds(pl.multiple_of(c * tk, tk), tk), :]
        for j in range(2):
            sc = _dot(kc, qts[j])
            s_ref[slot, j] = sc
            mp_ref[slot, j] = _col_max8(sc)

    def put_probs(slot, sub, first):
        for j in range(2):
            col_max = jnp.max(mp_ref[slot, j], axis=0, keepdims=True)
            if first:
                m_new = col_max
            else:
                m_old = m_ref[sub, j]
                m_new = jnp.maximum(m_old, col_max)
                alpha_ref[slot, j] = jnp.exp2(m_old - m_new)
            m_ref[sub, j] = m_new
            p_ref[slot, j] = jnp.exp2(s_ref[slot, j] - m_new).astype(BF16)

    def add_pv(slot, sub, c, first):
        for j in range(2):
            pv = _dot(vt_ref[0, 0, c], p_ref[slot, j])
            acc_ref[sub, j] = pv if first else alpha_ref[slot, j] * acc_ref[sub, j] + pv

    def sub_tile(sub, parity, qts, qts_after, sub_after):
        slot = lambda c: (c + parity) % 2
        put_scores(slot(0), qts, 2)
        put_probs(slot(1), sub, False)
        add_pv(slot(0), sub, 0, True)
        n_uniform = n_chunks - 3

        unroll = 2

        def group(u, carry):
            c = 1 + unroll * u
            for k in range(unroll):
                put_scores(slot(1 + k), qts, c + k + 2)
                put_probs(slot(k), sub, False)
                add_pv(slot(1 + k), sub, c + k, False)
            return carry

        lax.fori_loop(0, n_uniform // unroll, group, 0)
        c = n_chunks - 2
        put_scores(slot(c), qts_after, 0)
        put_probs(slot(c + 1), sub, False)
        add_pv(slot(c), sub, c, False)
        c = n_chunks - 1
        put_scores(slot(c), qts_after, 1)
        put_probs(slot(c + 1), sub_after, True)
        add_pv(slot(c), sub, c, False)
        o_ref[0, sub * tq:(sub + 1) * tq, :] = _attn_out(
            acc_ref[sub, 0], acc_ref[sub, 1], lam_refs, sg_ref, lam_init, dv).astype(o_ref.dtype)

    @pl.when(i == 0)
    def _():
        put_scores(0, q_sub[0], 0)
        put_probs(0, 0, True)
        put_scores(1, q_sub[0], 1)

    sub_tile(0, 0, q_sub[0], q_sub[1], 1)
    sub_tile(1, 1, q_sub[1], q_next, 0)


def _ctx_attn_kernel(qt_ref, k_ref, vt_ref, lq1_ref, lk1_ref, lq2_ref, lk2_ref, sg_ref, o_ref,
                     *, lam_init, half):
    acc = []
    for qtj in _split_maps(qt_ref[0, 0], half):
        sc = _dot(k_ref[0], qtj)
        col_max = jnp.max(_col_max8(sc), axis=0, keepdims=True)
        acc.append(_dot(vt_ref[0, 0, 0], jnp.exp2(sc - col_max).astype(BF16)))
    o_ref[0] = _attn_out(acc[0], acc[1], (lq1_ref, lk1_ref, lq2_ref, lk2_ref), sg_ref, lam_init,
                         2 * half).astype(o_ref.dtype)


def _attn_chunk(s):
    return next(t for t in (3 * TT, TT) if s % t == 0 and (s // t) % 2 == 1 and s // t >= 5)


def _diff_attention(qt, k, vt, lam_vecs, subln_g, *, n_lat_tiles, lam_init, head_dim):
    nb, s, qk = k.shape
    dv = 2 * head_dim
    heads = qk // dv
    l = n_lat_tiles * TT
    n_chunks, tk = vt.shape[2], vt.shape[4]
    tq = ATTN_SUB_Q if l % (2 * ATTN_SUB_Q) == 0 else TT
    assert l % (2 * tq) == 0 and n_chunks % 2 == 1 and n_chunks >= 5 and s - l == TT
    n_steps = l // (2 * tq)
    vec = lambda n: pl.BlockSpec((1, n), lambda b, h, i: (0, 0))
    vecs = [vec(head_dim)] * 4 + [vec(dv)]
    o_lat = pl.pallas_call(
        functools.partial(_attn_kernel, n_chunks=n_chunks, tk=tk, lam_init=lam_init, half=head_dim),
        grid=(nb, heads, n_steps),
        in_specs=[
            pl.BlockSpec((1, 1, dv, 2 * tq), lambda b, h, i: (b, h, 0, i)),
            pl.BlockSpec((1, 1, dv, tq),
                         lambda b, h, i: (b, h, 0, jnp.minimum(2 * i + 2, 2 * n_steps - 1))),
            pl.BlockSpec((1, s, dv), lambda b, h, i: (b, 0, h)),
            pl.BlockSpec((1, 1, n_chunks, VT_ROWS, tk), lambda b, h, i: (b, h, 0, 0, 0)),
        ] + vecs,
        out_specs=pl.BlockSpec((1, 2 * tq, dv), lambda b, h, i: (b, i, h)),
        out_shape=jax.ShapeDtypeStruct((nb, l, qk), BF16),
        scratch_shapes=[pltpu.VMEM((2, 2, tk, tq), F32), pltpu.VMEM((2, 2, SUBLANES, tq), F32),
                        pltpu.VMEM((2, 2, tk, tq), BF16), pltpu.VMEM((2, 2, VT_ROWS, tq), F32),
                        pltpu.VMEM((2, 2, 1, tq), F32), pltpu.VMEM((2, 2, 1, tq), F32)],
        compiler_params=_cp("parallel", "parallel", "arbitrary"),
        name="diff_attn",
    )(qt, qt, k, vt, *lam_vecs, subln_g)
    per = tk // TT
    vec2 = lambda n: pl.BlockSpec((1, n), lambda b, h: (0, 0))
    o_ctx = pl.pallas_call(
        functools.partial(_ctx_attn_kernel, lam_init=lam_init, half=head_dim),
        grid=(nb, heads),
        in_specs=[pl.BlockSpec((1, 1, dv, TT), lambda b, h: (b, h, 0, n_lat_tiles)),
                  pl.BlockSpec((1, TT, dv), lambda b, h: (b, n_lat_tiles, h)),
                  pl.BlockSpec((1, 1, 1, VT_ROWS, TT),
                               lambda b, h: (b, h, n_lat_tiles // per, 0, n_lat_tiles % per)),
                  ] + [vec2(head_dim)] * 4 + [vec2(dv)],
        out_specs=pl.BlockSpec((1, TT, dv), lambda b, h: (b, 0, h)),
        out_shape=jax.ShapeDtypeStruct((nb, TT, qk), BF16),
        compiler_params=_cp("parallel", "parallel"),
        name="ctx_attn",
    )(qt, k, vt, *lam_vecs, subln_g)
    return o_lat, o_ctx


def _halo_rows(dtype):
    return SUBLANES * 4 // jnp.dtype(dtype).itemsize


def _halo_fill(xp_ref, prev_ref, x_ref, next_ref, has_prev, has_next):
    hr = prev_ref.shape[2]
    zero = jnp.zeros((hr, xp_ref.shape[-1]), F32)
    xp_ref[hr:hr + TT, :] = x_ref[0].astype(F32)
    xp_ref[0:hr, :] = jnp.where(has_prev, prev_ref[0, 0].astype(F32), zero)
    xp_ref[hr + TT:2 * hr + TT, :] = jnp.where(has_next, next_ref[0, 0].astype(F32), zero)


def _conv_taps(xp, w, bias, left):
    rows = xp.shape[0]
    hr = (rows - TT) // 2
    before = None
    for j in range(left):
        z = w[j:j + 1, :] * xp
        before = pltpu.roll(z if before is None else before + z, 1, 0)
    after = None
    for j in range(w.shape[0] - 1, left, -1):
        z = w[j:j + 1, :] * xp
        after = pltpu.roll(z if after is None else after + z, rows - 1, 0)
    acc = w[left:left + 1, :] * xp + bias
    for part in (before, after):
        if part is not None:
            acc = acc + part
    return acc[hr:hr + TT, :]


def _shortconv_kernel(prev_ref, x_ref, next_ref, w_ref, b_ref, v_ref, x1_ref, x2_ref, xp_ref,
                      *, n_seg_tiles, hy):
    i = pl.program_id(1)
    _halo_fill(xp_ref, prev_ref, x_ref, next_ref, i > 0, i < n_seg_tiles - 1)
    y = _conv_taps(xp_ref[...], w_ref[...], b_ref[...], HY_SHORT_LEFT)
    v_ref[0] = y[:, :hy].astype(v_ref.dtype)
    x1_ref[0] = y[:, hy:2 * hy].astype(x1_ref.dtype)
    x2_ref[0] = y[:, 2 * hy:].astype(x2_ref.dtype)


def _shortconv(u, w, b, *, tile0, n_seg_tiles):
    nb, s, hy3 = u.shape
    hy = hy3 // 3
    hr = _halo_rows(u.dtype)
    nh, per = s // hr, TT // hr
    uh = u.reshape(nb, nh, hr, hy3)
    prev = pl.BlockSpec((1, 1, hr, hy3),
                        lambda b, i: (b, jnp.maximum((tile0 + i) * per - 1, 0), 0, 0))
    nxt = pl.BlockSpec((1, 1, hr, hy3),
                       lambda b, i: (b, jnp.minimum((tile0 + i + 1) * per, nh - 1), 0, 0))
    out = jax.ShapeDtypeStruct((nb, n_seg_tiles * TT, hy), BF16)
    ospec = pl.BlockSpec((1, TT, hy), lambda b, i: (b, i, 0))
    return pl.pallas_call(
        functools.partial(_shortconv_kernel, n_seg_tiles=n_seg_tiles, hy=hy),
        grid=(nb, n_seg_tiles),
        in_specs=[
            prev,
            pl.BlockSpec((1, TT, hy3), lambda b, i: (b, tile0 + i, 0)),
            nxt,
            pl.BlockSpec(w.shape, lambda b, i: (0, 0)),
            pl.BlockSpec((1, hy3), lambda b, i: (0, 0)),
        ],
        out_specs=[ospec, ospec, ospec],
        out_shape=[out, out, out],
        scratch_shapes=[pltpu.VMEM((TT + 2 * hr, hy3), F32)],
        compiler_params=_cp("parallel", "parallel"),
        name="hy_shortconv",
    )(uh, u, uh, w, b.reshape(1, hy3))


def _filter_feats(lh, n_cols):
    n = np.arange(2 * lh).reshape(-1, n_cols).T.reshape(-1)
    lag = np.where(n < lh, n, 2 * lh - n).astype(np.float64)
    t = (lag / lh).astype(np.float32).astype(np.float64)
    bands = np.arange(1, N_BANDS + 1, dtype=np.float64)
    ang = 2.0 * math.pi * t[:, None] * bands
    feats = np.concatenate([t[:, None], np.cos(ang), np.sin(ang)], axis=-1)
    pad = (-(feats.shape[1] + 1)) % SUBLANES
    return np.concatenate([feats, np.zeros((2 * lh, pad)), n[:, None].astype(np.float64)], axis=-1)


def _filt_kernel(ft_ref, w1_ref, b1_ref, w2_ref, b2_ref, w3_ref, b3_ref, fr_ref, dec_ref, o_ref,
                 *, lh, c):
    ft = ft_ref[...]
    freq = fr_ref[...]
    h = jnp.sin(freq * (_dot(ft, w1_ref[...], precision=HIGHEST) + b1_ref[...]))
    h = jnp.sin(freq * (_dot(h, w2_ref[...], precision=HIGHEST) + b2_ref[...])).astype(BF16)
    tr = o_ref.shape[1]
    half = tr // 2
    for q in range(ft.shape[0] // tr):
        for dr in range(2):
            rs = slice(q * tr + dr * half, q * tr + (dr + 1) * half)
            t = ft[rs, 0:1]
            f = (_dot(h[rs], w3_ref[dr]) + b3_ref[dr]) * jnp.exp(-t * jnp.abs(dec_ref[dr]))
            if dr == 1:
                f = jnp.where(ft[rs, ft.shape[1] - 1:] == lh, 0.0, f)
            for o in range(2):
                o_ref[o, dr * half:(dr + 1) * half, q * c:(q + 1) * c] = f[:, o * c:(o + 1) * c]


def _hyena_filter(lh, n_cols, w1, b1, w2, b2, w3, b3, freq, decay):
    c = decay.shape[-1]
    feats = jnp.asarray(_filter_feats(lh, n_cols), F32)
    fe = feats.shape[1]
    w1p = jnp.pad(w1, ((0, fe - w1.shape[0]), (0, 0)))
    hid = w1.shape[1]
    tr = 2 * lh // n_cols
    cb = math.gcd(n_cols, FILT_COL_BLOCKS)
    by_dir = lambda a, lead: jnp.moveaxis(a.reshape(lead, 2, 2, c), 2, 0).reshape(2, lead, 2 * c)
    full = lambda a: pl.BlockSpec(a.shape, lambda i: (0,) * a.ndim)
    args = (w1p, b1.reshape(1, hid), w2, b2.reshape(1, hid), by_dir(w3, hid).astype(BF16),
            by_dir(b3, 1), freq.reshape(1, hid), by_dir(decay, 1))
    return pl.pallas_call(
        functools.partial(_filt_kernel, lh=lh, c=c),
        grid=(n_cols // cb,),
        in_specs=[pl.BlockSpec((cb * tr, fe), lambda i: (i, 0))] + [full(a) for a in args],
        out_specs=pl.BlockSpec((2, tr, cb * c), lambda i: (0, 0, i)),
        out_shape=jax.ShapeDtypeStruct((2, tr, n_cols * c), F32),
        compiler_params=_cp("parallel"),
        name="hy_filter",
    )(feats, *args)


@functools.lru_cache(maxsize=None)
def _dft_tables(n):
    nn = n * n
    h = n // 2
    k = np.arange(n)
    th = 2.0 * math.pi * np.outer(k, k) / n
    c, s = np.cos(th), np.sin(th)
    f1_data = np.block([[c[:, :h], s[:, :h]], [-s[:, :h], c[:, :h]]])
    f1_real = np.concatenate([c, -s], axis=0)
    idx = (k[None, None, :] * (k[:, None, None] + n * k[None, :, None])) % nn
    phi = 2.0 * math.pi * idx / nn
    cp, sp = np.cos(phi), np.sin(phi)
    g = np.concatenate([np.concatenate([cp, sp], axis=2), np.concatenate([-sp, cp], axis=2)], axis=1)
    hmat = np.transpose(g, (0, 2, 1)) / nn
    ci, si = c[:h, :], s[:h, :]
    f3 = np.zeros((n, 2 * n))
    f3[:h, 0::2], f3[:h, 1::2] = ci, -si
    f3[h:, 0::2], f3[h:, 1::2] = si, ci
    return f1_data, f1_real, g, hmat, f3


def _fft_s1_kernel(x_ref, f_ref, o_ref):
    o_ref[0] = _dot(f_ref[...], x_ref[0].astype(BF16)).astype(o_ref.dtype)


def _fft_s1(x, f1, n, c):
    p = x.shape[0]
    tc = FFT_N2_TILE * c
    return pl.pallas_call(
        _fft_s1_kernel,
        grid=(p, n * c // tc),
        in_specs=[pl.BlockSpec((1, n, tc), lambda q, j: (q, 0, j)),
                  pl.BlockSpec(f1.shape, lambda q, j: (0, 0))],
        out_specs=pl.BlockSpec((1, 2 * n, tc), lambda q, j: (q, 0, j)),
        out_shape=jax.ShapeDtypeStruct((p, 2 * n, n * c), BF16),
        compiler_params=_cp("parallel", "parallel"),
        name="fft_s1",
    )(x, f1)


def _fft_spec_kernel(a_ref, g_ref, o_ref, *, tk):
    for j in range(tk):
        x = jnp.concatenate([a_ref[0, 0, j], a_ref[0, 1, j]], axis=0)
        o_ref[0, j] = _dot(g_ref[j], x).astype(o_ref.dtype)


def _fft_mid_kernel(a_ref, g_ref, h_ref, kh_ref, o_ref, *, tk, n):
    for j in range(tk):
        x = jnp.concatenate([a_ref[0, 0, j], a_ref[0, 1, j]], axis=0)
        t = _dot(g_ref[j], x)
        tr, ti = t[:n], t[n:]
        kr, ki = kh_ref[0, j, :n].astype(F32), kh_ref[0, j, n:].astype(F32)
        y = jnp.concatenate([tr * kr - ti * ki, tr * ki + ti * kr], axis=0).astype(BF16)
        o_ref[0, j] = _dot(h_ref[j], y).astype(o_ref.dtype)


def _fft_spectrum(a, g, n, c):
    p = a.shape[0]
    tk = FFT_K1_TILE
    a5 = a.reshape(p, 2, n, n, c)
    return pl.pallas_call(
        functools.partial(_fft_spec_kernel, tk=tk),
        grid=(p, n // tk),
        in_specs=[pl.BlockSpec((1, 2, tk, n, c), lambda q, j: (q, 0, j, 0, 0)),
                  pl.BlockSpec((tk, 2 * n, 2 * n), lambda q, j: (j, 0, 0))],
        out_specs=pl.BlockSpec((1, tk, 2 * n, c), lambda q, j: (q, j, 0, 0)),
        out_shape=jax.ShapeDtypeStruct((p, n, 2 * n, c), BF16),
        compiler_params=_cp("parallel", "parallel"),
        name="fft_spectrum",
    )(a5, g)


def _fft_mid(a, g, hm, khat, order, n, c):
    p = a.shape[0]
    tk = FFT_K1_TILE
    a5 = a.reshape(p, 2, n, n, c)
    return pl.pallas_call(
        functools.partial(_fft_mid_kernel, tk=tk, n=n),
        grid=(p, n // tk),
        in_specs=[pl.BlockSpec((1, 2, tk, n, c), lambda q, j: (q, 0, j, 0, 0)),
                  pl.BlockSpec((tk, 2 * n, 2 * n), lambda q, j: (j, 0, 0)),
                  pl.BlockSpec((tk, 2 * n, 2 * n), lambda q, j: (j, 0, 0)),
                  pl.BlockSpec((1, tk, 2 * n, c), lambda q, j: (order, j, 0, 0))],
        out_specs=pl.BlockSpec((1, tk, 2 * n, c), lambda q, j: (q, j, 0, 0)),
        out_shape=jax.ShapeDtypeStruct((p, n, 2 * n, c), BF16),
        compiler_params=_cp("parallel", "parallel"),
        name="fft_mid",
    )(a5, g, hm, khat)


def _fft_s3_kernel(*refs, gated, chained):
    c_ref, f_ref, v_ref, b_ref = refs[:4]
    rest = list(refs[4:])
    y = _dot(f_ref[...], c_ref[0]) + v_ref[0].astype(F32) * b_ref[...]
    if gated:
        y = rest.pop(0)[0].astype(F32) * y
    f1_ref = rest.pop(0) if chained else None
    o_ref = rest.pop(0)
    z = y.astype(o_ref.dtype)
    o_ref[0] = z
    if chained:
        a_ref = rest.pop(0)
        a_ref[0] = _dot(f1_ref[...], z).astype(a_ref.dtype)


def _fft_s3(cm, f3, vin, xg, bias, n, c, f1_next=None):
    p = cm.shape[0]
    tc = FFT_N2_TILE * c
    c2 = cm.reshape(p, 2 * n, n * c)
    bias_t = jnp.tile(bias.reshape(1, c), (1, FFT_N2_TILE))
    blk = pl.BlockSpec((1, n, tc), lambda q, j: (q, 0, j))
    full = lambda a: pl.BlockSpec(a.shape, lambda q, j: (0, 0))
    in_specs = [pl.BlockSpec((1, 2 * n, tc), lambda q, j: (q, 0, j)), full(f3), blk,
                pl.BlockSpec((1, tc), lambda q, j: (0, 0))]
    args = [c2, f3, vin, bias_t]
    out_specs, out_shape = [blk], [jax.ShapeDtypeStruct((p, n, n * c), BF16)]
    if xg is not None:
        in_specs.append(blk)
        args.append(xg)
    if f1_next is not None:
        in_specs.append(full(f1_next))
        args.append(f1_next)
        out_specs.append(pl.BlockSpec((1, 2 * n, tc), lambda q, j: (q, 0, j)))
        out_shape.append(jax.ShapeDtypeStruct((p, 2 * n, n * c), BF16))
    return pl.pallas_call(
        functools.partial(_fft_s3_kernel, gated=xg is not None, chained=f1_next is not None),
        grid=(p, n * c // tc),
        in_specs=in_specs,
        out_specs=out_specs,
        out_shape=out_shape,
        compiler_params=_cp("parallel", "parallel"),
        name="fft_s3",
    )(*args)


def _hyena_long(v, x1, kk, hy_bias):
    nb, l, c = v.shape
    n = math.isqrt(2 * l)
    assert n * n == 2 * l and nb % 2 == 0
    p = nb // 2
    f1d, f1r, g, hm, f3 = (jnp.asarray(t, F32).astype(BF16) for t in _dft_tables(n))
    pair = lambda a: a.reshape(p, n, n * c)
    khat = _fft_spectrum(_fft_s1(kk, f1r, n, c), g, n, c)
    v = pair(v)
    cm = _fft_mid(_fft_s1(v, f1d, n, c), g, hm, khat, 0, n, c)
    z1, a = _fft_s3(cm, f3, v, pair(x1), hy_bias[0], n, c, f1_next=f1d)
    cm = _fft_mid(a, g, hm, khat, 1, n, c)
    (w2,) = _fft_s3(cm, f3, z1, None, hy_bias[1], n, c)
    return w2.reshape(nb, l, c)


@functools.lru_cache(maxsize=None)
def _ctx_dft_tables(lc):
    m = 2 * lc
    k = np.arange(m)
    th = 2.0 * math.pi * np.outer(k, k) / m
    c, s = np.cos(th), np.sin(th)
    f_data = np.block([[c[:, :lc], s[:, :lc]], [-s[:, :lc], c[:, :lc]]])
    f_real = np.concatenate([c, -s], axis=0)
    ci, si = c[:lc, :], s[:lc, :]
    f_inv = np.block([[ci, -si], [si, ci]]) / m
    return f_data, f_real, f_inv


def _ctxconv_kernel(v_ref, x1_ref, x2_ref, kk_ref, fd_ref, fr_ref, fi_ref, b_ref, o_ref, *, m):
    def conv(u, order):
        kh = _dot(fr_ref[...], kk_ref[order].astype(BF16))
        t = _dot(fd_ref[...], u.astype(BF16))
        tr, ti, kr, ki = t[:m], t[m:], kh[:m], kh[m:]
        y = jnp.concatenate([tr * kr - ti * ki, tr * ki + ti * kr], axis=0).astype(BF16)
        return _dot(fi_ref[...], y)

    v = v_ref[0].astype(F32)
    z1 = x1_ref[0].astype(F32) * (conv(v, 0) + v * b_ref[0:1, :])
    o_ref[0] = (x2_ref[0].astype(F32) * (conv(z1, 1) + z1 * b_ref[1:2, :])).astype(o_ref.dtype)


def _hyena_ctx(v, x1, x2, kk, hy_bias):
    nb, lc, c = v.shape
    p, m = nb // 2, 2 * lc
    fd, fr, fi = (jnp.asarray(t, F32).astype(BF16) for t in _ctx_dft_tables(lc))
    pair = lambda a: a.reshape(p, m, c)
    blk = pl.BlockSpec((1, m, c), lambda q: (q, 0, 0))
    full = lambda a: pl.BlockSpec(a.shape, lambda q: (0,) * a.ndim)
    z = pl.pallas_call(
        functools.partial(_ctxconv_kernel, m=m),
        grid=(p,),
        in_specs=[blk, blk, blk, full(kk), full(fd), full(fr), full(fi), full(hy_bias)],
        out_specs=blk,
        out_shape=jax.ShapeDtypeStruct((p, m, c), BF16),
        compiler_params=_cp("parallel"),
        name="hy_ctx",
    )(pair(v), pair(x1), pair(x2), kk, fd, fr, fi, hy_bias)
    return z.reshape(nb, lc, c)


def _od_inproj_kernel(prev_ref, h_ref, next_ref, mod_ref, g_ref, w_ref, cw_ref, cb_ref,
                      gate_ref, xc_ref, *, d, r, n_lat_tiles, n_tiles):
    i = pl.program_id(1)
    seg_first = jnp.logical_or(i == 0, i == n_lat_tiles)
    seg_last = jnp.logical_or(i == n_lat_tiles - 1, i == n_tiles - 1)
    m = mod_ref[0]
    rows = jnp.concatenate([prev_ref[0, 0], h_ref[0], next_ref[0, 0]], axis=0)
    xn = _rms_mod(rows, g_ref[...], m[:, 0:d], m[:, d:2 * d]).astype(BF16)
    y = _dot(xn, w_ref[...])
    gate_ref[0] = y[SUBLANES:SUBLANES + TT, :r].astype(gate_ref.dtype)
    x = y[:, r:]
    row = lax.broadcasted_iota(jnp.int32, (x.shape[0], 1), 0)
    outside = jnp.logical_or(jnp.logical_and(row < SUBLANES, seg_first),
                             jnp.logical_and(row >= SUBLANES + TT, seg_last))
    x = jnp.where(outside, 0.0, x)
    xc_ref[0] = _conv_taps(x, cw_ref[...], cb_ref[...], RG_CONV_LEFT)


def _od_inproj(h, mods, rows, layer, g, w_in, conv_w, conv_b, *, n_lat_tiles):
    nb, s, d = h.shape
    r = w_in.shape[1] // 2
    n_tiles = s // TT
    n8, per = s // SUBLANES, TT // SUBLANES
    h8 = h.reshape(nb, n8, SUBLANES, d)
    ospec = pl.BlockSpec((1, TT, r), lambda b, i: (b, i, 0))
    return pl.pallas_call(
        functools.partial(_od_inproj_kernel, d=d, r=r, n_lat_tiles=n_lat_tiles, n_tiles=n_tiles),
        grid=(nb, n_tiles),
        in_specs=[
            pl.BlockSpec((1, 1, SUBLANES, d), lambda b, i: (b, jnp.maximum(i * per - 1, 0), 0, 0)),
            pl.BlockSpec((1, TT, d), lambda b, i: (b, i, 0)),
            pl.BlockSpec((1, 1, SUBLANES, d),
                         lambda b, i: (b, jnp.minimum((i + 1) * per, n8 - 1), 0, 0)),
            _mod_spec(layer, rows, nb, n_lat_tiles, mods.shape[-1]),
            pl.BlockSpec((1, d), lambda b, i: (0, 0)),
            pl.BlockSpec(w_in.shape, lambda b, i: (0, 0)),
            pl.BlockSpec(conv_w.shape, lambda b, i: (0, 0)),
            pl.BlockSpec((1, r), lambda b, i: (0, 0)),
        ],
        out_specs=[ospec, ospec],
        out_shape=[jax.ShapeDtypeStruct((nb, s, r), BF16), jax.ShapeDtypeStruct((nb, s, r), F32)],
        compiler_params=_cp("parallel", "parallel"),
        name="od_inproj",
    )(h8, h, h8, mods, g, w_in, conv_w, conv_b.reshape(1, r))


def _rglru_kernel(x_ref, w_ref, b_ref, lam_ref, o_ref, a_ref, bb_ref, carry_ref, *, r, windows):
    dr = pl.program_id(1)
    i = pl.program_id(2)
    half = TT // 2
    nl = -lam_ref[0]
    softplus = jnp.maximum(nl, 0.0) + jnp.log1p(jnp.exp(-jnp.abs(nl)))
    neg_rate = (0.5 * RG_C) * softplus
    exp2_rate = (-0.5 * RG_C * math.log2(math.e)) * softplus

    def gates(r0):
        rs = slice(r0, r0 + half)
        xc = x_ref[0, rs, :]
        xb = xc.astype(BF16)
        half_x = 0.5 * xc
        for c0, c1, k0, k1 in windows:
            xk = xb[:, k0:k1]
            tr = jnp.tanh(_dot(xk, w_ref[0, k0:k1, c0:c1]) + b_ref[0, :, c0:c1]) + 1.0
            ti = jnp.tanh(_dot(xk, w_ref[0, k0:k1, r + c0:r + c1]) + b_ref[0, :, r + c0:r + c1]) + 1.0
            a = jnp.exp2(exp2_rate[:, c0:c1] * tr)
            a_ref[rs, c0:c1] = a
            e = jnp.tanh(neg_rate[:, c0:c1] * tr) * (a * a + 1.0)
            root = jnp.where(e > 0.0, e * lax.rsqrt(e), 0.0)
            bb_ref[rs, c0:c1] = root * (ti * half_x[:, c0:c1])

    def scan_rows(rows, h):
        for row in rows:
            h = a_ref[row:row + 1, :] * h + bb_ref[row:row + 1, :]
            o_ref[0, 0, row:row + 1, :] = h
        return h

    def scan_loop(r0, reverse, h):
        n_blk = half // SUBLANES

        def block(g, h):
            base = pl.multiple_of(r0 + (n_blk - 1 - g if reverse else g) * SUBLANES, SUBLANES)
            for k in range(SUBLANES):
                row = base + (SUBLANES - 1 - k if reverse else k)
                h = a_ref[pl.ds(row, 1), :] * h + bb_ref[pl.ds(row, 1), :]
                o_ref[0, 0, pl.ds(row, 1), :] = h
            return h

        return lax.fori_loop(0, n_blk, block, h)

    @pl.when(i == 0)
    def _():
        carry_ref[...] = jnp.zeros_like(carry_ref)

    def run(reverse):
        first, second = (half, 0) if reverse else (0, half)
        gates(first)
        gates(second)
        rows = range(first + half - 1, first - 1, -1) if reverse else range(first, first + half)
        h = scan_rows(rows, carry_ref[0:1, :])
        carry_ref[0:1, :] = scan_loop(second, reverse, h)

    pl.when(dr == 0)(lambda: run(False))
    pl.when(dr == 1)(lambda: run(True))


def _scan_tile(dr, i, n_lat_tiles, n_tiles):
    fwd = jnp.where(i == 0, n_lat_tiles, i - 1)
    bwd = jnp.where(i == 0, n_lat_tiles, n_lat_tiles - i)
    return jnp.where(dr == 0, fwd, bwd)


def _gate_windows(r, bs):
    out = []
    for c0 in range(0, r, 2 * LANES):
        c1 = min(c0 + 2 * LANES, r)
        k0 = (c0 // bs) * bs // LANES * LANES
        k1 = min(-(-(((c1 - 1) // bs + 1) * bs) // LANES) * LANES, r)
        out.append((c0, c1, k0, k1))
    return tuple(out)


def _rglru(xc, wcat, bcat, lam, *, n_lat_tiles, block_size):
    nb, s, r = xc.shape
    n_tiles = s // TT
    assert n_tiles == n_lat_tiles + 1
    tile = lambda d, i: _scan_tile(d, i, n_lat_tiles, n_tiles)
    return pl.pallas_call(
        functools.partial(_rglru_kernel, r=r, windows=_gate_windows(r, block_size)),
        grid=(nb, 2, n_tiles),
        in_specs=[
            pl.BlockSpec((1, TT, r), lambda b, d, i: (b, tile(d, i), 0)),
            pl.BlockSpec((1, r, 2 * r), lambda b, d, i: (d, 0, 0)),
            pl.BlockSpec((1, 1, 2 * r), lambda b, d, i: (d, 0, 0)),
            pl.BlockSpec((1, 1, r), lambda b, d, i: (d, 0, 0)),
        ],
        out_specs=pl.BlockSpec((1, 1, TT, r), lambda b, d, i: (d, b, tile(d, i), 0)),
        out_shape=jax.ShapeDtypeStruct((2, nb, s, r), F32),
        scratch_shapes=[pltpu.VMEM((TT, r), F32), pltpu.VMEM((TT, r), F32),
                        pltpu.VMEM((SUBLANES, r), F32)],
        compiler_params=_cp("parallel", "parallel", "arbitrary"),
        name="rglru",
    )(xc, wcat, bcat, lam.reshape(2, 1, r))


def _block_diag(w):
    n, bs, _ = w.shape
    eye = jnp.eye(n, dtype=w.dtype)
    return (eye[:, None, :, None] * w[:, :, None, :]).reshape(n * bs, n * bs)


def _gelu_tanh(x):
    return 0.5 * x * (1.0 + jnp.tanh(math.sqrt(2.0 / math.pi) * (x + 0.044715 * (x * x * x))))


def _post_kernel(*refs, kind, final, d, n_lat_tiles):
    if kind == "even":
        (hl_ref, hc_ref, x2_ref, wl_ref, zc_ref, ol_ref, oc_ref, mod_ref, g2_ref, wo_ref, w1_ref,
         w2_ref) = refs[:12]
        rest = refs[12:]
        is_ctx = pl.program_id(1) >= n_lat_tiles
        z_lat = (x2_ref[0].astype(F32) * wl_ref[0].astype(F32)).astype(BF16)
        z = jnp.where(is_ctx, zc_ref[0], z_lat)
        o = jnp.where(is_ctx, oc_ref[0], ol_ref[0])
        half = z.shape[-1]
        y = _dot(z, wo_ref[:half, :]) + _dot(o, wo_ref[half:, :])
    else:
        hl_ref, hc_ref, hd_ref, gate_ref, mod_ref, g2_ref, wo_ref, w1_ref, w2_ref = refs[:9]
        rest = refs[9:]
        mix = (hd_ref[0, 0] + hd_ref[1, 0]) * _gelu_tanh(gate_ref[0].astype(F32))
        y = _dot(mix.astype(BF16), wo_ref[...])
    out_ref = rest[-1]
    m = mod_ref[0]
    h1 = _stream_tile(hl_ref, hc_ref, n_lat_tiles) + m[:, 2 * d:3 * d] * y
    xn = _rms_mod(h1, g2_ref[...], m[:, 3 * d:4 * d], m[:, 4 * d:5 * d]).astype(BF16)
    acc = jnp.zeros_like(h1)
    dff = w1_ref.shape[1]
    for c0 in range(0, dff, FF_CHUNK):
        a = jnp.maximum(_dot(xn, w1_ref[:, c0:c0 + FF_CHUNK]), 0.0)
        acc = acc + _dot((a * a).astype(BF16), w2_ref[c0:c0 + FF_CHUNK, :])
    h2 = h1 + m[:, 5 * d:6 * d] * acc
    if final:
        fg_ref = rest[0]
        h2 = h2 * lax.rsqrt(jnp.mean(h2 * h2, axis=-1, keepdims=True) + EPS) * fg_ref[...]
    out_ref[0] = h2


def _post(kind, h, mix_args, mods, rows, layer, g2, w_out, w1, w2, final_g, *, n_lat_tiles, final):
    h_specs, h_args = _stream_specs(h, n_lat_tiles)
    nb, _, d = h_args[0].shape
    n_tiles = n_lat_tiles if final else n_lat_tiles + 1
    tok = lambda w: pl.BlockSpec((1, TT, w), lambda b, i: (b, i, 0))
    full = lambda a: pl.BlockSpec(a.shape, lambda b, i: (0,) * a.ndim)
    if kind == "even":
        lat = lambda w: pl.BlockSpec((1, TT, w), lambda b, i: (b, jnp.minimum(i, n_lat_tiles - 1), 0))
        ctx = lambda w: pl.BlockSpec((1, TT, w), lambda b, i: (b, 0, 0))
        x2_lat, w_lat, z_ctx, o_lat, o_ctx = mix_args
        mix_specs = [lat(x2_lat.shape[-1]), lat(w_lat.shape[-1]), ctx(z_ctx.shape[-1]),
                     lat(o_lat.shape[-1]), ctx(o_ctx.shape[-1])]
    else:
        hd, gate = mix_args
        r = gate.shape[-1]
        mix_specs = [pl.BlockSpec((2, 1, TT, r), lambda b, i: (0, b, i, 0)), tok(r)]
    in_specs = h_specs + mix_specs + [
        _mod_spec(layer, rows, nb, n_lat_tiles, mods.shape[-1]),
        pl.BlockSpec((1, d), lambda b, i: (0, 0)), full(w_out), full(w1), full(w2)]
    args = [*h_args, *mix_args, mods, g2, w_out, w1, w2]
    if final:
        in_specs.append(pl.BlockSpec((1, d), lambda b, i: (0, 0)))
        args.append(final_g)
    return pl.pallas_call(
        functools.partial(_post_kernel, kind=kind, final=final, d=d, n_lat_tiles=n_lat_tiles),
        grid=(nb, n_tiles),
        in_specs=in_specs,
        out_specs=tok(d),
        out_shape=jax.ShapeDtypeStruct((nb, n_tiles * TT, d), F32),
        compiler_params=_cp("parallel", "parallel"),
        name="post_" + kind,
    )(*args)


@functools.lru_cache(maxsize=None)
def _rope_tables(l, lc, head_dim):
    axis = head_dim // 2
    freqs = ROPE_BASE ** (-np.arange(0, axis, 2, dtype=np.float64) / axis)
    freqs = freqs.astype(np.float32).astype(np.float64)
    t = np.arange(l)
    ang_r = (t // GRID_W)[:, None] * freqs
    ang_c = (t % GRID_W)[:, None] * freqs
    cos = np.concatenate([np.cos(ang_r)] * 2 + [np.cos(ang_c)] * 2, axis=-1)
    sin = np.concatenate([-np.sin(ang_r), np.sin(ang_r), -np.sin(ang_c), np.sin(ang_c)], axis=-1)
    cos = np.concatenate([cos, np.ones((lc, head_dim))], axis=0)
    sin = np.concatenate([sin, np.zeros((lc, head_dim))], axis=0)
    rep = LANES // head_dim
    return np.tile(cos, (1, rep)), np.tile(sin, (1, rep))


def kernel(x, c, ctx, c_ctx, ada_w, ada_b, norm1_g, norm2_g, mlp_w1, mlp_w2, final_g, ev_w_in, ev_w_out, hy_short_w, hy_short_b, hy_f_w1, hy_f_b1, hy_f_w2, hy_f_b2, hy_f_w3, hy_f_b3, hy_f_freq, hy_f_decay, hy_bias, df_lq1, df_lk1, df_lq2, df_lk2, df_subln_g, od_w_in, od_w_out, rg_conv_w, rg_conv_b, rg_wa, rg_ba, rg_wx, rg_bx, rg_lam):
    nb, l, d = x.shape
    lc = ctx.shape[1]
    depth = ada_w.shape[0]
    hy = hy_bias.shape[-1]
    head_dim = df_lq1.shape[-1]
    qk = (ev_w_in.shape[-1] - 3 * hy) // 3
    r = rg_lam.shape[-1]
    assert l % TT == 0 and lc == TT and l % GRID_W == 0
    n_lat_tiles = l // TT

    rows = -(-(nb + 1) // SUBLANES) * SUBLANES
    cond = jnp.zeros((rows, d), F32).at[:nb].set(c).at[nb].set(c_ctx)
    mods = _ada_mods(cond, ada_w, ada_b).reshape(depth * rows, 1, ada_w.shape[-1])

    cos_t, sin_t = (jnp.asarray(t, F32) for t in _rope_tables(l, lc, head_dim))
    h = (x, ctx)
    w1_bf, w2_bf = mlp_w1.astype(BF16), mlp_w2.astype(BF16)

    for i in range(depth):
        j = i // 2
        final = i == depth - 1
        g1, g2 = norm1_g[i].reshape(1, d), norm2_g[i].reshape(1, d)
        if i % 2 == 0:
            lam_init = 0.8 - 0.6 * math.exp(-0.3 * i)
            u, qt, k, vt = _ev_inproj(h, mods, rows, i, g1, ev_w_in[j].astype(BF16), cos_t, sin_t,
                                      n_lat_tiles=n_lat_tiles, hy3=3 * hy, qk=qk, head_dim=head_dim,
                                      tk=_attn_chunk(l + lc))
            lam_vecs = [a[j].reshape(1, -1) for a in (df_lq1, df_lk1, df_lq2, df_lk2)]
            o_lat, o_ctx = _diff_attention(qt, k, vt, lam_vecs, df_subln_g[j].reshape(1, -1),
                                           n_lat_tiles=n_lat_tiles, lam_init=lam_init,
                                           head_dim=head_dim)
            fparams = (hy_f_w1[j], hy_f_b1[j], hy_f_w2[j], hy_f_b2[j], hy_f_w3[j], hy_f_b3[j],
                       hy_f_freq[j], hy_f_decay[j])
            vl, x1l, x2l = _shortconv(u, hy_short_w[j], hy_short_b[j], tile0=0,
                                      n_seg_tiles=n_lat_tiles)
            vc, x1c, x2c = _shortconv(u, hy_short_w[j], hy_short_b[j], tile0=n_lat_tiles,
                                      n_seg_tiles=lc // TT)
            w_lat = _hyena_long(vl, x1l, _hyena_filter(l, math.isqrt(2 * l), *fparams), hy_bias[j])
            z_ctx = _hyena_ctx(vc, x1c, x2c, _hyena_filter(lc, 1, *fparams), hy_bias[j])
            h = _post("even", h, (x2l, w_lat, z_ctx, o_lat, o_ctx), mods, rows, i, g2, ev_w_out[j].astype(BF16),
                      w1_bf[i], w2_bf[i], final_g.reshape(1, d), n_lat_tiles=n_lat_tiles,
                      final=final)
        else:
            gate, xc = _od_inproj(h, mods, rows, i, g1, od_w_in[j].astype(BF16), rg_conv_w[j],
                                  rg_conv_b[j], n_lat_tiles=n_lat_tiles)
            wcat = (0.5 * jnp.stack(
                [jnp.concatenate([_block_diag(rg_wa[j, dd]), _block_diag(rg_wx[j, dd])], axis=1)
                 for dd in range(2)])).astype(BF16)
            bcat = 0.5 * jnp.concatenate([rg_ba[j], rg_bx[j]], axis=-1).reshape(2, 1, 2 * r)
            hd = _rglru(xc, wcat, bcat, rg_lam[j], n_lat_tiles=n_lat_tiles,
                        block_size=rg_wa.shape[-1])
            h = _post("odd", h, (hd, gate), mods, rows, i, g2, od_w_out[j].astype(BF16),
                      w1_bf[i], w2_bf[i], final_g.reshape(1, d), n_lat_tiles=n_lat_tiles,
                      final=final)
    return h
```

```python
import functools
import math

import numpy as np
import jax
import jax.numpy as jnp
from jax import lax
from jax.experimental import pallas as pl
from jax.experimental.pallas import tpu as pltpu

F32 = jnp.float32
BF16 = jnp.bfloat16
HIGHEST = lax.Precision.HIGHEST

EPS = 1e-6
GRID_W = 64
ROPE_BASE = 10000.0
N_BANDS = 16
RG_C = 8.0
RG_CONV_LEFT = 2
HY_SHORT_LEFT = 1

TT = 256
LANES = 128
SUBLANES = 8
VMEM_LIMIT = 56 * 1024 * 1024
FF_CHUNK = 1024
FFT_N2_TILE = 8
FFT_K1_TILE = 8
ADA_K_TILE = 256
ATTN_SUB_Q = 512
FILT_COL_BLOCKS = 4
VT_ROWS = LANES + 16


def _cp(*sem):
    return pltpu.CompilerParams(dimension_semantics=sem, vmem_limit_bytes=VMEM_LIMIT)


def _dot(a, b, **kw):
    return jnp.dot(a, b, preferred_element_type=F32, **kw)


def _sigmoid(x):
    return 1.0 / (1.0 + jnp.exp(-x))


def _rms_mod(x, g, shift, scale):
    y = x * lax.rsqrt(jnp.mean(x * x, axis=-1, keepdims=True) + EPS)
    return (y * g) * (1.0 + scale) + shift


def _ada_kernel(c_ref, w_ref, b_ref, o_ref):
    k = pl.program_id(1)
    c = c_ref[k]
    part = _dot((c * _sigmoid(c)).astype(BF16), w_ref[0].astype(BF16))

    @pl.when(k == 0)
    def _():
        o_ref[0] = part + b_ref[0]

    @pl.when(k > 0)
    def _():
        o_ref[0] += part


def _ada_mods(cond, ada_w, ada_b):
    depth, d, n = ada_w.shape
    rows = cond.shape[0]
    tk = min(d, ADA_K_TILE)
    cond_k = cond.reshape(rows, d // tk, tk).transpose(1, 0, 2)
    return pl.pallas_call(
        _ada_kernel,
        grid=(depth, d // tk),
        in_specs=[
            pl.BlockSpec(cond_k.shape, lambda l, k: (0, 0, 0)),
            pl.BlockSpec((1, tk, n), lambda l, k: (l, k, 0)),
            pl.BlockSpec((1, 1, n), lambda l, k: (l, 0, 0)),
        ],
        out_specs=pl.BlockSpec((1, rows, n), lambda l, k: (l, 0, 0)),
        out_shape=jax.ShapeDtypeStruct((depth, rows, n), F32),
        compiler_params=_cp("parallel", "arbitrary"),
        name="ada_mods",
    )(cond_k, ada_w, ada_b.reshape(depth, 1, n))


def _stream_specs(h, n_lat_tiles):
    separate = isinstance(h, tuple)
    lat_arr, ctx_arr = h if separate else (h, h)
    ctx_blk = 0 if separate else n_lat_tiles
    d = lat_arr.shape[-1]
    lat = pl.BlockSpec((1, TT, d), lambda b, i: (b, jnp.minimum(i, n_lat_tiles - 1), 0))
    ctx = pl.BlockSpec((1, TT, d), lambda b, i: (b, ctx_blk, 0))
    return [lat, ctx], [lat_arr, ctx_arr]


def _stream_tile(hl_ref, hc_ref, n_lat_tiles):
    return jnp.where(pl.program_id(1) >= n_lat_tiles, hc_ref[0], hl_ref[0])


def _mod_spec(layer, rows, nb, n_lat_tiles, n6):
    def imap(b, i):
        return (layer * rows + jnp.where(i >= n_lat_tiles, nb, b), 0, 0)
    return pl.BlockSpec((1, 1, n6), imap)


def _ev_inproj_kernel(hl_ref, hc_ref, mod_ref, g_ref, w_ref, cos_ref, sin_ref,
                      u_ref, qt_ref, k_ref, vt_ref, *, d, hy3, qk, qscale, n_lat_tiles):
    m = mod_ref[0]
    x = _stream_tile(hl_ref, hc_ref, n_lat_tiles)
    xn = _rms_mod(x, g_ref[...], m[:, 0:d], m[:, d:2 * d]).astype(BF16)
    cos = cos_ref[...]
    sin = sin_ref[...]
    lane = lax.broadcasted_iota(jnp.int32, cos.shape, 1)
    first = (lane % 32) < 16

    def rope(z):
        sw = jnp.where(first, pltpu.roll(z, LANES - 16, 1), pltpu.roll(z, 16, 1))
        return z * cos + sw * sin

    extra = vt_ref.shape[3] - LANES
    ones_row = jnp.where(lax.broadcasted_iota(jnp.int32, (extra, cos.shape[0]), 0) == 0,
                         1.0, 0.0).astype(BF16)
    yq = _dot(xn, w_ref[:, hy3:hy3 + qk])
    for c in range(qk // LANES):
        qt_ref[0, c] = (rope(yq[:, c * LANES:(c + 1) * LANES]) * qscale).T.astype(BF16)
    yk = _dot(xn, w_ref[:, hy3 + qk:hy3 + 2 * qk])
    for c in range(qk // LANES):
        k_ref[0, :, c * LANES:(c + 1) * LANES] = rope(yk[:, c * LANES:(c + 1) * LANES]).astype(BF16)
    yv = _dot(xn, w_ref[:, hy3 + 2 * qk:])
    for c in range(qk // LANES):
        vt_ref[0, c, 0, 0:LANES, :] = yv[:, c * LANES:(c + 1) * LANES].T.astype(BF16)
        vt_ref[0, c, 0, LANES:, :] = ones_row
    u_ref[0] = _dot(xn, w_ref[:, :hy3]).astype(u_ref.dtype)


def _ev_inproj(h, mods, rows, layer, g, w_in, cos_t, sin_t, *, n_lat_tiles, hy3, qk, head_dim, tk):
    h_specs, h_args = _stream_specs(h, n_lat_tiles)
    nb, _, d = h_args[0].shape
    s = (n_lat_tiles + 1) * TT
    n_in = w_in.shape[1]
    assert 2 * head_dim == LANES
    heads, per = qk // LANES, tk // TT
    kern = functools.partial(_ev_inproj_kernel, d=d, hy3=hy3, qk=qk, n_lat_tiles=n_lat_tiles,
                             qscale=head_dim ** -0.5 * math.log2(math.e))
    return pl.pallas_call(
        kern,
        grid=(nb, s // TT),
        in_specs=h_specs + [
            _mod_spec(layer, rows, nb, n_lat_tiles, mods.shape[-1]),
            pl.BlockSpec((1, d), lambda b, i: (0, 0)),
            pl.BlockSpec((d, n_in), lambda b, i: (0, 0)),
            pl.BlockSpec((TT, LANES), lambda b, i: (i, 0)),
            pl.BlockSpec((TT, LANES), lambda b, i: (i, 0)),
        ],
        out_specs=[
            pl.BlockSpec((1, TT, hy3), lambda b, i: (b, i, 0)),
            pl.BlockSpec((1, heads, LANES, TT), lambda b, i: (b, 0, 0, i)),
            pl.BlockSpec((1, TT, qk), lambda b, i: (b, i, 0)),
            pl.BlockSpec((1, heads, 1, VT_ROWS, TT), lambda b, i: (b, 0, i // per, 0, i % per)),
        ],
        out_shape=[
            jax.ShapeDtypeStruct((nb, s, hy3), BF16),
            jax.ShapeDtypeStruct((nb, heads, LANES, s), BF16),
            jax.ShapeDtypeStruct((nb, s, qk), BF16),
            jax.ShapeDtypeStruct((nb, heads, s // tk, VT_ROWS, tk), BF16),
        ],
        compiler_params=_cp("parallel", "parallel"),
        name="ev_inproj",
    )(*h_args, mods, g, w_in, cos_t, sin_t)


def _split_maps(qt, half):
    row = lax.broadcasted_iota(jnp.int32, qt.shape, 0)
    zero = jnp.zeros_like(qt)
    return jnp.where(row < half, qt, zero), jnp.where(row >= half, qt, zero)


def _attn_out(a0, a1, lam_refs, sg_ref, lam_init, dv):
    lq1_ref, lk1_ref, lq2_ref, lk2_ref = lam_refs
    lam = (jnp.exp(jnp.sum(lq1_ref[...] * lk1_ref[...], axis=-1, keepdims=True))
           - jnp.exp(jnp.sum(lq2_ref[...] * lk2_ref[...], axis=-1, keepdims=True)) + lam_init)
    o = (a0[:dv] / a0[dv:dv + 1] - lam * (a1[:dv] / a1[dv:dv + 1])).T
    on = o * lax.rsqrt(jnp.mean(o * o, axis=-1, keepdims=True) + EPS)
    return on * sg_ref[...] * (1.0 - lam_init)


def _col_max8(sc):
    part = sc[0:SUBLANES]
    for g in range(1, sc.shape[0] // SUBLANES):
        part = jnp.maximum(part, sc[g * SUBLANES:(g + 1) * SUBLANES])
    return part


def _attn_kernel(qt_ref, qnt_ref, k_ref, vt_ref, lq1_ref, lk1_ref, lq2_ref, lk2_ref, sg_ref, o_ref,
                 s_ref, mp_ref, p_ref, acc_ref, m_ref, alpha_ref, *, n_chunks, tk, lam_init, half):
    i = pl.program_id(2)
    dv = 2 * half
    lam_refs = (lq1_ref, lk1_ref, lq2_ref, lk2_ref)
    tq = qnt_ref.shape[-1]
    q_sub = (_split_maps(qt_ref[0, 0, :, 0:tq], half), _split_maps(qt_ref[0, 0, :, tq:2 * tq], half))
    q_next = _split_maps(qnt_ref[0, 0], half)

    def put_scores(slot, qts, c):
        kc = k_ref[0, pl.ds(pl.multiple_of(c * tk, tk), tk), :]
        for j in range(2):
            sc = _dot(kc, qts[j])
            s_ref[slot, j] = sc
            mp_ref[slot, j] = _col_max8(sc)

    def put_probs(slot, sub, first):
        for j in range(2):
            col_max = jnp.max(mp_ref[slot, j], axis=0, keepdims=True)
            if first:
                m_new = col_max
            else:
                m_old = m_ref[sub, j]
                m_new = jnp.maximum(m_old, col_max)
                alpha_ref[slot, j] = jnp.exp2(m_old - m_new)
            m_ref[sub, j] = m_new
            p_ref[slot, j] = jnp.exp2(s_ref[slot, j] - m_new).astype(BF16)

    def add_pv(slot, sub, c, first):
        for j in range(2):
            pv = _dot(vt_ref[0, 0, c], p_ref[slot, j])
            acc_ref[sub, j] = pv if first else alpha_ref[slot, j] * acc_ref[sub, j] + pv

    def sub_tile(sub, parity, qts, qts_after, sub_after):
        slot = lambda c: (c + parity) % 2
        put_scores(slot(0), qts, 2)
        put_probs(slot(1), sub, False)
        add_pv(slot(0), sub, 0, True)
        n_uniform = n_chunks - 3

        unroll = 2

        def group(u, carry):
            c = 1 + unroll * u
            for k in range(unroll):
                put_scores(slot(1 + k), qts, c + k + 2)
                put_probs(slot(k), sub, False)
                add_pv(slot(1 + k), sub, c + k, False)
            return carry

        lax.fori_loop(0, n_uniform // unroll, group, 0)
        c = n_chunks - 2
        put_scores(slot(c), qts_after, 0)
        put_probs(slot(c + 1), sub, False)
        add_pv(slot(c), sub, c, False)
        c = n_chunks - 1
        put_scores(slot(c), qts_after, 1)
        put_probs(slot(c + 1), sub_after, True)
        add_pv(slot(c), sub, c, False)
        o_ref[0, sub * tq:(sub + 1) * tq, :] = _attn_out(
            acc_ref[sub, 0], acc_ref[sub, 1], lam_refs, sg_ref, lam_init, dv).astype(o_ref.dtype)

    @pl.when(i == 0)
    def _():
        put_scores(0, q_sub[0], 0)
        put_probs(0, 0, True)
        put_scores(1, q_sub[0], 1)

    sub_tile(0, 0, q_sub[0], q_sub[1], 1)
    sub_tile(1, 1, q_sub[1], q_next, 0)


def _ctx_attn_kernel(qt_ref, k_ref, vt_ref, lq1_ref, lk1_ref, lq2_ref, lk2_ref, sg_ref, o_ref,
                     *, lam_init, half):
    acc = []
    for qtj in _split_maps(qt_ref[0, 0], half):
        sc = _dot(k_ref[0], qtj)
        col_max = jnp.max(_col_max8(sc), axis=0, keepdims=True)
        acc.append(_dot(vt_ref[0, 0, 0], jnp.exp2(sc - col_max).astype(BF16)))
    o_ref[0] = _attn_out(acc[0], acc[1], (lq1_ref, lk1_ref, lq2_ref, lk2_ref), sg_ref, lam_init,
                         2 * half).astype(o_ref.dtype)


def _attn_chunk(s):
    return next(t for t in (3 * TT, TT) if s % t == 0 and (s // t) % 2 == 1 and s // t >= 5)


def _diff_attention(qt, k, vt, lam_vecs, subln_g, *, n_lat_tiles, lam_init, head_dim):
    nb, s, qk = k.shape
    dv = 2 * head_dim
    heads = qk // dv
    l = n_lat_tiles * TT
    n_chunks, tk = vt.shape[2], vt.shape[4]
    tq = ATTN_SUB_Q if l % (2 * ATTN_SUB_Q) == 0 else TT
    assert l % (2 * tq) == 0 and n_chunks % 2 == 1 and n_chunks >= 5 and s - l == TT
    n_steps = l // (2 * tq)
    vec = lambda n: pl.BlockSpec((1, n), lambda b, h, i: (0, 0))
    vecs = [vec(head_dim)] * 4 + [vec(dv)]
    o_lat = pl.pallas_call(
        functools.partial(_attn_kernel, n_chunks=n_chunks, tk=tk, lam_init=lam_init, half=head_dim),
        grid=(nb, heads, n_steps),
        in_specs=[
            pl.BlockSpec((1, 1, dv, 2 * tq), lambda b, h, i: (b, h, 0, i)),
            pl.BlockSpec((1, 1, dv, tq),
                         lambda b, h, i: (b, h, 0, jnp.minimum(2 * i + 2, 2 * n_steps - 1))),
            pl.BlockSpec((1, s, dv), lambda b, h, i: (b, 0, h)),
            pl.BlockSpec((1, 1, n_chunks, VT_ROWS, tk), lambda b, h, i: (b, h, 0, 0, 0)),
        ] + vecs,
        out_specs=pl.BlockSpec((1, 2 * tq, dv), lambda b, h, i: (b, i, h)),
        out_shape=jax.ShapeDtypeStruct((nb, l, qk), BF16),
        scratch_shapes=[pltpu.VMEM((2, 2, tk, tq), F32), pltpu.VMEM((2, 2, SUBLANES, tq), F32),
                        pltpu.VMEM((2, 2, tk, tq), BF16), pltpu.VMEM((2, 2, VT_ROWS, tq), F32),
                        pltpu.VMEM((2, 2, 1, tq), F32), pltpu.VMEM((2, 2, 1, tq), F32)],
        compiler_params=_cp("parallel", "parallel", "arbitrary"),
        name="diff_attn",
    )(qt, qt, k, vt, *lam_vecs, subln_g)
    per = tk // TT
    vec2 = lambda n: pl.BlockSpec((1, n), lambda b, h: (0, 0))
    o_ctx = pl.pallas_call(
        functools.partial(_ctx_attn_kernel, lam_init=lam_init, half=head_dim),
        grid=(nb, heads),
        in_specs=[pl.BlockSpec((1, 1, dv, TT), lambda b, h: (b, h, 0, n_lat_tiles)),
                  pl.BlockSpec((1, TT, dv), lambda b, h: (b, n_lat_tiles, h)),
                  pl.BlockSpec((1, 1, 1, VT_ROWS, TT),
                               lambda b, h: (b, h, n_lat_tiles // per, 0, n_lat_tiles % per)),
                  ] + [vec2(head_dim)] * 4 + [vec2(dv)],
        out_specs=pl.BlockSpec((1, TT, dv), lambda b, h: (b, 0, h)),
        out_shape=jax.ShapeDtypeStruct((nb, TT, qk), BF16),
        compiler_params=_cp("parallel", "parallel"),
        name="ctx_attn",
    )(qt, k, vt, *lam_vecs, subln_g)
    return o_lat, o_ctx


def _halo_rows(dtype):
    return SUBLANES * 4 // jnp.dtype(dtype).itemsize


def _halo_fill(xp_ref, prev_ref, x_ref, next_ref, has_prev, has_next):
    hr = prev_ref.shape[2]
    zero = jnp.zeros((hr, xp_ref.shape[-1]), F32)
    xp_ref[hr:hr + TT, :] = x_ref[0].astype(F32)
    xp_ref[0:hr, :] = jnp.where(has_prev, prev_ref[0, 0].astype(F32), zero)
    xp_ref[hr + TT:2 * hr + TT, :] = jnp.where(has_next, next_ref[0, 0].astype(F32), zero)


def _conv_taps(xp, w, bias, left):
    rows = xp.shape[0]
    hr = (rows - TT) // 2
    before = None
    for j in range(left):
        z = w[j:j + 1, :] * xp
        before = pltpu.roll(z if before is None else before + z, 1, 0)
    after = None
    for j in range(w.shape[0] - 1, left, -1):
        z = w[j:j + 1, :] * xp
        after = pltpu.roll(z if after is None else after + z, rows - 1, 0)
    acc = w[left:left + 1, :] * xp + bias
    for part in (before, after):
        if part is not None:
            acc = acc + part
    return acc[hr:hr + TT, :]


def _shortconv_kernel(prev_ref, x_ref, next_ref, w_ref, b_ref, v_ref, x1_ref, x2_ref, xp_ref,
                      *, n_seg_tiles, hy):
    i = pl.program_id(1)
    _halo_fill(xp_ref, prev_ref, x_ref, next_ref, i > 0, i < n_seg_tiles - 1)
    y = _conv_taps(xp_ref[...], w_ref[...], b_ref[...], HY_SHORT_LEFT)
    v_ref[0] = y[:, :hy].astype(v_ref.dtype)
    x1_ref[0] = y[:, hy:2 * hy].astype(x1_ref.dtype)
    x2_ref[0] = y[:, 2 * hy:].astype(x2_ref.dtype)


def _shortconv(u, w, b, *, tile0, n_seg_tiles):
    nb, s, hy3 = u.shape
    hy = hy3 // 3
    hr = _halo_rows(u.dtype)
    nh, per = s // hr, TT // hr
    uh = u.reshape(nb, nh, hr, hy3)
    prev = pl.BlockSpec((1, 1, hr, hy3),
                        lambda b, i: (b, jnp.maximum((tile0 + i) * per - 1, 0), 0, 0))
    nxt = pl.BlockSpec((1, 1, hr, hy3),
                       lambda b, i: (b, jnp.minimum((tile0 + i + 1) * per, nh - 1), 0, 0))
    out = jax.ShapeDtypeStruct((nb, n_seg_tiles * TT, hy), BF16)
    ospec = pl.BlockSpec((1, TT, hy), lambda b, i: (b, i, 0))
    return pl.pallas_call(
        functools.partial(_shortconv_kernel, n_seg_tiles=n_seg_tiles, hy=hy),
        grid=(nb, n_seg_tiles),
        in_specs=[
            prev,
            pl.BlockSpec((1, TT, hy3), lambda b, i: (b, tile0 + i, 0)),
            nxt,
            pl.BlockSpec(w.shape, lambda b, i: (0, 0)),
            pl.BlockSpec((1, hy3), lambda b, i: (0, 0)),
        ],
        out_specs=[ospec, ospec, ospec],
        out_shape=[out, out, out],
        scratch_shapes=[pltpu.VMEM((TT + 2 * hr, hy3), F32)],
        compiler_params=_cp("parallel", "parallel"),
        name="hy_shortconv",
    )(uh, u, uh, w, b.reshape(1, hy3))


def _filter_feats(lh, n_cols):
    n = np.arange(2 * lh).reshape(-1, n_cols).T.reshape(-1)
    lag = np.where(n < lh, n, 2 * lh - n).astype(np.float64)
    t = (lag / lh).astype(np.float32).astype(np.float64)
    bands = np.arange(1, N_BANDS + 1, dtype=np.float64)
    ang = 2.0 * math.pi * t[:, None] * bands
    feats = np.concatenate([t[:, None], np.cos(ang), np.sin(ang)], axis=-1)
    pad = (-(feats.shape[1] + 1)) % SUBLANES
    return np.concatenate([feats, np.zeros((2 * lh, pad)), n[:, None].astype(np.float64)], axis=-1)


def _filt_kernel(ft_ref, w1_ref, b1_ref, w2_ref, b2_ref, w3_ref, b3_ref, fr_ref, dec_ref, o_ref,
                 *, lh, c):
    ft = ft_ref[...]
    freq = fr_ref[...]
    h = jnp.sin(freq * (_dot(ft, w1_ref[...], precision=HIGHEST) + b1_ref[...]))
    h = jnp.sin(freq * (_dot(h, w2_ref[...], precision=HIGHEST) + b2_ref[...])).astype(BF16)
    tr = o_ref.shape[1]
    half = tr // 2
    for q in range(ft.shape[0] // tr):
        for dr in range(2):
            rs = slice(q * tr + dr * half, q * tr + (dr + 1) * half)
            t = ft[rs, 0:1]
            f = (_dot(h[rs], w3_ref[dr]) + b3_ref[dr]) * jnp.exp(-t * jnp.abs(dec_ref[dr]))
            if dr == 1:
                f = jnp.where(ft[rs, ft.shape[1] - 1:] == lh, 0.0, f)
            for o in range(2):
                o_ref[o, dr * half:(dr + 1) * half, q * c:(q + 1) * c] = f[:, o * c:(o + 1) * c]


def _hyena_filter(lh, n_cols, w1, b1, w2, b2, w3, b3, freq, decay):
    c = decay.shape[-1]
    feats = jnp.asarray(_filter_feats(lh, n_cols), F32)
    fe = feats.shape[1]
    w1p = jnp.pad(w1, ((0, fe - w1.shape[0]), (0, 0)))
    hid = w1.shape[1]
    tr = 2 * lh // n_cols
    cb = math.gcd(n_cols, FILT_COL_BLOCKS)
    by_dir = lambda a, lead: jnp.moveaxis(a.reshape(lead, 2, 2, c), 2, 0).reshape(2, lead, 2 * c)
    full = lambda a: pl.BlockSpec(a.shape, lambda i: (0,) * a.ndim)
    args = (w1p, b1.reshape(1, hid), w2, b2.reshape(1, hid), by_dir(w3, hid).astype(BF16),
            by_dir(b3, 1), freq.reshape(1, hid), by_dir(decay, 1))
    return pl.pallas_call(
        functools.partial(_filt_kernel, lh=lh, c=c),
        grid=(n_cols // cb,),
        in_specs=[pl.BlockSpec((cb * tr, fe), lambda i: (i, 0))] + [full(a) for a in args],
        out_specs=pl.BlockSpec((2, tr, cb * c), lambda i: (0, 0, i)),
        out_shape=jax.ShapeDtypeStruct((2, tr, n_cols * c), F32),
        compiler_params=_cp("parallel"),
        name="hy_filter",
    )(feats, *args)


@functools.lru_cache(maxsize=None)
def _dft_tables(n):
    nn = n * n
    h = n // 2
    k = np.arange(n)
    th = 2.0 * math.pi * np.outer(k, k) / n
    c, s = np.cos(th), np.sin(th)
    f1_data = np.block([[c[:, :h], s[:, :h]], [-s[:, :h], c[:, :h]]])
    f1_real = np.concatenate([c, -s], axis=0)
    idx = (k[None, None, :] * (k[:, None, None] + n * k[None, :, None])) % nn
    phi = 2.0 * math.pi * idx / nn
    cp, sp = np.cos(phi), np.sin(phi)
    g = np.concatenate([np.concatenate([cp, sp], axis=2), np.concatenate([-sp, cp], axis=2)], axis=1)
    hmat = np.transpose(g, (0, 2, 1)) / nn
    ci, si = c[:h, :], s[:h, :]
    f3 = np.zeros((n, 2 * n))
    f3[:h, 0::2], f3[:h, 1::2] = ci, -si
    f3[h:, 0::2], f3[h:, 1::2] = si, ci
    return f1_data, f1_real, g, hmat, f3


def _fft_s1_kernel(x_ref, f_ref, o_ref):
    o_ref[0] = _dot(f_ref[...], x_ref[0].astype(BF16)).astype(o_ref.dtype)


def _fft_s1(x, f1, n, c):
    p = x.shape[0]
    tc = FFT_N2_TILE * c
    return pl.pallas_call(
        _fft_s1_kernel,
        grid=(p, n * c // tc),
        in_specs=[pl.BlockSpec((1, n, tc), lambda q, j: (q, 0, j)),
                  pl.BlockSpec(f1.shape, lambda q, j: (0, 0))],
        out_specs=pl.BlockSpec((1, 2 * n, tc), lambda q, j: (q, 0, j)),
        out_shape=jax.ShapeDtypeStruct((p, 2 * n, n * c), BF16),
        compiler_params=_cp("parallel", "parallel"),
        name="fft_s1",
    )(x, f1)


def _fft_spec_kernel(a_ref, g_ref, o_ref, *, tk):
    for j in range(tk):
        x = jnp.concatenate([a_ref[0, 0, j], a_ref[0, 1, j]], axis=0)
        o_ref[0, j] = _dot(g_ref[j], x).astype(o_ref.dtype)


def _fft_mid_kernel(a_ref, g_ref, h_ref, kh_ref, o_ref, *, tk, n):
    for j in range(tk):
        x = jnp.concatenate([a_ref[0, 0, j], a_ref[0, 1, j]], axis=0)
        t = _dot(g_ref[j], x)
        tr, ti = t[:n], t[n:]
        kr, ki = kh_ref[0, j, :n].astype(F32), kh_ref[0, j, n:].astype(F32)
        y = jnp.concatenate([tr * kr - ti * ki, tr * ki + ti * kr], axis=0).astype(BF16)
        o_ref[0, j] = _dot(h_ref[j], y).astype(o_ref.dtype)


def _fft_spectrum(a, g, n, c):
    p = a.shape[0]
    tk = FFT_K1_TILE
    a5 = a.reshape(p, 2, n, n, c)
    return pl.pallas_call(
        functools.partial(_fft_spec_kernel, tk=tk),
        grid=(p, n // tk),
        in_specs=[pl.BlockSpec((1, 2, tk, n, c), lambda q, j: (q, 0, j, 0, 0)),
                  pl.BlockSpec((tk, 2 * n, 2 * n), lambda q, j: (j, 0, 0))],
        out_specs=pl.BlockSpec((1, tk, 2 * n, c), lambda q, j: (q, j, 0, 0)),
        out_shape=jax.ShapeDtypeStruct((p, n, 2 * n, c), BF16),
        compiler_params=_cp("parallel", "parallel"),
        name="fft_spectrum",
    )(a5, g)


def _fft_mid(a, g, hm, khat, order, n, c):
    p = a.shape[0]
    tk = FFT_K1_TILE
    a5 = a.reshape(p, 2, n, n, c)
    return pl.pallas_call(
        functools.partial(_fft_mid_kernel, tk=tk, n=n),
        grid=(p, n // tk),
        in_specs=[pl.BlockSpec((1, 2, tk, n, c), lambda q, j: (q, 0, j, 0, 0)),
                  pl.BlockSpec((tk, 2 * n, 2 * n), lambda q, j: (j, 0, 0)),
                  pl.BlockSpec((tk, 2 * n, 2 * n), lambda q, j: (j, 0, 0)),
                  pl.BlockSpec((1, tk, 2 * n, c), lambda q, j: (order, j, 0, 0))],
        out_specs=pl.BlockSpec((1, tk, 2 * n, c), lambda q, j: (q, j, 0, 0)),
        out_shape=jax.ShapeDtypeStruct((p, n, 2 * n, c), BF16),
        compiler_params=_cp("parallel", "parallel"),
        name="fft_mid",
    )(a5, g, hm, khat)


def _fft_s3_kernel(*refs, gated, chained):
    c_ref, f_ref, v_ref, b_ref = refs[:4]
    rest = list(refs[4:])
    y = _dot(f_ref[...], c_ref[0]) + v_ref[0].astype(F32) * b_ref[...]
    if gated:
        y = rest.pop(0)[0].astype(F32) * y
    f1_ref = rest.pop(0) if chained else None
    o_ref = rest.pop(0)
    z = y.astype(o_ref.dtype)
    o_ref[0] = z
    if chained:
        a_ref = rest.pop(0)
        a_ref[0] = _dot(f1_ref[...], z).astype(a_ref.dtype)


def _fft_s3(cm, f3, vin, xg, bias, n, c, f1_next=None):
    p = cm.shape[0]
    tc = FFT_N2_TILE * c
    c2 = cm.reshape(p, 2 * n, n * c)
    bias_t = jnp.tile(bias.reshape(1, c), (1, FFT_N2_TILE))
    blk = pl.BlockSpec((1, n, tc), lambda q, j: (q, 0, j))
    full = lambda a: pl.BlockSpec(a.shape, lambda q, j: (0, 0))
    in_specs = [pl.BlockSpec((1, 2 * n, tc), lambda q, j: (q, 0, j)), full(f3), blk,
                pl.BlockSpec((1, tc), lambda q, j: (0, 0))]
    args = [c2, f3, vin, bias_t]
    out_specs, out_shape = [blk], [jax.ShapeDtypeStruct((p, n, n * c), BF16)]
    if xg is not None:
        in_specs.append(blk)
        args.append(xg)
    if f1_next is not None:
        in_specs.append(full(f1_next))
        args.append(f1_next)
        out_specs.append(pl.BlockSpec((1, 2 * n, tc), lambda q, j: (q, 0, j)))
        out_shape.append(jax.ShapeDtypeStruct((p, 2 * n, n * c), BF16))
    return pl.pallas_call(
        functools.partial(_fft_s3_kernel, gated=xg is not None, chained=f1_next is not None),
        grid=(p, n * c // tc),
        in_specs=in_specs,
        out_specs=out_specs,
        out_shape=out_shape,
        compiler_params=_cp("parallel", "parallel"),
        name="fft_s3",
    )(*args)


def _hyena_long(v, x1, kk, hy_bias):
    nb, l, c = v.shape
    n = math.isqrt(2 * l)
    assert n * n == 2 * l and nb % 2 == 0
    p = nb // 2
    f1d, f1r, g, hm, f3 = (jnp.asarray(t, F32).astype(BF16) for t in _dft_tables(n))
    pair = lambda a: a.reshape(p, n, n * c)
    khat = _fft_spectrum(_fft_s1(kk, f1r, n, c), g, n, c)
    v = pair(v)
    cm = _fft_mid(_fft_s1(v, f1d, n, c), g, hm, khat, 0, n, c)
    z1, a = _fft_s3(cm, f3, v, pair(x1), hy_bias[0], n, c, f1_next=f1d)
    cm = _fft_mid(a, g, hm, khat, 1, n, c)
    (w2,) = _fft_s3(cm, f3, z1, None, hy_bias[1], n, c)
    return w2.reshape(nb, l, c)


@functools.lru_cache(maxsize=None)
def _ctx_dft_tables(lc):
    m = 2 * lc
    k = np.arange(m)
    th = 2.0 * math.pi * np.outer(k, k) / m
    c, s = np.cos(th), np.sin(th)
    f_data = np.block([[c[:, :lc], s[:, :lc]], [-s[:, :lc], c[:, :lc]]])
    f_real = np.concatenate([c, -s], axis=0)
    ci, si = c[:lc, :], s[:lc, :]
    f_inv = np.block([[ci, -si], [si, ci]]) / m
    return f_data, f_real, f_inv


def _ctxconv_kernel(v_ref, x1_ref, x2_ref, kk_ref, fd_ref, fr_ref, fi_ref, b_ref, o_ref, *, m):
    def conv(u, order):
        kh = _dot(fr_ref[...], kk_ref[order].astype(BF16))
        t = _dot(fd_ref[...], u.astype(BF16))
        tr, ti, kr, ki = t[:m], t[m:], kh[:m], kh[m:]
        y = jnp.concatenate([tr * kr - ti * ki, tr * ki + ti * kr], axis=0).astype(BF16)
        return _dot(fi_ref[...], y)

    v = v_ref[0].astype(F32)
    z1 = x1_ref[0].astype(F32) * (conv(v, 0) + v * b_ref[0:1, :])
    o_ref[0] = (x2_ref[0].astype(F32) * (conv(z1, 1) + z1 * b_ref[1:2, :])).astype(o_ref.dtype)


def _hyena_ctx(v, x1, x2, kk, hy_bias):
    nb, lc, c = v.shape
    p, m = nb // 2, 2 * lc
    fd, fr, fi = (jnp.asarray(t, F32).astype(BF16) for t in _ctx_dft_tables(lc))
    pair = lambda a: a.reshape(p, m, c)
    blk = pl.BlockSpec((1, m, c), lambda q: (q, 0, 0))
    full = lambda a: pl.BlockSpec(a.shape, lambda q: (0,) * a.ndim)
    z = pl.pallas_call(
        functools.partial(_ctxconv_kernel, m=m),
        grid=(p,),
        in_specs=[blk, blk, blk, full(kk), full(fd), full(fr), full(fi), full(hy_bias)],
        out_specs=blk,
        out_shape=jax.ShapeDtypeStruct((p, m, c), BF16),
        compiler_params=_cp("parallel"),
        name="hy_ctx",
    )(pair(v), pair(x1), pair(x2), kk, fd, fr, fi, hy_bias)
    return z.reshape(nb, lc, c)


def _od_inproj_kernel(prev_ref, h_ref, next_ref, mod_ref, g_ref, w_ref, cw_ref, cb_ref,
                      gate_ref, xc_ref, *, d, r, n_lat_tiles, n_tiles):
    i = pl.program_id(1)
    seg_first = jnp.logical_or(i == 0, i == n_lat_tiles)
    seg_last = jnp.logical_or(i == n_lat_tiles - 1, i == n_tiles - 1)
    m = mod_ref[0]
    rows = jnp.concatenate([prev_ref[0, 0], h_ref[0], next_ref[0, 0]], axis=0)
    xn = _rms_mod(rows, g_ref[...], m[:, 0:d], m[:, d:2 * d]).astype(BF16)
    x = _dot(xn, w_ref[:, r:])
    gate_ref[0] = _dot(xn, w_ref[:, :r])[SUBLANES:SUBLANES + TT].astype(gate_ref.dtype)
    row = lax.broadcasted_iota(jnp.int32, (x.shape[0], 1), 0)
    outside = jnp.logical_or(jnp.logical_and(row < SUBLANES, seg_first),
                             jnp.logical_and(row >= SUBLANES + TT, seg_last))
    x = jnp.where(outside, 0.0, x)
    xc_ref[0] = _conv_taps(x, cw_ref[...], cb_ref[...], RG_CONV_LEFT)


def _od_inproj(h, mods, rows, layer, g, w_in, conv_w, conv_b, *, n_lat_tiles):
    nb, s, d = h.shape
    r = w_in.shape[1] // 2
    n_tiles = s // TT
    n8, per = s // SUBLANES, TT // SUBLANES
    h8 = h.reshape(nb, n8, SUBLANES, d)
    ospec = pl.BlockSpec((1, TT, r), lambda b, i: (b, i, 0))
    return pl.pallas_call(
        functools.partial(_od_inproj_kernel, d=d, r=r, n_lat_tiles=n_lat_tiles, n_tiles=n_tiles),
        grid=(nb, n_tiles),
        in_specs=[
            pl.BlockSpec((1, 1, SUBLANES, d), lambda b, i: (b, jnp.maximum(i * per - 1, 0), 0, 0)),
            pl.BlockSpec((1, TT, d), lambda b, i: (b, i, 0)),
            pl.BlockSpec((1, 1, SUBLANES, d),
                         lambda b, i: (b, jnp.minimum((i + 1) * per, n8 - 1), 0, 0)),
            _mod_spec(layer, rows, nb, n_lat_tiles, mods.shape[-1]),
            pl.BlockSpec((1, d), lambda b, i: (0, 0)),
            pl.BlockSpec(w_in.shape, lambda b, i: (0, 0)),
            pl.BlockSpec(conv_w.shape, lambda b, i: (0, 0)),
            pl.BlockSpec((1, r), lambda b, i: (0, 0)),
        ],
        out_specs=[ospec, ospec],
        out_shape=[jax.ShapeDtypeStruct((nb, s, r), BF16), jax.ShapeDtypeStruct((nb, s, r), F32)],
        compiler_params=_cp("parallel", "parallel"),
        name="od_inproj",
    )(h8, h, h8, mods, g, w_in, conv_w, conv_b.reshape(1, r))


def _rglru_kernel(x_ref, w_ref, b_ref, lam_ref, o_ref, a_ref, bb_ref, carry_ref, *, r, windows):
    dr = pl.program_id(1)
    i = pl.program_id(2)
    half = TT // 2
    nl = -lam_ref[0]
    softplus = jnp.maximum(nl, 0.0) + jnp.log1p(jnp.exp(-jnp.abs(nl)))
    neg_rate = (0.5 * RG_C) * softplus
    exp2_rate = (-0.5 * RG_C * math.log2(math.e)) * softplus

    def gates(r0):
        rs = slice(r0, r0 + half)
        xc = x_ref[0, rs, :]
        xb = xc.astype(BF16)
        half_x = 0.5 * xc
        for c0, c1, k0, k1 in windows:
            xk = xb[:, k0:k1]
            tr = jnp.tanh(_dot(xk, w_ref[0, k0:k1, c0:c1]) + b_ref[0, :, c0:c1]) + 1.0
            ti = jnp.tanh(_dot(xk, w_ref[0, k0:k1, r + c0:r + c1]) + b_ref[0, :, r + c0:r + c1]) + 1.0
            a = jnp.exp2(exp2_rate[:, c0:c1] * tr)
            a_ref[rs, c0:c1] = a
            e = jnp.tanh(neg_rate[:, c0:c1] * tr) * (a * a + 1.0)
            root = jnp.where(e > 0.0, e * lax.rsqrt(e), 0.0)
            bb_ref[rs, c0:c1] = root * (ti * half_x[:, c0:c1])

    def scan_rows(rows, h):
        for row in rows:
            h = a_ref[row:row + 1, :] * h + bb_ref[row:row + 1, :]
            o_ref[0, 0, row:row + 1, :] = h
        return h

    def scan_loop(r0, reverse, h):
        n_blk = half // SUBLANES

        def block(g, h):
            base = pl.multiple_of(r0 + (n_blk - 1 - g if reverse else g) * SUBLANES, SUBLANES)
            for k in range(SUBLANES):
                row = base + (SUBLANES - 1 - k if reverse else k)
                h = a_ref[pl.ds(row, 1), :] * h + bb_ref[pl.ds(row, 1), :]
                o_ref[0, 0, pl.ds(row, 1), :] = h
            return h

        return lax.fori_loop(0, n_blk, block, h)

    @pl.when(i == 0)
    def _():
        carry_ref[...] = jnp.zeros_like(carry_ref)

    def run(reverse):
        first, second = (half, 0) if reverse else (0, half)
        gates(first)
        gates(second)
        rows = range(first + half - 1, first - 1, -1) if reverse else range(first, first + half)
        h = scan_rows(rows, carry_ref[0:1, :])
        carry_ref[0:1, :] = scan_loop(second, reverse, h)

    pl.when(dr == 0)(lambda: run(False))
    pl.when(dr == 1)(lambda: run(True))


def _scan_tile(dr, i, n_lat_tiles, n_tiles):
    fwd = jnp.where(i == 0, n_lat_tiles, i - 1)
    bwd = jnp.where(i == 0, n_lat_tiles, n_lat_tiles - i)
    return jnp.where(dr == 0, fwd, bwd)


def _gate_windows(r, bs):
    out = []
    for c0 in range(0, r, 2 * LANES):
        c1 = min(c0 + 2 * LANES, r)
        k0 = (c0 // bs) * bs // LANES * LANES
        k1 = min(-(-(((c1 - 1) // bs + 1) * bs) // LANES) * LANES, r)
        out.append((c0, c1, k0, k1))
    return tuple(out)


def _rglru(xc, wcat, bcat, lam, *, n_lat_tiles, block_size):
    nb, s, r = xc.shape
    n_tiles = s // TT
    assert n_tiles == n_lat_tiles + 1
    tile = lambda d, i: _scan_tile(d, i, n_lat_tiles, n_tiles)
    return pl.pallas_call(
        functools.partial(_rglru_kernel, r=r, windows=_gate_windows(r, block_size)),
        grid=(nb, 2, n_tiles),
        in_specs=[
            pl.BlockSpec((1, TT, r), lambda b, d, i: (b, tile(d, i), 0)),
            pl.BlockSpec((1, r, 2 * r), lambda b, d, i: (d, 0, 0)),
            pl.BlockSpec((1, 1, 2 * r), lambda b, d, i: (d, 0, 0)),
            pl.BlockSpec((1, 1, r), lambda b, d, i: (d, 0, 0)),
        ],
        out_specs=pl.BlockSpec((1, 1, TT, r), lambda b, d, i: (d, b, tile(d, i), 0)),
        out_shape=jax.ShapeDtypeStruct((2, nb, s, r), F32),
        scratch_shapes=[pltpu.VMEM((TT, r), F32), pltpu.VMEM((TT, r), F32),
                        pltpu.VMEM((SUBLANES, r), F32)],
        compiler_params=_cp("parallel", "parallel", "arbitrary"),
        name="rglru",
    )(xc, wcat, bcat, lam.reshape(2, 1, r))


def _block_diag(w):
    n, bs, _ = w.shape
    eye = jnp.eye(n, dtype=w.dtype)
    return (eye[:, None, :, None] * w[:, :, None, :]).reshape(n * bs, n * bs)


def _gelu_tanh(x):
    return 0.5 * x * (1.0 + jnp.tanh(math.sqrt(2.0 / math.pi) * (x + 0.044715 * (x * x * x))))


def _post_kernel(*refs, kind, final, d, n_lat_tiles):
    if kind == "even":
        (hl_ref, hc_ref, x2_ref, wl_ref, zc_ref, ol_ref, oc_ref, mod_ref, g2_ref, wo_ref, w1_ref,
         w2_ref) = refs[:12]
        rest = refs[12:]
        is_ctx = pl.program_id(1) >= n_lat_tiles
        z_lat = (x2_ref[0].astype(F32) * wl_ref[0].astype(F32)).astype(BF16)
        z = jnp.where(is_ctx, zc_ref[0], z_lat)
        o = jnp.where(is_ctx, oc_ref[0], ol_ref[0])
        half = z.shape[-1]
        y = _dot(z, wo_ref[:half, :]) + _dot(o, wo_ref[half:, :])
    else:
        hl_ref, hc_ref, hd_ref, gate_ref, mod_ref, g2_ref, wo_ref, w1_ref, w2_ref = refs[:9]
        rest = refs[9:]
        mix = (hd_ref[0, 0] + hd_ref[1, 0]) * _gelu_tanh(gate_ref[0].astype(F32))
        y = _dot(mix.astype(BF16), wo_ref[...])
    out_ref = rest[-1]
    m = mod_ref[0]
    h1 = _stream_tile(hl_ref, hc_ref, n_lat_tiles) + m[:, 2 * d:3 * d] * y
    xn = _rms_mod(h1, g2_ref[...], m[:, 3 * d:4 * d], m[:, 4 * d:5 * d]).astype(BF16)
    acc = jnp.zeros_like(h1)
    dff = w1_ref.shape[1]
    for c0 in range(0, dff, FF_CHUNK):
        a = jnp.maximum(_dot(xn, w1_ref[:, c0:c0 + FF_CHUNK]), 0.0)
        acc = acc + _dot((a * a).astype(BF16), w2_ref[c0:c0 + FF_CHUNK, :])
    h2 = h1 + m[:, 5 * d:6 * d] * acc
    if final:
        fg_ref = rest[0]
        h2 = h2 * lax.rsqrt(jnp.mean(h2 * h2, axis=-1, keepdims=True) + EPS) * fg_ref[...]
    out_ref[0] = h2


def _post(kind, h, mix_args, mods, rows, layer, g2, w_out, w1, w2, final_g, *, n_lat_tiles, final):
    h_specs, h_args = _stream_specs(h, n_lat_tiles)
    nb, _, d = h_args[0].shape
    n_tiles = n_lat_tiles if final else n_lat_tiles + 1
    tok = lambda w: pl.BlockSpec((1, TT, w), lambda b, i: (b, i, 0))
    full = lambda a: pl.BlockSpec(a.shape, lambda b, i: (0,) * a.ndim)
    if kind == "even":
        lat = lambda w: pl.BlockSpec((1, TT, w), lambda b, i: (b, jnp.minimum(i, n_lat_tiles - 1), 0))
        ctx = lambda w: pl.BlockSpec((1, TT, w), lambda b, i: (b, 0, 0))
        x2_lat, w_lat, z_ctx, o_lat, o_ctx = mix_args
        mix_specs = [lat(x2_lat.shape[-1]), lat(w_lat.shape[-1]), ctx(z_ctx.shape[-1]),
                     lat(o_lat.shape[-1]), ctx(o_ctx.shape[-1])]
    else:
        hd, gate = mix_args
        r = gate.shape[-1]
        mix_specs = [pl.BlockSpec((2, 1, TT, r), lambda b, i: (0, b, i, 0)), tok(r)]
    in_specs = h_specs + mix_specs + [
        _mod_spec(layer, rows, nb, n_lat_tiles, mods.shape[-1]),
        pl.BlockSpec((1, d), lambda b, i: (0, 0)), full(w_out), full(w1), full(w2)]
    args = [*h_args, *mix_args, mods, g2, w_out, w1, w2]
    if final:
        in_specs.append(pl.BlockSpec((1, d), lambda b, i: (0, 0)))
        args.append(final_g)
    return pl.pallas_call(
        functools.partial(_post_kernel, kind=kind, final=final, d=d, n_lat_tiles=n_lat_tiles),
        grid=(nb, n_tiles),
        in_specs=in_specs,
        out_specs=tok(d),
        out_shape=jax.ShapeDtypeStruct((nb, n_tiles * TT, d), F32),
        compiler_params=_cp("parallel", "parallel"),
        name="post_" + kind,
    )(*args)


@functools.lru_cache(maxsize=None)
def _rope_tables(l, lc, head_dim):
    axis = head_dim // 2
    freqs = ROPE_BASE ** (-np.arange(0, axis, 2, dtype=np.float64) / axis)
    freqs = freqs.astype(np.float32).astype(np.float64)
    t = np.arange(l)
    ang_r = (t // GRID_W)[:, None] * freqs
    ang_c = (t % GRID_W)[:, None] * freqs
    cos = np.concatenate([np.cos(ang_r)] * 2 + [np.cos(ang_c)] * 2, axis=-1)
    sin = np.concatenate([-np.sin(ang_r), np.sin(ang_r), -np.sin(ang_c), np.sin(ang_c)], axis=-1)
    cos = np.concatenate([cos, np.ones((lc, head_dim))], axis=0)
    sin = np.concatenate([sin, np.zeros((lc, head_dim))], axis=0)
    rep = LANES // head_dim
    return np.tile(cos, (1, rep)), np.tile(sin, (1, rep))


def kernel(x, c, ctx, c_ctx, ada_w, ada_b, norm1_g, norm2_g, mlp_w1, mlp_w2, final_g, ev_w_in, ev_w_out, hy_short_w, hy_short_b, hy_f_w1, hy_f_b1, hy_f_w2, hy_f_b2, hy_f_w3, hy_f_b3, hy_f_freq, hy_f_decay, hy_bias, df_lq1, df_lk1, df_lq2, df_lk2, df_subln_g, od_w_in, od_w_out, rg_conv_w, rg_conv_b, rg_wa, rg_ba, rg_wx, rg_bx, rg_lam):
    nb, l, d = x.shape
    lc = ctx.shape[1]
    depth = ada_w.shape[0]
    hy = hy_bias.shape[-1]
    head_dim = df_lq1.shape[-1]
    qk = (ev_w_in.shape[-1] - 3 * hy) // 3
    r = rg_lam.shape[-1]
    assert l % TT == 0 and lc == TT and l % GRID_W == 0
    n_lat_tiles = l // TT

    rows = -(-(nb + 1) // SUBLANES) * SUBLANES
    cond = jnp.zeros((rows, d), F32).at[:nb].set(c).at[nb].set(c_ctx)
    mods = _ada_mods(cond, ada_w, ada_b).reshape(depth * rows, 1, ada_w.shape[-1])

    cos_t, sin_t = (jnp.asarray(t, F32) for t in _rope_tables(l, lc, head_dim))
    h = (x, ctx)
    w1_bf, w2_bf = mlp_w1.astype(BF16), mlp_w2.astype(BF16)

    for i in range(depth):
        j = i // 2
        final = i == depth - 1
        g1, g2 = norm1_g[i].reshape(1, d), norm2_g[i].reshape(1, d)
        if i % 2 == 0:
            lam_init = 0.8 - 0.6 * math.exp(-0.3 * i)
            u, qt, k, vt = _ev_inproj(h, mods, rows, i, g1, ev_w_in[j].astype(BF16), cos_t, sin_t,
                                      n_lat_tiles=n_lat_tiles, hy3=3 * hy, qk=qk, head_dim=head_dim,
                                      tk=_attn_chunk(l + lc))
            lam_vecs = [a[j].reshape(1, -1) for a in (df_lq1, df_lk1, df_lq2, df_lk2)]
            o_lat, o_ctx = _diff_attention(qt, k, vt, lam_vecs, df_subln_g[j].reshape(1, -1),
                                           n_lat_tiles=n_lat_tiles, lam_init=lam_init,
                                           head_dim=head_dim)
            fparams = (hy_f_w1[j], hy_f_b1[j], hy_f_w2[j], hy_f_b2[j], hy_f_w3[j], hy_f_b3[j],
                       hy_f_freq[j], hy_f_decay[j])
            vl, x1l, x2l = _shortconv(u, hy_short_w[j], hy_short_b[j], tile0=0,
                                      n_seg_tiles=n_lat_tiles)
            vc, x1c, x2c = _shortconv(u, hy_short_w[j], hy_short_b[j], tile0=n_lat_tiles,
                                      n_seg_tiles=lc // TT)
            w_lat = _hyena_long(vl, x1l, _hyena_filter(l, math.isqrt(2 * l), *fparams), hy_bias[j])
            z_ctx = _hyena_ctx(vc, x1c, x2c, _hyena_filter(lc, 1, *fparams), hy_bias[j])
            h = _post("even", h, (x2l, w_lat, z_ctx, o_lat, o_ctx), mods, rows, i, g2, ev_w_out[j].astype(BF16),
                      w1_bf[i], w2_bf[i], final_g.reshape(1, d), n_lat_tiles=n_lat_tiles,
                      final=final)
        else:
            gate, xc = _od_inproj(h, mods, rows, i, g1, od_w_in[j].astype(BF16), rg_conv_w[j],
                                  rg_conv_b[j], n_lat_tiles=n_lat_tiles)
            wcat = (0.5 * jnp.stack(
                [jnp.concatenate([_block_diag(rg_wa[j, dd]), _block_diag(rg_wx[j, dd])], axis=1)
                 for dd in range(2)])).astype(BF16)
            bcat = 0.5 * jnp.concatenate([rg_ba[j], rg_bx[j]], axis=-1).reshape(2, 1, 2 * r)
            hd = _rglru(xc, wcat, bcat, rg_lam[j], n_lat_tiles=n_lat_tiles,
                        block_size=rg_wa.shape[-1])
            h = _post("odd", h, (hd, gate), mods, rows, i, g2, od_w_out[j].astype(BF16),
                      w1_bf[i], w2_bf[i], final_g.reshape(1, d), n_lat_tiles=n_lat_tiles,
                      final=final)
    return h
```

```python
import functools
import math

import numpy as np
import jax
import jax.numpy as jnp
from jax import lax
from jax.experimental import pallas as pl
from jax.experimental.pallas import tpu as pltpu

F32 = jnp.float32
BF16 = jnp.bfloat16
HIGHEST = lax.Precision.HIGHEST

EPS = 1e-6
GRID_W = 64
ROPE_BASE = 10000.0
N_BANDS = 16
RG_C = 8.0
RG_CONV_LEFT = 2
HY_SHORT_LEFT = 1

TT = 256
LANES = 128
SUBLANES = 8
VMEM_LIMIT = 56 * 1024 * 1024
FF_CHUNK = 1024
FFT_N2_TILE = 8
FFT_K1_TILE = 8
ADA_K_TILE = 256
ATTN_SUB_Q = 512
FILT_COL_BLOCKS = 4
VT_ROWS = LANES + 16


def _cp(*sem):
    return pltpu.CompilerParams(dimension_semantics=sem, vmem_limit_bytes=VMEM_LIMIT)


def _dot(a, b, **kw):
    return jnp.dot(a, b, preferred_element_type=F32, **kw)


def _sigmoid(x):
    return 1.0 / (1.0 + jnp.exp(-x))


def _rms_mod(x, g, shift, scale):
    y = x * lax.rsqrt(jnp.mean(x * x, axis=-1, keepdims=True) + EPS)
    return (y * g) * (1.0 + scale) + shift


def _ada_kernel(c_ref, w_ref, b_ref, o_ref):
    k = pl.program_id(1)
    c = c_ref[k]
    part = _dot((c * _sigmoid(c)).astype(BF16), w_ref[0].astype(BF16))

    @pl.when(k == 0)
    def _():
        o_ref[0] = part + b_ref[0]

    @pl.when(k > 0)
    def _():
        o_ref[0] += part


def _ada_mods(cond, ada_w, ada_b):
    depth, d, n = ada_w.shape
    rows = cond.shape[0]
    tk = min(d, ADA_K_TILE)
    cond_k = cond.reshape(rows, d // tk, tk).transpose(1, 0, 2)
    return pl.pallas_call(
        _ada_kernel,
        grid=(depth, d // tk),
        in_specs=[
            pl.BlockSpec(cond_k.shape, lambda l, k: (0, 0, 0)),
            pl.BlockSpec((1, tk, n), lambda l, k: (l, k, 0)),
            pl.BlockSpec((1, 1, n), lambda l, k: (l, 0, 0)),
        ],
        out_specs=pl.BlockSpec((1, rows, n), lambda l, k: (l, 0, 0)),
        out_shape=jax.ShapeDtypeStruct((depth, rows, n), F32),
        compiler_params=_cp("parallel", "arbitrary"),
        name="ada_mods",
    )(cond_k, ada_w, ada_b.reshape(depth, 1, n))


def _stream_specs(h, n_lat_tiles):
    separate = isinstance(h, tuple)
    lat_arr, ctx_arr = h if separate else (h, h)
    ctx_blk = 0 if separate else n_lat_tiles
    d = lat_arr.shape[-1]
    lat = pl.BlockSpec((1, TT, d), lambda b, i: (b, jnp.minimum(i, n_lat_tiles - 1), 0))
    ctx = pl.BlockSpec((1, TT, d), lambda b, i: (b, ctx_blk, 0))
    return [lat, ctx], [lat_arr, ctx_arr]


def _stream_tile(hl_ref, hc_ref, n_lat_tiles):
    return jnp.where(pl.program_id(1) >= n_lat_tiles, hc_ref[0], hl_ref[0])


def _mod_spec(layer, rows, nb, n_lat_tiles, n6):
    def imap(b, i):
        return (layer * rows + jnp.where(i >= n_lat_tiles, nb, b), 0, 0)
    return pl.BlockSpec((1, 1, n6), imap)


def _ev_inproj_kernel(hl_ref, hc_ref, mod_ref, g_ref, w_ref, cos_ref, sin_ref,
                      u_ref, qt_ref, k_ref, vt_ref, *, d, hy3, qk, qscale, n_lat_tiles):
    m = mod_ref[0]
    x = _stream_tile(hl_ref, hc_ref, n_lat_tiles)
    xn = _rms_mod(x, g_ref[...], m[:, 0:d], m[:, d:2 * d]).astype(BF16)
    cos = cos_ref[...]
    sin = sin_ref[...]
    lane = lax.broadcasted_iota(jnp.int32, cos.shape, 1)
    first = (lane % 32) < 16

    def rope(z):
        sw = jnp.where(first, pltpu.roll(z, LANES - 16, 1), pltpu.roll(z, 16, 1))
        return z * cos + sw * sin

    extra = vt_ref.shape[3] - LANES
    ones_row = jnp.where(lax.broadcasted_iota(jnp.int32, (extra, cos.shape[0]), 0) == 0,
                         1.0, 0.0).astype(BF16)
    yq = _dot(xn, w_ref[:, hy3:hy3 + qk])
    for c in range(qk // LANES):
        qt_ref[0, c] = (rope(yq[:, c * LANES:(c + 1) * LANES]) * qscale).T.astype(BF16)
    yk = _dot(xn, w_ref[:, hy3 + qk:hy3 + 2 * qk])
    for c in range(qk // LANES):
        k_ref[0, :, c * LANES:(c + 1) * LANES] = rope(yk[:, c * LANES:(c + 1) * LANES]).astype(BF16)
    yv = _dot(xn, w_ref[:, hy3 + 2 * qk:])
    for c in range(qk // LANES):
        vt_ref[0, c, 0, 0:LANES, :] = yv[:, c * LANES:(c + 1) * LANES].T.astype(BF16)
        vt_ref[0, c, 0, LANES:, :] = ones_row
    u_ref[0] = _dot(xn, w_ref[:, :hy3]).astype(u_ref.dtype)


def _ev_inproj(h, mods, rows, layer, g, w_in, cos_t, sin_t, *, n_lat_tiles, hy3, qk, head_dim, tk):
    h_specs, h_args = _stream_specs(h, n_lat_tiles)
    nb, _, d = h_args[0].shape
    s = (n_lat_tiles + 1) * TT
    n_in = w_in.shape[1]
    assert 2 * head_dim == LANES
    heads, per = qk // LANES, tk // TT
    kern = functools.partial(_ev_inproj_kernel, d=d, hy3=hy3, qk=qk, n_lat_tiles=n_lat_tiles,
                             qscale=head_dim ** -0.5 * math.log2(math.e))
    return pl.pallas_call(
        kern,
        grid=(nb, s // TT),
        in_specs=h_specs + [
            _mod_spec(layer, rows, nb, n_lat_tiles, mods.shape[-1]),
            pl.BlockSpec((1, d), lambda b, i: (0, 0)),
            pl.BlockSpec((d, n_in), lambda b, i: (0, 0)),
            pl.BlockSpec((TT, LANES), lambda b, i: (i, 0)),
            pl.BlockSpec((TT, LANES), lambda b, i: (i, 0)),
        ],
        out_specs=[
            pl.BlockSpec((1, TT, hy3), lambda b, i: (b, i, 0)),
            pl.BlockSpec((1, heads, LANES, TT), lambda b, i: (b, 0, 0, i)),
            pl.BlockSpec((1, TT, qk), lambda b, i: (b, i, 0)),
            pl.BlockSpec((1, heads, 1, VT_ROWS, TT), lambda b, i: (b, 0, i // per, 0, i % per)),
        ],
        out_shape=[
            jax.ShapeDtypeStruct((nb, s, hy3), BF16),
            jax.ShapeDtypeStruct((nb, heads, LANES, s), BF16),
            jax.ShapeDtypeStruct((nb, s, qk), BF16),
            jax.ShapeDtypeStruct((nb, heads, s // tk, VT_ROWS, tk), BF16),
        ],
        compiler_params=_cp("parallel", "parallel"),
        name="ev_inproj",
    )(*h_args, mods, g, w_in, cos_t, sin_t)


def _split_maps(qt, half):
    row = lax.broadcasted_iota(jnp.int32, qt.shape, 0)
    zero = jnp.zeros_like(qt)
    return jnp.where(row < half, qt, zero), jnp.where(row >= half, qt, zero)


def _attn_out(a0, a1, lam_refs, sg_ref, lam_init, dv):
    lq1_ref, lk1_ref, lq2_ref, lk2_ref = lam_refs
    lam = (jnp.exp(jnp.sum(lq1_ref[...] * lk1_ref[...], axis=-1, keepdims=True))
           - jnp.exp(jnp.sum(lq2_ref[...] * lk2_ref[...], axis=-1, keepdims=True)) + lam_init)
    o = (a0[:dv] / a0[dv:dv + 1] - lam * (a1[:dv] / a1[dv:dv + 1])).T
    on = o * lax.rsqrt(jnp.mean(o * o, axis=-1, keepdims=True) + EPS)
    return on * sg_ref[...] * (1.0 - lam_init)


def _col_max8(sc):
    part = sc[0:SUBLANES]
    for g in range(1, sc.shape[0] // SUBLANES):
        part = jnp.maximum(part, sc[g * SUBLANES:(g + 1) * SUBLANES])
    return part


def _attn_kernel(qt_ref, qnt_ref, k_ref, vt_ref, lq1_ref, lk1_ref, lq2_ref, lk2_ref, sg_ref, o_ref,
                 s_ref, mp_ref, p_ref, acc_ref, m_ref, alpha_ref, *, n_chunks, tk, lam_init, half):
    i = pl.program_id(2)
    dv = 2 * half
    lam_refs = (lq1_ref, lk1_ref, lq2_ref, lk2_ref)
    tq = qnt_ref.shape[-1]
    q_sub = (_split_maps(qt_ref[0, 0, :, 0:tq], half), _split_maps(qt_ref[0, 0, :, tq:2 * tq], half))
    q_next = _split_maps(qnt_ref[0, 0], half)

    def put_scores(slot, qts, c):
        kc = k_ref[0, pl.ds(pl.multiple_of(c * tk, tk), tk), :]
        for j in range(2):
            sc = _dot(kc, qts[j])
            s_ref[slot, j] = sc
            mp_ref[slot, j] = _col_max8(sc)

    def put_probs(slot, sub, first):
        for j in range(2):
            col_max = jnp.max(mp_ref[slot, j], axis=0, keepdims=True)
            if first:
                m_new = col_max
            else:
                m_old = m_ref[sub, j]
                m_new = jnp.maximum(m_old, col_max)
                alpha_ref[slot, j] = jnp.exp2(m_old - m_new)
            m_ref[sub, j] = m_new
            p_ref[slot, j] = jnp.exp2(s_ref[slot, j] - m_new).astype(BF16)

    def add_pv(slot, sub, c, first):
        for j in range(2):
            pv = _dot(vt_ref[0, 0, c], p_ref[slot, j])
            acc_ref[sub, j] = pv if first else alpha_ref[slot, j] * acc_ref[sub, j] + pv

    def sub_tile(sub, parity, qts, qts_after, sub_after):
        slot = lambda c: (c + parity) % 2
        put_scores(slot(0), qts, 2)
        put_probs(slot(1), sub, False)
        add_pv(slot(0), sub, 0, True)
        n_uniform = n_chunks - 3

        unroll = 2

        def group(u, carry):
            c = 1 + unroll * u
            for k in range(unroll):
                put_scores(slot(1 + k), qts, c + k + 2)
                put_probs(slot(k), sub, False)
                add_pv(slot(1 + k), sub, c + k, False)
            return carry

        lax.fori_loop(0, n_uniform // unroll, group, 0)
        c = n_chunks - 2
        put_scores(slot(c), qts_after, 0)
        put_probs(slot(c + 1), sub, False)
        add_pv(slot(c), sub, c, False)
        c = n_chunks - 1
        put_scores(slot(c), qts_after, 1)
        put_probs(slot(c + 1), sub_after, True)
        add_pv(slot(c), sub, c, False)
        o_ref[0, sub * tq:(sub + 1) * tq, :] = _attn_out(
            acc_ref[sub, 0], acc_ref[sub, 1], lam_refs, sg_ref, lam_init, dv).astype(o_ref.dtype)

    @pl.when(i == 0)
    def _():
        put_scores(0, q_sub[0], 0)
        put_probs(0, 0, True)
        put_scores(1, q_sub[0], 1)

    sub_tile(0, 0, q_sub[0], q_sub[1], 1)
    sub_tile(1, 1, q_sub[1], q_next, 0)


def _ctx_attn_kernel(qt_ref, k_ref, vt_ref, lq1_ref, lk1_ref, lq2_ref, lk2_ref, sg_ref, o_ref,
                     *, lam_init, half):
    acc = []
    for qtj in _split_maps(qt_ref[0, 0], half):
        sc = _dot(k_ref[0], qtj)
        col_max = jnp.max(_col_max8(sc), axis=0, keepdims=True)
        acc.append(_dot(vt_ref[0, 0, 0], jnp.exp2(sc - col_max).astype(BF16)))
    o_ref[0] = _attn_out(acc[0], acc[1], (lq1_ref, lk1_ref, lq2_ref, lk2_ref), sg_ref, lam_init,
                         2 * half).astype(o_ref.dtype)


def _attn_chunk(s):
    return next(t for t in (3 * TT, TT) if s % t == 0 and (s // t) % 2 == 1 and s // t >= 5)


def _diff_attention(qt, k, vt, lam_vecs, subln_g, *, n_lat_tiles, lam_init, head_dim):
    nb, s, qk = k.shape
    dv = 2 * head_dim
    heads = qk // dv
    l = n_lat_tiles * TT
    n_chunks, tk = vt.shape[2], vt.shape[4]
    tq = ATTN_SUB_Q if l % (2 * ATTN_SUB_Q) == 0 else TT
    assert l % (2 * tq) == 0 and n_chunks % 2 == 1 and n_chunks >= 5 and s - l == TT
    n_steps = l // (2 * tq)
    vec = lambda n: pl.BlockSpec((1, n), lambda b, h, i: (0, 0))
    vecs = [vec(head_dim)] * 4 + [vec(dv)]
    o_lat = pl.pallas_call(
        functools.partial(_attn_kernel, n_chunks=n_chunks, tk=tk, lam_init=lam_init, half=head_dim),
        grid=(nb, heads, n_steps),
        in_specs=[
            pl.BlockSpec((1, 1, dv, 2 * tq), lambda b, h, i: (b, h, 0, i)),
            pl.BlockSpec((1, 1, dv, tq),
                         lambda b, h, i: (b, h, 0, jnp.minimum(2 * i + 2, 2 * n_steps - 1))),
            pl.BlockSpec((1, s, dv), lambda b, h, i: (b, 0, h)),
            pl.BlockSpec((1, 1, n_chunks, VT_ROWS, tk), lambda b, h, i: (b, h, 0, 0, 0)),
        ] + vecs,
        out_specs=pl.BlockSpec((1, 2 * tq, dv), lambda b, h, i: (b, i, h)),
        out_shape=jax.ShapeDtypeStruct((nb, l, qk), BF16),
        scratch_shapes=[pltpu.VMEM((2, 2, tk, tq), F32), pltpu.VMEM((2, 2, SUBLANES, tq), F32),
                        pltpu.VMEM((2, 2, tk, tq), BF16), pltpu.VMEM((2, 2, VT_ROWS, tq), F32),
                        pltpu.VMEM((2, 2, 1, tq), F32), pltpu.VMEM((2, 2, 1, tq), F32)],
        compiler_params=_cp("parallel", "parallel", "arbitrary"),
        name="diff_attn",
    )(qt, qt, k, vt, *lam_vecs, subln_g)
    per = tk // TT
    vec2 = lambda n: pl.BlockSpec((1, n), lambda b, h: (0, 0))
    o_ctx = pl.pallas_call(
        functools.partial(_ctx_attn_kernel, lam_init=lam_init, half=head_dim),
        grid=(nb, heads),
        in_specs=[pl.BlockSpec((1, 1, dv, TT), lambda b, h: (b, h, 0, n_lat_tiles)),
                  pl.BlockSpec((1, TT, dv), lambda b, h: (b, n_lat_tiles, h)),
                  pl.BlockSpec((1, 1, 1, VT_ROWS, TT),
                               lambda b, h: (b, h, n_lat_tiles // per, 0, n_lat_tiles % per)),
                  ] + [vec2(head_dim)] * 4 + [vec2(dv)],
        out_specs=pl.BlockSpec((1, TT, dv), lambda b, h: (b, 0, h)),
        out_shape=jax.ShapeDtypeStruct((nb, TT, qk), BF16),
        compiler_params=_cp("parallel", "parallel"),
        name="ctx_attn",
    )(qt, k, vt, *lam_vecs, subln_g)
    return o_lat, o_ctx


def _halo_rows(dtype):
    return SUBLANES * 4 // jnp.dtype(dtype).itemsize


def _halo_fill(xp_ref, prev_ref, x_ref, next_ref, has_prev, has_next):
    hr = prev_ref.shape[2]
    zero = jnp.zeros((hr, xp_ref.shape[-1]), F32)
    xp_ref[hr:hr + TT, :] = x_ref[0].astype(F32)
    xp_ref[0:hr, :] = jnp.where(has_prev, prev_ref[0, 0].astype(F32), zero)
    xp_ref[hr + TT:2 * hr + TT, :] = jnp.where(has_next, next_ref[0, 0].astype(F32), zero)


def _conv_taps(xp, w, bias, left):
    rows = xp.shape[0]
    hr = (rows - TT) // 2
    before = None
    for j in range(left):
        z = w[j:j + 1, :] * xp
        before = pltpu.roll(z if before is None else before + z, 1, 0)
    after = None
    for j in range(w.shape[0] - 1, left, -1):
        z = w[j:j + 1, :] * xp
        after = pltpu.roll(z if after is None else after + z, rows - 1, 0)
    acc = w[left:left + 1, :] * xp + bias
    for part in (before, after):
        if part is not None:
            acc = acc + part
    return acc[hr:hr + TT, :]


def _shortconv_kernel(prev_ref, x_ref, next_ref, w_ref, b_ref, v_ref, x1_ref, x2_ref, xp_ref,
                      *, n_seg_tiles, hy):
    i = pl.program_id(1)
    _halo_fill(xp_ref, prev_ref, x_ref, next_ref, i > 0, i < n_seg_tiles - 1)
    y = _conv_taps(xp_ref[...], w_ref[...], b_ref[...], HY_SHORT_LEFT)
    v_ref[0] = y[:, :hy].astype(v_ref.dtype)
    x1_ref[0] = y[:, hy:2 * hy].astype(x1_ref.dtype)
    x2_ref[0] = y[:, 2 * hy:].astype(x2_ref.dtype)


def _shortconv(u, w, b, *, tile0, n_seg_tiles):
    nb, s, hy3 = u.shape
    hy = hy3 // 3
    hr = _halo_rows(u.dtype)
    nh, per = s // hr, TT // hr
    uh = u.reshape(nb, nh, hr, hy3)
    prev = pl.BlockSpec((1, 1, hr, hy3),
                        lambda b, i: (b, jnp.maximum((tile0 + i) * per - 1, 0), 0, 0))
    nxt = pl.BlockSpec((1, 1, hr, hy3),
                       lambda b, i: (b, jnp.minimum((tile0 + i + 1) * per, nh - 1), 0, 0))
    out = jax.ShapeDtypeStruct((nb, n_seg_tiles * TT, hy), BF16)
    ospec = pl.BlockSpec((1, TT, hy), lambda b, i: (b, i, 0))
    return pl.pallas_call(
        functools.partial(_shortconv_kernel, n_seg_tiles=n_seg_tiles, hy=hy),
        grid=(nb, n_seg_tiles),
        in_specs=[
            prev,
            pl.BlockSpec((1, TT, hy3), lambda b, i: (b, tile0 + i, 0)),
            nxt,
            pl.BlockSpec(w.shape, lambda b, i: (0, 0)),
            pl.BlockSpec((1, hy3), lambda b, i: (0, 0)),
        ],
        out_specs=[ospec, ospec, ospec],
        out_shape=[out, out, out],
        scratch_shapes=[pltpu.VMEM((TT + 2 * hr, hy3), F32)],
        compiler_params=_cp("parallel", "parallel"),
        name="hy_shortconv",
    )(uh, u, uh, w, b.reshape(1, hy3))


def _filter_feats(lh, n_cols):
    n = np.arange(2 * lh).reshape(-1, n_cols).T.reshape(-1)
    lag = np.where(n < lh, n, 2 * lh - n).astype(np.float64)
    t = (lag / lh).astype(np.float32).astype(np.float64)
    bands = np.arange(1, N_BANDS + 1, dtype=np.float64)
    ang = 2.0 * math.pi * t[:, None] * bands
    feats = np.concatenate([t[:, None], np.cos(ang), np.sin(ang)], axis=-1)
    pad = (-(feats.shape[1] + 1)) % SUBLANES
    return np.concatenate([feats, np.zeros((2 * lh, pad)), n[:, None].astype(np.float64)], axis=-1)


def _filt_kernel(ft_ref, w1_ref, b1_ref, w2_ref, b2_ref, w3_ref, b3_ref, fr_ref, dec_ref, o_ref,
                 *, lh, c):
    ft = ft_ref[...]
    freq = fr_ref[...]
    h = jnp.sin(freq * (_dot(ft, w1_ref[...], precision=HIGHEST) + b1_ref[...]))
    h = jnp.sin(freq * (_dot(h, w2_ref[...], precision=HIGHEST) + b2_ref[...])).astype(BF16)
    tr = o_ref.shape[1]
    half = tr // 2
    for q in range(ft.shape[0] // tr):
        for dr in range(2):
            rs = slice(q * tr + dr * half, q * tr + (dr + 1) * half)
            t = ft[rs, 0:1]
            f = (_dot(h[rs], w3_ref[dr]) + b3_ref[dr]) * jnp.exp(-t * jnp.abs(dec_ref[dr]))
            if dr == 1:
                f = jnp.where(ft[rs, ft.shape[1] - 1:] == lh, 0.0, f)
            for o in range(2):
                o_ref[o, dr * half:(dr + 1) * half, q * c:(q + 1) * c] = f[:, o * c:(o + 1) * c]


def _hyena_filter(lh, n_cols, w1, b1, w2, b2, w3, b3, freq, decay):
    c = decay.shape[-1]
    feats = jnp.asarray(_filter_feats(lh, n_cols), F32)
    fe = feats.shape[1]
    w1p = jnp.pad(w1, ((0, fe - w1.shape[0]), (0, 0)))
    hid = w1.shape[1]
    tr = 2 * lh // n_cols
    cb = math.gcd(n_cols, FILT_COL_BLOCKS)
    by_dir = lambda a, lead: jnp.moveaxis(a.reshape(lead, 2, 2, c), 2, 0).reshape(2, lead, 2 * c)
    full = lambda a: pl.BlockSpec(a.shape, lambda i: (0,) * a.ndim)
    args = (w1p, b1.reshape(1, hid), w2, b2.reshape(1, hid), by_dir(w3, hid).astype(BF16),
            by_dir(b3, 1), freq.reshape(1, hid), by_dir(decay, 1))
    return pl.pallas_call(
        functools.partial(_filt_kernel, lh=lh, c=c),
        grid=(n_cols // cb,),
        in_specs=[pl.BlockSpec((cb * tr, fe), lambda i: (i, 0))] + [full(a) for a in args],
        out_specs=pl.BlockSpec((2, tr, cb * c), lambda i: (0, 0, i)),
        out_shape=jax.ShapeDtypeStruct((2, tr, n_cols * c), F32),
        compiler_params=_cp("parallel"),
        name="hy_filter",
    )(feats, *args)


@functools.lru_cache(maxsize=None)
def _dft_tables(n):
    nn = n * n
    h = n // 2
    k = np.arange(n)
    th = 2.0 * math.pi * np.outer(k, k) / n
    c, s = np.cos(th), np.sin(th)
    f1_data = np.block([[c[:, :h], s[:, :h]], [-s[:, :h], c[:, :h]]])
    f1_real = np.concatenate([c, -s], axis=0)
    idx = (k[None, None, :] * (k[:, None, None] + n * k[None, :, None])) % nn
    phi = 2.0 * math.pi * idx / nn
    cp, sp = np.cos(phi), np.sin(phi)
    g = np.concatenate([np.concatenate([cp, sp], axis=2), np.concatenate([-sp, cp], axis=2)], axis=1)
    hmat = np.transpose(g, (0, 2, 1)) / nn
    ci, si = c[:h, :], s[:h, :]
    f3 = np.zeros((n, 2 * n))
    f3[:h, 0::2], f3[:h, 1::2] = ci, -si
    f3[h:, 0::2], f3[h:, 1::2] = si, ci
    return f1_data, f1_real, g, hmat, f3


def _fft_s1_kernel(x_ref, f_ref, o_ref):
    o_ref[0] = _dot(f_ref[...], x_ref[0].astype(BF16)).astype(o_ref.dtype)


def _fft_s1(x, f1, n, c):
    p = x.shape[0]
    tc = FFT_N2_TILE * c
    return pl.pallas_call(
        _fft_s1_kernel,
        grid=(p, n * c // tc),
        in_specs=[pl.BlockSpec((1, n, tc), lambda q, j: (q, 0, j)),
                  pl.BlockSpec(f1.shape, lambda q, j: (0, 0))],
        out_specs=pl.BlockSpec((1, 2 * n, tc), lambda q, j: (q, 0, j)),
        out_shape=jax.ShapeDtypeStruct((p, 2 * n, n * c), BF16),
        compiler_params=_cp("parallel", "parallel"),
        name="fft_s1",
    )(x, f1)


def _fft_spec_kernel(a_ref, g_ref, o_ref, *, tk):
    for j in range(tk):
        x = jnp.concatenate([a_ref[0, 0, j], a_ref[0, 1, j]], axis=0)
        o_ref[0, j] = _dot(g_ref[j], x).astype(o_ref.dtype)


def _fft_mid_kernel(a_ref, g_ref, h_ref, kh_ref, o_ref, *, tk, n):
    for j in range(tk):
        x = jnp.concatenate([a_ref[0, 0, j], a_ref[0, 1, j]], axis=0)
        t = _dot(g_ref[j], x)
        tr, ti = t[:n], t[n:]
        kr, ki = kh_ref[0, j, :n].astype(F32), kh_ref[0, j, n:].astype(F32)
        y = jnp.concatenate([tr * kr - ti * ki, tr * ki + ti * kr], axis=0).astype(BF16)
        o_ref[0, j] = _dot(h_ref[j], y).astype(o_ref.dtype)


def _fft_spectrum(a, g, n, c):
    p = a.shape[0]
    tk = FFT_K1_TILE
    a5 = a.reshape(p, 2, n, n, c)
    return pl.pallas_call(
        functools.partial(_fft_spec_kernel, tk=tk),
        grid=(n // tk, p),
        in_specs=[pl.BlockSpec((1, 2, tk, n, c), lambda j, q: (q, 0, j, 0, 0)),
                  pl.BlockSpec((tk, 2 * n, 2 * n), lambda j, q: (j, 0, 0))],
        out_specs=pl.BlockSpec((1, tk, 2 * n, c), lambda j, q: (q, j, 0, 0)),
        out_shape=jax.ShapeDtypeStruct((p, n, 2 * n, c), BF16),
        compiler_params=_cp("parallel", "parallel"),
        name="fft_spectrum",
    )(a5, g)


def _fft_mid(a, g, hm, khat, order, n, c):
    p = a.shape[0]
    tk = FFT_K1_TILE
    a5 = a.reshape(p, 2, n, n, c)
    return pl.pallas_call(
        functools.partial(_fft_mid_kernel, tk=tk, n=n),
        grid=(n // tk, p),
        in_specs=[pl.BlockSpec((1, 2, tk, n, c), lambda j, q: (q, 0, j, 0, 0)),
                  pl.BlockSpec((tk, 2 * n, 2 * n), lambda j, q: (j, 0, 0)),
                  pl.BlockSpec((tk, 2 * n, 2 * n), lambda j, q: (j, 0, 0)),
                  pl.BlockSpec((1, tk, 2 * n, c), lambda j, q: (order, j, 0, 0))],
        out_specs=pl.BlockSpec((1, tk, 2 * n, c), lambda j, q: (q, j, 0, 0)),
        out_shape=jax.ShapeDtypeStruct((p, n, 2 * n, c), BF16),
        compiler_params=_cp("parallel", "parallel"),
        name="fft_mid",
    )(a5, g, hm, khat)


def _fft_s3_kernel(*refs, gated, chained):
    c_ref, f_ref, v_ref, b_ref = refs[:4]
    rest = list(refs[4:])
    y = _dot(f_ref[...], c_ref[0]) + v_ref[0].astype(F32) * b_ref[...]
    if gated:
        y = rest.pop(0)[0].astype(F32) * y
    f1_ref = rest.pop(0) if chained else None
    o_ref = rest.pop(0)
    z = y.astype(o_ref.dtype)
    o_ref[0] = z
    if chained:
        a_ref = rest.pop(0)
        a_ref[0] = _dot(f1_ref[...], z).astype(a_ref.dtype)


def _fft_s3(cm, f3, vin, xg, bias, n, c, f1_next=None):
    p = cm.shape[0]
    tc = FFT_N2_TILE * c
    c2 = cm.reshape(p, 2 * n, n * c)
    bias_t = jnp.tile(bias.reshape(1, c), (1, FFT_N2_TILE))
    blk = pl.BlockSpec((1, n, tc), lambda q, j: (q, 0, j))
    full = lambda a: pl.BlockSpec(a.shape, lambda q, j: (0, 0))
    in_specs = [pl.BlockSpec((1, 2 * n, tc), lambda q, j: (q, 0, j)), full(f3), blk,
                pl.BlockSpec((1, tc), lambda q, j: (0, 0))]
    args = [c2, f3, vin, bias_t]
    out_specs, out_shape = [blk], [jax.ShapeDtypeStruct((p, n, n * c), BF16)]
    if xg is not None:
        in_specs.append(blk)
        args.append(xg)
    if f1_next is not None:
        in_specs.append(full(f1_next))
        args.append(f1_next)
        out_specs.append(pl.BlockSpec((1, 2 * n, tc), lambda q, j: (q, 0, j)))
        out_shape.append(jax.ShapeDtypeStruct((p, 2 * n, n * c), BF16))
    return pl.pallas_call(
        functools.partial(_fft_s3_kernel, gated=xg is not None, chained=f1_next is not None),
        grid=(p, n * c // tc),
        in_specs=in_specs,
        out_specs=out_specs,
        out_shape=out_shape,
        compiler_params=_cp("parallel", "parallel"),
        name="fft_s3",
    )(*args)


def _hyena_long(v, x1, kk, hy_bias):
    nb, l, c = v.shape
    n = math.isqrt(2 * l)
    assert n * n == 2 * l and nb % 2 == 0
    p = nb // 2
    f1d, f1r, g, hm, f3 = (jnp.asarray(t, F32).astype(BF16) for t in _dft_tables(n))
    pair = lambda a: a.reshape(p, n, n * c)
    khat = _fft_spectrum(_fft_s1(kk, f1r, n, c), g, n, c)
    v = pair(v)
    cm = _fft_mid(_fft_s1(v, f1d, n, c), g, hm, khat, 0, n, c)
    z1, a = _fft_s3(cm, f3, v, pair(x1), hy_bias[0], n, c, f1_next=f1d)
    cm = _fft_mid(a, g, hm, khat, 1, n, c)
    (w2,) = _fft_s3(cm, f3, z1, None, hy_bias[1], n, c)
    return w2.reshape(nb, l, c)


@functools.lru_cache(maxsize=None)
def _ctx_dft_tables(lc):
    m = 2 * lc
    k = np.arange(m)
    th = 2.0 * math.pi * np.outer(k, k) / m
    c, s = np.cos(th), np.sin(th)
    f_data = np.block([[c[:, :lc], s[:, :lc]], [-s[:, :lc], c[:, :lc]]])
    f_real = np.concatenate([c, -s], axis=0)
    ci, si = c[:lc, :], s[:lc, :]
    f_inv = np.block([[ci, -si], [si, ci]]) / m
    return f_data, f_real, f_inv


def _ctxconv_kernel(v_ref, x1_ref, x2_ref, kk_ref, fd_ref, fr_ref, fi_ref, b_ref, o_ref, *, m):
    def conv(u, order):
        kh = _dot(fr_ref[...], kk_ref[order].astype(BF16))
        t = _dot(fd_ref[...], u.astype(BF16))
        tr, ti, kr, ki = t[:m], t[m:], kh[:m], kh[m:]
        y = jnp.concatenate([tr * kr - ti * ki, tr * ki + ti * kr], axis=0).astype(BF16)
        return _dot(fi_ref[...], y)

    v = v_ref[0].astype(F32)
    z1 = x1_ref[0].astype(F32) * (conv(v, 0) + v * b_ref[0:1, :])
    o_ref[0] = (x2_ref[0].astype(F32) * (conv(z1, 1) + z1 * b_ref[1:2, :])).astype(o_ref.dtype)


def _hyena_ctx(v, x1, x2, kk, hy_bias):
    nb, lc, c = v.shape
    p, m = nb // 2, 2 * lc
    fd, fr, fi = (jnp.asarray(t, F32).astype(BF16) for t in _ctx_dft_tables(lc))
    pair = lambda a: a.reshape(p, m, c)
    blk = pl.BlockSpec((1, m, c), lambda q: (q, 0, 0))
    full = lambda a: pl.BlockSpec(a.shape, lambda q: (0,) * a.ndim)
    z = pl.pallas_call(
        functools.partial(_ctxconv_kernel, m=m),
        grid=(p,),
        in_specs=[blk, blk, blk, full(kk), full(fd), full(fr), full(fi), full(hy_bias)],
        out_specs=blk,
        out_shape=jax.ShapeDtypeStruct((p, m, c), BF16),
        compiler_params=_cp("parallel"),
        name="hy_ctx",
    )(pair(v), pair(x1), pair(x2), kk, fd, fr, fi, hy_bias)
    return z.reshape(nb, lc, c)


def _od_inproj_kernel(prev_ref, h_ref, next_ref, mod_ref, g_ref, w_ref, cw_ref, cb_ref,
                      gate_ref, xc_ref, *, d, r, n_lat_tiles, n_tiles):
    i = pl.program_id(1)
    seg_first = jnp.logical_or(i == 0, i == n_lat_tiles)
    seg_last = jnp.logical_or(i == n_lat_tiles - 1, i == n_tiles - 1)
    m = mod_ref[0]
    rows = jnp.concatenate([prev_ref[0, 0], h_ref[0], next_ref[0, 0]], axis=0)
    xn = _rms_mod(rows, g_ref[...], m[:, 0:d], m[:, d:2 * d]).astype(BF16)
    y = _dot(xn, w_ref[...])
    gate_ref[0] = y[SUBLANES:SUBLANES + TT, :r].astype(gate_ref.dtype)
    x = y[:, r:]
    row = lax.broadcasted_iota(jnp.int32, (x.shape[0], 1), 0)
    outside = jnp.logical_or(jnp.logical_and(row < SUBLANES, seg_first),
                             jnp.logical_and(row >= SUBLANES + TT, seg_last))
    x = jnp.where(outside, 0.0, x)
    xc_ref[0] = _conv_taps(x, cw_ref[...], cb_ref[...], RG_CONV_LEFT)


def _od_inproj(h, mods, rows, layer, g, w_in, conv_w, conv_b, *, n_lat_tiles):
    nb, s, d = h.shape
    r = w_in.shape[1] // 2
    n_tiles = s // TT
    n8, per = s // SUBLANES, TT // SUBLANES
    h8 = h.reshape(nb, n8, SUBLANES, d)
    ospec = pl.BlockSpec((1, TT, r), lambda b, i: (b, i, 0))
    return pl.pallas_call(
        functools.partial(_od_inproj_kernel, d=d, r=r, n_lat_tiles=n_lat_tiles, n_tiles=n_tiles),
        grid=(nb, n_tiles),
        in_specs=[
            pl.BlockSpec((1, 1, SUBLANES, d), lambda b, i: (b, jnp.maximum(i * per - 1, 0), 0, 0)),
            pl.BlockSpec((1, TT, d), lambda b, i: (b, i, 0)),
            pl.BlockSpec((1, 1, SUBLANES, d),
                         lambda b, i: (b, jnp.minimum((i + 1) * per, n8 - 1), 0, 0)),
            _mod_spec(layer, rows, nb, n_lat_tiles, mods.shape[-1]),
            pl.BlockSpec((1, d), lambda b, i: (0, 0)),
            pl.BlockSpec(w_in.shape, lambda b, i: (0, 0)),
            pl.BlockSpec(conv_w.shape, lambda b, i: (0, 0)),
            pl.BlockSpec((1, r), lambda b, i: (0, 0)),
        ],
        out_specs=[ospec, ospec],
        out_shape=[jax.ShapeDtypeStruct((nb, s, r), BF16), jax.ShapeDtypeStruct((nb, s, r), F32)],
        compiler_params=_cp("parallel", "parallel"),
        name="od_inproj",
    )(h8, h, h8, mods, g, w_in, conv_w, conv_b.reshape(1, r))


def _rglru_kernel(x_ref, w_ref, b_ref, lam_ref, o_ref, a_ref, bb_ref, carry_ref, *, r, windows):
    dr = pl.program_id(1)
    i = pl.program_id(2)
    half = TT // 2
    nl = -lam_ref[0]
    softplus = jnp.maximum(nl, 0.0) + jnp.log1p(jnp.exp(-jnp.abs(nl)))
    neg_rate = (0.5 * RG_C) * softplus
    exp2_rate = (-0.5 * RG_C * math.log2(math.e)) * softplus

    def gates(r0):
        rs = slice(r0, r0 + half)
        xc = x_ref[0, rs, :]
        xb = xc.astype(BF16)
        half_x = 0.5 * xc
        for c0, c1, k0, k1 in windows:
            xk = xb[:, k0:k1]
            tr = jnp.tanh(_dot(xk, w_ref[0, k0:k1, c0:c1]) + b_ref[0, :, c0:c1]) + 1.0
            ti = jnp.tanh(_dot(xk, w_ref[0, k0:k1, r + c0:r + c1]) + b_ref[0, :, r + c0:r + c1]) + 1.0
            a = jnp.exp2(exp2_rate[:, c0:c1] * tr)
            a_ref[rs, c0:c1] = a
            e = jnp.tanh(neg_rate[:, c0:c1] * tr) * (a * a + 1.0)
            root = jnp.where(e > 0.0, e * lax.rsqrt(e), 0.0)
            bb_ref[rs, c0:c1] = root * (ti * half_x[:, c0:c1])

    def scan_rows(rows, h):
        for row in rows:
            h = a_ref[row:row + 1, :] * h + bb_ref[row:row + 1, :]
            o_ref[0, 0, row:row + 1, :] = h
        return h

    def scan_loop(r0, reverse, h):
        n_blk = half // SUBLANES

        def block(g, h):
            base = pl.multiple_of(r0 + (n_blk - 1 - g if reverse else g) * SUBLANES, SUBLANES)
            for k in range(SUBLANES):
                row = base + (SUBLANES - 1 - k if reverse else k)
                h = a_ref[pl.ds(row, 1), :] * h + bb_ref[pl.ds(row, 1), :]
                o_ref[0, 0, pl.ds(row, 1), :] = h
            return h

        return lax.fori_loop(0, n_blk, block, h)

    @pl.when(i == 0)
    def _():
        carry_ref[...] = jnp.zeros_like(carry_ref)

    def run(reverse):
        first, second = (half, 0) if reverse else (0, half)
        gates(first)
        gates(second)
        rows = range(first + half - 1, first - 1, -1) if reverse else range(first, first + half)
        h = scan_rows(rows, carry_ref[0:1, :])
        carry_ref[0:1, :] = scan_loop(second, reverse, h)

    pl.when(dr == 0)(lambda: run(False))
    pl.when(dr == 1)(lambda: run(True))


def _scan_tile(dr, i, n_lat_tiles, n_tiles):
    fwd = jnp.where(i == 0, n_lat_tiles, i - 1)
    bwd = jnp.where(i == 0, n_lat_tiles, n_lat_tiles - i)
    return jnp.where(dr == 0, fwd, bwd)


def _gate_windows(r, bs):
    out = []
    for c0 in range(0, r, 2 * LANES):
        c1 = min(c0 + 2 * LANES, r)
        k0 = (c0 // bs) * bs // LANES * LANES
        k1 = min(-(-(((c1 - 1) // bs + 1) * bs) // LANES) * LANES, r)
        out.append((c0, c1, k0, k1))
    return tuple(out)


def _rglru(xc, wcat, bcat, lam, *, n_lat_tiles, block_size):
    nb, s, r = xc.shape
    n_tiles = s // TT
    assert n_tiles == n_lat_tiles + 1
    tile = lambda d, i: _scan_tile(d, i, n_lat_tiles, n_tiles)
    return pl.pallas_call(
        functools.partial(_rglru_kernel, r=r, windows=_gate_windows(r, block_size)),
        grid=(nb, 2, n_tiles),
        in_specs=[
            pl.BlockSpec((1, TT, r), lambda b, d, i: (b, tile(d, i), 0)),
            pl.BlockSpec((1, r, 2 * r), lambda b, d, i: (d, 0, 0)),
            pl.BlockSpec((1, 1, 2 * r), lambda b, d, i: (d, 0, 0)),
            pl.BlockSpec((1, 1, r), lambda b, d, i: (d, 0, 0)),
        ],
        out_specs=pl.BlockSpec((1, 1, TT, r), lambda b, d, i: (d, b, tile(d, i), 0)),
        out_shape=jax.ShapeDtypeStruct((2, nb, s, r), F32),
        scratch_shapes=[pltpu.VMEM((TT, r), F32), pltpu.VMEM((TT, r), F32),
                        pltpu.VMEM((SUBLANES, r), F32)],
        compiler_params=_cp("parallel", "parallel", "arbitrary"),
        name="rglru",
    )(xc, wcat, bcat, lam.reshape(2, 1, r))


def _block_diag(w):
    n, bs, _ = w.shape
    eye = jnp.eye(n, dtype=w.dtype)
    return (eye[:, None, :, None] * w[:, :, None, :]).reshape(n * bs, n * bs)


def _gelu_tanh(x):
    return 0.5 * x * (1.0 + jnp.tanh(math.sqrt(2.0 / math.pi) * (x + 0.044715 * (x * x * x))))


def _post_kernel(*refs, kind, final, d, n_lat_tiles):
    if kind == "even":
        (hl_ref, hc_ref, x2_ref, wl_ref, zc_ref, ol_ref, oc_ref, mod_ref, g2_ref, wo_ref, w1_ref,
         w2_ref) = refs[:12]
        rest = refs[12:]
        is_ctx = pl.program_id(1) >= n_lat_tiles
        z_lat = (x2_ref[0].astype(F32) * wl_ref[0].astype(F32)).astype(BF16)
        z = jnp.where(is_ctx, zc_ref[0], z_lat)
        o = jnp.where(is_ctx, oc_ref[0], ol_ref[0])
        half = z.shape[-1]
        y = _dot(z, wo_ref[:half, :]) + _dot(o, wo_ref[half:, :])
    else:
        hl_ref, hc_ref, hd_ref, gate_ref, mod_ref, g2_ref, wo_ref, w1_ref, w2_ref = refs[:9]
        rest = refs[9:]
        mix = (hd_ref[0, 0] + hd_ref[1, 0]) * _gelu_tanh(gate_ref[0].astype(F32))
        y = _dot(mix.astype(BF16), wo_ref[...])
    out_ref = rest[-1]
    m = mod_ref[0]
    h1 = _stream_tile(hl_ref, hc_ref, n_lat_tiles) + m[:, 2 * d:3 * d] * y
    xn = _rms_mod(h1, g2_ref[...], m[:, 3 * d:4 * d], m[:, 4 * d:5 * d]).astype(BF16)
    acc = jnp.zeros_like(h1)
    dff = w1_ref.shape[1]
    for c0 in range(0, dff, FF_CHUNK):
        a = jnp.maximum(_dot(xn, w1_ref[:, c0:c0 + FF_CHUNK]), 0.0)
        acc = acc + _dot((a * a).astype(BF16), w2_ref[c0:c0 + FF_CHUNK, :])
    h2 = h1 + m[:, 5 * d:6 * d] * acc
    if final:
        fg_ref = rest[0]
        h2 = h2 * lax.rsqrt(jnp.mean(h2 * h2, axis=-1, keepdims=True) + EPS) * fg_ref[...]
    out_ref[0] = h2


def _post(kind, h, mix_args, mods, rows, layer, g2, w_out, w1, w2, final_g, *, n_lat_tiles, final):
    h_specs, h_args = _stream_specs(h, n_lat_tiles)
    nb, _, d = h_args[0].shape
    n_tiles = n_lat_tiles if final else n_lat_tiles + 1
    tok = lambda w: pl.BlockSpec((1, TT, w), lambda b, i: (b, i, 0))
    full = lambda a: pl.BlockSpec(a.shape, lambda b, i: (0,) * a.ndim)
    if kind == "even":
        lat = lambda w: pl.BlockSpec((1, TT, w), lambda b, i: (b, jnp.minimum(i, n_lat_tiles - 1), 0))
        ctx = lambda w: pl.BlockSpec((1, TT, w), lambda b, i: (b, 0, 0))
        x2_lat, w_lat, z_ctx, o_lat, o_ctx = mix_args
        mix_specs = [lat(x2_lat.shape[-1]), lat(w_lat.shape[-1]), ctx(z_ctx.shape[-1]),
                     lat(o_lat.shape[-1]), ctx(o_ctx.shape[-1])]
    else:
        hd, gate = mix_args
        r = gate.shape[-1]
        mix_specs = [pl.BlockSpec((2, 1, TT, r), lambda b, i: (0, b, i, 0)), tok(r)]
    in_specs = h_specs + mix_specs + [
        _mod_spec(layer, rows, nb, n_lat_tiles, mods.shape[-1]),
        pl.BlockSpec((1, d), lambda b, i: (0, 0)), full(w_out), full(w1), full(w2)]
    args = [*h_args, *mix_args, mods, g2, w_out, w1, w2]
    if final:
        in_specs.append(pl.BlockSpec((1, d), lambda b, i: (0, 0)))
        args.append(final_g)
    return pl.pallas_call(
        functools.partial(_post_kernel, kind=kind, final=final, d=d, n_lat_tiles=n_lat_tiles),
        grid=(nb, n_tiles),
        in_specs=in_specs,
        out_specs=tok(d),
        out_shape=jax.ShapeDtypeStruct((nb, n_tiles * TT, d), F32),
        compiler_params=_cp("parallel", "parallel"),
        name="post_" + kind,
    )(*args)


@functools.lru_cache(maxsize=None)
def _rope_tables(l, lc, head_dim):
    axis = head_dim // 2
    freqs = ROPE_BASE ** (-np.arange(0, axis, 2, dtype=np.float64) / axis)
    freqs = freqs.astype(np.float32).astype(np.float64)
    t = np.arange(l)
    ang_r = (t // GRID_W)[:, None] * freqs
    ang_c = (t % GRID_W)[:, None] * freqs
    cos = np.concatenate([np.cos(ang_r)] * 2 + [np.cos(ang_c)] * 2, axis=-1)
    sin = np.concatenate([-np.sin(ang_r), np.sin(ang_r), -np.sin(ang_c), np.sin(ang_c)], axis=-1)
    cos = np.concatenate([cos, np.ones((lc, head_dim))], axis=0)
    sin = np.concatenate([sin, np.zeros((lc, head_dim))], axis=0)
    rep = LANES // head_dim
    return np.tile(cos, (1, rep)), np.tile(sin, (1, rep))


def kernel(x, c, ctx, c_ctx, ada_w, ada_b, norm1_g, norm2_g, mlp_w1, mlp_w2, final_g, ev_w_in, ev_w_out, hy_short_w, hy_short_b, hy_f_w1, hy_f_b1, hy_f_w2, hy_f_b2, hy_f_w3, hy_f_b3, hy_f_freq, hy_f_decay, hy_bias, df_lq1, df_lk1, df_lq2, df_lk2, df_subln_g, od_w_in, od_w_out, rg_conv_w, rg_conv_b, rg_wa, rg_ba, rg_wx, rg_bx, rg_lam):
    nb, l, d = x.shape
    lc = ctx.shape[1]
    depth = ada_w.shape[0]
    hy = hy_bias.shape[-1]
    head_dim = df_lq1.shape[-1]
    qk = (ev_w_in.shape[-1] - 3 * hy) // 3
    r = rg_lam.shape[-1]
    assert l % TT == 0 and lc == TT and l % GRID_W == 0
    n_lat_tiles = l // TT

    rows = -(-(nb + 1) // SUBLANES) * SUBLANES
    cond = jnp.zeros((rows, d), F32).at[:nb].set(c).at[nb].set(c_ctx)
    mods = _ada_mods(cond, ada_w, ada_b).reshape(depth * rows, 1, ada_w.shape[-1])

    cos_t, sin_t = (jnp.asarray(t, F32) for t in _rope_tables(l, lc, head_dim))
    h = (x, ctx)
    w1_bf, w2_bf = mlp_w1.astype(BF16), mlp_w2.astype(BF16)

    for i in range(depth):
        j = i // 2
        final = i == depth - 1
        g1, g2 = norm1_g[i].reshape(1, d), norm2_g[i].reshape(1, d)
        if i % 2 == 0:
            lam_init = 0.8 - 0.6 * math.exp(-0.3 * i)
            u, qt, k, vt = _ev_inproj(h, mods, rows, i, g1, ev_w_in[j].astype(BF16), cos_t, sin_t,
                                      n_lat_tiles=n_lat_tiles, hy3=3 * hy, qk=qk, head_dim=head_dim,
                                      tk=_attn_chunk(l + lc))
            lam_vecs = [a[j].reshape(1, -1) for a in (df_lq1, df_lk1, df_lq2, df_lk2)]
            o_lat, o_ctx = _diff_attention(qt, k, vt, lam_vecs, df_subln_g[j].reshape(1, -1),
                                           n_lat_tiles=n_lat_tiles, lam_init=lam_init,
                                           head_dim=head_dim)
            fparams = (hy_f_w1[j], hy_f_b1[j], hy_f_w2[j], hy_f_b2[j], hy_f_w3[j], hy_f_b3[j],
                       hy_f_freq[j], hy_f_decay[j])
            vl, x1l, x2l = _shortconv(u, hy_short_w[j], hy_short_b[j], tile0=0,
                                      n_seg_tiles=n_lat_tiles)
            vc, x1c, x2c = _shortconv(u, hy_short_w[j], hy_short_b[j], tile0=n_lat_tiles,
                                      n_seg_tiles=lc // TT)
            w_lat = _hyena_long(vl, x1l, _hyena_filter(l, math.isqrt(2 * l), *fparams), hy_bias[j])
            z_ctx = _hyena_ctx(vc, x1c, x2c, _hyena_filter(lc, 1, *fparams), hy_bias[j])
            h = _post("even", h, (x2l, w_lat, z_ctx, o_lat, o_ctx), mods, rows, i, g2, ev_w_out[j].astype(BF16),
                      w1_bf[i], w2_bf[i], final_g.reshape(1, d), n_lat_tiles=n_lat_tiles,
                      final=final)
        else:
            gate, xc = _od_inproj(h, mods, rows, i, g1, od_w_in[j].astype(BF16), rg_conv_w[j],
                                  rg_conv_b[j], n_lat_tiles=n_lat_tiles)
            wcat = (0.5 * jnp.stack(
                [jnp.concatenate([_block_diag(rg_wa[j, dd]), _block_diag(rg_wx[j, dd])], axis=1)
                 for dd in range(2)])).astype(BF16)
            bcat = 0.5 * jnp.concatenate([rg_ba[j], rg_bx[j]], axis=-1).reshape(2, 1, 2 * r)
            hd = _rglru(xc, wcat, bcat, rg_lam[j], n_lat_tiles=n_lat_tiles,
                        block_size=rg_wa.shape[-1])
            h = _post("odd", h, (hd, gate), mods, rows, i, g2, od_w_out[j].astype(BF16),
                      w1_bf[i], w2_bf[i], final_g.reshape(1, d), n_lat_tiles=n_lat_tiles,
                      final=final)
    return h
```

```python
import functools
import math

import numpy as np
import jax
import jax.numpy as jnp
from jax import lax
from jax.experimental import pallas as pl
from jax.experimental.pallas import tpu as pltpu

F32 = jnp.float32
BF16 = jnp.bfloat16
HIGHEST = lax.Precision.HIGHEST

EPS = 1e-6
GRID_W = 64
ROPE_BASE = 10000.0
N_BANDS = 16
RG_C = 8.0
RG_CONV_LEFT = 2
HY_SHORT_LEFT = 1

TT = 256
LANES = 128
SUBLANES = 8
VMEM_LIMIT = 56 * 1024 * 1024
FF_CHUNK = 1024
FFT_N2_TILE = 16
FFT_K1_TILE = 8
ADA_K_TILE = 256
ATTN_SUB_Q = 512
FILT_COL_BLOCKS = 4
VT_ROWS = LANES + 16


def _cp(*sem):
    return pltpu.CompilerParams(dimension_semantics=sem, vmem_limit_bytes=VMEM_LIMIT)


def _dot(a, b, **kw):
    return jnp.dot(a, b, preferred_element_type=F32, **kw)


def _sigmoid(x):
    return 1.0 / (1.0 + jnp.exp(-x))


def _rms_mod(x, g, shift, scale):
    y = x * lax.rsqrt(jnp.mean(x * x, axis=-1, keepdims=True) + EPS)
    return (y * g) * (1.0 + scale) + shift


def _ada_kernel(c_ref, w_ref, b_ref, o_ref):
    k = pl.program_id(1)
    c = c_ref[k]
    part = _dot((c * _sigmoid(c)).astype(BF16), w_ref[0].astype(BF16))

    @pl.when(k == 0)
    def _():
        o_ref[0] = part + b_ref[0]

    @pl.when(k > 0)
    def _():
        o_ref[0] += part


def _ada_mods(cond, ada_w, ada_b):
    depth, d, n = ada_w.shape
    rows = cond.shape[0]
    tk = min(d, ADA_K_TILE)
    cond_k = cond.reshape(rows, d // tk, tk).transpose(1, 0, 2)
    return pl.pallas_call(
        _ada_kernel,
        grid=(depth, d // tk),
        in_specs=[
            pl.BlockSpec(cond_k.shape, lambda l, k: (0, 0, 0)),
            pl.BlockSpec((1, tk, n), lambda l, k: (l, k, 0)),
            pl.BlockSpec((1, 1, n), lambda l, k: (l, 0, 0)),
        ],
        out_specs=pl.BlockSpec((1, rows, n), lambda l, k: (l, 0, 0)),
        out_shape=jax.ShapeDtypeStruct((depth, rows, n), F32),
        compiler_params=_cp("parallel", "arbitrary"),
        name="ada_mods",
    )(cond_k, ada_w, ada_b.reshape(depth, 1, n))


def _stream_specs(h, n_lat_tiles):
    separate = isinstance(h, tuple)
    lat_arr, ctx_arr = h if separate else (h, h)
    ctx_blk = 0 if separate else n_lat_tiles
    d = lat_arr.shape[-1]
    lat = pl.BlockSpec((1, TT, d), lambda b, i: (b, jnp.minimum(i, n_lat_tiles - 1), 0))
    ctx = pl.BlockSpec((1, TT, d), lambda b, i: (b, ctx_blk, 0))
    return [lat, ctx], [lat_arr, ctx_arr]


def _stream_tile(hl_ref, hc_ref, n_lat_tiles):
    return jnp.where(pl.program_id(1) >= n_lat_tiles, hc_ref[0], hl_ref[0])


def _mod_spec(layer, rows, nb, n_lat_tiles, n6):
    def imap(b, i):
        return (layer * rows + jnp.where(i >= n_lat_tiles, nb, b), 0, 0)
    return pl.BlockSpec((1, 1, n6), imap)


def _ev_inproj_kernel(hl_ref, hc_ref, mod_ref, g_ref, w_ref, cos_ref, sin_ref,
                      u_ref, qt_ref, k_ref, vt_ref, *, d, hy3, qk, qscale, n_lat_tiles):
    m = mod_ref[0]
    x = _stream_tile(hl_ref, hc_ref, n_lat_tiles)
    xn = _rms_mod(x, g_ref[...], m[:, 0:d], m[:, d:2 * d]).astype(BF16)
    cos = cos_ref[...]
    sin = sin_ref[...]
    lane = lax.broadcasted_iota(jnp.int32, cos.shape, 1)
    first = (lane % 32) < 16

    def rope(z):
        sw = jnp.where(first, pltpu.roll(z, LANES - 16, 1), pltpu.roll(z, 16, 1))
        return z * cos + sw * sin

    extra = vt_ref.shape[3] - LANES
    ones_row = jnp.where(lax.broadcasted_iota(jnp.int32, (extra, cos.shape[0]), 0) == 0,
                         1.0, 0.0).astype(BF16)
    yq = _dot(xn, w_ref[:, hy3:hy3 + qk])
    for c in range(qk // LANES):
        qt_ref[0, c] = (rope(yq[:, c * LANES:(c + 1) * LANES]) * qscale).T.astype(BF16)
    yk = _dot(xn, w_ref[:, hy3 + qk:hy3 + 2 * qk])
    for c in range(qk // LANES):
        k_ref[0, :, c * LANES:(c + 1) * LANES] = rope(yk[:, c * LANES:(c + 1) * LANES]).astype(BF16)
    yv = _dot(xn, w_ref[:, hy3 + 2 * qk:])
    for c in range(qk // LANES):
        vt_ref[0, c, 0, 0:LANES, :] = yv[:, c * LANES:(c + 1) * LANES].T.astype(BF16)
        vt_ref[0, c, 0, LANES:, :] = ones_row
    u_ref[0] = _dot(xn, w_ref[:, :hy3]).astype(u_ref.dtype)


def _ev_inproj(h, mods, rows, layer, g, w_in, cos_t, sin_t, *, n_lat_tiles, hy3, qk, head_dim, tk):
    h_specs, h_args = _stream_specs(h, n_lat_tiles)
    nb, _, d = h_args[0].shape
    s = (n_lat_tiles + 1) * TT
    n_in = w_in.shape[1]
    assert 2 * head_dim == LANES
    heads, per = qk // LANES, tk // TT
    kern = functools.partial(_ev_inproj_kernel, d=d, hy3=hy3, qk=qk, n_lat_tiles=n_lat_tiles,
                             qscale=head_dim ** -0.5 * math.log2(math.e))
    return pl.pallas_call(
        kern,
        grid=(nb, s // TT),
        in_specs=h_specs + [
            _mod_spec(layer, rows, nb, n_lat_tiles, mods.shape[-1]),
            pl.BlockSpec((1, d), lambda b, i: (0, 0)),
            pl.BlockSpec((d, n_in), lambda b, i: (0, 0)),
            pl.BlockSpec((TT, LANES), lambda b, i: (i, 0)),
            pl.BlockSpec((TT, LANES), lambda b, i: (i, 0)),
        ],
        out_specs=[
            pl.BlockSpec((1, TT, hy3), lambda b, i: (b, i, 0)),
            pl.BlockSpec((1, heads, LANES, TT), lambda b, i: (b, 0, 0, i)),
            pl.BlockSpec((1, TT, qk), lambda b, i: (b, i, 0)),
            pl.BlockSpec((1, heads, 1, VT_ROWS, TT), lambda b, i: (b, 0, i // per, 0, i % per)),
        ],
        out_shape=[
            jax.ShapeDtypeStruct((nb, s, hy3), BF16),
            jax.ShapeDtypeStruct((nb, heads, LANES, s), BF16),
            jax.ShapeDtypeStruct((nb, s, qk), BF16),
            jax.ShapeDtypeStruct((nb, heads, s // tk, VT_ROWS, tk), BF16),
        ],
        compiler_params=_cp("parallel", "parallel"),
        name="ev_inproj",
    )(*h_args, mods, g, w_in, cos_t, sin_t)


def _split_maps(qt, half):
    row = lax.broadcasted_iota(jnp.int32, qt.shape, 0)
    zero = jnp.zeros_like(qt)
    return jnp.where(row < half, qt, zero), jnp.where(row >= half, qt, zero)


def _attn_out(a0, a1, lam_refs, sg_ref, lam_init, dv):
    lq1_ref, lk1_ref, lq2_ref, lk2_ref = lam_refs
    lam = (jnp.exp(jnp.sum(lq1_ref[...] * lk1_ref[...], axis=-1, keepdims=True))
           - jnp.exp(jnp.sum(lq2_ref[...] * lk2_ref[...], axis=-1, keepdims=True)) + lam_init)
    o = (a0[:dv] / a0[dv:dv + 1] - lam * (a1[:dv] / a1[dv:dv + 1])).T
    on = o * lax.rsqrt(jnp.mean(o * o, axis=-1, keepdims=True) + EPS)
    return on * sg_ref[...] * (1.0 - lam_init)


def _col_max8(sc):
    part = sc[0:SUBLANES]
    for g in range(1, sc.shape[0] // SUBLANES):
        part = jnp.maximum(part, sc[g * SUBLANES:(g + 1) * SUBLANES])
    return part


def _attn_kernel(qt_ref, qnt_ref, k_ref, vt_ref, lq1_ref, lk1_ref, lq2_ref, lk2_ref, sg_ref, o_ref,
                 s_ref, mp_ref, p_ref, acc_ref, m_ref, alpha_ref, *, n_chunks, tk, lam_init, half):
    i = pl.program_id(2)
    dv = 2 * half
    lam_refs = (lq1_ref, lk1_ref, lq2_ref, lk2_ref)
    tq = qnt_ref.shape[-1]
    q_sub = (_split_maps(qt_ref[0, 0, :, 0:tq], half), _split_maps(qt_ref[0, 0, :, tq:2 * tq], half))
    q_next = _split_maps(qnt_ref[0, 0], half)

    def put_scores(slot, qts, c):
        kc = k_ref[0, pl.ds(pl.multiple_of(c * tk, tk), tk), :]
        for j in range(2):
            sc = _dot(kc, qts[j])
            s_ref[slot, j] = sc
            mp_ref[slot, j] = _col_max8(sc)

    def put_probs(slot, sub, first):
        for j in range(2):
            col_max = jnp.max(mp_ref[slot, j], axis=0, keepdims=True)
            if first:
                m_new = col_max
            else:
                m_old = m_ref[sub, j]
                m_new = jnp.maximum(m_old, col_max)
                alpha_ref[slot, j] = jnp.exp2(m_old - m_new)
            m_ref[sub, j] = m_new
            p_ref[slot, j] = jnp.exp2(s_ref[slot, j] - m_new).astype(BF16)

    def add_pv(slot, sub, c, first):
        for j in range(2):
            pv = _dot(vt_ref[0, 0, c], p_ref[slot, j])
            acc_ref[sub, j] = pv if first else alpha_ref[slot, j] * acc_ref[sub, j] + pv

    def sub_tile(sub, parity, qts, qts_after, sub_after):
        slot = lambda c: (c + parity) % 2
        put_scores(slot(0), qts, 2)
        put_probs(slot(1), sub, False)
        add_pv(slot(0), sub, 0, True)
        n_uniform = n_chunks - 3

        unroll = 2

        def group(u, carry):
            c = 1 + unroll * u
            for k in range(unroll):
                put_scores(slot(1 + k), qts, c + k + 2)
                put_probs(slot(k), sub, False)
                add_pv(slot(1 + k), sub, c + k, False)
            return carry

        lax.fori_loop(0, n_uniform // unroll, group, 0)
        c = n_chunks - 2
        put_scores(slot(c), qts_after, 0)
        put_probs(slot(c + 1), sub, False)
        add_pv(slot(c), sub, c, False)
        c = n_chunks - 1
        put_scores(slot(c), qts_after, 1)
        put_probs(slot(c + 1), sub_after, True)
        add_pv(slot(c), sub, c, False)
        o_ref[0, sub * tq:(sub + 1) * tq, :] = _attn_out(
            acc_ref[sub, 0], acc_ref[sub, 1], lam_refs, sg_ref, lam_init, dv).astype(o_ref.dtype)

    @pl.when(i == 0)
    def _():
        put_scores(0, q_sub[0], 0)
        put_probs(0, 0, True)
        put_scores(1, q_sub[0], 1)

    sub_tile(0, 0, q_sub[0], q_sub[1], 1)
    sub_tile(1, 1, q_sub[1], q_next, 0)


def _ctx_attn_kernel(qt_ref, k_ref, vt_ref, lq1_ref, lk1_ref, lq2_ref, lk2_ref, sg_ref, o_ref,
                     *, lam_init, half):
    acc = []
    for qtj in _split_maps(qt_ref[0, 0], half):
        sc = _dot(k_ref[0], qtj)
        col_max = jnp.max(_col_max8(sc), axis=0, keepdims=True)
        acc.append(_dot(vt_ref[0, 0, 0], jnp.exp2(sc - col_max).astype(BF16)))
    o_ref[0] = _attn_out(acc[0], acc[1], (lq1_ref, lk1_ref, lq2_ref, lk2_ref), sg_ref, lam_init,
                         2 * half).astype(o_ref.dtype)


def _attn_chunk(s):
    return next(t for t in (3 * TT, TT) if s % t == 0 and (s // t) % 2 == 1 and s // t >= 5)


def _diff_attention(qt, k, vt, lam_vecs, subln_g, *, n_lat_tiles, lam_init, head_dim):
    nb, s, qk = k.shape
    dv = 2 * head_dim
    heads = qk // dv
    l = n_lat_tiles * TT
    n_chunks, tk = vt.shape[2], vt.shape[4]
    tq = ATTN_SUB_Q if l % (2 * ATTN_SUB_Q) == 0 else TT
    assert l % (2 * tq) == 0 and n_chunks % 2 == 1 and n_chunks >= 5 and s - l == TT
    n_steps = l // (2 * tq)
    vec = lambda n: pl.BlockSpec((1, n), lambda b, h, i: (0, 0))
    vecs = [vec(head_dim)] * 4 + [vec(dv)]
    o_lat = pl.pallas_call(
        functools.partial(_attn_kernel, n_chunks=n_chunks, tk=tk, lam_init=lam_init, half=head_dim),
        grid=(nb, heads, n_steps),
        in_specs=[
            pl.BlockSpec((1, 1, dv, 2 * tq), lambda b, h, i: (b, h, 0, i)),
            pl.BlockSpec((1, 1, dv, tq),
                         lambda b, h, i: (b, h, 0, jnp.minimum(2 * i + 2, 2 * n_steps - 1))),
            pl.BlockSpec((1, s, dv), lambda b, h, i: (b, 0, h)),
            pl.BlockSpec((1, 1, n_chunks, VT_ROWS, tk), lambda b, h, i: (b, h, 0, 0, 0)),
        ] + vecs,
        out_specs=pl.BlockSpec((1, 2 * tq, dv), lambda b, h, i: (b, i, h)),
        out_shape=jax.ShapeDtypeStruct((nb, l, qk), BF16),
        scratch_shapes=[pltpu.VMEM((2, 2, tk, tq), F32), pltpu.VMEM((2, 2, SUBLANES, tq), F32),
                        pltpu.VMEM((2, 2, tk, tq), BF16), pltpu.VMEM((2, 2, VT_ROWS, tq), F32),
                        pltpu.VMEM((2, 2, 1, tq), F32), pltpu.VMEM((2, 2, 1, tq), F32)],
        compiler_params=_cp("parallel", "parallel", "arbitrary"),
        name="diff_attn",
    )(qt, qt, k, vt, *lam_vecs, subln_g)
    per = tk // TT
    vec2 = lambda n: pl.BlockSpec((1, n), lambda b, h: (0, 0))
    o_ctx = pl.pallas_call(
        functools.partial(_ctx_attn_kernel, lam_init=lam_init, half=head_dim),
        grid=(nb, heads),
        in_specs=[pl.BlockSpec((1, 1, dv, TT), lambda b, h: (b, h, 0, n_lat_tiles)),
                  pl.BlockSpec((1, TT, dv), lambda b, h: (b, n_lat_tiles, h)),
                  pl.BlockSpec((1, 1, 1, VT_ROWS, TT),
                               lambda b, h: (b, h, n_lat_tiles // per, 0, n_lat_tiles % per)),
                  ] + [vec2(head_dim)] * 4 + [vec2(dv)],
        out_specs=pl.BlockSpec((1, TT, dv), lambda b, h: (b, 0, h)),
        out_shape=jax.ShapeDtypeStruct((nb, TT, qk), BF16),
        compiler_params=_cp("parallel", "parallel"),
        name="ctx_attn",
    )(qt, k, vt, *lam_vecs, subln_g)
    return o_lat, o_ctx


def _halo_rows(dtype):
    return SUBLANES * 4 // jnp.dtype(dtype).itemsize


def _halo_fill(xp_ref, prev_ref, x_ref, next_ref, has_prev, has_next):
    hr = prev_ref.shape[2]
    zero = jnp.zeros((hr, xp_ref.shape[-1]), F32)
    xp_ref[hr:hr + TT, :] = x_ref[0].astype(F32)
    xp_ref[0:hr, :] = jnp.where(has_prev, prev_ref[0, 0].astype(F32), zero)
    xp_ref[hr + TT:2 * hr + TT, :] = jnp.where(has_next, next_ref[0, 0].astype(F32), zero)


def _conv_taps(xp, w, bias, left):
    rows = xp.shape[0]
    hr = (rows - TT) // 2
    before = None
    for j in range(left):
        z = w[j:j + 1, :] * xp
        before = pltpu.roll(z if before is None else before + z, 1, 0)
    after = None
    for j in range(w.shape[0] - 1, left, -1):
        z = w[j:j + 1, :] * xp
        after = pltpu.roll(z if after is None else after + z, rows - 1, 0)
    acc = w[left:left + 1, :] * xp + bias
    for part in (before, after):
        if part is not None:
            acc = acc + part
    return acc[hr:hr + TT, :]


def _shortconv_kernel(prev_ref, x_ref, next_ref, w_ref, b_ref, v_ref, x1_ref, x2_ref, xp_ref,
                      *, n_seg_tiles, hy):
    i = pl.program_id(1)
    _halo_fill(xp_ref, prev_ref, x_ref, next_ref, i > 0, i < n_seg_tiles - 1)
    y = _conv_taps(xp_ref[...], w_ref[...], b_ref[...], HY_SHORT_LEFT)
    v_ref[0] = y[:, :hy].astype(v_ref.dtype)
    x1_ref[0] = y[:, hy:2 * hy].astype(x1_ref.dtype)
    x2_ref[0] = y[:, 2 * hy:].astype(x2_ref.dtype)


def _shortconv(u, w, b, *, tile0, n_seg_tiles):
    nb, s, hy3 = u.shape
    hy = hy3 // 3
    hr = _halo_rows(u.dtype)
    nh, per = s // hr, TT // hr
    uh = u.reshape(nb, nh, hr, hy3)
    prev = pl.BlockSpec((1, 1, hr, hy3),
                        lambda b, i: (b, jnp.maximum((tile0 + i) * per - 1, 0), 0, 0))
    nxt = pl.BlockSpec((1, 1, hr, hy3),
                       lambda b, i: (b, jnp.minimum((tile0 + i + 1) * per, nh - 1), 0, 0))
    out = jax.ShapeDtypeStruct((nb, n_seg_tiles * TT, hy), BF16)
    ospec = pl.BlockSpec((1, TT, hy), lambda b, i: (b, i, 0))
    return pl.pallas_call(
        functools.partial(_shortconv_kernel, n_seg_tiles=n_seg_tiles, hy=hy),
        grid=(nb, n_seg_tiles),
        in_specs=[
            prev,
            pl.BlockSpec((1, TT, hy3), lambda b, i: (b, tile0 + i, 0)),
            nxt,
            pl.BlockSpec(w.shape, lambda b, i: (0, 0)),
            pl.BlockSpec((1, hy3), lambda b, i: (0, 0)),
        ],
        out_specs=[ospec, ospec, ospec],
        out_shape=[out, out, out],
        scratch_shapes=[pltpu.VMEM((TT + 2 * hr, hy3), F32)],
        compiler_params=_cp("parallel", "parallel"),
        name="hy_shortconv",
    )(uh, u, uh, w, b.reshape(1, hy3))


def _filter_feats(lh, n_cols):
    n = np.arange(2 * lh).reshape(-1, n_cols).T.reshape(-1)
    lag = np.where(n < lh, n, 2 * lh - n).astype(np.float64)
    t = (lag / lh).astype(np.float32).astype(np.float64)
    bands = np.arange(1, N_BANDS + 1, dtype=np.float64)
    ang = 2.0 * math.pi * t[:, None] * bands
    feats = np.concatenate([t[:, None], np.cos(ang), np.sin(ang)], axis=-1)
    pad = (-(feats.shape[1] + 1)) % SUBLANES
    return np.concatenate([feats, np.zeros((2 * lh, pad)), n[:, None].astype(np.float64)], axis=-1)


def _filt_kernel(ft_ref, w1_ref, b1_ref, w2_ref, b2_ref, w3_ref, b3_ref, fr_ref, dec_ref, o_ref,
                 *, lh, c):
    ft = ft_ref[...]
    freq = fr_ref[...]
    h = jnp.sin(freq * (_dot(ft, w1_ref[...], precision=HIGHEST) + b1_ref[...]))
    h = jnp.sin(freq * (_dot(h, w2_ref[...], precision=HIGHEST) + b2_ref[...])).astype(BF16)
    tr = o_ref.shape[1]
    half = tr // 2
    for q in range(ft.shape[0] // tr):
        for dr in range(2):
            rs = slice(q * tr + dr * half, q * tr + (dr + 1) * half)
            t = ft[rs, 0:1]
            f = (_dot(h[rs], w3_ref[dr]) + b3_ref[dr]) * jnp.exp(-t * jnp.abs(dec_ref[dr]))
            if dr == 1:
                f = jnp.where(ft[rs, ft.shape[1] - 1:] == lh, 0.0, f)
            for o in range(2):
                o_ref[o, dr * half:(dr + 1) * half, q * c:(q + 1) * c] = f[:, o * c:(o + 1) * c]


def _hyena_filter(lh, n_cols, w1, b1, w2, b2, w3, b3, freq, decay):
    c = decay.shape[-1]
    feats = jnp.asarray(_filter_feats(lh, n_cols), F32)
    fe = feats.shape[1]
    w1p = jnp.pad(w1, ((0, fe - w1.shape[0]), (0, 0)))
    hid = w1.shape[1]
    tr = 2 * lh // n_cols
    cb = math.gcd(n_cols, FILT_COL_BLOCKS)
    by_dir = lambda a, lead: jnp.moveaxis(a.reshape(lead, 2, 2, c), 2, 0).reshape(2, lead, 2 * c)
    full = lambda a: pl.BlockSpec(a.shape, lambda i: (0,) * a.ndim)
    args = (w1p, b1.reshape(1, hid), w2, b2.reshape(1, hid), by_dir(w3, hid).astype(BF16),
            by_dir(b3, 1), freq.reshape(1, hid), by_dir(decay, 1))
    return pl.pallas_call(
        functools.partial(_filt_kernel, lh=lh, c=c),
        grid=(n_cols // cb,),
        in_specs=[pl.BlockSpec((cb * tr, fe), lambda i: (i, 0))] + [full(a) for a in args],
        out_specs=pl.BlockSpec((2, tr, cb * c), lambda i: (0, 0, i)),
        out_shape=jax.ShapeDtypeStruct((2, tr, n_cols * c), F32),
        compiler_params=_cp("parallel"),
        name="hy_filter",
    )(feats, *args)


@functools.lru_cache(maxsize=None)
def _dft_tables(n):
    nn = n * n
    h = n // 2
    k = np.arange(n)
    th = 2.0 * math.pi * np.outer(k, k) / n
    c, s = np.cos(th), np.sin(th)
    f1_data = np.block([[c[:, :h], s[:, :h]], [-s[:, :h], c[:, :h]]])
    f1_real = np.concatenate([c, -s], axis=0)
    idx = (k[None, None, :] * (k[:, None, None] + n * k[None, :, None])) % nn
    phi = 2.0 * math.pi * idx / nn
    cp, sp = np.cos(phi), np.sin(phi)
    g = np.concatenate([np.concatenate([cp, sp], axis=2), np.concatenate([-sp, cp], axis=2)], axis=1)
    hmat = np.transpose(g, (0, 2, 1)) / nn
    ci, si = c[:h, :], s[:h, :]
    f3 = np.zeros((n, 2 * n))
    f3[:h, 0::2], f3[:h, 1::2] = ci, -si
    f3[h:, 0::2], f3[h:, 1::2] = si, ci
    return f1_data, f1_real, g, hmat, f3


def _fft_s1_kernel(x_ref, f_ref, o_ref):
    o_ref[0] = _dot(f_ref[...], x_ref[0].astype(BF16)).astype(o_ref.dtype)


def _fft_s1(x, f1, n, c):
    p = x.shape[0]
    tc = FFT_N2_TILE * c
    return pl.pallas_call(
        _fft_s1_kernel,
        grid=(p, n * c // tc),
        in_specs=[pl.BlockSpec((1, n, tc), lambda q, j: (q, 0, j)),
                  pl.BlockSpec(f1.shape, lambda q, j: (0, 0))],
        out_specs=pl.BlockSpec((1, 2 * n, tc), lambda q, j: (q, 0, j)),
        out_shape=jax.ShapeDtypeStruct((p, 2 * n, n * c), BF16),
        compiler_params=_cp("parallel", "parallel"),
        name="fft_s1",
    )(x, f1)


def _fft_spec_kernel(a_ref, g_ref, o_ref, *, tk):
    for j in range(tk):
        x = jnp.concatenate([a_ref[0, 0, j], a_ref[0, 1, j]], axis=0)
        o_ref[0, j] = _dot(g_ref[j], x).astype(o_ref.dtype)


def _fft_mid_kernel(a_ref, g_ref, h_ref, kh_ref, o_ref, *, tk, n):
    ts = [_dot(g_ref[j], jnp.concatenate([a_ref[0, 0, j], a_ref[0, 1, j]], axis=0))
          for j in range(tk)]
    ys = []
    for j, t in enumerate(ts):
        tr, ti = t[:n], t[n:]
        kr, ki = kh_ref[0, j, :n].astype(F32), kh_ref[0, j, n:].astype(F32)
        ys.append(jnp.concatenate([tr * kr - ti * ki, tr * ki + ti * kr], axis=0).astype(BF16))
    for j, y in enumerate(ys):
        o_ref[0, j] = _dot(h_ref[j], y).astype(o_ref.dtype)


def _fft_spectrum(a, g, n, c):
    p = a.shape[0]
    tk = FFT_K1_TILE
    a5 = a.reshape(p, 2, n, n, c)
    return pl.pallas_call(
        functools.partial(_fft_spec_kernel, tk=tk),
        grid=(n // tk, p),
        in_specs=[pl.BlockSpec((1, 2, tk, n, c), lambda j, q: (q, 0, j, 0, 0)),
                  pl.BlockSpec((tk, 2 * n, 2 * n), lambda j, q: (j, 0, 0))],
        out_specs=pl.BlockSpec((1, tk, 2 * n, c), lambda j, q: (q, j, 0, 0)),
        out_shape=jax.ShapeDtypeStruct((p, n, 2 * n, c), BF16),
        compiler_params=_cp("parallel", "parallel"),
        name="fft_spectrum",
    )(a5, g)


def _fft_mid(a, g, hm, khat, order, n, c):
    p = a.shape[0]
    tk = FFT_K1_TILE
    a5 = a.reshape(p, 2, n, n, c)
    return pl.pallas_call(
        functools.partial(_fft_mid_kernel, tk=tk, n=n),
        grid=(n // tk, p),
        in_specs=[pl.BlockSpec((1, 2, tk, n, c), lambda j, q: (q, 0, j, 0, 0)),
                  pl.BlockSpec((tk, 2 * n, 2 * n), lambda j, q: (j, 0, 0)),
                  pl.BlockSpec((tk, 2 * n, 2 * n), lambda j, q: (j, 0, 0)),
                  pl.BlockSpec((1, tk, 2 * n, c), lambda j, q: (order, j, 0, 0))],
        out_specs=pl.BlockSpec((1, tk, 2 * n, c), lambda j, q: (q, j, 0, 0)),
        out_shape=jax.ShapeDtypeStruct((p, n, 2 * n, c), BF16),
        compiler_params=_cp("parallel", "parallel"),
        name="fft_mid",
    )(a5, g, hm, khat)


def _fft_s3_kernel(*refs, gated, chained):
    c_ref, f_ref, v_ref, b_ref = refs[:4]
    rest = list(refs[4:])
    y = _dot(f_ref[...], c_ref[0]) + v_ref[0].astype(F32) * b_ref[...]
    if gated:
        y = rest.pop(0)[0].astype(F32) * y
    f1_ref = rest.pop(0) if chained else None
    o_ref = rest.pop(0)
    z = y.astype(o_ref.dtype)
    o_ref[0] = z
    if chained:
        a_ref = rest.pop(0)
        a_ref[0] = _dot(f1_ref[...], z).astype(a_ref.dtype)


def _fft_s3(cm, f3, vin, xg, bias, n, c, f1_next=None):
    p = cm.shape[0]
    tc = FFT_N2_TILE * c
    c2 = cm.reshape(p, 2 * n, n * c)
    bias_t = jnp.tile(bias.reshape(1, c), (1, FFT_N2_TILE))
    blk = pl.BlockSpec((1, n, tc), lambda q, j: (q, 0, j))
    full = lambda a: pl.BlockSpec(a.shape, lambda q, j: (0, 0))
    in_specs = [pl.BlockSpec((1, 2 * n, tc), lambda q, j: (q, 0, j)), full(f3), blk,
                pl.BlockSpec((1, tc), lambda q, j: (0, 0))]
    args = [c2, f3, vin, bias_t]
    out_specs, out_shape = [blk], [jax.ShapeDtypeStruct((p, n, n * c), BF16)]
    if xg is not None:
        in_specs.append(blk)
        args.append(xg)
    if f1_next is not None:
        in_specs.append(full(f1_next))
        args.append(f1_next)
        out_specs.append(pl.BlockSpec((1, 2 * n, tc), lambda q, j: (q, 0, j)))
        out_shape.append(jax.ShapeDtypeStruct((p, 2 * n, n * c), BF16))
    return pl.pallas_call(
        functools.partial(_fft_s3_kernel, gated=xg is not None, chained=f1_next is not None),
        grid=(p, n * c // tc),
        in_specs=in_specs,
        out_specs=out_specs,
        out_shape=out_shape,
        compiler_params=_cp("parallel", "parallel"),
        name="fft_s3",
    )(*args)


def _hyena_long(v, x1, kk, hy_bias):
    nb, l, c = v.shape
    n = math.isqrt(2 * l)
    assert n * n == 2 * l and nb % 2 == 0
    p = nb // 2
    f1d, f1r, g, hm, f3 = (jnp.asarray(t, F32).astype(BF16) for t in _dft_tables(n))
    pair = lambda a: a.reshape(p, n, n * c)
    khat = _fft_spectrum(_fft_s1(kk, f1r, n, c), g, n, c)
    v = pair(v)
    cm = _fft_mid(_fft_s1(v, f1d, n, c), g, hm, khat, 0, n, c)
    z1, a = _fft_s3(cm, f3, v, pair(x1), hy_bias[0], n, c, f1_next=f1d)
    cm = _fft_mid(a, g, hm, khat, 1, n, c)
    (w2,) = _fft_s3(cm, f3, z1, None, hy_bias[1], n, c)
    return w2.reshape(nb, l, c)


@functools.lru_cache(maxsize=None)
def _ctx_dft_tables(lc):
    m = 2 * lc
    k = np.arange(m)
    th = 2.0 * math.pi * np.outer(k, k) / m
    c, s = np.cos(th), np.sin(th)
    f_data = np.block([[c[:, :lc], s[:, :lc]], [-s[:, :lc], c[:, :lc]]])
    f_real = np.concatenate([c, -s], axis=0)
    ci, si = c[:lc, :], s[:lc, :]
    f_inv = np.block([[ci, -si], [si, ci]]) / m
    return f_data, f_real, f_inv


def _ctxconv_kernel(v_ref, x1_ref, x2_ref, kk_ref, fd_ref, fr_ref, fi_ref, b_ref, o_ref, *, m):
    def conv(u, order):
        kh = _dot(fr_ref[...], kk_ref[order].astype(BF16))
        t = _dot(fd_ref[...], u.astype(BF16))
        tr, ti, kr, ki = t[:m], t[m:], kh[:m], kh[m:]
        y = jnp.concatenate([tr * kr - ti * ki, tr * ki + ti * kr], axis=0).astype(BF16)
        return _dot(fi_ref[...], y)

    v = v_ref[0].astype(F32)
    z1 = x1_ref[0].astype(F32) * (conv(v, 0) + v * b_ref[0:1, :])
    o_ref[0] = (x2_ref[0].astype(F32) * (conv(z1, 1) + z1 * b_ref[1:2, :])).astype(o_ref.dtype)


def _hyena_ctx(v, x1, x2, kk, hy_bias):
    nb, lc, c = v.shape
    p, m = nb // 2, 2 * lc
    fd, fr, fi = (jnp.asarray(t, F32).astype(BF16) for t in _ctx_dft_tables(lc))
    pair = lambda a: a.reshape(p, m, c)
    blk = pl.BlockSpec((1, m, c), lambda q: (q, 0, 0))
    full = lambda a: pl.BlockSpec(a.shape, lambda q: (0,) * a.ndim)
    z = pl.pallas_call(
        functools.partial(_ctxconv_kernel, m=m),
        grid=(p,),
        in_specs=[blk, blk, blk, full(kk), full(fd), full(fr), full(fi), full(hy_bias)],
        out_specs=blk,
        out_shape=jax.ShapeDtypeStruct((p, m, c), BF16),
        compiler_params=_cp("parallel"),
        name="hy_ctx",
    )(pair(v), pair(x1), pair(x2), kk, fd, fr, fi, hy_bias)
    return z.reshape(nb, lc, c)


def _od_inproj_kernel(prev_ref, h_ref, next_ref, mod_ref, g_ref, w_ref, cw_ref, cb_ref,
                      gate_ref, xc_ref, *, d, r, n_lat_tiles, n_tiles):
    i = pl.program_id(1)
    seg_first = jnp.logical_or(i == 0, i == n_lat_tiles)
    seg_last = jnp.logical_or(i == n_lat_tiles - 1, i == n_tiles - 1)
    m = mod_ref[0]
    rows = jnp.concatenate([prev_ref[0, 0], h_ref[0], next_ref[0, 0]], axis=0)
    xn = _rms_mod(rows, g_ref[...], m[:, 0:d], m[:, d:2 * d]).astype(BF16)
    y = _dot(xn, w_ref[...])
    gate_ref[0] = y[SUBLANES:SUBLANES + TT, :r].astype(gate_ref.dtype)
    x = y[:, r:]
    row = lax.broadcasted_iota(jnp.int32, (x.shape[0], 1), 0)
    outside = jnp.logical_or(jnp.logical_and(row < SUBLANES, seg_first),
                             jnp.logical_and(row >= SUBLANES + TT, seg_last))
    x = jnp.where(outside, 0.0, x)
    xc_ref[0] = _conv_taps(x, cw_ref[...], cb_ref[...], RG_CONV_LEFT)


def _od_inproj(h, mods, rows, layer, g, w_in, conv_w, conv_b, *, n_lat_tiles):
    nb, s, d = h.shape
    r = w_in.shape[1] // 2
    n_tiles = s // TT
    n8, per = s // SUBLANES, TT // SUBLANES
    h8 = h.reshape(nb, n8, SUBLANES, d)
    ospec = pl.BlockSpec((1, TT, r), lambda b, i: (b, i, 0))
    return pl.pallas_call(
        functools.partial(_od_inproj_kernel, d=d, r=r, n_lat_tiles=n_lat_tiles, n_tiles=n_tiles),
        grid=(nb, n_tiles),
        in_specs=[
            pl.BlockSpec((1, 1, SUBLANES, d), lambda b, i: (b, jnp.maximum(i * per - 1, 0), 0, 0)),
            pl.BlockSpec((1, TT, d), lambda b, i: (b, i, 0)),
            pl.BlockSpec((1, 1, SUBLANES, d),
                         lambda b, i: (b, jnp.minimum((i + 1) * per, n8 - 1), 0, 0)),
            _mod_spec(layer, rows, nb, n_lat_tiles, mods.shape[-1]),
            pl.BlockSpec((1, d), lambda b, i: (0, 0)),
            pl.BlockSpec(w_in.shape, lambda b, i: (0, 0)),
            pl.BlockSpec(conv_w.shape, lambda b, i: (0, 0)),
            pl.BlockSpec((1, r), lambda b, i: (0, 0)),
        ],
        out_specs=[ospec, ospec],
        out_shape=[jax.ShapeDtypeStruct((nb, s, r), BF16), jax.ShapeDtypeStruct((nb, s, r), F32)],
        compiler_params=_cp("parallel", "parallel"),
        name="od_inproj",
    )(h8, h, h8, mods, g, w_in, conv_w, conv_b.reshape(1, r))


def _rglru_kernel(x_ref, w_ref, b_ref, lam_ref, o_ref, a_ref, bb_ref, carry_ref, *, r, windows):
    dr = pl.program_id(1)
    i = pl.program_id(2)
    half = TT // 2
    nl = -lam_ref[0]
    softplus = jnp.maximum(nl, 0.0) + jnp.log1p(jnp.exp(-jnp.abs(nl)))
    neg_rate = (0.5 * RG_C) * softplus
    exp2_rate = (-0.5 * RG_C * math.log2(math.e)) * softplus

    def gates(r0):
        rs = slice(r0, r0 + half)
        xc = x_ref[0, rs, :]
        xb = xc.astype(BF16)
        half_x = 0.5 * xc
        for c0, c1, k0, k1 in windows:
            xk = xb[:, k0:k1]
            tr = jnp.tanh(_dot(xk, w_ref[0, k0:k1, c0:c1]) + b_ref[0, :, c0:c1]) + 1.0
            ti = jnp.tanh(_dot(xk, w_ref[0, k0:k1, r + c0:r + c1]) + b_ref[0, :, r + c0:r + c1]) + 1.0
            a = jnp.exp2(exp2_rate[:, c0:c1] * tr)
            a_ref[rs, c0:c1] = a
            e = jnp.tanh(neg_rate[:, c0:c1] * tr) * (a * a + 1.0)
            root = jnp.where(e > 0.0, e * lax.rsqrt(e), 0.0)
            bb_ref[rs, c0:c1] = root * (ti * half_x[:, c0:c1])

    def scan_rows(rows, h):
        for row in rows:
            h = a_ref[row:row + 1, :] * h + bb_ref[row:row + 1, :]
            o_ref[0, 0, row:row + 1, :] = h
        return h

    def scan_loop(r0, reverse, h):
        n_blk = half // SUBLANES

        def block(g, h):
            base = pl.multiple_of(r0 + (n_blk - 1 - g if reverse else g) * SUBLANES, SUBLANES)
            for k in range(SUBLANES):
                row = base + (SUBLANES - 1 - k if reverse else k)
                h = a_ref[pl.ds(row, 1), :] * h + bb_ref[pl.ds(row, 1), :]
                o_ref[0, 0, pl.ds(row, 1), :] = h
            return h

        return lax.fori_loop(0, n_blk, block, h)

    @pl.when(i == 0)
    def _():
        carry_ref[...] = jnp.zeros_like(carry_ref)

    def run(reverse):
        first, second = (half, 0) if reverse else (0, half)
        gates(first)
        gates(second)
        rows = range(first + half - 1, first - 1, -1) if reverse else range(first, first + half)
        h = scan_rows(rows, carry_ref[0:1, :])
        carry_ref[0:1, :] = scan_loop(second, reverse, h)

    pl.when(dr == 0)(lambda: run(False))
    pl.when(dr == 1)(lambda: run(True))


def _scan_tile(dr, i, n_lat_tiles, n_tiles):
    fwd = jnp.where(i == 0, n_lat_tiles, i - 1)
    bwd = jnp.where(i == 0, n_lat_tiles, n_lat_tiles - i)
    return jnp.where(dr == 0, fwd, bwd)


def _gate_windows(r, bs):
    out = []
    for c0 in range(0, r, 2 * LANES):
        c1 = min(c0 + 2 * LANES, r)
        k0 = (c0 // bs) * bs // LANES * LANES
        k1 = min(-(-(((c1 - 1) // bs + 1) * bs) // LANES) * LANES, r)
        out.append((c0, c1, k0, k1))
    return tuple(out)


def _rglru(xc, wcat, bcat, lam, *, n_lat_tiles, block_size):
    nb, s, r = xc.shape
    n_tiles = s // TT
    assert n_tiles == n_lat_tiles + 1
    tile = lambda d, i: _scan_tile(d, i, n_lat_tiles, n_tiles)
    return pl.pallas_call(
        functools.partial(_rglru_kernel, r=r, windows=_gate_windows(r, block_size)),
        grid=(nb, 2, n_tiles),
        in_specs=[
            pl.BlockSpec((1, TT, r), lambda b, d, i: (b, tile(d, i), 0)),
            pl.BlockSpec((1, r, 2 * r), lambda b, d, i: (d, 0, 0)),
            pl.BlockSpec((1, 1, 2 * r), lambda b, d, i: (d, 0, 0)),
            pl.BlockSpec((1, 1, r), lambda b, d, i: (d, 0, 0)),
        ],
        out_specs=pl.BlockSpec((1, 1, TT, r), lambda b, d, i: (d, b, tile(d, i), 0)),
        out_shape=jax.ShapeDtypeStruct((2, nb, s, r), F32),
        scratch_shapes=[pltpu.VMEM((TT, r), F32), pltpu.VMEM((TT, r), F32),
                        pltpu.VMEM((SUBLANES, r), F32)],
        compiler_params=_cp("parallel", "parallel", "arbitrary"),
        name="rglru",
    )(xc, wcat, bcat, lam.reshape(2, 1, r))


def _block_diag(w):
    n, bs, _ = w.shape
    eye = jnp.eye(n, dtype=w.dtype)
    return (eye[:, None, :, None] * w[:, :, None, :]).reshape(n * bs, n * bs)


def _gelu_tanh(x):
    return 0.5 * x * (1.0 + jnp.tanh(math.sqrt(2.0 / math.pi) * (x + 0.044715 * (x * x * x))))


def _post_kernel(*refs, kind, final, d, n_lat_tiles):
    if kind == "even":
        (hl_ref, hc_ref, x2_ref, wl_ref, zc_ref, ol_ref, oc_ref, mod_ref, g2_ref, wo_ref, w1_ref,
         w2_ref) = refs[:12]
        rest = refs[12:]
        is_ctx = pl.program_id(1) >= n_lat_tiles
        z_lat = (x2_ref[0].astype(F32) * wl_ref[0].astype(F32)).astype(BF16)
        z = jnp.where(is_ctx, zc_ref[0], z_lat)
        o = jnp.where(is_ctx, oc_ref[0], ol_ref[0])
        half = z.shape[-1]
        y = _dot(z, wo_ref[:half, :]) + _dot(o, wo_ref[half:, :])
    else:
        hl_ref, hc_ref, hd_ref, gate_ref, mod_ref, g2_ref, wo_ref, w1_ref, w2_ref = refs[:9]
        rest = refs[9:]
        mix = (hd_ref[0, 0] + hd_ref[1, 0]) * _gelu_tanh(gate_ref[0].astype(F32))
        y = _dot(mix.astype(BF16), wo_ref[...])
    out_ref = rest[-1]
    m = mod_ref[0]
    h1 = _stream_tile(hl_ref, hc_ref, n_lat_tiles) + m[:, 2 * d:3 * d] * y
    xn = _rms_mod(h1, g2_ref[...], m[:, 3 * d:4 * d], m[:, 4 * d:5 * d]).astype(BF16)
    acc = jnp.zeros_like(h1)
    dff = w1_ref.shape[1]
    for c0 in range(0, dff, FF_CHUNK):
        a = jnp.maximum(_dot(xn, w1_ref[:, c0:c0 + FF_CHUNK]), 0.0)
        acc = acc + _dot((a * a).astype(BF16), w2_ref[c0:c0 + FF_CHUNK, :])
    h2 = h1 + m[:, 5 * d:6 * d] * acc
    if final:
        fg_ref = rest[0]
        h2 = h2 * lax.rsqrt(jnp.mean(h2 * h2, axis=-1, keepdims=True) + EPS) * fg_ref[...]
    out_ref[0] = h2


def _post(kind, h, mix_args, mods, rows, layer, g2, w_out, w1, w2, final_g, *, n_lat_tiles, final):
    h_specs, h_args = _stream_specs(h, n_lat_tiles)
    nb, _, d = h_args[0].shape
    n_tiles = n_lat_tiles if final else n_lat_tiles + 1
    tok = lambda w: pl.BlockSpec((1, TT, w), lambda b, i: (b, i, 0))
    full = lambda a: pl.BlockSpec(a.shape, lambda b, i: (0,) * a.ndim)
    if kind == "even":
        lat = lambda w: pl.BlockSpec((1, TT, w), lambda b, i: (b, jnp.minimum(i, n_lat_tiles - 1), 0))
        ctx = lambda w: pl.BlockSpec((1, TT, w), lambda b, i: (b, 0, 0))
        x2_lat, w_lat, z_ctx, o_lat, o_ctx = mix_args
        mix_specs = [lat(x2_lat.shape[-1]), lat(w_lat.shape[-1]), ctx(z_ctx.shape[-1]),
                     lat(o_lat.shape[-1]), ctx(o_ctx.shape[-1])]
    else:
        hd, gate = mix_args
        r = gate.shape[-1]
        mix_specs = [pl.BlockSpec((2, 1, TT, r), lambda b, i: (0, b, i, 0)), tok(r)]
    in_specs = h_specs + mix_specs + [
        _mod_spec(layer, rows, nb, n_lat_tiles, mods.shape[-1]),
        pl.BlockSpec((1, d), lambda b, i: (0, 0)), full(w_out), full(w1), full(w2)]
    args = [*h_args, *mix_args, mods, g2, w_out, w1, w2]
    if final:
        in_specs.append(pl.BlockSpec((1, d), lambda b, i: (0, 0)))
        args.append(final_g)
    return pl.pallas_call(
        functools.partial(_post_kernel, kind=kind, final=final, d=d, n_lat_tiles=n_lat_tiles),
        grid=(nb, n_tiles),
        in_specs=in_specs,
        out_specs=tok(d),
        out_shape=jax.ShapeDtypeStruct((nb, n_tiles * TT, d), F32),
        compiler_params=_cp("parallel", "parallel"),
        name="post_" + kind,
    )(*args)


@functools.lru_cache(maxsize=None)
def _rope_tables(l, lc, head_dim):
    axis = head_dim // 2
    freqs = ROPE_BASE ** (-np.arange(0, axis, 2, dtype=np.float64) / axis)
    freqs = freqs.astype(np.float32).astype(np.float64)
    t = np.arange(l)
    ang_r = (t // GRID_W)[:, None] * freqs
    ang_c = (t % GRID_W)[:, None] * freqs
    cos = np.concatenate([np.cos(ang_r)] * 2 + [np.cos(ang_c)] * 2, axis=-1)
    sin = np.concatenate([-np.sin(ang_r), np.sin(ang_r), -np.sin(ang_c), np.sin(ang_c)], axis=-1)
    cos = np.concatenate([cos, np.ones((lc, head_dim))], axis=0)
    sin = np.concatenate([sin, np.zeros((lc, head_dim))], axis=0)
    rep = LANES // head_dim
    return np.tile(cos, (1, rep)), np.tile(sin, (1, rep))


def kernel(x, c, ctx, c_ctx, ada_w, ada_b, norm1_g, norm2_g, mlp_w1, mlp_w2, final_g, ev_w_in, ev_w_out, hy_short_w, hy_short_b, hy_f_w1, hy_f_b1, hy_f_w2, hy_f_b2, hy_f_w3, hy_f_b3, hy_f_freq, hy_f_decay, hy_bias, df_lq1, df_lk1, df_lq2, df_lk2, df_subln_g, od_w_in, od_w_out, rg_conv_w, rg_conv_b, rg_wa, rg_ba, rg_wx, rg_bx, rg_lam):
    nb, l, d = x.shape
    lc = ctx.shape[1]
    depth = ada_w.shape[0]
    hy = hy_bias.shape[-1]
    head_dim = df_lq1.shape[-1]
    qk = (ev_w_in.shape[-1] - 3 * hy) // 3
    r = rg_lam.shape[-1]
    assert l % TT == 0 and lc == TT and l % GRID_W == 0
    n_lat_tiles = l // TT

    rows = -(-(nb + 1) // SUBLANES) * SUBLANES
    cond = jnp.zeros((rows, d), F32).at[:nb].set(c).at[nb].set(c_ctx)
    mods = _ada_mods(cond, ada_w, ada_b).reshape(depth * rows, 1, ada_w.shape[-1])

    cos_t, sin_t = (jnp.asarray(t, F32) for t in _rope_tables(l, lc, head_dim))
    h = (x, ctx)
    w1_bf, w2_bf = mlp_w1.astype(BF16), mlp_w2.astype(BF16)

    for i in range(depth):
        j = i // 2
        final = i == depth - 1
        g1, g2 = norm1_g[i].reshape(1, d), norm2_g[i].reshape(1, d)
        if i % 2 == 0:
            lam_init = 0.8 - 0.6 * math.exp(-0.3 * i)
            u, qt, k, vt = _ev_inproj(h, mods, rows, i, g1, ev_w_in[j].astype(BF16), cos_t, sin_t,
                                      n_lat_tiles=n_lat_tiles, hy3=3 * hy, qk=qk, head_dim=head_dim,
                                      tk=_attn_chunk(l + lc))
            lam_vecs = [a[j].reshape(1, -1) for a in (df_lq1, df_lk1, df_lq2, df_lk2)]
            o_lat, o_ctx = _diff_attention(qt, k, vt, lam_vecs, df_subln_g[j].reshape(1, -1),
                                           n_lat_tiles=n_lat_tiles, lam_init=lam_init,
                                           head_dim=head_dim)
            fparams = (hy_f_w1[j], hy_f_b1[j], hy_f_w2[j], hy_f_b2[j], hy_f_w3[j], hy_f_b3[j],
                       hy_f_freq[j], hy_f_decay[j])
            vl, x1l, x2l = _shortconv(u, hy_short_w[j], hy_short_b[j], tile0=0,
                                      n_seg_tiles=n_lat_tiles)
            vc, x1c, x2c = _shortconv(u, hy_short_w[j], hy_short_b[j], tile0=n_lat_tiles,
                                      n_seg_tiles=lc // TT)
            w_lat = _hyena_long(vl, x1l, _hyena_filter(l, math.isqrt(2 * l), *fparams), hy_bias[j])
            z_ctx = _hyena_ctx(vc, x1c, x2c, _hyena_filter(lc, 1, *fparams), hy_bias[j])
            h = _post("even", h, (x2l, w_lat, z_ctx, o_lat, o_ctx), mods, rows, i, g2, ev_w_out[j].astype(BF16),
                      w1_bf[i], w2_bf[i], final_g.reshape(1, d), n_lat_tiles=n_lat_tiles,
                      final=final)
        else:
            gate, xc = _od_inproj(h, mods, rows, i, g1, od_w_in[j].astype(BF16), rg_conv_w[j],
                                  rg_conv_b[j], n_lat_tiles=n_lat_tiles)
            wcat = (0.5 * jnp.stack(
                [jnp.concatenate([_block_diag(rg_wa[j, dd]), _block_diag(rg_wx[j, dd])], axis=1)
                 for dd in range(2)])).astype(BF16)
            bcat = 0.5 * jnp.concatenate([rg_ba[j], rg_bx[j]], axis=-1).reshape(2, 1, 2 * r)
            hd = _rglru(xc, wcat, bcat, rg_lam[j], n_lat_tiles=n_lat_tiles,
                        block_size=rg_wa.shape[-1])
            h = _post("odd", h, (hd, gate), mods, rows, i, g2, od_w_out[j].astype(BF16),
                      w1_bf[i], w2_bf[i], final_g.reshape(1, d), n_lat_tiles=n_lat_tiles,
                      final=final)
    return h
```

```python
import functools
import math

import numpy as np
import jax
import jax.numpy as jnp
from jax import lax
from jax.experimental import pallas as pl
from jax.experimental.pallas import tpu as pltpu

F32 = jnp.float32
BF16 = jnp.bfloat16
HIGHEST = lax.Precision.HIGHEST

EPS = 1e-6
GRID_W = 64
ROPE_BASE = 10000.0
N_BANDS = 16
RG_C = 8.0
RG_CONV_LEFT = 2
HY_SHORT_LEFT = 1

TT = 256
LANES = 128
SUBLANES = 8
VMEM_LIMIT = 56 * 1024 * 1024
FF_CHUNK = 1024
FFT_N2_TILE = 16
FFT_K1_TILE = 16
ADA_K_TILE = 256
ATTN_SUB_Q = 512
RG_PARTS = 2
FILT_COL_BLOCKS = 4
VT_ROWS = LANES + 16


def _cp(*sem):
    return pltpu.CompilerParams(dimension_semantics=sem, vmem_limit_bytes=VMEM_LIMIT)


def _dot(a, b, **kw):
    return jnp.dot(a, b, preferred_element_type=F32, **kw)


def _sigmoid(x):
    return 1.0 / (1.0 + jnp.exp(-x))


def _rms_mod(x, g, shift, scale):
    y = x * lax.rsqrt(jnp.mean(x * x, axis=-1, keepdims=True) + EPS)
    return (y * g) * (1.0 + scale) + shift


def _ada_kernel(c_ref, w_ref, b_ref, o_ref):
    k = pl.program_id(1)
    c = c_ref[k]
    part = _dot((c * _sigmoid(c)).astype(BF16), w_ref[0].astype(BF16))

    @pl.when(k == 0)
    def _():
        o_ref[0] = part + b_ref[0]

    @pl.when(k > 0)
    def _():
        o_ref[0] += part


def _ada_mods(cond, ada_w, ada_b):
    depth, d, n = ada_w.shape
    rows = cond.shape[0]
    tk = min(d, ADA_K_TILE)
    cond_k = cond.reshape(rows, d // tk, tk).transpose(1, 0, 2)
    return pl.pallas_call(
        _ada_kernel,
        grid=(depth, d // tk),
        in_specs=[
            pl.BlockSpec(cond_k.shape, lambda l, k: (0, 0, 0)),
            pl.BlockSpec((1, tk, n), lambda l, k: (l, k, 0)),
            pl.BlockSpec((1, 1, n), lambda l, k: (l, 0, 0)),
        ],
        out_specs=pl.BlockSpec((1, rows, n), lambda l, k: (l, 0, 0)),
        out_shape=jax.ShapeDtypeStruct((depth, rows, n), F32),
        compiler_params=_cp("parallel", "arbitrary"),
        name="ada_mods",
    )(cond_k, ada_w, ada_b.reshape(depth, 1, n))


def _stream_specs(h, n_lat_tiles):
    separate = isinstance(h, tuple)
    lat_arr, ctx_arr = h if separate else (h, h)
    ctx_blk = 0 if separate else n_lat_tiles
    d = lat_arr.shape[-1]
    lat = pl.BlockSpec((1, TT, d), lambda b, i: (b, jnp.minimum(i, n_lat_tiles - 1), 0))
    ctx = pl.BlockSpec((1, TT, d), lambda b, i: (b, ctx_blk, 0))
    return [lat, ctx], [lat_arr, ctx_arr]


def _stream_tile(hl_ref, hc_ref, n_lat_tiles):
    return jnp.where(pl.program_id(1) >= n_lat_tiles, hc_ref[0], hl_ref[0])


def _mod_spec(layer, rows, nb, n_lat_tiles, n6):
    def imap(b, i):
        return (layer * rows + jnp.where(i >= n_lat_tiles, nb, b), 0, 0)
    return pl.BlockSpec((1, 1, n6), imap)


def _ev_inproj_kernel(hl_ref, hc_ref, mod_ref, g_ref, w_ref, cos_ref, sin_ref,
                      u_ref, qt_ref, k_ref, vt_ref, *, d, hy3, qk, qscale, n_lat_tiles):
    m = mod_ref[0]
    x = _stream_tile(hl_ref, hc_ref, n_lat_tiles)
    xn = _rms_mod(x, g_ref[...], m[:, 0:d], m[:, d:2 * d]).astype(BF16)
    cos = cos_ref[...]
    sin = sin_ref[...]
    lane = lax.broadcasted_iota(jnp.int32, cos.shape, 1)
    first = (lane % 32) < 16

    def rope(z):
        sw = jnp.where(first, pltpu.roll(z, LANES - 16, 1), pltpu.roll(z, 16, 1))
        return z * cos + sw * sin

    extra = vt_ref.shape[3] - LANES
    ones_row = jnp.where(lax.broadcasted_iota(jnp.int32, (extra, cos.shape[0]), 0) == 0,
                         1.0, 0.0).astype(BF16)
    yq = _dot(xn, w_ref[:, hy3:hy3 + qk])
    for c in range(qk // LANES):
        qt_ref[0, c] = (rope(yq[:, c * LANES:(c + 1) * LANES]) * qscale).T.astype(BF16)
    yk = _dot(xn, w_ref[:, hy3 + qk:hy3 + 2 * qk])
    for c in range(qk // LANES):
        k_ref[0, :, c * LANES:(c + 1) * LANES] = rope(yk[:, c * LANES:(c + 1) * LANES]).astype(BF16)
    yv = _dot(xn, w_ref[:, hy3 + 2 * qk:])
    for c in range(qk // LANES):
        vt_ref[0, c, 0, 0:LANES, :] = yv[:, c * LANES:(c + 1) * LANES].T.astype(BF16)
        vt_ref[0, c, 0, LANES:, :] = ones_row
    u_ref[0] = _dot(xn, w_ref[:, :hy3]).astype(u_ref.dtype)


def _ev_inproj(h, mods, rows, layer, g, w_in, cos_t, sin_t, *, n_lat_tiles, hy3, qk, head_dim, tk):
    h_specs, h_args = _stream_specs(h, n_lat_tiles)
    nb, _, d = h_args[0].shape
    s = (n_lat_tiles + 1) * TT
    n_in = w_in.shape[1]
    assert 2 * head_dim == LANES
    heads, per = qk // LANES, tk // TT
    kern = functools.partial(_ev_inproj_kernel, d=d, hy3=hy3, qk=qk, n_lat_tiles=n_lat_tiles,
                             qscale=head_dim ** -0.5 * math.log2(math.e))
    return pl.pallas_call(
        kern,
        grid=(nb, s // TT),
        in_specs=h_specs + [
            _mod_spec(layer, rows, nb, n_lat_tiles, mods.shape[-1]),
            pl.BlockSpec((1, d), lambda b, i: (0, 0)),
            pl.BlockSpec((d, n_in), lambda b, i: (0, 0)),
            pl.BlockSpec((TT, LANES), lambda b, i: (i, 0)),
            pl.BlockSpec((TT, LANES), lambda b, i: (i, 0)),
        ],
        out_specs=[
            pl.BlockSpec((1, TT, hy3), lambda b, i: (b, i, 0)),
            pl.BlockSpec((1, heads, LANES, TT), lambda b, i: (b, 0, 0, i)),
            pl.BlockSpec((1, TT, qk), lambda b, i: (b, i, 0)),
            pl.BlockSpec((1, heads, 1, VT_ROWS, TT), lambda b, i: (b, 0, i // per, 0, i % per)),
        ],
        out_shape=[
            jax.ShapeDtypeStruct((nb, s, hy3), BF16),
            jax.ShapeDtypeStruct((nb, heads, LANES, s), BF16),
            jax.ShapeDtypeStruct((nb, s, qk), BF16),
            jax.ShapeDtypeStruct((nb, heads, s // tk, VT_ROWS, tk), BF16),
        ],
        compiler_params=_cp("parallel", "parallel"),
        name="ev_inproj",
    )(*h_args, mods, g, w_in, cos_t, sin_t)


def _split_maps(qt, half):
    row = lax.broadcasted_iota(jnp.int32, qt.shape, 0)
    zero = jnp.zeros_like(qt)
    return jnp.where(row < half, qt, zero), jnp.where(row >= half, qt, zero)


def _attn_out(a0, a1, lam_refs, sg_ref, lam_init, dv):
    lq1_ref, lk1_ref, lq2_ref, lk2_ref = lam_refs
    lam = (jnp.exp(jnp.sum(lq1_ref[...] * lk1_ref[...], axis=-1, keepdims=True))
           - jnp.exp(jnp.sum(lq2_ref[...] * lk2_ref[...], axis=-1, keepdims=True)) + lam_init)
    o = (a0[:dv] / a0[dv:dv + 1] - lam * (a1[:dv] / a1[dv:dv + 1])).T
    on = o * lax.rsqrt(jnp.mean(o * o, axis=-1, keepdims=True) + EPS)
    return on * sg_ref[...] * (1.0 - lam_init)


def _col_max8(sc):
    part = sc[0:SUBLANES]
    for g in range(1, sc.shape[0] // SUBLANES):
        part = jnp.maximum(part, sc[g * SUBLANES:(g + 1) * SUBLANES])
    return part


def _attn_kernel(qt_ref, qnt_ref, k_ref, vt_ref, lq1_ref, lk1_ref, lq2_ref, lk2_ref, sg_ref, o_ref,
                 s_ref, mp_ref, p_ref, acc_ref, m_ref, alpha_ref, *, n_chunks, tk, lam_init, half):
    i = pl.program_id(2)
    dv = 2 * half
    lam_refs = (lq1_ref, lk1_ref, lq2_ref, lk2_ref)
    tq = qnt_ref.shape[-1]
    q_sub = (_split_maps(qt_ref[0, 0, :, 0:tq], half), _split_maps(qt_ref[0, 0, :, tq:2 * tq], half))
    q_next = _split_maps(qnt_ref[0, 0], half)

    def put_scores(slot, qts, c):
        kc = k_ref[0, pl.ds(pl.multiple_of(c * tk, tk), tk), :]
        for j in range(2):
            sc = _dot(kc, qts[j])
            s_ref[slot, j] = sc
            mp_ref[slot, j] = _col_max8(sc)

    def put_probs(slot, sub, first):
        for j in range(2):
            col_max = jnp.max(mp_ref[slot, j], axis=0, keepdims=True)
            if first:
                m_new = col_max
            else:
                m_old = m_ref[sub, j]
                m_new = jnp.maximum(m_old, col_max)
                alpha_ref[slot, j] = jnp.exp2(m_old - m_new)
            m_ref[sub, j] = m_new
            p_ref[slot, j] = jnp.exp2(s_ref[slot, j] - m_new).astype(BF16)

    def add_pv(slot, sub, c, first):
        for j in range(2):
            pv = _dot(vt_ref[0, 0, c], p_ref[slot, j])
            acc_ref[sub, j] = pv if first else alpha_ref[slot, j] * acc_ref[sub, j] + pv

    def sub_tile(sub, parity, qts, qts_after, sub_after):
        slot = lambda c: (c + parity) % 2
        put_scores(slot(0), qts, 2)
        put_probs(slot(1), sub, False)
        add_pv(slot(0), sub, 0, True)
        n_uniform = n_chunks - 3

        unroll = 2

        def group(u, carry):
            c = 1 + unroll * u
            for k in range(unroll):
                put_scores(slot(1 + k), qts, c + k + 2)
                put_probs(slot(k), sub, False)
                add_pv(slot(1 + k), sub, c + k, False)
            return carry

        lax.fori_loop(0, n_uniform // unroll, group, 0)
        c = n_chunks - 2
        put_scores(slot(c), qts_after, 0)
        put_probs(slot(c + 1), sub, False)
        add_pv(slot(c), sub, c, False)
        c = n_chunks - 1
        put_scores(slot(c), qts_after, 1)
        put_probs(slot(c + 1), sub_after, True)
        add_pv(slot(c), sub, c, False)
        o_ref[0, sub * tq:(sub + 1) * tq, :] = _attn_out(
            acc_ref[sub, 0], acc_ref[sub, 1], lam_refs, sg_ref, lam_init, dv).astype(o_ref.dtype)

    @pl.when(i == 0)
    def _():
        put_scores(0, q_sub[0], 0)
        put_probs(0, 0, True)
        put_scores(1, q_sub[0], 1)

    sub_tile(0, 0, q_sub[0], q_sub[1], 1)
    sub_tile(1, 1, q_sub[1], q_next, 0)


def _ctx_attn_kernel(qt_ref, k_ref, vt_ref, lq1_ref, lk1_ref, lq2_ref, lk2_ref, sg_ref, o_ref,
                     *, lam_init, half):
    acc = []
    for qtj in _split_maps(qt_ref[0, 0], half):
        sc = _dot(k_ref[0], qtj)
        col_max = jnp.max(_col_max8(sc), axis=0, keepdims=True)
        acc.append(_dot(vt_ref[0, 0, 0], jnp.exp2(sc - col_max).astype(BF16)))
    o_ref[0] = _attn_out(acc[0], acc[1], (lq1_ref, lk1_ref, lq2_ref, lk2_ref), sg_ref, lam_init,
                         2 * half).astype(o_ref.dtype)


def _attn_chunk(s):
    return next(t for t in (3 * TT, TT) if s % t == 0 and (s // t) % 2 == 1 and s // t >= 5)


def _diff_attention(qt, k, vt, lam_vecs, subln_g, *, n_lat_tiles, lam_init, head_dim):
    nb, s, qk = k.shape
    dv = 2 * head_dim
    heads = qk // dv
    l = n_lat_tiles * TT
    n_chunks, tk = vt.shape[2], vt.shape[4]
    tq = ATTN_SUB_Q if l % (2 * ATTN_SUB_Q) == 0 else TT
    assert l % (2 * tq) == 0 and n_chunks % 2 == 1 and n_chunks >= 5 and s - l == TT
    n_steps = l // (2 * tq)
    vec = lambda n: pl.BlockSpec((1, n), lambda b, h, i: (0, 0))
    vecs = [vec(head_dim)] * 4 + [vec(dv)]
    o_lat = pl.pallas_call(
        functools.partial(_attn_kernel, n_chunks=n_chunks, tk=tk, lam_init=lam_init, half=head_dim),
        grid=(nb, heads, n_steps),
        in_specs=[
            pl.BlockSpec((1, 1, dv, 2 * tq), lambda b, h, i: (b, h, 0, i)),
            pl.BlockSpec((1, 1, dv, tq),
                         lambda b, h, i: (b, h, 0, jnp.minimum(2 * i + 2, 2 * n_steps - 1))),
            pl.BlockSpec((1, s, dv), lambda b, h, i: (b, 0, h)),
            pl.BlockSpec((1, 1, n_chunks, VT_ROWS, tk), lambda b, h, i: (b, h, 0, 0, 0)),
        ] + vecs,
        out_specs=pl.BlockSpec((1, 2 * tq, dv), lambda b, h, i: (b, i, h)),
        out_shape=jax.ShapeDtypeStruct((nb, l, qk), BF16),
        scratch_shapes=[pltpu.VMEM((2, 2, tk, tq), F32), pltpu.VMEM((2, 2, SUBLANES, tq), F32),
                        pltpu.VMEM((2, 2, tk, tq), BF16), pltpu.VMEM((2, 2, VT_ROWS, tq), F32),
                        pltpu.VMEM((2, 2, 1, tq), F32), pltpu.VMEM((2, 2, 1, tq), F32)],
        compiler_params=_cp("parallel", "parallel", "arbitrary"),
        name="diff_attn",
    )(qt, qt, k, vt, *lam_vecs, subln_g)
    per = tk // TT
    vec2 = lambda n: pl.BlockSpec((1, n), lambda b, h: (0, 0))
    o_ctx = pl.pallas_call(
        functools.partial(_ctx_attn_kernel, lam_init=lam_init, half=head_dim),
        grid=(nb, heads),
        in_specs=[pl.BlockSpec((1, 1, dv, TT), lambda b, h: (b, h, 0, n_lat_tiles)),
                  pl.BlockSpec((1, TT, dv), lambda b, h: (b, n_lat_tiles, h)),
                  pl.BlockSpec((1, 1, 1, VT_ROWS, TT),
                               lambda b, h: (b, h, n_lat_tiles // per, 0, n_lat_tiles % per)),
                  ] + [vec2(head_dim)] * 4 + [vec2(dv)],
        out_specs=pl.BlockSpec((1, TT, dv), lambda b, h: (b, 0, h)),
        out_shape=jax.ShapeDtypeStruct((nb, TT, qk), BF16),
        compiler_params=_cp("parallel", "parallel"),
        name="ctx_attn",
    )(qt, k, vt, *lam_vecs, subln_g)
    return o_lat, o_ctx


def _halo_rows(dtype):
    return SUBLANES * 4 // jnp.dtype(dtype).itemsize


def _halo_fill(xp_ref, prev_ref, x_ref, next_ref, has_prev, has_next):
    hr = prev_ref.shape[2]
    zero = jnp.zeros((hr, xp_ref.shape[-1]), F32)
    xp_ref[hr:hr + TT, :] = x_ref[0].astype(F32)
    xp_ref[0:hr, :] = jnp.where(has_prev, prev_ref[0, 0].astype(F32), zero)
    xp_ref[hr + TT:2 * hr + TT, :] = jnp.where(has_next, next_ref[0, 0].astype(F32), zero)


def _conv_taps(xp, w, bias, left):
    rows = xp.shape[0]
    hr = (rows - TT) // 2
    before = None
    for j in range(left):
        z = w[j:j + 1, :] * xp
        before = pltpu.roll(z if before is None else before + z, 1, 0)
    after = None
    for j in range(w.shape[0] - 1, left, -1):
        z = w[j:j + 1, :] * xp
        after = pltpu.roll(z if after is None else after + z, rows - 1, 0)
    acc = w[left:left + 1, :] * xp + bias
    for part in (before, after):
        if part is not None:
            acc = acc + part
    return acc[hr:hr + TT, :]


def _shortconv_kernel(prev_ref, x_ref, next_ref, w_ref, b_ref, v_ref, x1_ref, x2_ref, xp_ref,
                      *, n_seg_tiles, hy):
    i = pl.program_id(1)
    _halo_fill(xp_ref, prev_ref, x_ref, next_ref, i > 0, i < n_seg_tiles - 1)
    y = _conv_taps(xp_ref[...], w_ref[...], b_ref[...], HY_SHORT_LEFT)
    v_ref[0] = y[:, :hy].astype(v_ref.dtype)
    x1_ref[0] = y[:, hy:2 * hy].astype(x1_ref.dtype)
    x2_ref[0] = y[:, 2 * hy:].astype(x2_ref.dtype)


def _shortconv(u, w, b, *, tile0, n_seg_tiles):
    nb, s, hy3 = u.shape
    hy = hy3 // 3
    hr = _halo_rows(u.dtype)
    nh, per = s // hr, TT // hr
    uh = u.reshape(nb, nh, hr, hy3)
    prev = pl.BlockSpec((1, 1, hr, hy3),
                        lambda b, i: (b, jnp.maximum((tile0 + i) * per - 1, 0), 0, 0))
    nxt = pl.BlockSpec((1, 1, hr, hy3),
                       lambda b, i: (b, jnp.minimum((tile0 + i + 1) * per, nh - 1), 0, 0))
    out = jax.ShapeDtypeStruct((nb, n_seg_tiles * TT, hy), BF16)
    ospec = pl.BlockSpec((1, TT, hy), lambda b, i: (b, i, 0))
    return pl.pallas_call(
        functools.partial(_shortconv_kernel, n_seg_tiles=n_seg_tiles, hy=hy),
        grid=(nb, n_seg_tiles),
        in_specs=[
            prev,
            pl.BlockSpec((1, TT, hy3), lambda b, i: (b, tile0 + i, 0)),
            nxt,
            pl.BlockSpec(w.shape, lambda b, i: (0, 0)),
            pl.BlockSpec((1, hy3), lambda b, i: (0, 0)),
        ],
        out_specs=[ospec, ospec, ospec],
        out_shape=[out, out, out],
        scratch_shapes=[pltpu.VMEM((TT + 2 * hr, hy3), F32)],
        compiler_params=_cp("parallel", "parallel"),
        name="hy_shortconv",
    )(uh, u, uh, w, b.reshape(1, hy3))


def _filter_feats(lh, n_cols):
    n = np.arange(2 * lh).reshape(-1, n_cols).T.reshape(-1)
    lag = np.where(n < lh, n, 2 * lh - n).astype(np.float64)
    t = (lag / lh).astype(np.float32).astype(np.float64)
    bands = np.arange(1, N_BANDS + 1, dtype=np.float64)
    ang = 2.0 * math.pi * t[:, None] * bands
    feats = np.concatenate([t[:, None], np.cos(ang), np.sin(ang)], axis=-1)
    pad = (-(feats.shape[1] + 1)) % SUBLANES
    return np.concatenate([feats, np.zeros((2 * lh, pad)), n[:, None].astype(np.float64)], axis=-1)


def _filt_kernel(ft_ref, w1_ref, b1_ref, w2_ref, b2_ref, w3_ref, b3_ref, fr_ref, dec_ref, o_ref,
                 *, lh, c):
    ft = ft_ref[...]
    freq = fr_ref[...]
    h = jnp.sin(freq * (_dot(ft, w1_ref[...], precision=HIGHEST) + b1_ref[...]))
    h = jnp.sin(freq * (_dot(h, w2_ref[...], precision=HIGHEST) + b2_ref[...])).astype(BF16)
    tr = o_ref.shape[1]
    half = tr // 2
    for q in range(ft.shape[0] // tr):
        for dr in range(2):
            rs = slice(q * tr + dr * half, q * tr + (dr + 1) * half)
            t = ft[rs, 0:1]
            f = (_dot(h[rs], w3_ref[dr]) + b3_ref[dr]) * jnp.exp(-t * jnp.abs(dec_ref[dr]))
            if dr == 1:
                f = jnp.where(ft[rs, ft.shape[1] - 1:] == lh, 0.0, f)
            for o in range(2):
                o_ref[o, dr * half:(dr + 1) * half, q * c:(q + 1) * c] = f[:, o * c:(o + 1) * c]


def _hyena_filter(lh, n_cols, w1, b1, w2, b2, w3, b3, freq, decay):
    c = decay.shape[-1]
    feats = jnp.asarray(_filter_feats(lh, n_cols), F32)
    fe = feats.shape[1]
    w1p = jnp.pad(w1, ((0, fe - w1.shape[0]), (0, 0)))
    hid = w1.shape[1]
    tr = 2 * lh // n_cols
    cb = math.gcd(n_cols, FILT_COL_BLOCKS)
    by_dir = lambda a, lead: jnp.moveaxis(a.reshape(lead, 2, 2, c), 2, 0).reshape(2, lead, 2 * c)
    full = lambda a: pl.BlockSpec(a.shape, lambda i: (0,) * a.ndim)
    args = (w1p, b1.reshape(1, hid), w2, b2.reshape(1, hid), by_dir(w3, hid).astype(BF16),
            by_dir(b3, 1), freq.reshape(1, hid), by_dir(decay, 1))
    return pl.pallas_call(
        functools.partial(_filt_kernel, lh=lh, c=c),
        grid=(n_cols // cb,),
        in_specs=[pl.BlockSpec((cb * tr, fe), lambda i: (i, 0))] + [full(a) for a in args],
        out_specs=pl.BlockSpec((2, tr, cb * c), lambda i: (0, 0, i)),
        out_shape=jax.ShapeDtypeStruct((2, tr, n_cols * c), F32),
        compiler_params=_cp("parallel"),
        name="hy_filter",
    )(feats, *args)


@functools.lru_cache(maxsize=None)
def _dft_tables(n):
    nn = n * n
    h = n // 2
    k = np.arange(n)
    th = 2.0 * math.pi * np.outer(k, k) / n
    c, s = np.cos(th), np.sin(th)
    f1_data = np.block([[c[:, :h], s[:, :h]], [-s[:, :h], c[:, :h]]])
    f1_real = np.concatenate([c, -s], axis=0)
    idx = (k[None, None, :] * (k[:, None, None] + n * k[None, :, None])) % nn
    phi = 2.0 * math.pi * idx / nn
    cp, sp = np.cos(phi), np.sin(phi)
    g = np.concatenate([np.concatenate([cp, sp], axis=2), np.concatenate([-sp, cp], axis=2)], axis=1)
    hmat = np.transpose(g, (0, 2, 1)) / nn
    ci, si = c[:h, :], s[:h, :]
    f3 = np.zeros((n, 2 * n))
    f3[:h, 0::2], f3[:h, 1::2] = ci, -si
    f3[h:, 0::2], f3[h:, 1::2] = si, ci
    return f1_data, f1_real, g, hmat, f3


def _fft_s1_kernel(x_ref, f_ref, o_ref):
    o_ref[0] = _dot(f_ref[...], x_ref[0].astype(BF16)).astype(o_ref.dtype)


def _fft_s1(x, f1, n, c):
    p = x.shape[0]
    tc = FFT_N2_TILE * c
    return pl.pallas_call(
        _fft_s1_kernel,
        grid=(p, n * c // tc),
        in_specs=[pl.BlockSpec((1, n, tc), lambda q, j: (q, 0, j)),
                  pl.BlockSpec(f1.shape, lambda q, j: (0, 0))],
        out_specs=pl.BlockSpec((1, 2 * n, tc), lambda q, j: (q, 0, j)),
        out_shape=jax.ShapeDtypeStruct((p, 2 * n, n * c), BF16),
        compiler_params=_cp("parallel", "parallel"),
        name="fft_s1",
    )(x, f1)


def _fft_spec_kernel(a_ref, g_ref, o_ref, *, tk):
    for j in range(tk):
        x = jnp.concatenate([a_ref[0, 0, j], a_ref[0, 1, j]], axis=0)
        o_ref[0, j] = _dot(g_ref[j], x).astype(o_ref.dtype)


def _fft_mid_kernel(a_ref, g_ref, h_ref, kh_ref, o_ref, *, tk, n):
    ts = [_dot(g_ref[j], jnp.concatenate([a_ref[0, 0, j], a_ref[0, 1, j]], axis=0))
          for j in range(tk)]
    ys = []
    for j, t in enumerate(ts):
        tr, ti = t[:n], t[n:]
        kr, ki = kh_ref[0, j, :n].astype(F32), kh_ref[0, j, n:].astype(F32)
        ys.append(jnp.concatenate([tr * kr - ti * ki, tr * ki + ti * kr], axis=0).astype(BF16))
    for j, y in enumerate(ys):
        o_ref[0, j] = _dot(h_ref[j], y).astype(o_ref.dtype)


def _fft_spectrum(a, g, n, c):
    p = a.shape[0]
    tk = FFT_K1_TILE
    a5 = a.reshape(p, 2, n, n, c)
    return pl.pallas_call(
        functools.partial(_fft_spec_kernel, tk=tk),
        grid=(n // tk, p),
        in_specs=[pl.BlockSpec((1, 2, tk, n, c), lambda j, q: (q, 0, j, 0, 0)),
                  pl.BlockSpec((tk, 2 * n, 2 * n), lambda j, q: (j, 0, 0))],
        out_specs=pl.BlockSpec((1, tk, 2 * n, c), lambda j, q: (q, j, 0, 0)),
        out_shape=jax.ShapeDtypeStruct((p, n, 2 * n, c), BF16),
        compiler_params=_cp("parallel", "parallel"),
        name="fft_spectrum",
    )(a5, g)


def _fft_mid(a, g, hm, khat, order, n, c):
    p = a.shape[0]
    tk = FFT_K1_TILE
    a5 = a.reshape(p, 2, n, n, c)
    return pl.pallas_call(
        functools.partial(_fft_mid_kernel, tk=tk, n=n),
        grid=(n // tk, p),
        in_specs=[pl.BlockSpec((1, 2, tk, n, c), lambda j, q: (q, 0, j, 0, 0)),
                  pl.BlockSpec((tk, 2 * n, 2 * n), lambda j, q: (j, 0, 0)),
                  pl.BlockSpec((tk, 2 * n, 2 * n), lambda j, q: (j, 0, 0)),
                  pl.BlockSpec((1, tk, 2 * n, c), lambda j, q: (order, j, 0, 0))],
        out_specs=pl.BlockSpec((1, tk, 2 * n, c), lambda j, q: (q, j, 0, 0)),
        out_shape=jax.ShapeDtypeStruct((p, n, 2 * n, c), BF16),
        compiler_params=_cp("parallel", "parallel"),
        name="fft_mid",
    )(a5, g, hm, khat)


def _fft_s3_kernel(*refs, gated, chained):
    c_ref, f_ref, v_ref, b_ref = refs[:4]
    rest = list(refs[4:])
    y = _dot(f_ref[...], c_ref[0]) + v_ref[0].astype(F32) * b_ref[...]
    if gated:
        y = rest.pop(0)[0].astype(F32) * y
    f1_ref = rest.pop(0) if chained else None
    o_ref = rest.pop(0)
    z = y.astype(o_ref.dtype)
    o_ref[0] = z
    if chained:
        a_ref = rest.pop(0)
        a_ref[0] = _dot(f1_ref[...], z).astype(a_ref.dtype)


def _fft_s3(cm, f3, vin, xg, bias, n, c, f1_next=None):
    p = cm.shape[0]
    tc = FFT_N2_TILE * c
    c2 = cm.reshape(p, 2 * n, n * c)
    bias_t = jnp.tile(bias.reshape(1, c), (1, FFT_N2_TILE))
    blk = pl.BlockSpec((1, n, tc), lambda q, j: (q, 0, j))
    full = lambda a: pl.BlockSpec(a.shape, lambda q, j: (0, 0))
    in_specs = [pl.BlockSpec((1, 2 * n, tc), lambda q, j: (q, 0, j)), full(f3), blk,
                pl.BlockSpec((1, tc), lambda q, j: (0, 0))]
    args = [c2, f3, vin, bias_t]
    out_specs, out_shape = [blk], [jax.ShapeDtypeStruct((p, n, n * c), BF16)]
    if xg is not None:
        in_specs.append(blk)
        args.append(xg)
    if f1_next is not None:
        in_specs.append(full(f1_next))
        args.append(f1_next)
        out_specs.append(pl.BlockSpec((1, 2 * n, tc), lambda q, j: (q, 0, j)))
        out_shape.append(jax.ShapeDtypeStruct((p, 2 * n, n * c), BF16))
    return pl.pallas_call(
        functools.partial(_fft_s3_kernel, gated=xg is not None, chained=f1_next is not None),
        grid=(p, n * c // tc),
        in_specs=in_specs,
        out_specs=out_specs,
        out_shape=out_shape,
        compiler_params=_cp("parallel", "parallel"),
        name="fft_s3",
    )(*args)


def _hyena_long(v, x1, kk, hy_bias):
    nb, l, c = v.shape
    n = math.isqrt(2 * l)
    assert n * n == 2 * l and nb % 2 == 0
    p = nb // 2
    f1d, f1r, g, hm, f3 = (jnp.asarray(t, F32).astype(BF16) for t in _dft_tables(n))
    pair = lambda a: a.reshape(p, n, n * c)
    khat = _fft_spectrum(_fft_s1(kk, f1r, n, c), g, n, c)
    v = pair(v)
    cm = _fft_mid(_fft_s1(v, f1d, n, c), g, hm, khat, 0, n, c)
    z1, a = _fft_s3(cm, f3, v, pair(x1), hy_bias[0], n, c, f1_next=f1d)
    cm = _fft_mid(a, g, hm, khat, 1, n, c)
    (w2,) = _fft_s3(cm, f3, z1, None, hy_bias[1], n, c)
    return w2.reshape(nb, l, c)


@functools.lru_cache(maxsize=None)
def _ctx_dft_tables(lc):
    m = 2 * lc
    k = np.arange(m)
    th = 2.0 * math.pi * np.outer(k, k) / m
    c, s = np.cos(th), np.sin(th)
    f_data = np.block([[c[:, :lc], s[:, :lc]], [-s[:, :lc], c[:, :lc]]])
    f_real = np.concatenate([c, -s], axis=0)
    ci, si = c[:lc, :], s[:lc, :]
    f_inv = np.block([[ci, -si], [si, ci]]) / m
    return f_data, f_real, f_inv


def _ctxconv_kernel(v_ref, x1_ref, x2_ref, kk_ref, fd_ref, fr_ref, fi_ref, b_ref, o_ref, *, m):
    def conv(u, order):
        kh = _dot(fr_ref[...], kk_ref[order].astype(BF16))
        t = _dot(fd_ref[...], u.astype(BF16))
        tr, ti, kr, ki = t[:m], t[m:], kh[:m], kh[m:]
        y = jnp.concatenate([tr * kr - ti * ki, tr * ki + ti * kr], axis=0).astype(BF16)
        return _dot(fi_ref[...], y)

    v = v_ref[0].astype(F32)
    z1 = x1_ref[0].astype(F32) * (conv(v, 0) + v * b_ref[0:1, :])
    o_ref[0] = (x2_ref[0].astype(F32) * (conv(z1, 1) + z1 * b_ref[1:2, :])).astype(o_ref.dtype)


def _hyena_ctx(v, x1, x2, kk, hy_bias):
    nb, lc, c = v.shape
    p, m = nb // 2, 2 * lc
    fd, fr, fi = (jnp.asarray(t, F32).astype(BF16) for t in _ctx_dft_tables(lc))
    pair = lambda a: a.reshape(p, m, c)
    blk = pl.BlockSpec((1, m, c), lambda q: (q, 0, 0))
    full = lambda a: pl.BlockSpec(a.shape, lambda q: (0,) * a.ndim)
    z = pl.pallas_call(
        functools.partial(_ctxconv_kernel, m=m),
        grid=(p,),
        in_specs=[blk, blk, blk, full(kk), full(fd), full(fr), full(fi), full(hy_bias)],
        out_specs=blk,
        out_shape=jax.ShapeDtypeStruct((p, m, c), BF16),
        compiler_params=_cp("parallel"),
        name="hy_ctx",
    )(pair(v), pair(x1), pair(x2), kk, fd, fr, fi, hy_bias)
    return z.reshape(nb, lc, c)


def _od_inproj_kernel(prev_ref, h_ref, next_ref, mod_ref, g_ref, w_ref, cw_ref, cb_ref,
                      gate_ref, xc_ref, *, d, r, n_lat_tiles, n_tiles):
    i = pl.program_id(1)
    seg_first = jnp.logical_or(i == 0, i == n_lat_tiles)
    seg_last = jnp.logical_or(i == n_lat_tiles - 1, i == n_tiles - 1)
    m = mod_ref[0]
    rows = jnp.concatenate([prev_ref[0, 0], h_ref[0], next_ref[0, 0]], axis=0)
    xn = _rms_mod(rows, g_ref[...], m[:, 0:d], m[:, d:2 * d]).astype(BF16)
    y = _dot(xn, w_ref[...])
    gate_ref[0] = y[SUBLANES:SUBLANES + TT, :r].astype(gate_ref.dtype)
    x = y[:, r:]
    row = lax.broadcasted_iota(jnp.int32, (x.shape[0], 1), 0)
    outside = jnp.logical_or(jnp.logical_and(row < SUBLANES, seg_first),
                             jnp.logical_and(row >= SUBLANES + TT, seg_last))
    x = jnp.where(outside, 0.0, x)
    xc_ref[0] = _conv_taps(x, cw_ref[...], cb_ref[...], RG_CONV_LEFT)


def _od_inproj(h, mods, rows, layer, g, w_in, conv_w, conv_b, *, n_lat_tiles):
    nb, s, d = h.shape
    r = w_in.shape[1] // 2
    n_tiles = s // TT
    n8, per = s // SUBLANES, TT // SUBLANES
    h8 = h.reshape(nb, n8, SUBLANES, d)
    ospec = pl.BlockSpec((1, TT, r), lambda b, i: (b, i, 0))
    return pl.pallas_call(
        functools.partial(_od_inproj_kernel, d=d, r=r, n_lat_tiles=n_lat_tiles, n_tiles=n_tiles),
        grid=(nb, n_tiles),
        in_specs=[
            pl.BlockSpec((1, 1, SUBLANES, d), lambda b, i: (b, jnp.maximum(i * per - 1, 0), 0, 0)),
            pl.BlockSpec((1, TT, d), lambda b, i: (b, i, 0)),
            pl.BlockSpec((1, 1, SUBLANES, d),
                         lambda b, i: (b, jnp.minimum((i + 1) * per, n8 - 1), 0, 0)),
            _mod_spec(layer, rows, nb, n_lat_tiles, mods.shape[-1]),
            pl.BlockSpec((1, d), lambda b, i: (0, 0)),
            pl.BlockSpec(w_in.shape, lambda b, i: (0, 0)),
            pl.BlockSpec(conv_w.shape, lambda b, i: (0, 0)),
            pl.BlockSpec((1, r), lambda b, i: (0, 0)),
        ],
        out_specs=[ospec, ospec],
        out_shape=[jax.ShapeDtypeStruct((nb, s, r), BF16), jax.ShapeDtypeStruct((nb, s, r), F32)],
        compiler_params=_cp("parallel", "parallel"),
        name="od_inproj",
    )(h8, h, h8, mods, g, w_in, conv_w, conv_b.reshape(1, r))


def _rglru_kernel(x_ref, w_ref, b_ref, lam_ref, o_ref, a_ref, bb_ref, carry_ref, *, r, windows):
    dr = pl.program_id(1)
    i = pl.program_id(2)
    part = TT // RG_PARTS
    nl = -lam_ref[0]
    softplus = jnp.maximum(nl, 0.0) + jnp.log1p(jnp.exp(-jnp.abs(nl)))
    neg_rate = (0.5 * RG_C) * softplus
    exp2_rate = (-0.5 * RG_C * math.log2(math.e)) * softplus

    def gates(r0):
        rs = slice(r0, r0 + part)
        xc = x_ref[0, rs, :]
        xb = xc.astype(BF16)
        half_x = 0.5 * xc
        for c0, c1, k0, k1 in windows:
            xk = xb[:, k0:k1]
            tr = jnp.tanh(_dot(xk, w_ref[0, k0:k1, c0:c1]) + b_ref[0, :, c0:c1]) + 1.0
            ti = jnp.tanh(_dot(xk, w_ref[0, k0:k1, r + c0:r + c1]) + b_ref[0, :, r + c0:r + c1]) + 1.0
            a = jnp.exp2(exp2_rate[:, c0:c1] * tr)
            a_ref[rs, c0:c1] = a
            e = jnp.tanh(neg_rate[:, c0:c1] * tr) * (a * a + 1.0)
            root = jnp.where(e > 0.0, e * lax.rsqrt(e), 0.0)
            bb_ref[rs, c0:c1] = root * (ti * half_x[:, c0:c1])

    def scan_rows(rows, h):
        for row in rows:
            h = a_ref[row:row + 1, :] * h + bb_ref[row:row + 1, :]
            o_ref[0, 0, row:row + 1, :] = h
        return h

    def scan_loop(r0, reverse, h):
        n_blk = part // SUBLANES

        def block(g, h):
            base = pl.multiple_of(r0 + (n_blk - 1 - g if reverse else g) * SUBLANES, SUBLANES)
            for k in range(SUBLANES):
                row = base + (SUBLANES - 1 - k if reverse else k)
                h = a_ref[pl.ds(row, 1), :] * h + bb_ref[pl.ds(row, 1), :]
                o_ref[0, 0, pl.ds(row, 1), :] = h
            return h

        return lax.fori_loop(0, n_blk, block, h)

    @pl.when(i == 0)
    def _():
        carry_ref[...] = jnp.zeros_like(carry_ref)

    def run(reverse):
        starts = [k * part for k in (reversed(range(RG_PARTS)) if reverse else range(RG_PARTS))]
        gates(starts[0])
        h = carry_ref[0:1, :]
        for k, r0 in enumerate(starts[:-1]):
            gates(starts[k + 1])
            rows = range(r0 + part - 1, r0 - 1, -1) if reverse else range(r0, r0 + part)
            h = scan_rows(rows, h)
        carry_ref[0:1, :] = scan_loop(starts[-1], reverse, h)

    pl.when(dr == 0)(lambda: run(False))
    pl.when(dr == 1)(lambda: run(True))


def _scan_tile(dr, i, n_lat_tiles, n_tiles):
    fwd = jnp.where(i == 0, n_lat_tiles, i - 1)
    bwd = jnp.where(i == 0, n_lat_tiles, n_lat_tiles - i)
    return jnp.where(dr == 0, fwd, bwd)


def _gate_windows(r, bs):
    out = []
    for c0 in range(0, r, 2 * LANES):
        c1 = min(c0 + 2 * LANES, r)
        k0 = (c0 // bs) * bs // LANES * LANES
        k1 = min(-(-(((c1 - 1) // bs + 1) * bs) // LANES) * LANES, r)
        out.append((c0, c1, k0, k1))
    return tuple(out)


def _rglru(xc, wcat, bcat, lam, *, n_lat_tiles, block_size):
    nb, s, r = xc.shape
    n_tiles = s // TT
    assert n_tiles == n_lat_tiles + 1
    tile = lambda d, i: _scan_tile(d, i, n_lat_tiles, n_tiles)
    return pl.pallas_call(
        functools.partial(_rglru_kernel, r=r, windows=_gate_windows(r, block_size)),
        grid=(nb, 2, n_tiles),
        in_specs=[
            pl.BlockSpec((1, TT, r), lambda b, d, i: (b, tile(d, i), 0)),
            pl.BlockSpec((1, r, 2 * r), lambda b, d, i: (d, 0, 0)),
            pl.BlockSpec((1, 1, 2 * r), lambda b, d, i: (d, 0, 0)),
            pl.BlockSpec((1, 1, r), lambda b, d, i: (d, 0, 0)),
        ],
        out_specs=pl.BlockSpec((1, 1, TT, r), lambda b, d, i: (d, b, tile(d, i), 0)),
        out_shape=jax.ShapeDtypeStruct((2, nb, s, r), F32),
        scratch_shapes=[pltpu.VMEM((TT, r), F32), pltpu.VMEM((TT, r), F32),
                        pltpu.VMEM((SUBLANES, r), F32)],
        compiler_params=_cp("parallel", "parallel", "arbitrary"),
        name="rglru",
    )(xc, wcat, bcat, lam.reshape(2, 1, r))


def _block_diag(w):
    n, bs, _ = w.shape
    eye = jnp.eye(n, dtype=w.dtype)
    return (eye[:, None, :, None] * w[:, :, None, :]).reshape(n * bs, n * bs)


def _gelu_tanh(x):
    return 0.5 * x * (1.0 + jnp.tanh(math.sqrt(2.0 / math.pi) * (x + 0.044715 * (x * x * x))))


def _post_kernel(*refs, kind, final, d, n_lat_tiles):
    if kind == "even":
        (hl_ref, hc_ref, x2_ref, wl_ref, zc_ref, ol_ref, oc_ref, mod_ref, g2_ref, wo_ref, w1_ref,
         w2_ref) = refs[:12]
        rest = refs[12:]
        is_ctx = pl.program_id(1) >= n_lat_tiles
        z_lat = (x2_ref[0].astype(F32) * wl_ref[0].astype(F32)).astype(BF16)
        z = jnp.where(is_ctx, zc_ref[0], z_lat)
        o = jnp.where(is_ctx, oc_ref[0], ol_ref[0])
        half = z.shape[-1]
        y = _dot(z, wo_ref[:half, :]) + _dot(o, wo_ref[half:, :])
    else:
        hl_ref, hc_ref, hd_ref, gate_ref, mod_ref, g2_ref, wo_ref, w1_ref, w2_ref = refs[:9]
        rest = refs[9:]
        mix = (hd_ref[0, 0] + hd_ref[1, 0]) * _gelu_tanh(gate_ref[0].astype(F32))
        y = _dot(mix.astype(BF16), wo_ref[...])
    out_ref = rest[-1]
    m = mod_ref[0]
    h1 = _stream_tile(hl_ref, hc_ref, n_lat_tiles) + m[:, 2 * d:3 * d] * y
    xn = _rms_mod(h1, g2_ref[...], m[:, 3 * d:4 * d], m[:, 4 * d:5 * d]).astype(BF16)
    acc = jnp.zeros_like(h1)
    dff = w1_ref.shape[1]
    for c0 in range(0, dff, FF_CHUNK):
        a = jnp.maximum(_dot(xn, w1_ref[:, c0:c0 + FF_CHUNK]), 0.0)
        acc = acc + _dot((a * a).astype(BF16), w2_ref[c0:c0 + FF_CHUNK, :])
    h2 = h1 + m[:, 5 * d:6 * d] * acc
    if final:
        fg_ref = rest[0]
        h2 = h2 * lax.rsqrt(jnp.mean(h2 * h2, axis=-1, keepdims=True) + EPS) * fg_ref[...]
    out_ref[0] = h2


def _post(kind, h, mix_args, mods, rows, layer, g2, w_out, w1, w2, final_g, *, n_lat_tiles, final):
    h_specs, h_args = _stream_specs(h, n_lat_tiles)
    nb, _, d = h_args[0].shape
    n_tiles = n_lat_tiles if final else n_lat_tiles + 1
    tok = lambda w: pl.BlockSpec((1, TT, w), lambda b, i: (b, i, 0))
    full = lambda a: pl.BlockSpec(a.shape, lambda b, i: (0,) * a.ndim)
    if kind == "even":
        lat = lambda w: pl.BlockSpec((1, TT, w), lambda b, i: (b, jnp.minimum(i, n_lat_tiles - 1), 0))
        ctx = lambda w: pl.BlockSpec((1, TT, w), lambda b, i: (b, 0, 0))
        x2_lat, w_lat, z_ctx, o_lat, o_ctx = mix_args
        mix_specs = [lat(x2_lat.shape[-1]), lat(w_lat.shape[-1]), ctx(z_ctx.shape[-1]),
                     lat(o_lat.shape[-1]), ctx(o_ctx.shape[-1])]
    else:
        hd, gate = mix_args
        r = gate.shape[-1]
        mix_specs = [pl.BlockSpec((2, 1, TT, r), lambda b, i: (0, b, i, 0)), tok(r)]
    in_specs = h_specs + mix_specs + [
        _mod_spec(layer, rows, nb, n_lat_tiles, mods.shape[-1]),
        pl.BlockSpec((1, d), lambda b, i: (0, 0)), full(w_out), full(w1), full(w2)]
    args = [*h_args, *mix_args, mods, g2, w_out, w1, w2]
    if final:
        in_specs.append(pl.BlockSpec((1, d), lambda b, i: (0, 0)))
        args.append(final_g)
    return pl.pallas_call(
        functools.partial(_post_kernel, kind=kind, final=final, d=d, n_lat_tiles=n_lat_tiles),
        grid=(nb, n_tiles),
        in_specs=in_specs,
        out_specs=tok(d),
        out_shape=jax.ShapeDtypeStruct((nb, n_tiles * TT, d), F32),
        compiler_params=_cp("parallel", "parallel"),
        name="post_" + kind,
    )(*args)


@functools.lru_cache(maxsize=None)
def _rope_tables(l, lc, head_dim):
    axis = head_dim // 2
    freqs = ROPE_BASE ** (-np.arange(0, axis, 2, dtype=np.float64) / axis)
    freqs = freqs.astype(np.float32).astype(np.float64)
    t = np.arange(l)
    ang_r = (t // GRID_W)[:, None] * freqs
    ang_c = (t % GRID_W)[:, None] * freqs
    cos = np.concatenate([np.cos(ang_r)] * 2 + [np.cos(ang_c)] * 2, axis=-1)
    sin = np.concatenate([-np.sin(ang_r), np.sin(ang_r), -np.sin(ang_c), np.sin(ang_c)], axis=-1)
    cos = np.concatenate([cos, np.ones((lc, head_dim))], axis=0)
    sin = np.concatenate([sin, np.zeros((lc, head_dim))], axis=0)
    rep = LANES // head_dim
    return np.tile(cos, (1, rep)), np.tile(sin, (1, rep))


def kernel(x, c, ctx, c_ctx, ada_w, ada_b, norm1_g, norm2_g, mlp_w1, mlp_w2, final_g, ev_w_in, ev_w_out, hy_short_w, hy_short_b, hy_f_w1, hy_f_b1, hy_f_w2, hy_f_b2, hy_f_w3, hy_f_b3, hy_f_freq, hy_f_decay, hy_bias, df_lq1, df_lk1, df_lq2, df_lk2, df_subln_g, od_w_in, od_w_out, rg_conv_w, rg_conv_b, rg_wa, rg_ba, rg_wx, rg_bx, rg_lam):
    nb, l, d = x.shape
    lc = ctx.shape[1]
    depth = ada_w.shape[0]
    hy = hy_bias.shape[-1]
    head_dim = df_lq1.shape[-1]
    qk = (ev_w_in.shape[-1] - 3 * hy) // 3
    r = rg_lam.shape[-1]
    assert l % TT == 0 and lc == TT and l % GRID_W == 0
    n_lat_tiles = l // TT

    rows = -(-(nb + 1) // SUBLANES) * SUBLANES
    cond = jnp.zeros((rows, d), F32).at[:nb].set(c).at[nb].set(c_ctx)
    mods = _ada_mods(cond, ada_w, ada_b).reshape(depth * rows, 1, ada_w.shape[-1])

    cos_t, sin_t = (jnp.asarray(t, F32) for t in _rope_tables(l, lc, head_dim))
    h = (x, ctx)
    w1_bf, w2_bf = mlp_w1.astype(BF16), mlp_w2.astype(BF16)

    for i in range(depth):
        j = i // 2
        final = i == depth - 1
        g1, g2 = norm1_g[i].reshape(1, d), norm2_g[i].reshape(1, d)
        if i % 2 == 0:
            lam_init = 0.8 - 0.6 * math.exp(-0.3 * i)
            u, qt, k, vt = _ev_inproj(h, mods, rows, i, g1, ev_w_in[j].astype(BF16), cos_t, sin_t,
                                      n_lat_tiles=n_lat_tiles, hy3=3 * hy, qk=qk, head_dim=head_dim,
                                      tk=_attn_chunk(l + lc))
            lam_vecs = [a[j].reshape(1, -1) for a in (df_lq1, df_lk1, df_lq2, df_lk2)]
            o_lat, o_ctx = _diff_attention(qt, k, vt, lam_vecs, df_subln_g[j].reshape(1, -1),
                                           n_lat_tiles=n_lat_tiles, lam_init=lam_init,
                                           head_dim=head_dim)
            fparams = (hy_f_w1[j], hy_f_b1[j], hy_f_w2[j], hy_f_b2[j], hy_f_w3[j], hy_f_b3[j],
                       hy_f_freq[j], hy_f_decay[j])
            vl, x1l, x2l = _shortconv(u, hy_short_w[j], hy_short_b[j], tile0=0,
                                      n_seg_tiles=n_lat_tiles)
            vc, x1c, x2c = _shortconv(u, hy_short_w[j], hy_short_b[j], tile0=n_lat_tiles,
                                      n_seg_tiles=lc // TT)
            w_lat = _hyena_long(vl, x1l, _hyena_filter(l, math.isqrt(2 * l), *fparams), hy_bias[j])
            z_ctx = _hyena_ctx(vc, x1c, x2c, _hyena_filter(lc, 1, *fparams), hy_bias[j])
            h = _post("even", h, (x2l, w_lat, z_ctx, o_lat, o_ctx), mods, rows, i, g2, ev_w_out[j].astype(BF16),
                      w1_bf[i], w2_bf[i], final_g.reshape(1, d), n_lat_tiles=n_lat_tiles,
                      final=final)
        else:
            gate, xc = _od_inproj(h, mods, rows, i, g1, od_w_in[j].astype(BF16), rg_conv_w[j],
                                  rg_conv_b[j], n_lat_tiles=n_lat_tiles)
            wcat = (0.5 * jnp.stack(
                [jnp.concatenate([_block_diag(rg_wa[j, dd]), _block_diag(rg_wx[j, dd])], axis=1)
                 for dd in range(2)])).astype(BF16)
            bcat = 0.5 * jnp.concatenate([rg_ba[j], rg_bx[j]], axis=-1).reshape(2, 1, 2 * r)
            hd = _rglru(xc, wcat, bcat, rg_lam[j], n_lat_tiles=n_lat_tiles,
                        block_size=rg_wa.shape[-1])
            h = _post("odd", h, (hd, gate), mods, rows, i, g2, od_w_out[j].astype(BF16),
                      w1_bf[i], w2_bf[i], final_g.reshape(1, d), n_lat_tiles=n_lat_tiles,
                      final=final)
    return h
```

```python
import functools
import math

import numpy as np
import jax
import jax.numpy as jnp
from jax import lax
from jax.experimental import pallas as pl
from jax.experimental.pallas import tpu as pltpu

F32 = jnp.float32
BF16 = jnp.bfloat16
HIGHEST = lax.Precision.HIGHEST

EPS = 1e-6
GRID_W = 64
ROPE_BASE = 10000.0
N_BANDS = 16
RG_C = 8.0
RG_CONV_LEFT = 2
HY_SHORT_LEFT = 1

TT = 256
LANES = 128
SUBLANES = 8
VMEM_LIMIT = 56 * 1024 * 1024
FF_CHUNK = 1024
FFT_N2_TILE = 16
FFT_K1_TILE = 16
ADA_K_TILE = 256
ATTN_SUB_Q = 512
RG_PARTS = 2
FILT_COL_BLOCKS = 4
VT_ROWS = LANES + 16


def _cp(*sem):
    return pltpu.CompilerParams(dimension_semantics=sem, vmem_limit_bytes=VMEM_LIMIT)


def _dot(a, b, **kw):
    return jnp.dot(a, b, preferred_element_type=F32, **kw)


def _sigmoid(x):
    return 1.0 / (1.0 + jnp.exp(-x))


def _rms_mod(x, g, shift, scale):
    y = x * lax.rsqrt(jnp.mean(x * x, axis=-1, keepdims=True) + EPS)
    return (y * g) * (1.0 + scale) + shift


def _ada_kernel(c_ref, w_ref, b_ref, o_ref):
    k = pl.program_id(1)
    c = c_ref[k]
    part = _dot((c * _sigmoid(c)).astype(BF16), w_ref[0].astype(BF16))

    @pl.when(k == 0)
    def _():
        o_ref[0] = part + b_ref[0]

    @pl.when(k > 0)
    def _():
        o_ref[0] += part


def _ada_mods(cond, ada_w, ada_b):
    depth, d, n = ada_w.shape
    rows = cond.shape[0]
    tk = min(d, ADA_K_TILE)
    cond_k = cond.reshape(rows, d // tk, tk).transpose(1, 0, 2)
    return pl.pallas_call(
        _ada_kernel,
        grid=(depth, d // tk),
        in_specs=[
            pl.BlockSpec(cond_k.shape, lambda l, k: (0, 0, 0)),
            pl.BlockSpec((1, tk, n), lambda l, k: (l, k, 0)),
            pl.BlockSpec((1, 1, n), lambda l, k: (l, 0, 0)),
        ],
        out_specs=pl.BlockSpec((1, rows, n), lambda l, k: (l, 0, 0)),
        out_shape=jax.ShapeDtypeStruct((depth, rows, n), F32),
        compiler_params=_cp("parallel", "arbitrary"),
        name="ada_mods",
    )(cond_k, ada_w, ada_b.reshape(depth, 1, n))


def _stream_specs(h, n_lat_tiles):
    separate = isinstance(h, tuple)
    lat_arr, ctx_arr = h if separate else (h, h)
    ctx_blk = 0 if separate else n_lat_tiles
    d = lat_arr.shape[-1]
    lat = pl.BlockSpec((1, TT, d), lambda b, i: (b, jnp.minimum(i, n_lat_tiles - 1), 0))
    ctx = pl.BlockSpec((1, TT, d), lambda b, i: (b, ctx_blk, 0))
    return [lat, ctx], [lat_arr, ctx_arr]


def _stream_tile(hl_ref, hc_ref, n_lat_tiles):
    return jnp.where(pl.program_id(1) >= n_lat_tiles, hc_ref[0], hl_ref[0])


def _mod_spec(layer, rows, nb, n_lat_tiles, n6):
    def imap(b, i):
        return (layer * rows + jnp.where(i >= n_lat_tiles, nb, b), 0, 0)
    return pl.BlockSpec((1, 1, n6), imap)


def _ev_inproj_kernel(hl_ref, hc_ref, mod_ref, g_ref, w_ref, cos_ref, sin_ref,
                      u_ref, qt_ref, k_ref, vt_ref, *, d, hy3, qk, qscale, n_lat_tiles):
    m = mod_ref[0]
    x = _stream_tile(hl_ref, hc_ref, n_lat_tiles)
    xn = _rms_mod(x, g_ref[...], m[:, 0:d], m[:, d:2 * d]).astype(BF16)
    cos = cos_ref[...]
    sin = sin_ref[...]
    lane = lax.broadcasted_iota(jnp.int32, cos.shape, 1)
    first = (lane % 32) < 16

    def rope(z):
        sw = jnp.where(first, pltpu.roll(z, LANES - 16, 1), pltpu.roll(z, 16, 1))
        return z * cos + sw * sin

    extra = vt_ref.shape[3] - LANES
    ones_row = jnp.where(lax.broadcasted_iota(jnp.int32, (extra, cos.shape[0]), 0) == 0,
                         1.0, 0.0).astype(BF16)
    yq = _dot(xn, w_ref[:, hy3:hy3 + qk])
    for c in range(qk // LANES):
        qt_ref[0, c] = (rope(yq[:, c * LANES:(c + 1) * LANES]) * qscale).T.astype(BF16)
    yk = _dot(xn, w_ref[:, hy3 + qk:hy3 + 2 * qk])
    for c in range(qk // LANES):
        k_ref[0, :, c * LANES:(c + 1) * LANES] = rope(yk[:, c * LANES:(c + 1) * LANES]).astype(BF16)
    yv = _dot(xn, w_ref[:, hy3 + 2 * qk:])
    for c in range(qk // LANES):
        vt_ref[0, c, 0, 0:LANES, :] = yv[:, c * LANES:(c + 1) * LANES].T.astype(BF16)
        vt_ref[0, c, 0, LANES:, :] = ones_row
    u_ref[0] = _dot(xn, w_ref[:, :hy3]).astype(u_ref.dtype)


def _ev_inproj(h, mods, rows, layer, g, w_in, cos_t, sin_t, *, n_lat_tiles, hy3, qk, head_dim, tk):
    h_specs, h_args = _stream_specs(h, n_lat_tiles)
    nb, _, d = h_args[0].shape
    s = (n_lat_tiles + 1) * TT
    n_in = w_in.shape[1]
    assert 2 * head_dim == LANES
    heads, per = qk // LANES, tk // TT
    kern = functools.partial(_ev_inproj_kernel, d=d, hy3=hy3, qk=qk, n_lat_tiles=n_lat_tiles,
                             qscale=head_dim ** -0.5 * math.log2(math.e))
    return pl.pallas_call(
        kern,
        grid=(nb, s // TT),
        in_specs=h_specs + [
            _mod_spec(layer, rows, nb, n_lat_tiles, mods.shape[-1]),
            pl.BlockSpec((1, d), lambda b, i: (0, 0)),
            pl.BlockSpec((d, n_in), lambda b, i: (0, 0)),
            pl.BlockSpec((TT, LANES), lambda b, i: (i, 0)),
            pl.BlockSpec((TT, LANES), lambda b, i: (i, 0)),
        ],
        out_specs=[
            pl.BlockSpec((1, TT, hy3), lambda b, i: (b, i, 0)),
            pl.BlockSpec((1, heads, LANES, TT), lambda b, i: (b, 0, 0, i)),
            pl.BlockSpec((1, TT, qk), lambda b, i: (b, i, 0)),
            pl.BlockSpec((1, heads, 1, VT_ROWS, TT), lambda b, i: (b, 0, i // per, 0, i % per)),
        ],
        out_shape=[
            jax.ShapeDtypeStruct((nb, s, hy3), BF16),
            jax.ShapeDtypeStruct((nb, heads, LANES, s), BF16),
            jax.ShapeDtypeStruct((nb, s, qk), BF16),
            jax.ShapeDtypeStruct((nb, heads, s // tk, VT_ROWS, tk), BF16),
        ],
        compiler_params=_cp("parallel", "parallel"),
        name="ev_inproj",
    )(*h_args, mods, g, w_in, cos_t, sin_t)


def _split_maps(qt, half):
    row = lax.broadcasted_iota(jnp.int32, qt.shape, 0)
    zero = jnp.zeros_like(qt)
    return jnp.where(row < half, qt, zero), jnp.where(row >= half, qt, zero)


def _attn_out(a0, a1, lam_refs, sg_ref, lam_init, dv):
    lq1_ref, lk1_ref, lq2_ref, lk2_ref = lam_refs
    lam = (jnp.exp(jnp.sum(lq1_ref[...] * lk1_ref[...], axis=-1, keepdims=True))
           - jnp.exp(jnp.sum(lq2_ref[...] * lk2_ref[...], axis=-1, keepdims=True)) + lam_init)
    o = (a0[:dv] / a0[dv:dv + 1] - lam * (a1[:dv] / a1[dv:dv + 1])).T
    on = o * lax.rsqrt(jnp.mean(o * o, axis=-1, keepdims=True) + EPS)
    return on * sg_ref[...] * (1.0 - lam_init)


def _col_max8(sc):
    part = sc[0:SUBLANES]
    for g in range(1, sc.shape[0] // SUBLANES):
        part = jnp.maximum(part, sc[g * SUBLANES:(g + 1) * SUBLANES])
    return part


def _attn_kernel(qt_ref, qnt_ref, k_ref, vt_ref, lq1_ref, lk1_ref, lq2_ref, lk2_ref, sg_ref, o_ref,
                 s_ref, mp_ref, p_ref, acc_ref, m_ref, alpha_ref, *, n_chunks, tk, lam_init, half):
    i = pl.program_id(2)
    dv = 2 * half
    lam_refs = (lq1_ref, lk1_ref, lq2_ref, lk2_ref)
    tq = qnt_ref.shape[-1]
    q_sub = (_split_maps(qt_ref[0, 0, :, 0:tq], half), _split_maps(qt_ref[0, 0, :, tq:2 * tq], half))
    q_next = _split_maps(qnt_ref[0, 0], half)

    def put_scores(slot, qts, c):
        kc = k_ref[0, pl.ds(pl.multiple_of(c * tk, tk), tk), :]
        for j in range(2):
            sc = _dot(kc, qts[j])
            s_ref[slot, j] = sc
            mp_ref[slot, j] = _col_max8(sc)

    def put_probs(slot, sub, first):
        for j in range(2):
            col_max = jnp.max(mp_ref[slot, j], axis=0, keepdims=True)
            if first:
                m_new = col_max
            else:
                m_old = m_ref[sub, j]
                m_new = jnp.maximum(m_old, col_max)
                alpha_ref[slot, j] = jnp.exp2(m_old - m_new)
            m_ref[sub, j] = m_new
            p_ref[slot, j] = jnp.exp2(s_ref[slot, j] - m_new).astype(BF16)

    def add_pv(slot, sub, c, first):
        for j in range(2):
            pv = _dot(vt_ref[0, 0, c], p_ref[slot, j])
            acc_ref[sub, j] = pv if first else alpha_ref[slot, j] * acc_ref[sub, j] + pv

    def sub_tile(sub, parity, qts, qts_after, sub_after):
        slot = lambda c: (c + parity) % 2
        put_scores(slot(0), qts, 2)
        put_probs(slot(1), sub, False)
        add_pv(slot(0), sub, 0, True)
        n_uniform = n_chunks - 3

        unroll = 2

        def group(u, carry):
            c = 1 + unroll * u
            for k in range(unroll):
                put_scores(slot(1 + k), qts, c + k + 2)
                put_probs(slot(k), sub, False)
                add_pv(slot(1 + k), sub, c + k, False)
            return carry

        lax.fori_loop(0, n_uniform // unroll, group, 0)
        c = n_chunks - 2
        put_scores(slot(c), qts_after, 0)
        put_probs(slot(c + 1), sub, False)
        add_pv(slot(c), sub, c, False)
        c = n_chunks - 1
        put_scores(slot(c), qts_after, 1)
        put_probs(slot(c + 1), sub_after, True)
        add_pv(slot(c), sub, c, False)
        o_ref[0, sub * tq:(sub + 1) * tq, :] = _attn_out(
            acc_ref[sub, 0], acc_ref[sub, 1], lam_refs, sg_ref, lam_init, dv).astype(o_ref.dtype)

    @pl.when(i == 0)
    def _():
        put_scores(0, q_sub[0], 0)
        put_probs(0, 0, True)
        put_scores(1, q_sub[0], 1)

    sub_tile(0, 0, q_sub[0], q_sub[1], 1)
    sub_tile(1, 1, q_sub[1], q_next, 0)


def _ctx_attn_kernel(qt_ref, k_ref, vt_ref, lq1_ref, lk1_ref, lq2_ref, lk2_ref, sg_ref, o_ref,
                     *, lam_init, half):
    acc = []
    for qtj in _split_maps(qt_ref[0, 0], half):
        sc = _dot(k_ref[0], qtj)
        col_max = jnp.max(_col_max8(sc), axis=0, keepdims=True)
        acc.append(_dot(vt_ref[0, 0, 0], jnp.exp2(sc - col_max).astype(BF16)))
    o_ref[0] = _attn_out(acc[0], acc[1], (lq1_ref, lk1_ref, lq2_ref, lk2_ref), sg_ref, lam_init,
                         2 * half).astype(o_ref.dtype)


def _attn_chunk(s):
    return next(t for t in (3 * TT, TT) if s % t == 0 and (s // t) % 2 == 1 and s // t >= 5)


def _diff_attention(qt, k, vt, lam_vecs, subln_g, *, n_lat_tiles, lam_init, head_dim):
    nb, s, qk = k.shape
    dv = 2 * head_dim
    heads = qk // dv
    l = n_lat_tiles * TT
    n_chunks, tk = vt.shape[2], vt.shape[4]
    tq = ATTN_SUB_Q if l % (2 * ATTN_SUB_Q) == 0 else TT
    assert l % (2 * tq) == 0 and n_chunks % 2 == 1 and n_chunks >= 5 and s - l == TT
    n_steps = l // (2 * tq)
    vec = lambda n: pl.BlockSpec((1, n), lambda b, h, i: (0, 0))
    vecs = [vec(head_dim)] * 4 + [vec(dv)]
    o_lat = pl.pallas_call(
        functools.partial(_attn_kernel, n_chunks=n_chunks, tk=tk, lam_init=lam_init, half=head_dim),
        grid=(nb, heads, n_steps),
        in_specs=[
            pl.BlockSpec((1, 1, dv, 2 * tq), lambda b, h, i: (b, h, 0, i)),
            pl.BlockSpec((1, 1, dv, tq),
                         lambda b, h, i: (b, h, 0, jnp.minimum(2 * i + 2, 2 * n_steps - 1))),
            pl.BlockSpec((1, s, dv), lambda b, h, i: (b, 0, h)),
            pl.BlockSpec((1, 1, n_chunks, VT_ROWS, tk), lambda b, h, i: (b, h, 0, 0, 0)),
        ] + vecs,
        out_specs=pl.BlockSpec((1, 2 * tq, dv), lambda b, h, i: (b, i, h)),
        out_shape=jax.ShapeDtypeStruct((nb, l, qk), BF16),
        scratch_shapes=[pltpu.VMEM((2, 2, tk, tq), F32), pltpu.VMEM((2, 2, SUBLANES, tq), F32),
                        pltpu.VMEM((2, 2, tk, tq), BF16), pltpu.VMEM((2, 2, VT_ROWS, tq), F32),
                        pltpu.VMEM((2, 2, 1, tq), F32), pltpu.VMEM((2, 2, 1, tq), F32)],
        compiler_params=_cp("parallel", "parallel", "arbitrary"),
        name="diff_attn",
    )(qt, qt, k, vt, *lam_vecs, subln_g)
    per = tk // TT
    vec2 = lambda n: pl.BlockSpec((1, n), lambda b, h: (0, 0))
    o_ctx = pl.pallas_call(
        functools.partial(_ctx_attn_kernel, lam_init=lam_init, half=head_dim),
        grid=(nb, heads),
        in_specs=[pl.BlockSpec((1, 1, dv, TT), lambda b, h: (b, h, 0, n_lat_tiles)),
                  pl.BlockSpec((1, TT, dv), lambda b, h: (b, n_lat_tiles, h)),
                  pl.BlockSpec((1, 1, 1, VT_ROWS, TT),
                               lambda b, h: (b, h, n_lat_tiles // per, 0, n_lat_tiles % per)),
                  ] + [vec2(head_dim)] * 4 + [vec2(dv)],
        out_specs=pl.BlockSpec((1, TT, dv), lambda b, h: (b, 0, h)),
        out_shape=jax.ShapeDtypeStruct((nb, TT, qk), BF16),
        compiler_params=_cp("parallel", "parallel"),
        name="ctx_attn",
    )(qt, k, vt, *lam_vecs, subln_g)
    return o_lat, o_ctx


def _halo_rows(dtype):
    return SUBLANES * 4 // jnp.dtype(dtype).itemsize


def _halo_fill(xp_ref, prev_ref, x_ref, next_ref, has_prev, has_next):
    hr = prev_ref.shape[2]
    zero = jnp.zeros((hr, xp_ref.shape[-1]), F32)
    xp_ref[hr:hr + TT, :] = x_ref[0].astype(F32)
    xp_ref[0:hr, :] = jnp.where(has_prev, prev_ref[0, 0].astype(F32), zero)
    xp_ref[hr + TT:2 * hr + TT, :] = jnp.where(has_next, next_ref[0, 0].astype(F32), zero)


def _conv_taps(xp, w, bias, left):
    rows = xp.shape[0]
    hr = (rows - TT) // 2
    before = None
    for j in range(left):
        z = w[j:j + 1, :] * xp
        before = pltpu.roll(z if before is None else before + z, 1, 0)
    after = None
    for j in range(w.shape[0] - 1, left, -1):
        z = w[j:j + 1, :] * xp
        after = pltpu.roll(z if after is None else after + z, rows - 1, 0)
    acc = w[left:left + 1, :] * xp + bias
    for part in (before, after):
        if part is not None:
            acc = acc + part
    return acc[hr:hr + TT, :]


def _shortconv_kernel(prev_ref, x_ref, next_ref, w_ref, b_ref, v_ref, x1_ref, x2_ref, xp_ref,
                      *, n_seg_tiles, hy):
    i = pl.program_id(1)
    _halo_fill(xp_ref, prev_ref, x_ref, next_ref, i > 0, i < n_seg_tiles - 1)
    y = _conv_taps(xp_ref[...], w_ref[...], b_ref[...], HY_SHORT_LEFT)
    v_ref[0] = y[:, :hy].astype(v_ref.dtype)
    x1_ref[0] = y[:, hy:2 * hy].astype(x1_ref.dtype)
    x2_ref[0] = y[:, 2 * hy:].astype(x2_ref.dtype)


def _shortconv(u, w, b, *, tile0, n_seg_tiles):
    nb, s, hy3 = u.shape
    hy = hy3 // 3
    hr = _halo_rows(u.dtype)
    nh, per = s // hr, TT // hr
    uh = u.reshape(nb, nh, hr, hy3)
    prev = pl.BlockSpec((1, 1, hr, hy3),
                        lambda b, i: (b, jnp.maximum((tile0 + i) * per - 1, 0), 0, 0))
    nxt = pl.BlockSpec((1, 1, hr, hy3),
                       lambda b, i: (b, jnp.minimum((tile0 + i + 1) * per, nh - 1), 0, 0))
    out = jax.ShapeDtypeStruct((nb, n_seg_tiles * TT, hy), BF16)
    ospec = pl.BlockSpec((1, TT, hy), lambda b, i: (b, i, 0))
    return pl.pallas_call(
        functools.partial(_shortconv_kernel, n_seg_tiles=n_seg_tiles, hy=hy),
        grid=(nb, n_seg_tiles),
        in_specs=[
            prev,
            pl.BlockSpec((1, TT, hy3), lambda b, i: (b, tile0 + i, 0)),
            nxt,
            pl.BlockSpec(w.shape, lambda b, i: (0, 0)),
            pl.BlockSpec((1, hy3), lambda b, i: (0, 0)),
        ],
        out_specs=[ospec, ospec, ospec],
        out_shape=[out, out, out],
        scratch_shapes=[pltpu.VMEM((TT + 2 * hr, hy3), F32)],
        compiler_params=_cp("parallel", "parallel"),
        name="hy_shortconv",
    )(uh, u, uh, w, b.reshape(1, hy3))


def _filter_feats(lh, n_cols):
    n = np.arange(2 * lh).reshape(-1, n_cols).T.reshape(-1)
    lag = np.where(n < lh, n, 2 * lh - n).astype(np.float64)
    t = (lag / lh).astype(np.float32).astype(np.float64)
    bands = np.arange(1, N_BANDS + 1, dtype=np.float64)
    ang = 2.0 * math.pi * t[:, None] * bands
    feats = np.concatenate([t[:, None], np.cos(ang), np.sin(ang)], axis=-1)
    pad = (-(feats.shape[1] + 1)) % SUBLANES
    return np.concatenate([feats, np.zeros((2 * lh, pad)), n[:, None].astype(np.float64)], axis=-1)


def _filt_kernel(ft_ref, w1_ref, b1_ref, w2_ref, b2_ref, w3_ref, b3_ref, fr_ref, dec_ref, o_ref,
                 *, lh, c):
    ft = ft_ref[...]
    freq = fr_ref[...]
    h = jnp.sin(freq * (_dot(ft, w1_ref[...], precision=HIGHEST) + b1_ref[...]))
    h = jnp.sin(freq * (_dot(h, w2_ref[...], precision=HIGHEST) + b2_ref[...])).astype(BF16)
    tr = o_ref.shape[1]
    half = tr // 2
    for q in range(ft.shape[0] // tr):
        for dr in range(2):
            rs = slice(q * tr + dr * half, q * tr + (dr + 1) * half)
            t = ft[rs, 0:1]
            f = (_dot(h[rs], w3_ref[dr]) + b3_ref[dr]) * jnp.exp(-t * jnp.abs(dec_ref[dr]))
            if dr == 1:
                f = jnp.where(ft[rs, ft.shape[1] - 1:] == lh, 0.0, f)
            for o in range(2):
                o_ref[o, dr * half:(dr + 1) * half, q * c:(q + 1) * c] = f[:, o * c:(o + 1) * c]


def _hyena_filter(lh, n_cols, w1, b1, w2, b2, w3, b3, freq, decay):
    c = decay.shape[-1]
    feats = jnp.asarray(_filter_feats(lh, n_cols), F32)
    fe = feats.shape[1]
    w1p = jnp.pad(w1, ((0, fe - w1.shape[0]), (0, 0)))
    hid = w1.shape[1]
    tr = 2 * lh // n_cols
    cb = math.gcd(n_cols, FILT_COL_BLOCKS)
    by_dir = lambda a, lead: jnp.moveaxis(a.reshape(lead, 2, 2, c), 2, 0).reshape(2, lead, 2 * c)
    full = lambda a: pl.BlockSpec(a.shape, lambda i: (0,) * a.ndim)
    args = (w1p, b1.reshape(1, hid), w2, b2.reshape(1, hid), by_dir(w3, hid).astype(BF16),
            by_dir(b3, 1), freq.reshape(1, hid), by_dir(decay, 1))
    return pl.pallas_call(
        functools.partial(_filt_kernel, lh=lh, c=c),
        grid=(n_cols // cb,),
        in_specs=[pl.BlockSpec((cb * tr, fe), lambda i: (i, 0))] + [full(a) for a in args],
        out_specs=pl.BlockSpec((2, tr, cb * c), lambda i: (0, 0, i)),
        out_shape=jax.ShapeDtypeStruct((2, tr, n_cols * c), F32),
        compiler_params=_cp("parallel"),
        name="hy_filter",
    )(feats, *args)


@functools.lru_cache(maxsize=None)
def _dft_tables(n):
    nn = n * n
    h = n // 2
    k = np.arange(n)
    th = 2.0 * math.pi * np.outer(k, k) / n
    c, s = np.cos(th), np.sin(th)
    f1_data = np.block([[c[:, :h], s[:, :h]], [-s[:, :h], c[:, :h]]])
    f1_real = np.concatenate([c, -s], axis=0)
    idx = (k[None, None, :] * (k[:, None, None] + n * k[None, :, None])) % nn
    phi = 2.0 * math.pi * idx / nn
    cp, sp = np.cos(phi), np.sin(phi)
    g = np.concatenate([np.concatenate([cp, sp], axis=2), np.concatenate([-sp, cp], axis=2)], axis=1)
    hmat = np.transpose(g, (0, 2, 1)) / nn
    ci, si = c[:h, :], s[:h, :]
    f3 = np.zeros((n, 2 * n))
    f3[:h, 0::2], f3[:h, 1::2] = ci, -si
    f3[h:, 0::2], f3[h:, 1::2] = si, ci
    return f1_data, f1_real, g, hmat, f3


def _fft_s1_kernel(x_ref, f_ref, o_ref):
    o_ref[0] = _dot(f_ref[...], x_ref[0].astype(BF16)).astype(o_ref.dtype)


def _fft_s1(x, f1, n, c):
    p = x.shape[0]
    tc = FFT_N2_TILE * c
    return pl.pallas_call(
        _fft_s1_kernel,
        grid=(p, n * c // tc),
        in_specs=[pl.BlockSpec((1, n, tc), lambda q, j: (q, 0, j)),
                  pl.BlockSpec(f1.shape, lambda q, j: (0, 0))],
        out_specs=pl.BlockSpec((1, 2 * n, tc), lambda q, j: (q, 0, j)),
        out_shape=jax.ShapeDtypeStruct((p, 2 * n, n * c), BF16),
        compiler_params=_cp("parallel", "parallel"),
        name="fft_s1",
    )(x, f1)


def _fft_spec_kernel(a_ref, g_ref, o_ref, *, tk):
    for j in range(tk):
        x = jnp.concatenate([a_ref[0, 0, j], a_ref[0, 1, j]], axis=0)
        o_ref[0, j] = _dot(g_ref[j], x).astype(o_ref.dtype)


def _fft_mid_kernel(a_ref, g_ref, h_ref, kh_ref, o_ref, *, tk, n):
    ts = [_dot(g_ref[j], jnp.concatenate([a_ref[0, 0, j], a_ref[0, 1, j]], axis=0))
          for j in range(tk)]
    ys = []
    for j, t in enumerate(ts):
        tr, ti = t[:n], t[n:]
        kr, ki = kh_ref[0, j, :n].astype(F32), kh_ref[0, j, n:].astype(F32)
        ys.append(jnp.concatenate([tr * kr - ti * ki, tr * ki + ti * kr], axis=0).astype(BF16))
    for j, y in enumerate(ys):
        o_ref[0, j] = _dot(h_ref[j], y).astype(o_ref.dtype)


def _fft_spectrum(a, g, n, c):
    p = a.shape[0]
    tk = FFT_K1_TILE
    a5 = a.reshape(p, 2, n, n, c)
    return pl.pallas_call(
        functools.partial(_fft_spec_kernel, tk=tk),
        grid=(n // tk, p),
        in_specs=[pl.BlockSpec((1, 2, tk, n, c), lambda j, q: (q, 0, j, 0, 0)),
                  pl.BlockSpec((tk, 2 * n, 2 * n), lambda j, q: (j, 0, 0))],
        out_specs=pl.BlockSpec((1, tk, 2 * n, c), lambda j, q: (q, j, 0, 0)),
        out_shape=jax.ShapeDtypeStruct((p, n, 2 * n, c), BF16),
        compiler_params=_cp("parallel", "parallel"),
        name="fft_spectrum",
    )(a5, g)


def _fft_mid(a, g, hm, khat, order, n, c):
    p = a.shape[0]
    tk = FFT_K1_TILE
    a5 = a.reshape(p, 2, n, n, c)
    return pl.pallas_call(
        functools.partial(_fft_mid_kernel, tk=tk, n=n),
        grid=(n // tk, p),
        in_specs=[pl.BlockSpec((1, 2, tk, n, c), lambda j, q: (q, 0, j, 0, 0)),
                  pl.BlockSpec((tk, 2 * n, 2 * n), lambda j, q: (j, 0, 0)),
                  pl.BlockSpec((tk, 2 * n, 2 * n), lambda j, q: (j, 0, 0)),
                  pl.BlockSpec((1, tk, 2 * n, c), lambda j, q: (order, j, 0, 0))],
        out_specs=pl.BlockSpec((1, tk, 2 * n, c), lambda j, q: (q, j, 0, 0)),
        out_shape=jax.ShapeDtypeStruct((p, n, 2 * n, c), BF16),
        compiler_params=_cp("parallel", "parallel"),
        name="fft_mid",
    )(a5, g, hm, khat)


def _fft_s3_kernel(*refs, gated, chained):
    c_ref, f_ref, v_ref, b_ref = refs[:4]
    rest = list(refs[4:])
    y = _dot(f_ref[...], c_ref[0]) + v_ref[0].astype(F32) * b_ref[...]
    if gated:
        y = rest.pop(0)[0].astype(F32) * y
    f1_ref = rest.pop(0) if chained else None
    o_ref = rest.pop(0)
    z = y.astype(o_ref.dtype)
    o_ref[0] = z
    if chained:
        a_ref = rest.pop(0)
        a_ref[0] = _dot(f1_ref[...], z).astype(a_ref.dtype)


def _fft_s3(cm, f3, vin, xg, bias, n, c, f1_next=None):
    p = cm.shape[0]
    tc = FFT_N2_TILE * c
    c2 = cm.reshape(p, 2 * n, n * c)
    bias_t = jnp.tile(bias.reshape(1, c), (1, FFT_N2_TILE))
    blk = pl.BlockSpec((1, n, tc), lambda q, j: (q, 0, j))
    full = lambda a: pl.BlockSpec(a.shape, lambda q, j: (0, 0))
    in_specs = [pl.BlockSpec((1, 2 * n, tc), lambda q, j: (q, 0, j)), full(f3), blk,
                pl.BlockSpec((1, tc), lambda q, j: (0, 0))]
    args = [c2, f3, vin, bias_t]
    out_specs, out_shape = [blk], [jax.ShapeDtypeStruct((p, n, n * c), BF16)]
    if xg is not None:
        in_specs.append(blk)
        args.append(xg)
    if f1_next is not None:
        in_specs.append(full(f1_next))
        args.append(f1_next)
        out_specs.append(pl.BlockSpec((1, 2 * n, tc), lambda q, j: (q, 0, j)))
        out_shape.append(jax.ShapeDtypeStruct((p, 2 * n, n * c), BF16))
    return pl.pallas_call(
        functools.partial(_fft_s3_kernel, gated=xg is not None, chained=f1_next is not None),
        grid=(p, n * c // tc),
        in_specs=in_specs,
        out_specs=out_specs,
        out_shape=out_shape,
        compiler_params=_cp("parallel", "parallel"),
        name="fft_s3",
    )(*args)


def _hyena_long(v, x1, kk, hy_bias):
    nb, l, c = v.shape
    n = math.isqrt(2 * l)
    assert n * n == 2 * l and nb % 2 == 0
    p = nb // 2
    f1d, f1r, g, hm, f3 = (jnp.asarray(t, F32).astype(BF16) for t in _dft_tables(n))
    pair = lambda a: a.reshape(p, n, n * c)
    khat = _fft_spectrum(_fft_s1(kk, f1r, n, c), g, n, c)
    v = pair(v)
    cm = _fft_mid(_fft_s1(v, f1d, n, c), g, hm, khat, 0, n, c)
    z1, a = _fft_s3(cm, f3, v, pair(x1), hy_bias[0], n, c, f1_next=f1d)
    cm = _fft_mid(a, g, hm, khat, 1, n, c)
    (w2,) = _fft_s3(cm, f3, z1, None, hy_bias[1], n, c)
    return w2.reshape(nb, l, c)


@functools.lru_cache(maxsize=None)
def _ctx_dft_tables(lc):
    m = 2 * lc
    k = np.arange(m)
    th = 2.0 * math.pi * np.outer(k, k) / m
    c, s = np.cos(th), np.sin(th)
    f_data = np.block([[c[:, :lc], s[:, :lc]], [-s[:, :lc], c[:, :lc]]])
    f_real = np.concatenate([c, -s], axis=0)
    ci, si = c[:lc, :], s[:lc, :]
    f_inv = np.block([[ci, -si], [si, ci]]) / m
    return f_data, f_real, f_inv


def _ctxconv_kernel(v_ref, x1_ref, x2_ref, kk_ref, fd_ref, fr_ref, fi_ref, b_ref, o_ref, *, m):
    def conv(u, order):
        kh = _dot(fr_ref[...], kk_ref[order].astype(BF16))
        t = _dot(fd_ref[...], u.astype(BF16))
        tr, ti, kr, ki = t[:m], t[m:], kh[:m], kh[m:]
        y = jnp.concatenate([tr * kr - ti * ki, tr * ki + ti * kr], axis=0).astype(BF16)
        return _dot(fi_ref[...], y)

    v = v_ref[0].astype(F32)
    z1 = x1_ref[0].astype(F32) * (conv(v, 0) + v * b_ref[0:1, :])
    o_ref[0] = (x2_ref[0].astype(F32) * (conv(z1, 1) + z1 * b_ref[1:2, :])).astype(o_ref.dtype)


def _hyena_ctx(v, x1, x2, kk, hy_bias):
    nb, lc, c = v.shape
    p, m = nb // 2, 2 * lc
    fd, fr, fi = (jnp.asarray(t, F32).astype(BF16) for t in _ctx_dft_tables(lc))
    pair = lambda a: a.reshape(p, m, c)
    blk = pl.BlockSpec((1, m, c), lambda q: (q, 0, 0))
    full = lambda a: pl.BlockSpec(a.shape, lambda q: (0,) * a.ndim)
    z = pl.pallas_call(
        functools.partial(_ctxconv_kernel, m=m),
        grid=(p,),
        in_specs=[blk, blk, blk, full(kk), full(fd), full(fr), full(fi), full(hy_bias)],
        out_specs=blk,
        out_shape=jax.ShapeDtypeStruct((p, m, c), BF16),
        compiler_params=_cp("parallel"),
        name="hy_ctx",
    )(pair(v), pair(x1), pair(x2), kk, fd, fr, fi, hy_bias)
    return z.reshape(nb, lc, c)


def _od_inproj_kernel(prev_ref, h_ref, next_ref, mod_ref, g_ref, w_ref, cw_ref, cb_ref,
                      gate_ref, xc_ref, *, d, r, n_lat_tiles, n_tiles):
    i = pl.program_id(1)
    seg_first = jnp.logical_or(i == 0, i == n_lat_tiles)
    seg_last = jnp.logical_or(i == n_lat_tiles - 1, i == n_tiles - 1)
    m = mod_ref[0]
    rows = jnp.concatenate([prev_ref[0, 0], h_ref[0], next_ref[0, 0]], axis=0)
    xn = _rms_mod(rows, g_ref[...], m[:, 0:d], m[:, d:2 * d]).astype(BF16)
    y = _dot(xn, w_ref[...])
    gate_ref[0] = y[SUBLANES:SUBLANES + TT, :r].astype(gate_ref.dtype)
    x = y[:, r:]
    row = lax.broadcasted_iota(jnp.int32, (x.shape[0], 1), 0)
    outside = jnp.logical_or(jnp.logical_and(row < SUBLANES, seg_first),
                             jnp.logical_and(row >= SUBLANES + TT, seg_last))
    x = jnp.where(outside, 0.0, x)
    xc_ref[0] = _conv_taps(x, cw_ref[...], cb_ref[...], RG_CONV_LEFT)


def _od_inproj(h, mods, rows, layer, g, w_in, conv_w, conv_b, *, n_lat_tiles):
    nb, s, d = h.shape
    r = w_in.shape[1] // 2
    n_tiles = s // TT
    n8, per = s // SUBLANES, TT // SUBLANES
    h8 = h.reshape(nb, n8, SUBLANES, d)
    ospec = pl.BlockSpec((1, TT, r), lambda b, i: (b, i, 0))
    return pl.pallas_call(
        functools.partial(_od_inproj_kernel, d=d, r=r, n_lat_tiles=n_lat_tiles, n_tiles=n_tiles),
        grid=(nb, n_tiles),
        in_specs=[
            pl.BlockSpec((1, 1, SUBLANES, d), lambda b, i: (b, jnp.maximum(i * per - 1, 0), 0, 0)),
            pl.BlockSpec((1, TT, d), lambda b, i: (b, i, 0)),
            pl.BlockSpec((1, 1, SUBLANES, d),
                         lambda b, i: (b, jnp.minimum((i + 1) * per, n8 - 1), 0, 0)),
            _mod_spec(layer, rows, nb, n_lat_tiles, mods.shape[-1]),
            pl.BlockSpec((1, d), lambda b, i: (0, 0)),
            pl.BlockSpec(w_in.shape, lambda b, i: (0, 0)),
            pl.BlockSpec(conv_w.shape, lambda b, i: (0, 0)),
            pl.BlockSpec((1, r), lambda b, i: (0, 0)),
        ],
        out_specs=[ospec, ospec],
        out_shape=[jax.ShapeDtypeStruct((nb, s, r), BF16), jax.ShapeDtypeStruct((nb, s, r), F32)],
        compiler_params=_cp("parallel", "parallel"),
        name="od_inproj",
    )(h8, h, h8, mods, g, w_in, conv_w, conv_b.reshape(1, r))


def _rglru_kernel(x_ref, w_ref, b_ref, lam_ref, o_ref, a_ref, bb_ref, carry_ref, *, r, windows):
    dr = pl.program_id(1)
    i = pl.program_id(2)
    part = TT // RG_PARTS
    nl = -lam_ref[0]
    softplus = jnp.maximum(nl, 0.0) + jnp.log1p(jnp.exp(-jnp.abs(nl)))
    neg_rate = (0.5 * RG_C) * softplus
    exp2_rate = (-0.5 * RG_C * math.log2(math.e)) * softplus

    def gates(r0):
        rs = slice(r0, r0 + part)
        xc = x_ref[0, rs, :]
        xb = xc.astype(BF16)
        half_x = 0.5 * xc
        for c0, c1, k0, k1 in windows:
            xk = xb[:, k0:k1]
            tr = jnp.tanh(_dot(xk, w_ref[0, k0:k1, c0:c1]) + b_ref[0, :, c0:c1]) + 1.0
            ti = jnp.tanh(_dot(xk, w_ref[0, k0:k1, r + c0:r + c1]) + b_ref[0, :, r + c0:r + c1]) + 1.0
            a = jnp.exp2(exp2_rate[:, c0:c1] * tr)
            a_ref[rs, c0:c1] = a
            e = jnp.tanh(neg_rate[:, c0:c1] * tr) * (a * a + 1.0)
            root = jnp.where(e > 0.0, e * lax.rsqrt(e), 0.0)
            bb_ref[rs, c0:c1] = root * (ti * half_x[:, c0:c1])

    def scan_rows(rows, h):
        for row in rows:
            h = a_ref[row:row + 1, :] * h + bb_ref[row:row + 1, :]
            o_ref[0, 0, row:row + 1, :] = h
        return h

    def scan_loop(r0, reverse, h):
        n_blk = part // SUBLANES

        def block(g, h):
            base = pl.multiple_of(r0 + (n_blk - 1 - g if reverse else g) * SUBLANES, SUBLANES)
            for k in range(SUBLANES):
                row = base + (SUBLANES - 1 - k if reverse else k)
                h = a_ref[pl.ds(row, 1), :] * h + bb_ref[pl.ds(row, 1), :]
                o_ref[0, 0, pl.ds(row, 1), :] = h
            return h

        return lax.fori_loop(0, n_blk, block, h)

    @pl.when(i == 0)
    def _():
        carry_ref[...] = jnp.zeros_like(carry_ref)

    def run(reverse):
        starts = [k * part for k in (reversed(range(RG_PARTS)) if reverse else range(RG_PARTS))]
        gates(starts[0])
        h = carry_ref[0:1, :]
        for k, r0 in enumerate(starts[:-1]):
            gates(starts[k + 1])
            rows = range(r0 + part - 1, r0 - 1, -1) if reverse else range(r0, r0 + part)
            h = scan_rows(rows, h)
        carry_ref[0:1, :] = scan_loop(starts[-1], reverse, h)

    pl.when(dr == 0)(lambda: run(False))
    pl.when(dr == 1)(lambda: run(True))


def _scan_tile(dr, i, n_lat_tiles, n_tiles):
    fwd = jnp.where(i == 0, n_lat_tiles, i - 1)
    bwd = jnp.where(i == 0, n_lat_tiles, n_lat_tiles - i)
    return jnp.where(dr == 0, fwd, bwd)


def _gate_windows(r, bs):
    out = []
    for c0 in range(0, r, 2 * LANES):
        c1 = min(c0 + 2 * LANES, r)
        k0 = (c0 // bs) * bs // LANES * LANES
        k1 = min(-(-(((c1 - 1) // bs + 1) * bs) // LANES) * LANES, r)
        out.append((c0, c1, k0, k1))
    return tuple(out)


def _rglru(xc, wcat, bcat, lam, *, n_lat_tiles, block_size):
    nb, s, r = xc.shape
    n_tiles = s // TT
    assert n_tiles == n_lat_tiles + 1
    tile = lambda d, i: _scan_tile(d, i, n_lat_tiles, n_tiles)
    return pl.pallas_call(
        functools.partial(_rglru_kernel, r=r, windows=_gate_windows(r, block_size)),
        grid=(nb, 2, n_tiles),
        in_specs=[
            pl.BlockSpec((1, TT, r), lambda b, d, i: (b, tile(d, i), 0)),
            pl.BlockSpec((1, r, 2 * r), lambda b, d, i: (d, 0, 0)),
            pl.BlockSpec((1, 1, 2 * r), lambda b, d, i: (d, 0, 0)),
            pl.BlockSpec((1, 1, r), lambda b, d, i: (d, 0, 0)),
        ],
        out_specs=pl.BlockSpec((1, 1, TT, r), lambda b, d, i: (d, b, tile(d, i), 0)),
        out_shape=jax.ShapeDtypeStruct((2, nb, s, r), F32),
        scratch_shapes=[pltpu.VMEM((TT, r), F32), pltpu.VMEM((TT, r), F32),
                        pltpu.VMEM((SUBLANES, r), F32)],
        compiler_params=_cp("parallel", "parallel", "arbitrary"),
        name="rglru",
    )(xc, wcat, bcat, lam.reshape(2, 1, r))


def _block_diag(w):
    n, bs, _ = w.shape
    eye = jnp.eye(n, dtype=w.dtype)
    return (eye[:, None, :, None] * w[:, :, None, :]).reshape(n * bs, n * bs)


def _gelu_tanh(x):
    return 0.5 * x * (1.0 + jnp.tanh(math.sqrt(2.0 / math.pi) * (x + 0.044715 * (x * x * x))))


def _post_kernel(*refs, kind, final, d, n_lat_tiles):
    if kind == "even":
        (hl_ref, hc_ref, x2_ref, wl_ref, zc_ref, ol_ref, oc_ref, mod_ref, g2_ref, wo_ref, w1_ref,
         w2_ref) = refs[:12]
        rest = refs[12:]
        is_ctx = pl.program_id(1) >= n_lat_tiles
        z_lat = (x2_ref[0].astype(F32) * wl_ref[0].astype(F32)).astype(BF16)
        z = jnp.where(is_ctx, zc_ref[0], z_lat)
        o = jnp.where(is_ctx, oc_ref[0], ol_ref[0])
        y = _dot(jnp.concatenate([z, o], axis=-1), wo_ref[...])
    else:
        hl_ref, hc_ref, hd_ref, gate_ref, mod_ref, g2_ref, wo_ref, w1_ref, w2_ref = refs[:9]
        rest = refs[9:]
        mix = (hd_ref[0, 0] + hd_ref[1, 0]) * _gelu_tanh(gate_ref[0].astype(F32))
        y = _dot(mix.astype(BF16), wo_ref[...])
    out_ref = rest[-1]
    m = mod_ref[0]
    h1 = _stream_tile(hl_ref, hc_ref, n_lat_tiles) + m[:, 2 * d:3 * d] * y
    xn = _rms_mod(h1, g2_ref[...], m[:, 3 * d:4 * d], m[:, 4 * d:5 * d]).astype(BF16)
    acc = jnp.zeros_like(h1)
    dff = w1_ref.shape[1]
    for c0 in range(0, dff, FF_CHUNK):
        a = jnp.maximum(_dot(xn, w1_ref[:, c0:c0 + FF_CHUNK]), 0.0)
        acc = acc + _dot((a * a).astype(BF16), w2_ref[c0:c0 + FF_CHUNK, :])
    h2 = h1 + m[:, 5 * d:6 * d] * acc
    if final:
        fg_ref = rest[0]
        h2 = h2 * lax.rsqrt(jnp.mean(h2 * h2, axis=-1, keepdims=True) + EPS) * fg_ref[...]
    out_ref[0] = h2


def _post(kind, h, mix_args, mods, rows, layer, g2, w_out, w1, w2, final_g, *, n_lat_tiles, final):
    h_specs, h_args = _stream_specs(h, n_lat_tiles)
    nb, _, d = h_args[0].shape
    n_tiles = n_lat_tiles if final else n_lat_tiles + 1
    tok = lambda w: pl.BlockSpec((1, TT, w), lambda b, i: (b, i, 0))
    full = lambda a: pl.BlockSpec(a.shape, lambda b, i: (0,) * a.ndim)
    if kind == "even":
        lat = lambda w: pl.BlockSpec((1, TT, w), lambda b, i: (b, jnp.minimum(i, n_lat_tiles - 1), 0))
        ctx = lambda w: pl.BlockSpec((1, TT, w), lambda b, i: (b, 0, 0))
        x2_lat, w_lat, z_ctx, o_lat, o_ctx = mix_args
        mix_specs = [lat(x2_lat.shape[-1]), lat(w_lat.shape[-1]), ctx(z_ctx.shape[-1]),
                     lat(o_lat.shape[-1]), ctx(o_ctx.shape[-1])]
    else:
        hd, gate = mix_args
        r = gate.shape[-1]
        mix_specs = [pl.BlockSpec((2, 1, TT, r), lambda b, i: (0, b, i, 0)), tok(r)]
    in_specs = h_specs + mix_specs + [
        _mod_spec(layer, rows, nb, n_lat_tiles, mods.shape[-1]),
        pl.BlockSpec((1, d), lambda b, i: (0, 0)), full(w_out), full(w1), full(w2)]
    args = [*h_args, *mix_args, mods, g2, w_out, w1, w2]
    if final:
        in_specs.append(pl.BlockSpec((1, d), lambda b, i: (0, 0)))
        args.append(final_g)
    return pl.pallas_call(
        functools.partial(_post_kernel, kind=kind, final=final, d=d, n_lat_tiles=n_lat_tiles),
        grid=(nb, n_tiles),
        in_specs=in_specs,
        out_specs=tok(d),
        out_shape=jax.ShapeDtypeStruct((nb, n_tiles * TT, d), F32),
        compiler_params=_cp("parallel", "parallel"),
        name="post_" + kind,
    )(*args)


@functools.lru_cache(maxsize=None)
def _rope_tables(l, lc, head_dim):
    axis = head_dim // 2
    freqs = ROPE_BASE ** (-np.arange(0, axis, 2, dtype=np.float64) / axis)
    freqs = freqs.astype(np.float32).astype(np.float64)
    t = np.arange(l)
    ang_r = (t // GRID_W)[:, None] * freqs
    ang_c = (t % GRID_W)[:, None] * freqs
    cos = np.concatenate([np.cos(ang_r)] * 2 + [np.cos(ang_c)] * 2, axis=-1)
    sin = np.concatenate([-np.sin(ang_r), np.sin(ang_r), -np.sin(ang_c), np.sin(ang_c)], axis=-1)
    cos = np.concatenate([cos, np.ones((lc, head_dim))], axis=0)
    sin = np.concatenate([sin, np.zeros((lc, head_dim))], axis=0)
    rep = LANES // head_dim
    return np.tile(cos, (1, rep)), np.tile(sin, (1, rep))


def kernel(x, c, ctx, c_ctx, ada_w, ada_b, norm1_g, norm2_g, mlp_w1, mlp_w2, final_g, ev_w_in, ev_w_out, hy_short_w, hy_short_b, hy_f_w1, hy_f_b1, hy_f_w2, hy_f_b2, hy_f_w3, hy_f_b3, hy_f_freq, hy_f_decay, hy_bias, df_lq1, df_lk1, df_lq2, df_lk2, df_subln_g, od_w_in, od_w_out, rg_conv_w, rg_conv_b, rg_wa, rg_ba, rg_wx, rg_bx, rg_lam):
    nb, l, d = x.shape
    lc = ctx.shape[1]
    depth = ada_w.shape[0]
    hy = hy_bias.shape[-1]
    head_dim = df_lq1.shape[-1]
    qk = (ev_w_in.shape[-1] - 3 * hy) // 3
    r = rg_lam.shape[-1]
    assert l % TT == 0 and lc == TT and l % GRID_W == 0
    n_lat_tiles = l // TT

    rows = -(-(nb + 1) // SUBLANES) * SUBLANES
    cond = jnp.zeros((rows, d), F32).at[:nb].set(c).at[nb].set(c_ctx)
    mods = _ada_mods(cond, ada_w, ada_b).reshape(depth * rows, 1, ada_w.shape[-1])

    cos_t, sin_t = (jnp.asarray(t, F32) for t in _rope_tables(l, lc, head_dim))
    h = (x, ctx)
    w1_bf, w2_bf = mlp_w1.astype(BF16), mlp_w2.astype(BF16)

    for i in range(depth):
        j = i // 2
        final = i == depth - 1
        g1, g2 = norm1_g[i].reshape(1, d), norm2_g[i].reshape(1, d)
        if i % 2 == 0:
            lam_init = 0.8 - 0.6 * math.exp(-0.3 * i)
            u, qt, k, vt = _ev_inproj(h, mods, rows, i, g1, ev_w_in[j].astype(BF16), cos_t, sin_t,
                                      n_lat_tiles=n_lat_tiles, hy3=3 * hy, qk=qk, head_dim=head_dim,
                                      tk=_attn_chunk(l + lc))
            lam_vecs = [a[j].reshape(1, -1) for a in (df_lq1, df_lk1, df_lq2, df_lk2)]
            o_lat, o_ctx = _diff_attention(qt, k, vt, lam_vecs, df_subln_g[j].reshape(1, -1),
                                           n_lat_tiles=n_lat_tiles, lam_init=lam_init,
                                           head_dim=head_dim)
            fparams = (hy_f_w1[j], hy_f_b1[j], hy_f_w2[j], hy_f_b2[j], hy_f_w3[j], hy_f_b3[j],
                       hy_f_freq[j], hy_f_decay[j])
            vl, x1l, x2l = _shortconv(u, hy_short_w[j], hy_short_b[j], tile0=0,
                                      n_seg_tiles=n_lat_tiles)
            vc, x1c, x2c = _shortconv(u, hy_short_w[j], hy_short_b[j], tile0=n_lat_tiles,
                                      n_seg_tiles=lc // TT)
            w_lat = _hyena_long(vl, x1l, _hyena_filter(l, math.isqrt(2 * l), *fparams), hy_bias[j])
            z_ctx = _hyena_ctx(vc, x1c, x2c, _hyena_filter(lc, 1, *fparams), hy_bias[j])
            h = _post("even", h, (x2l, w_lat, z_ctx, o_lat, o_ctx), mods, rows, i, g2, ev_w_out[j].astype(BF16),
                      w1_bf[i], w2_bf[i], final_g.reshape(1, d), n_lat_tiles=n_lat_tiles,
                      final=final)
        else:
            gate, xc = _od_inproj(h, mods, rows, i, g1, od_w_in[j].astype(BF16), rg_conv_w[j],
                                  rg_conv_b[j], n_lat_tiles=n_lat_tiles)
            wcat = (0.5 * jnp.stack(
                [jnp.concatenate([_block_diag(rg_wa[j, dd]), _block_diag(rg_wx[j, dd])], axis=1)
                 for dd in range(2)])).astype(BF16)
            bcat = 0.5 * jnp.concatenate([rg_ba[j], rg_bx[j]], axis=-1).reshape(2, 1, 2 * r)
            hd = _rglru(xc, wcat, bcat, rg_lam[j], n_lat_tiles=n_lat_tiles,
                        block_size=rg_wa.shape[-1])
            h = _post("odd", h, (hd, gate), mods, rows, i, g2, od_w_out[j].astype(BF16),
                      w1_bf[i], w2_bf[i], final_g.reshape(1, d), n_lat_tiles=n_lat_tiles,
                      final=final)
    return h
```

```python
import functools
import math

import numpy as np
import jax
import jax.numpy as jnp
from jax import lax
from jax.experimental import pallas as pl
from jax.experimental.pallas import tpu as pltpu

F32 = jnp.float32
BF16 = jnp.bfloat16
HIGHEST = lax.Precision.HIGHEST

EPS = 1e-6
GRID_W = 64
ROPE_BASE = 10000.0
N_BANDS = 16
RG_C = 8.0
RG_CONV_LEFT = 2
HY_SHORT_LEFT = 1

TT = 256
LANES = 128
SUBLANES = 8
VMEM_LIMIT = 56 * 1024 * 1024
FF_CHUNK = 1024
FFT_N2_TILE = 16
FFT_K1_TILE = 16
ADA_K_TILE = 256
ATTN_SUB_Q = 512
RG_PARTS = 2
FILT_COL_BLOCKS = 4
VT_ROWS = LANES + 16


def _cp(*sem):
    return pltpu.CompilerParams(dimension_semantics=sem, vmem_limit_bytes=VMEM_LIMIT)


def _dot(a, b, **kw):
    return jnp.dot(a, b, preferred_element_type=F32, **kw)


def _sigmoid(x):
    return 1.0 / (1.0 + jnp.exp(-x))


def _rms_mod(x, g, shift, scale):
    y = x * lax.rsqrt(jnp.mean(x * x, axis=-1, keepdims=True) + EPS)
    return (y * g) * (1.0 + scale) + shift


def _ada_kernel(c_ref, w_ref, b_ref, o_ref):
    k = pl.program_id(1)
    c = c_ref[k]
    part = _dot((c * _sigmoid(c)).astype(BF16), w_ref[0].astype(BF16))

    @pl.when(k == 0)
    def _():
        o_ref[0] = part + b_ref[0]

    @pl.when(k > 0)
    def _():
        o_ref[0] += part


def _ada_mods(cond, ada_w, ada_b):
    depth, d, n = ada_w.shape
    rows = cond.shape[0]
    tk = min(d, ADA_K_TILE)
    cond_k = cond.reshape(rows, d // tk, tk).transpose(1, 0, 2)
    return pl.pallas_call(
        _ada_kernel,
        grid=(depth, d // tk),
        in_specs=[
            pl.BlockSpec(cond_k.shape, lambda l, k: (0, 0, 0)),
            pl.BlockSpec((1, tk, n), lambda l, k: (l, k, 0)),
            pl.BlockSpec((1, 1, n), lambda l, k: (l, 0, 0)),
        ],
        out_specs=pl.BlockSpec((1, rows, n), lambda l, k: (l, 0, 0)),
        out_shape=jax.ShapeDtypeStruct((depth, rows, n), F32),
        compiler_params=_cp("parallel", "arbitrary"),
        name="ada_mods",
    )(cond_k, ada_w, ada_b.reshape(depth, 1, n))


def _stream_specs(h, n_lat_tiles):
    separate = isinstance(h, tuple)
    lat_arr, ctx_arr = h if separate else (h, h)
    ctx_blk = 0 if separate else n_lat_tiles
    d = lat_arr.shape[-1]
    lat = pl.BlockSpec((1, TT, d), lambda b, i: (b, jnp.minimum(i, n_lat_tiles - 1), 0))
    ctx = pl.BlockSpec((1, TT, d), lambda b, i: (b, ctx_blk, 0))
    return [lat, ctx], [lat_arr, ctx_arr]


def _stream_tile(hl_ref, hc_ref, n_lat_tiles):
    return jnp.where(pl.program_id(1) >= n_lat_tiles, hc_ref[0], hl_ref[0])


def _mod_spec(layer, rows, nb, n_lat_tiles, n6):
    def imap(b, i):
        return (layer * rows + jnp.where(i >= n_lat_tiles, nb, b), 0, 0)
    return pl.BlockSpec((1, 1, n6), imap)


def _ev_inproj_kernel(hl_ref, hc_ref, mod_ref, g_ref, w_ref, cos_ref, sin_ref,
                      u_ref, qt_ref, k_ref, vt_ref, *, d, hy3, qk, qscale, n_lat_tiles):
    m = mod_ref[0]
    x = _stream_tile(hl_ref, hc_ref, n_lat_tiles)
    xn = _rms_mod(x, g_ref[...], m[:, 0:d], m[:, d:2 * d]).astype(BF16)
    cos = cos_ref[...]
    sin = sin_ref[...]
    lane = lax.broadcasted_iota(jnp.int32, cos.shape, 1)
    first = (lane % 32) < 16

    def rope(z):
        sw = jnp.where(first, pltpu.roll(z, LANES - 16, 1), pltpu.roll(z, 16, 1))
        return z * cos + sw * sin

    extra = vt_ref.shape[3] - LANES
    ones_row = jnp.where(lax.broadcasted_iota(jnp.int32, (extra, cos.shape[0]), 0) == 0,
                         1.0, 0.0).astype(BF16)
    yq = _dot(xn, w_ref[:, hy3:hy3 + qk])
    for c in range(qk // LANES):
        qt_ref[0, c] = (rope(yq[:, c * LANES:(c + 1) * LANES]) * qscale).T.astype(BF16)
    yk = _dot(xn, w_ref[:, hy3 + qk:hy3 + 2 * qk])
    for c in range(qk // LANES):
        k_ref[0, :, c * LANES:(c + 1) * LANES] = rope(yk[:, c * LANES:(c + 1) * LANES]).astype(BF16)
    yv = _dot(xn, w_ref[:, hy3 + 2 * qk:])
    for c in range(qk // LANES):
        vt_ref[0, c, 0, 0:LANES, :] = yv[:, c * LANES:(c + 1) * LANES].T.astype(BF16)
        vt_ref[0, c, 0, LANES:, :] = ones_row
    u_ref[0] = _dot(xn, w_ref[:, :hy3]).astype(u_ref.dtype)


def _ev_inproj(h, mods, rows, layer, g, w_in, cos_t, sin_t, *, n_lat_tiles, hy3, qk, head_dim, tk):
    h_specs, h_args = _stream_specs(h, n_lat_tiles)
    nb, _, d = h_args[0].shape
    s = (n_lat_tiles + 1) * TT
    n_in = w_in.shape[1]
    assert 2 * head_dim == LANES
    heads, per = qk // LANES, tk // TT
    kern = functools.partial(_ev_inproj_kernel, d=d, hy3=hy3, qk=qk, n_lat_tiles=n_lat_tiles,
                             qscale=head_dim ** -0.5 * math.log2(math.e))
    return pl.pallas_call(
        kern,
        grid=(nb, s // TT),
        in_specs=h_specs + [
            _mod_spec(layer, rows, nb, n_lat_tiles, mods.shape[-1]),
            pl.BlockSpec((1, d), lambda b, i: (0, 0)),
            pl.BlockSpec((d, n_in), lambda b, i: (0, 0)),
            pl.BlockSpec((TT, LANES), lambda b, i: (i, 0)),
            pl.BlockSpec((TT, LANES), lambda b, i: (i, 0)),
        ],
        out_specs=[
            pl.BlockSpec((1, TT, hy3), lambda b, i: (b, i, 0)),
            pl.BlockSpec((1, heads, LANES, TT), lambda b, i: (b, 0, 0, i)),
            pl.BlockSpec((1, TT, qk), lambda b, i: (b, i, 0)),
            pl.BlockSpec((1, heads, 1, VT_ROWS, TT), lambda b, i: (b, 0, i // per, 0, i % per)),
        ],
        out_shape=[
            jax.ShapeDtypeStruct((nb, s, hy3), BF16),
            jax.ShapeDtypeStruct((nb, heads, LANES, s), BF16),
            jax.ShapeDtypeStruct((nb, s, qk), BF16),
            jax.ShapeDtypeStruct((nb, heads, s // tk, VT_ROWS, tk), BF16),
        ],
        compiler_params=_cp("parallel", "parallel"),
        name="ev_inproj",
    )(*h_args, mods, g, w_in, cos_t, sin_t)


def _split_maps(qt, half):
    row = lax.broadcasted_iota(jnp.int32, qt.shape, 0)
    zero = jnp.zeros_like(qt)
    return jnp.where(row < half, qt, zero), jnp.where(row >= half, qt, zero)


def _attn_out(a0, a1, lam_refs, sg_ref, lam_init, dv):
    lq1_ref, lk1_ref, lq2_ref, lk2_ref = lam_refs
    lam = (jnp.exp(jnp.sum(lq1_ref[...] * lk1_ref[...], axis=-1, keepdims=True))
           - jnp.exp(jnp.sum(lq2_ref[...] * lk2_ref[...], axis=-1, keepdims=True)) + lam_init)
    o = (a0[:dv] / a0[dv:dv + 1] - lam * (a1[:dv] / a1[dv:dv + 1])).T
    on = o * lax.rsqrt(jnp.mean(o * o, axis=-1, keepdims=True) + EPS)
    return on * sg_ref[...] * (1.0 - lam_init)


def _col_max8(sc):
    part = sc[0:SUBLANES]
    for g in range(1, sc.shape[0] // SUBLANES):
        part = jnp.maximum(part, sc[g * SUBLANES:(g + 1) * SUBLANES])
    return part


def _attn_kernel(qt_ref, qnt_ref, k_ref, vt_ref, lq1_ref, lk1_ref, lq2_ref, lk2_ref, sg_ref, o_ref,
                 s_ref, mp_ref, p_ref, acc_ref, m_ref, alpha_ref, *, n_chunks, tk, lam_init, half):
    i = pl.program_id(2)
    dv = 2 * half
    lam_refs = (lq1_ref, lk1_ref, lq2_ref, lk2_ref)
    tq = qnt_ref.shape[-1]
    q_sub = (_split_maps(qt_ref[0, 0, :, 0:tq], half), _split_maps(qt_ref[0, 0, :, tq:2 * tq], half))
    q_next = _split_maps(qnt_ref[0, 0], half)

    def put_scores(slot, qts, c):
        kc = k_ref[0, pl.ds(pl.multiple_of(c * tk, tk), tk), :]
        for j in range(2):
            sc = _dot(kc, qts[j])
            s_ref[slot, j] = sc
            mp_ref[slot, j] = _col_max8(sc)

    def put_probs(slot, sub, first):
        for j in range(2):
            col_max = jnp.max(mp_ref[slot, j], axis=0, keepdims=True)
            if first:
                m_new = col_max
            else:
                m_old = m_ref[sub, j]
                m_new = jnp.maximum(m_old, col_max)
                alpha_ref[slot, j] = jnp.exp2(m_old - m_new)
            m_ref[sub, j] = m_new
            p_ref[slot, j] = jnp.exp2(s_ref[slot, j] - m_new).astype(BF16)

    def add_pv(slot, sub, c, first):
        for j in range(2):
            pv = _dot(vt_ref[0, 0, c], p_ref[slot, j])
            acc_ref[sub, j] = pv if first else alpha_ref[slot, j] * acc_ref[sub, j] + pv

    def sub_tile(sub, parity, qts, qts_after, sub_after):
        slot = lambda c: (c + parity) % 2
        put_scores(slot(0), qts, 2)
        put_probs(slot(1), sub, False)
        add_pv(slot(0), sub, 0, True)
        n_uniform = n_chunks - 3

        unroll = 2

        def group(u, carry):
            c = 1 + unroll * u
            for k in range(unroll):
                put_scores(slot(1 + k), qts, c + k + 2)
                put_probs(slot(k), sub, False)
                add_pv(slot(1 + k), sub, c + k, False)
            return carry

        lax.fori_loop(0, n_uniform // unroll, group, 0)
        c = n_chunks - 2
        put_scores(slot(c), qts_after, 0)
        put_probs(slot(c + 1), sub, False)
        add_pv(slot(c), sub, c, False)
        c = n_chunks - 1
        put_scores(slot(c), qts_after, 1)
        put_probs(slot(c + 1), sub_after, True)
        add_pv(slot(c), sub, c, False)
        o_ref[0, sub * tq:(sub + 1) * tq, :] = _attn_out(
            acc_ref[sub, 0], acc_ref[sub, 1], lam_refs, sg_ref, lam_init, dv).astype(o_ref.dtype)

    @pl.when(i == 0)
    def _():
        put_scores(0, q_sub[0], 0)
        put_probs(0, 0, True)
        put_scores(1, q_sub[0], 1)

    sub_tile(0, 0, q_sub[0], q_sub[1], 1)
    sub_tile(1, 1, q_sub[1], q_next, 0)


def _ctx_attn_kernel(qt_ref, k_ref, vt_ref, lq1_ref, lk1_ref, lq2_ref, lk2_ref, sg_ref, o_ref,
                     *, lam_init, half):
    acc = []
    for qtj in _split_maps(qt_ref[0, 0], half):
        sc = _dot(k_ref[0], qtj)
        col_max = jnp.max(_col_max8(sc), axis=0, keepdims=True)
        acc.append(_dot(vt_ref[0, 0, 0], jnp.exp2(sc - col_max).astype(BF16)))
    o_ref[0] = _attn_out(acc[0], acc[1], (lq1_ref, lk1_ref, lq2_ref, lk2_ref), sg_ref, lam_init,
                         2 * half).astype(o_ref.dtype)


def _attn_chunk(s):
    return next(t for t in (3 * TT, TT) if s % t == 0 and (s // t) % 2 == 1 and s // t >= 5)


def _diff_attention(qt, k, vt, lam_vecs, subln_g, *, n_lat_tiles, lam_init, head_dim):
    nb, s, qk = k.shape
    dv = 2 * head_dim
    heads = qk // dv
    l = n_lat_tiles * TT
    n_chunks, tk = vt.shape[2], vt.shape[4]
    tq = ATTN_SUB_Q if l % (2 * ATTN_SUB_Q) == 0 else TT
    assert l % (2 * tq) == 0 and n_chunks % 2 == 1 and n_chunks >= 5 and s - l == TT
    n_steps = l // (2 * tq)
    vec = lambda n: pl.BlockSpec((1, n), lambda b, h, i: (0, 0))
    vecs = [vec(head_dim)] * 4 + [vec(dv)]
    o_lat = pl.pallas_call(
        functools.partial(_attn_kernel, n_chunks=n_chunks, tk=tk, lam_init=lam_init, half=head_dim),
        grid=(nb, heads, n_steps),
        in_specs=[
            pl.BlockSpec((1, 1, dv, 2 * tq), lambda b, h, i: (b, h, 0, i)),
            pl.BlockSpec((1, 1, dv, tq),
                         lambda b, h, i: (b, h, 0, jnp.minimum(2 * i + 2, 2 * n_steps - 1))),
            pl.BlockSpec((1, s, dv), lambda b, h, i: (b, 0, h)),
            pl.BlockSpec((1, 1, n_chunks, VT_ROWS, tk), lambda b, h, i: (b, h, 0, 0, 0)),
        ] + vecs,
        out_specs=pl.BlockSpec((1, 2 * tq, dv), lambda b, h, i: (b, i, h)),
        out_shape=jax.ShapeDtypeStruct((nb, l, qk), BF16),
        scratch_shapes=[pltpu.VMEM((2, 2, tk, tq), F32), pltpu.VMEM((2, 2, SUBLANES, tq), F32),
                        pltpu.VMEM((2, 2, tk, tq), BF16), pltpu.VMEM((2, 2, VT_ROWS, tq), F32),
                        pltpu.VMEM((2, 2, 1, tq), F32), pltpu.VMEM((2, 2, 1, tq), F32)],
        compiler_params=_cp("parallel", "parallel", "arbitrary"),
        name="diff_attn",
    )(qt, qt, k, vt, *lam_vecs, subln_g)
    per = tk // TT
    vec2 = lambda n: pl.BlockSpec((1, n), lambda b, h: (0, 0))
    o_ctx = pl.pallas_call(
        functools.partial(_ctx_attn_kernel, lam_init=lam_init, half=head_dim),
        grid=(nb, heads),
        in_specs=[pl.BlockSpec((1, 1, dv, TT), lambda b, h: (b, h, 0, n_lat_tiles)),
                  pl.BlockSpec((1, TT, dv), lambda b, h: (b, n_lat_tiles, h)),
                  pl.BlockSpec((1, 1, 1, VT_ROWS, TT),
                               lambda b, h: (b, h, n_lat_tiles // per, 0, n_lat_tiles % per)),
                  ] + [vec2(head_dim)] * 4 + [vec2(dv)],
        out_specs=pl.BlockSpec((1, TT, dv), lambda b, h: (b, 0, h)),
        out_shape=jax.ShapeDtypeStruct((nb, TT, qk), BF16),
        compiler_params=_cp("parallel", "parallel"),
        name="ctx_attn",
    )(qt, k, vt, *lam_vecs, subln_g)
    return o_lat, o_ctx


def _halo_rows(dtype):
    return SUBLANES * 4 // jnp.dtype(dtype).itemsize


def _halo_fill(xp_ref, prev_ref, x_ref, next_ref, has_prev, has_next):
    hr = prev_ref.shape[2]
    zero = jnp.zeros((hr, xp_ref.shape[-1]), F32)
    xp_ref[hr:hr + TT, :] = x_ref[0].astype(F32)
    xp_ref[0:hr, :] = jnp.where(has_prev, prev_ref[0, 0].astype(F32), zero)
    xp_ref[hr + TT:2 * hr + TT, :] = jnp.where(has_next, next_ref[0, 0].astype(F32), zero)


def _conv_taps(xp, w, bias, left):
    rows = xp.shape[0]
    hr = (rows - TT) // 2
    before = None
    for j in range(left):
        z = w[j:j + 1, :] * xp
        before = pltpu.roll(z if before is None else before + z, 1, 0)
    after = None
    for j in range(w.shape[0] - 1, left, -1):
        z = w[j:j + 1, :] * xp
        after = pltpu.roll(z if after is None else after + z, rows - 1, 0)
    acc = w[left:left + 1, :] * xp + bias
    for part in (before, after):
        if part is not None:
            acc = acc + part
    return acc[hr:hr + TT, :]


def _shortconv_kernel(prev_ref, x_ref, next_ref, w_ref, b_ref, v_ref, x1_ref, x2_ref, xp_ref,
                      *, n_seg_tiles, hy):
    i = pl.program_id(1)
    _halo_fill(xp_ref, prev_ref, x_ref, next_ref, i > 0, i < n_seg_tiles - 1)
    y = _conv_taps(xp_ref[...], w_ref[...], b_ref[...], HY_SHORT_LEFT)
    v_ref[0] = y[:, :hy].astype(v_ref.dtype)
    x1_ref[0] = y[:, hy:2 * hy].astype(x1_ref.dtype)
    x2_ref[0] = y[:, 2 * hy:].astype(x2_ref.dtype)


def _shortconv(u, w, b, *, tile0, n_seg_tiles):
    nb, s, hy3 = u.shape
    hy = hy3 // 3
    hr = _halo_rows(u.dtype)
    nh, per = s // hr, TT // hr
    uh = u.reshape(nb, nh, hr, hy3)
    prev = pl.BlockSpec((1, 1, hr, hy3),
                        lambda b, i: (b, jnp.maximum((tile0 + i) * per - 1, 0), 0, 0))
    nxt = pl.BlockSpec((1, 1, hr, hy3),
                       lambda b, i: (b, jnp.minimum((tile0 + i + 1) * per, nh - 1), 0, 0))
    out = jax.ShapeDtypeStruct((nb, n_seg_tiles * TT, hy), BF16)
    ospec = pl.BlockSpec((1, TT, hy), lambda b, i: (b, i, 0))
    return pl.pallas_call(
        functools.partial(_shortconv_kernel, n_seg_tiles=n_seg_tiles, hy=hy),
        grid=(nb, n_seg_tiles),
        in_specs=[
            prev,
            pl.BlockSpec((1, TT, hy3), lambda b, i: (b, tile0 + i, 0)),
            nxt,
            pl.BlockSpec(w.shape, lambda b, i: (0, 0)),
            pl.BlockSpec((1, hy3), lambda b, i: (0, 0)),
        ],
        out_specs=[ospec, ospec, ospec],
        out_shape=[out, out, out],
        scratch_shapes=[pltpu.VMEM((TT + 2 * hr, hy3), F32)],
        compiler_params=_cp("parallel", "parallel"),
        name="hy_shortconv",
    )(uh, u, uh, w, b.reshape(1, hy3))


def _filter_feats(lh, n_cols):
    n = np.arange(2 * lh).reshape(-1, n_cols).T.reshape(-1)
    lag = np.where(n < lh, n, 2 * lh - n).astype(np.float64)
    t = (lag / lh).astype(np.float32).astype(np.float64)
    bands = np.arange(1, N_BANDS + 1, dtype=np.float64)
    ang = 2.0 * math.pi * t[:, None] * bands
    feats = np.concatenate([t[:, None], np.cos(ang), np.sin(ang)], axis=-1)
    pad = (-(feats.shape[1] + 1)) % SUBLANES
    return np.concatenate([feats, np.zeros((2 * lh, pad)), n[:, None].astype(np.float64)], axis=-1)


def _filt_kernel(ft_ref, w1_ref, b1_ref, w2_ref, b2_ref, w3_ref, b3_ref, fr_ref, dec_ref, o_ref,
                 *, lh, c):
    ft = ft_ref[...]
    freq = fr_ref[...]
    h = jnp.sin(freq * (_dot(ft, w1_ref[...], precision=HIGHEST) + b1_ref[...]))
    h = jnp.sin(freq * (_dot(h, w2_ref[...], precision=HIGHEST) + b2_ref[...])).astype(BF16)
    tr = o_ref.shape[1]
    half = tr // 2
    for q in range(ft.shape[0] // tr):
        for dr in range(2):
            rs = slice(q * tr + dr * half, q * tr + (dr + 1) * half)
            t = ft[rs, 0:1]
            f = (_dot(h[rs], w3_ref[dr]) + b3_ref[dr]) * jnp.exp(-t * jnp.abs(dec_ref[dr]))
            if dr == 1:
                f = jnp.where(ft[rs, ft.shape[1] - 1:] == lh, 0.0, f)
            for o in range(2):
                o_ref[o, dr * half:(dr + 1) * half, q * c:(q + 1) * c] = f[:, o * c:(o + 1) * c]


def _hyena_filter(lh, n_cols, w1, b1, w2, b2, w3, b3, freq, decay):
    c = decay.shape[-1]
    feats = jnp.asarray(_filter_feats(lh, n_cols), F32)
    fe = feats.shape[1]
    w1p = jnp.pad(w1, ((0, fe - w1.shape[0]), (0, 0)))
    hid = w1.shape[1]
    tr = 2 * lh // n_cols
    cb = math.gcd(n_cols, FILT_COL_BLOCKS)
    by_dir = lambda a, lead: jnp.moveaxis(a.reshape(lead, 2, 2, c), 2, 0).reshape(2, lead, 2 * c)
    full = lambda a: pl.BlockSpec(a.shape, lambda i: (0,) * a.ndim)
    args = (w1p, b1.reshape(1, hid), w2, b2.reshape(1, hid), by_dir(w3, hid).astype(BF16),
            by_dir(b3, 1), freq.reshape(1, hid), by_dir(decay, 1))
    return pl.pallas_call(
        functools.partial(_filt_kernel, lh=lh, c=c),
        grid=(n_cols // cb,),
        in_specs=[pl.BlockSpec((cb * tr, fe), lambda i: (i, 0))] + [full(a) for a in args],
        out_specs=pl.BlockSpec((2, tr, cb * c), lambda i: (0, 0, i)),
        out_shape=jax.ShapeDtypeStruct((2, tr, n_cols * c), F32),
        compiler_params=_cp("parallel"),
        name="hy_filter",
    )(feats, *args)


@functools.lru_cache(maxsize=None)
def _dft_tables(n):
    nn = n * n
    h = n // 2
    k = np.arange(n)
    th = 2.0 * math.pi * np.outer(k, k) / n
    c, s = np.cos(th), np.sin(th)
    f1_data = np.block([[c[:, :h], s[:, :h]], [-s[:, :h], c[:, :h]]])
    f1_real = np.concatenate([c, -s], axis=0)
    idx = (k[None, None, :] * (k[:, None, None] + n * k[None, :, None])) % nn
    phi = 2.0 * math.pi * idx / nn
    cp, sp = np.cos(phi), np.sin(phi)
    g = np.concatenate([np.concatenate([cp, sp], axis=2), np.concatenate([-sp, cp], axis=2)], axis=1)
    hmat = np.transpose(g, (0, 2, 1)) / nn
    ci, si = c[:h, :], s[:h, :]
    f3 = np.zeros((n, 2 * n))
    f3[:h, 0::2], f3[:h, 1::2] = ci, -si
    f3[h:, 0::2], f3[h:, 1::2] = si, ci
    return f1_data, f1_real, g, hmat, f3


def _fft_s1_kernel(x_ref, f_ref, o_ref):
    o_ref[0] = _dot(f_ref[...], x_ref[0].astype(BF16)).astype(o_ref.dtype)


def _fft_s1(x, f1, n, c):
    p = x.shape[0]
    tc = FFT_N2_TILE * c
    return pl.pallas_call(
        _fft_s1_kernel,
        grid=(p, n * c // tc),
        in_specs=[pl.BlockSpec((1, n, tc), lambda q, j: (q, 0, j)),
                  pl.BlockSpec(f1.shape, lambda q, j: (0, 0))],
        out_specs=pl.BlockSpec((1, 2 * n, tc), lambda q, j: (q, 0, j)),
        out_shape=jax.ShapeDtypeStruct((p, 2 * n, n * c), BF16),
        compiler_params=_cp("parallel", "parallel"),
        name="fft_s1",
    )(x, f1)


def _fft_spec_kernel(a_ref, g_ref, o_ref, *, tk):
    for j in range(tk):
        x = jnp.concatenate([a_ref[0, 0, j], a_ref[0, 1, j]], axis=0)
        o_ref[0, j] = _dot(g_ref[j], x).astype(o_ref.dtype)


def _fft_mid_kernel(a_ref, g_ref, h_ref, kh_ref, o_ref, *, tk, n):
    ts = [_dot(g_ref[j], jnp.concatenate([a_ref[0, 0, j], a_ref[0, 1, j]], axis=0))
          for j in range(tk)]
    ys = []
    for j, t in enumerate(ts):
        tr, ti = t[:n], t[n:]
        kr, ki = kh_ref[0, j, :n].astype(F32), kh_ref[0, j, n:].astype(F32)
        ys.append(jnp.concatenate([tr * kr - ti * ki, tr * ki + ti * kr], axis=0).astype(BF16))
    for j, y in enumerate(ys):
        o_ref[0, j] = _dot(h_ref[j], y).astype(o_ref.dtype)


def _fft_spectrum(a, g, n, c):
    p = a.shape[0]
    tk = FFT_K1_TILE
    a5 = a.reshape(p, 2, n, n, c)
    return pl.pallas_call(
        functools.partial(_fft_spec_kernel, tk=tk),
        grid=(n // tk, p),
        in_specs=[pl.BlockSpec((1, 2, tk, n, c), lambda j, q: (q, 0, j, 0, 0)),
                  pl.BlockSpec((tk, 2 * n, 2 * n), lambda j, q: (j, 0, 0))],
        out_specs=pl.BlockSpec((1, tk, 2 * n, c), lambda j, q: (q, j, 0, 0)),
        out_shape=jax.ShapeDtypeStruct((p, n, 2 * n, c), BF16),
        compiler_params=_cp("parallel", "parallel"),
        name="fft_spectrum",
    )(a5, g)


def _fft_mid(a, g, hm, khat, order, n, c):
    p = a.shape[0]
    tk = FFT_K1_TILE
    a5 = a.reshape(p, 2, n, n, c)
    return pl.pallas_call(
        functools.partial(_fft_mid_kernel, tk=tk, n=n),
        grid=(n // tk, p),
        in_specs=[pl.BlockSpec((1, 2, tk, n, c), lambda j, q: (q, 0, j, 0, 0)),
                  pl.BlockSpec((tk, 2 * n, 2 * n), lambda j, q: (j, 0, 0)),
                  pl.BlockSpec((tk, 2 * n, 2 * n), lambda j, q: (j, 0, 0)),
                  pl.BlockSpec((1, tk, 2 * n, c), lambda j, q: (order, j, 0, 0))],
        out_specs=pl.BlockSpec((1, tk, 2 * n, c), lambda j, q: (q, j, 0, 0)),
        out_shape=jax.ShapeDtypeStruct((p, n, 2 * n, c), BF16),
        compiler_params=_cp("parallel", "parallel"),
        name="fft_mid",
    )(a5, g, hm, khat)


def _fft_s3_kernel(*refs, gated, chained):
    c_ref, f_ref, v_ref, b_ref = refs[:4]
    rest = list(refs[4:])
    y = _dot(f_ref[...], c_ref[0]) + v_ref[0].astype(F32) * b_ref[...]
    if gated:
        y = rest.pop(0)[0].astype(F32) * y
    f1_ref = rest.pop(0) if chained else None
    o_ref = rest.pop(0)
    z = y.astype(o_ref.dtype)
    o_ref[0] = z
    if chained:
        a_ref = rest.pop(0)
        a_ref[0] = _dot(f1_ref[...], z).astype(a_ref.dtype)


def _fft_s3(cm, f3, vin, xg, bias, n, c, f1_next=None):
    p = cm.shape[0]
    tc = FFT_N2_TILE * c
    c2 = cm.reshape(p, 2 * n, n * c)
    bias_t = jnp.tile(bias.reshape(1, c), (1, FFT_N2_TILE))
    blk = pl.BlockSpec((1, n, tc), lambda q, j: (q, 0, j))
    full = lambda a: pl.BlockSpec(a.shape, lambda q, j: (0, 0))
    in_specs = [pl.BlockSpec((1, 2 * n, tc), lambda q, j: (q, 0, j)), full(f3), blk,
                pl.BlockSpec((1, tc), lambda q, j: (0, 0))]
    args = [c2, f3, vin, bias_t]
    out_specs, out_shape = [blk], [jax.ShapeDtypeStruct((p, n, n * c), BF16)]
    if xg is not None:
        in_specs.append(blk)
        args.append(xg)
    if f1_next is not None:
        in_specs.append(full(f1_next))
        args.append(f1_next)
        out_specs.append(pl.BlockSpec((1, 2 * n, tc), lambda q, j: (q, 0, j)))
        out_shape.append(jax.ShapeDtypeStruct((p, 2 * n, n * c), BF16))
    return pl.pallas_call(
        functools.partial(_fft_s3_kernel, gated=xg is not None, chained=f1_next is not None),
        grid=(p, n * c // tc),
        in_specs=in_specs,
        out_specs=out_specs,
        out_shape=out_shape,
        compiler_params=_cp("parallel", "parallel"),
        name="fft_s3",
    )(*args)


def _hyena_long(v, x1, kk, hy_bias):
    nb, l, c = v.shape
    n = math.isqrt(2 * l)
    assert n * n == 2 * l and nb % 2 == 0
    p = nb // 2
    f1d, f1r, g, hm, f3 = (jnp.asarray(t, F32).astype(BF16) for t in _dft_tables(n))
    pair = lambda a: a.reshape(p, n, n * c)
    khat = _fft_spectrum(_fft_s1(kk, f1r, n, c), g, n, c)
    v = pair(v)
    cm = _fft_mid(_fft_s1(v, f1d, n, c), g, hm, khat, 0, n, c)
    z1, a = _fft_s3(cm, f3, v, pair(x1), hy_bias[0], n, c, f1_next=f1d)
    cm = _fft_mid(a, g, hm, khat, 1, n, c)
    (w2,) = _fft_s3(cm, f3, z1, None, hy_bias[1], n, c)
    return w2.reshape(nb, l, c)


@functools.lru_cache(maxsize=None)
def _ctx_dft_tables(lc):
    m = 2 * lc
    k = np.arange(m)
    th = 2.0 * math.pi * np.outer(k, k) / m
    c, s = np.cos(th), np.sin(th)
    f_data = np.block([[c[:, :lc], s[:, :lc]], [-s[:, :lc], c[:, :lc]]])
    f_real = np.concatenate([c, -s], axis=0)
    ci, si = c[:lc, :], s[:lc, :]
    f_inv = np.block([[ci, -si], [si, ci]]) / m
    return f_data, f_real, f_inv


def _ctxconv_kernel(v_ref, x1_ref, x2_ref, kk_ref, fd_ref, fr_ref, fi_ref, b_ref, o_ref, *, m):
    def conv(u, order):
        kh = _dot(fr_ref[...], kk_ref[order].astype(BF16))
        t = _dot(fd_ref[...], u.astype(BF16))
        tr, ti, kr, ki = t[:m], t[m:], kh[:m], kh[m:]
        y = jnp.concatenate([tr * kr - ti * ki, tr * ki + ti * kr], axis=0).astype(BF16)
        return _dot(fi_ref[...], y)

    v = v_ref[0].astype(F32)
    z1 = x1_ref[0].astype(F32) * (conv(v, 0) + v * b_ref[0:1, :])
    o_ref[0] = (x2_ref[0].astype(F32) * (conv(z1, 1) + z1 * b_ref[1:2, :])).astype(o_ref.dtype)


def _hyena_ctx(v, x1, x2, kk, hy_bias):
    nb, lc, c = v.shape
    p, m = nb // 2, 2 * lc
    fd, fr, fi = (jnp.asarray(t, F32).astype(BF16) for t in _ctx_dft_tables(lc))
    pair = lambda a: a.reshape(p, m, c)
    blk = pl.BlockSpec((1, m, c), lambda q: (q, 0, 0))
    full = lambda a: pl.BlockSpec(a.shape, lambda q: (0,) * a.ndim)
    z = pl.pallas_call(
        functools.partial(_ctxconv_kernel, m=m),
        grid=(p,),
        in_specs=[blk, blk, blk, full(kk), full(fd), full(fr), full(fi), full(hy_bias)],
        out_specs=blk,
        out_shape=jax.ShapeDtypeStruct((p, m, c), BF16),
        compiler_params=_cp("parallel"),
        name="hy_ctx",
    )(pair(v), pair(x1), pair(x2), kk, fd, fr, fi, hy_bias)
    return z.reshape(nb, lc, c)


def _od_inproj_kernel(prev_ref, h_ref, next_ref, mod_ref, g_ref, w_ref, cw_ref, cb_ref,
                      gate_ref, xc_ref, *, d, r, n_lat_tiles, n_tiles):
    i = pl.program_id(1)
    seg_first = jnp.logical_or(i == 0, i == n_lat_tiles)
    seg_last = jnp.logical_or(i == n_lat_tiles - 1, i == n_tiles - 1)
    m = mod_ref[0]
    rows = jnp.concatenate([prev_ref[0, 0], h_ref[0], next_ref[0, 0]], axis=0)
    xn = _rms_mod(rows, g_ref[...], m[:, 0:d], m[:, d:2 * d]).astype(BF16)
    y = _dot(xn, w_ref[...])
    gate_ref[0] = _gelu_tanh(y[SUBLANES:SUBLANES + TT, :r]).astype(gate_ref.dtype)
    x = y[:, r:]
    row = lax.broadcasted_iota(jnp.int32, (x.shape[0], 1), 0)
    outside = jnp.logical_or(jnp.logical_and(row < SUBLANES, seg_first),
                             jnp.logical_and(row >= SUBLANES + TT, seg_last))
    x = jnp.where(outside, 0.0, x)
    xc_ref[0] = _conv_taps(x, cw_ref[...], cb_ref[...], RG_CONV_LEFT)


def _od_inproj(h, mods, rows, layer, g, w_in, conv_w, conv_b, *, n_lat_tiles):
    nb, s, d = h.shape
    r = w_in.shape[1] // 2
    n_tiles = s // TT
    n8, per = s // SUBLANES, TT // SUBLANES
    h8 = h.reshape(nb, n8, SUBLANES, d)
    ospec = pl.BlockSpec((1, TT, r), lambda b, i: (b, i, 0))
    return pl.pallas_call(
        functools.partial(_od_inproj_kernel, d=d, r=r, n_lat_tiles=n_lat_tiles, n_tiles=n_tiles),
        grid=(nb, n_tiles),
        in_specs=[
            pl.BlockSpec((1, 1, SUBLANES, d), lambda b, i: (b, jnp.maximum(i * per - 1, 0), 0, 0)),
            pl.BlockSpec((1, TT, d), lambda b, i: (b, i, 0)),
            pl.BlockSpec((1, 1, SUBLANES, d),
                         lambda b, i: (b, jnp.minimum((i + 1) * per, n8 - 1), 0, 0)),
            _mod_spec(layer, rows, nb, n_lat_tiles, mods.shape[-1]),
            pl.BlockSpec((1, d), lambda b, i: (0, 0)),
            pl.BlockSpec(w_in.shape, lambda b, i: (0, 0)),
            pl.BlockSpec(conv_w.shape, lambda b, i: (0, 0)),
            pl.BlockSpec((1, r), lambda b, i: (0, 0)),
        ],
        out_specs=[ospec, ospec],
        out_shape=[jax.ShapeDtypeStruct((nb, s, r), BF16), jax.ShapeDtypeStruct((nb, s, r), F32)],
        compiler_params=_cp("parallel", "parallel"),
        name="od_inproj",
    )(h8, h, h8, mods, g, w_in, conv_w, conv_b.reshape(1, r))


def _rglru_kernel(x_ref, w_ref, b_ref, lam_ref, o_ref, a_ref, bb_ref, carry_ref, *, r, windows):
    dr = pl.program_id(1)
    i = pl.program_id(2)
    part = TT // RG_PARTS
    nl = -lam_ref[0]
    softplus = jnp.maximum(nl, 0.0) + jnp.log1p(jnp.exp(-jnp.abs(nl)))
    neg_rate = (0.5 * RG_C) * softplus
    exp2_rate = (-0.5 * RG_C * math.log2(math.e)) * softplus

    def gates(r0):
        rs = slice(r0, r0 + part)
        xc = x_ref[0, rs, :]
        xb = xc.astype(BF16)
        half_x = 0.5 * xc
        for c0, c1, k0, k1 in windows:
            xk = xb[:, k0:k1]
            tr = jnp.tanh(_dot(xk, w_ref[0, k0:k1, c0:c1]) + b_ref[0, :, c0:c1]) + 1.0
            ti = jnp.tanh(_dot(xk, w_ref[0, k0:k1, r + c0:r + c1]) + b_ref[0, :, r + c0:r + c1]) + 1.0
            a = jnp.exp2(exp2_rate[:, c0:c1] * tr)
            a_ref[rs, c0:c1] = a
            e = jnp.tanh(neg_rate[:, c0:c1] * tr) * (a * a + 1.0)
            root = jnp.where(e > 0.0, e * lax.rsqrt(e), 0.0)
            bb_ref[rs, c0:c1] = root * (ti * half_x[:, c0:c1])

    def scan_rows(rows, h):
        for row in rows:
            h = a_ref[row:row + 1, :] * h + bb_ref[row:row + 1, :]
            o_ref[0, 0, row:row + 1, :] = h
        return h

    def scan_loop(r0, reverse, h):
        n_blk = part // SUBLANES

        def block(g, h):
            base = pl.multiple_of(r0 + (n_blk - 1 - g if reverse else g) * SUBLANES, SUBLANES)
            for k in range(SUBLANES):
                row = base + (SUBLANES - 1 - k if reverse else k)
                h = a_ref[pl.ds(row, 1), :] * h + bb_ref[pl.ds(row, 1), :]
                o_ref[0, 0, pl.ds(row, 1), :] = h
            return h

        return lax.fori_loop(0, n_blk, block, h)

    @pl.when(i == 0)
    def _():
        carry_ref[...] = jnp.zeros_like(carry_ref)

    def run(reverse):
        starts = [k * part for k in (reversed(range(RG_PARTS)) if reverse else range(RG_PARTS))]
        gates(starts[0])
        h = carry_ref[0:1, :]
        for k, r0 in enumerate(starts[:-1]):
            gates(starts[k + 1])
            rows = range(r0 + part - 1, r0 - 1, -1) if reverse else range(r0, r0 + part)
            h = scan_rows(rows, h)
        carry_ref[0:1, :] = scan_loop(starts[-1], reverse, h)

    pl.when(dr == 0)(lambda: run(False))
    pl.when(dr == 1)(lambda: run(True))


def _scan_tile(dr, i, n_lat_tiles, n_tiles):
    fwd = jnp.where(i == 0, n_lat_tiles, i - 1)
    bwd = jnp.where(i == 0, n_lat_tiles, n_lat_tiles - i)
    return jnp.where(dr == 0, fwd, bwd)


def _gate_windows(r, bs):
    out = []
    for c0 in range(0, r, 2 * LANES):
        c1 = min(c0 + 2 * LANES, r)
        k0 = (c0 // bs) * bs // LANES * LANES
        k1 = min(-(-(((c1 - 1) // bs + 1) * bs) // LANES) * LANES, r)
        out.append((c0, c1, k0, k1))
    return tuple(out)


def _rglru(xc, wcat, bcat, lam, *, n_lat_tiles, block_size):
    nb, s, r = xc.shape
    n_tiles = s // TT
    assert n_tiles == n_lat_tiles + 1
    tile = lambda d, i: _scan_tile(d, i, n_lat_tiles, n_tiles)
    return pl.pallas_call(
        functools.partial(_rglru_kernel, r=r, windows=_gate_windows(r, block_size)),
        grid=(nb, 2, n_tiles),
        in_specs=[
            pl.BlockSpec((1, TT, r), lambda b, d, i: (b, tile(d, i), 0)),
            pl.BlockSpec((1, r, 2 * r), lambda b, d, i: (d, 0, 0)),
            pl.BlockSpec((1, 1, 2 * r), lambda b, d, i: (d, 0, 0)),
            pl.BlockSpec((1, 1, r), lambda b, d, i: (d, 0, 0)),
        ],
        out_specs=pl.BlockSpec((1, 1, TT, r), lambda b, d, i: (d, b, tile(d, i), 0)),
        out_shape=jax.ShapeDtypeStruct((2, nb, s, r), F32),
        scratch_shapes=[pltpu.VMEM((TT, r), F32), pltpu.VMEM((TT, r), F32),
                        pltpu.VMEM((SUBLANES, r), F32)],
        compiler_params=_cp("parallel", "parallel", "arbitrary"),
        name="rglru",
    )(xc, wcat, bcat, lam.reshape(2, 1, r))


def _block_diag(w):
    n, bs, _ = w.shape
    eye = jnp.eye(n, dtype=w.dtype)
    return (eye[:, None, :, None] * w[:, :, None, :]).reshape(n * bs, n * bs)


def _gelu_tanh(x):
    return 0.5 * x * (1.0 + jnp.tanh(math.sqrt(2.0 / math.pi) * (x + 0.044715 * (x * x * x))))


def _post_kernel(*refs, kind, final, d, n_lat_tiles):
    if kind == "even":
        (hl_ref, hc_ref, x2_ref, wl_ref, zc_ref, ol_ref, oc_ref, mod_ref, g2_ref, wo_ref, w1_ref,
         w2_ref) = refs[:12]
        rest = refs[12:]
        is_ctx = pl.program_id(1) >= n_lat_tiles
        z_lat = (x2_ref[0].astype(F32) * wl_ref[0].astype(F32)).astype(BF16)
        z = jnp.where(is_ctx, zc_ref[0], z_lat)
        o = jnp.where(is_ctx, oc_ref[0], ol_ref[0])
        y = _dot(jnp.concatenate([z, o], axis=-1), wo_ref[...])
    else:
        hl_ref, hc_ref, hd_ref, gate_ref, mod_ref, g2_ref, wo_ref, w1_ref, w2_ref = refs[:9]
        rest = refs[9:]
        mix = (hd_ref[0, 0] + hd_ref[1, 0]) * gate_ref[0].astype(F32)
        y = _dot(mix.astype(BF16), wo_ref[...])
    out_ref = rest[-1]
    m = mod_ref[0]
    h1 = _stream_tile(hl_ref, hc_ref, n_lat_tiles) + m[:, 2 * d:3 * d] * y
    xn = _rms_mod(h1, g2_ref[...], m[:, 3 * d:4 * d], m[:, 4 * d:5 * d]).astype(BF16)
    acc = jnp.zeros_like(h1)
    dff = w1_ref.shape[1]
    for c0 in range(0, dff, FF_CHUNK):
        a = jnp.maximum(_dot(xn, w1_ref[:, c0:c0 + FF_CHUNK]), 0.0)
        acc = acc + _dot((a * a).astype(BF16), w2_ref[c0:c0 + FF_CHUNK, :])
    h2 = h1 + m[:, 5 * d:6 * d] * acc
    if final:
        fg_ref = rest[0]
        h2 = h2 * lax.rsqrt(jnp.mean(h2 * h2, axis=-1, keepdims=True) + EPS) * fg_ref[...]
    out_ref[0] = h2


def _post(kind, h, mix_args, mods, rows, layer, g2, w_out, w1, w2, final_g, *, n_lat_tiles, final):
    h_specs, h_args = _stream_specs(h, n_lat_tiles)
    nb, _, d = h_args[0].shape
    n_tiles = n_lat_tiles if final else n_lat_tiles + 1
    tok = lambda w: pl.BlockSpec((1, TT, w), lambda b, i: (b, i, 0))
    full = lambda a: pl.BlockSpec(a.shape, lambda b, i: (0,) * a.ndim)
    if kind == "even":
        lat = lambda w: pl.BlockSpec((1, TT, w), lambda b, i: (b, jnp.minimum(i, n_lat_tiles - 1), 0))
        ctx = lambda w: pl.BlockSpec((1, TT, w), lambda b, i: (b, 0, 0))
        x2_lat, w_lat, z_ctx, o_lat, o_ctx = mix_args
        mix_specs = [lat(x2_lat.shape[-1]), lat(w_lat.shape[-1]), ctx(z_ctx.shape[-1]),
                     lat(o_lat.shape[-1]), ctx(o_ctx.shape[-1])]
    else:
        hd, gate = mix_args
        r = gate.shape[-1]
        mix_specs = [pl.BlockSpec((2, 1, TT, r), lambda b, i: (0, b, i, 0)), tok(r)]
    in_specs = h_specs + mix_specs + [
        _mod_spec(layer, rows, nb, n_lat_tiles, mods.shape[-1]),
        pl.BlockSpec((1, d), lambda b, i: (0, 0)), full(w_out), full(w1), full(w2)]
    args = [*h_args, *mix_args, mods, g2, w_out, w1, w2]
    if final:
        in_specs.append(pl.BlockSpec((1, d), lambda b, i: (0, 0)))
        args.append(final_g)
    return pl.pallas_call(
        functools.partial(_post_kernel, kind=kind, final=final, d=d, n_lat_tiles=n_lat_tiles),
        grid=(nb, n_tiles),
        in_specs=in_specs,
        out_specs=tok(d),
        out_shape=jax.ShapeDtypeStruct((nb, n_tiles * TT, d), F32),
        compiler_params=_cp("parallel", "parallel"),
        name="post_" + kind,
    )(*args)


@functools.lru_cache(maxsize=None)
def _rope_tables(l, lc, head_dim):
    axis = head_dim // 2
    freqs = ROPE_BASE ** (-np.arange(0, axis, 2, dtype=np.float64) / axis)
    freqs = freqs.astype(np.float32).astype(np.float64)
    t = np.arange(l)
    ang_r = (t // GRID_W)[:, None] * freqs
    ang_c = (t % GRID_W)[:, None] * freqs
    cos = np.concatenate([np.cos(ang_r)] * 2 + [np.cos(ang_c)] * 2, axis=-1)
    sin = np.concatenate([-np.sin(ang_r), np.sin(ang_r), -np.sin(ang_c), np.sin(ang_c)], axis=-1)
    cos = np.concatenate([cos, np.ones((lc, head_dim))], axis=0)
    sin = np.concatenate([sin, np.zeros((lc, head_dim))], axis=0)
    rep = LANES // head_dim
    return np.tile(cos, (1, rep)), np.tile(sin, (1, rep))


def kernel(x, c, ctx, c_ctx, ada_w, ada_b, norm1_g, norm2_g, mlp_w1, mlp_w2, final_g, ev_w_in, ev_w_out, hy_short_w, hy_short_b, hy_f_w1, hy_f_b1, hy_f_w2, hy_f_b2, hy_f_w3, hy_f_b3, hy_f_freq, hy_f_decay, hy_bias, df_lq1, df_lk1, df_lq2, df_lk2, df_subln_g, od_w_in, od_w_out, rg_conv_w, rg_conv_b, rg_wa, rg_ba, rg_wx, rg_bx, rg_lam):
    nb, l, d = x.shape
    lc = ctx.shape[1]
    depth = ada_w.shape[0]
    hy = hy_bias.shape[-1]
    head_dim = df_lq1.shape[-1]
    qk = (ev_w_in.shape[-1] - 3 * hy) // 3
    r = rg_lam.shape[-1]
    assert l % TT == 0 and lc == TT and l % GRID_W == 0
    n_lat_tiles = l // TT

    rows = -(-(nb + 1) // SUBLANES) * SUBLANES
    cond = jnp.zeros((rows, d), F32).at[:nb].set(c).at[nb].set(c_ctx)
    mods = _ada_mods(cond, ada_w, ada_b).reshape(depth * rows, 1, ada_w.shape[-1])

    cos_t, sin_t = (jnp.asarray(t, F32) for t in _rope_tables(l, lc, head_dim))
    h = (x, ctx)
    w1_bf, w2_bf = mlp_w1.astype(BF16), mlp_w2.astype(BF16)

    for i in range(depth):
        j = i // 2
        final = i == depth - 1
        g1, g2 = norm1_g[i].reshape(1, d), norm2_g[i].reshape(1, d)
        if i % 2 == 0:
            lam_init = 0.8 - 0.6 * math.exp(-0.3 * i)
            u, qt, k, vt = _ev_inproj(h, mods, rows, i, g1, ev_w_in[j].astype(BF16), cos_t, sin_t,
                                      n_lat_tiles=n_lat_tiles, hy3=3 * hy, qk=qk, head_dim=head_dim,
                                      tk=_attn_chunk(l + lc))
            lam_vecs = [a[j].reshape(1, -1) for a in (df_lq1, df_lk1, df_lq2, df_lk2)]
            o_lat, o_ctx = _diff_attention(qt, k, vt, lam_vecs, df_subln_g[j].reshape(1, -1),
                                           n_lat_tiles=n_lat_tiles, lam_init=lam_init,
                                           head_dim=head_dim)
            fparams = (hy_f_w1[j], hy_f_b1[j], hy_f_w2[j], hy_f_b2[j], hy_f_w3[j], hy_f_b3[j],
                       hy_f_freq[j], hy_f_decay[j])
            vl, x1l, x2l = _shortconv(u, hy_short_w[j], hy_short_b[j], tile0=0,
                                      n_seg_tiles=n_lat_tiles)
            vc, x1c, x2c = _shortconv(u, hy_short_w[j], hy_short_b[j], tile0=n_lat_tiles,
                                      n_seg_tiles=lc // TT)
            w_lat = _hyena_long(vl, x1l, _hyena_filter(l, math.isqrt(2 * l), *fparams), hy_bias[j])
            z_ctx = _hyena_ctx(vc, x1c, x2c, _hyena_filter(lc, 1, *fparams), hy_bias[j])
            h = _post("even", h, (x2l, w_lat, z_ctx, o_lat, o_ctx), mods, rows, i, g2, ev_w_out[j].astype(BF16),
                      w1_bf[i], w2_bf[i], final_g.reshape(1, d), n_lat_tiles=n_lat_tiles,
                      final=final)
        else:
            gate, xc = _od_inproj(h, mods, rows, i, g1, od_w_in[j].astype(BF16), rg_conv_w[j],
                                  rg_conv_b[j], n_lat_tiles=n_lat_tiles)
            wcat = (0.5 * jnp.stack(
                [jnp.concatenate([_block_diag(rg_wa[j, dd]), _block_diag(rg_wx[j, dd])], axis=1)
                 for dd in range(2)])).astype(BF16)
            bcat = 0.5 * jnp.concatenate([rg_ba[j], rg_bx[j]], axis=-1).reshape(2, 1, 2 * r)
            hd = _rglru(xc, wcat, bcat, rg_lam[j], n_lat_tiles=n_lat_tiles,
                        block_size=rg_wa.shape[-1])
            h = _post("odd", h, (hd, gate), mods, rows, i, g2, od_w_out[j].astype(BF16),
                      w1_bf[i], w2_bf[i], final_g.reshape(1, d), n_lat_tiles=n_lat_tiles,
                      final=final)
    return h
```

```python
import functools
import math

import numpy as np
import jax
import jax.numpy as jnp
from jax import lax
from jax.experimental import pallas as pl
from jax.experimental.pallas import tpu as pltpu

F32 = jnp.float32
BF16 = jnp.bfloat16
HIGHEST = lax.Precision.HIGHEST

EPS = 1e-6
GRID_W = 64
ROPE_BASE = 10000.0
N_BANDS = 16
RG_C = 8.0
RG_CONV_LEFT = 2
HY_SHORT_LEFT = 1

TT = 256
LANES = 128
SUBLANES = 8
VMEM_LIMIT = 56 * 1024 * 1024
FF_CHUNK = 1024
FFT_N2_TILE = 16
FFT_K1_TILE = 16
ADA_K_TILE = 256
ATTN_SUB_Q = 512
RG_PARTS = 2
FILT_COL_BLOCKS = 4
VT_ROWS = LANES + 16


def _cp(*sem):
    return pltpu.CompilerParams(dimension_semantics=sem, vmem_limit_bytes=VMEM_LIMIT)


def _dot(a, b, **kw):
    return jnp.dot(a, b, preferred_element_type=F32, **kw)


def _sigmoid(x):
    return 1.0 / (1.0 + jnp.exp(-x))


def _rms_mod(x, g, shift, scale):
    y = x * lax.rsqrt(jnp.mean(x * x, axis=-1, keepdims=True) + EPS)
    return (y * g) * (1.0 + scale) + shift


def _ada_kernel(c_ref, w_ref, b_ref, o_ref):
    k = pl.program_id(1)
    c = c_ref[k]
    part = _dot((c * _sigmoid(c)).astype(BF16), w_ref[0].astype(BF16))

    @pl.when(k == 0)
    def _():
        o_ref[0] = part + b_ref[0]

    @pl.when(k > 0)
    def _():
        o_ref[0] += part


def _ada_mods(cond, ada_w, ada_b):
    depth, d, n = ada_w.shape
    rows = cond.shape[0]
    tk = min(d, ADA_K_TILE)
    cond_k = cond.reshape(rows, d // tk, tk).transpose(1, 0, 2)
    return pl.pallas_call(
        _ada_kernel,
        grid=(depth, d // tk),
        in_specs=[
            pl.BlockSpec(cond_k.shape, lambda l, k: (0, 0, 0)),
            pl.BlockSpec((1, tk, n), lambda l, k: (l, k, 0)),
            pl.BlockSpec((1, 1, n), lambda l, k: (l, 0, 0)),
        ],
        out_specs=pl.BlockSpec((1, rows, n), lambda l, k: (l, 0, 0)),
        out_shape=jax.ShapeDtypeStruct((depth, rows, n), F32),
        compiler_params=_cp("parallel", "arbitrary"),
        name="ada_mods",
    )(cond_k, ada_w, ada_b.reshape(depth, 1, n))


def _stream_specs(h, n_lat_tiles):
    separate = isinstance(h, tuple)
    lat_arr, ctx_arr = h if separate else (h, h)
    ctx_blk = 0 if separate else n_lat_tiles
    d = lat_arr.shape[-1]
    lat = pl.BlockSpec((1, TT, d), lambda b, i: (b, jnp.minimum(i, n_lat_tiles - 1), 0))
    ctx = pl.BlockSpec((1, TT, d), lambda b, i: (b, ctx_blk, 0))
    return [lat, ctx], [lat_arr, ctx_arr]


def _stream_tile(hl_ref, hc_ref, n_lat_tiles):
    return jnp.where(pl.program_id(1) >= n_lat_tiles, hc_ref[0], hl_ref[0])


def _mod_spec(layer, rows, nb, n_lat_tiles, n6):
    def imap(b, i):
        return (layer * rows + jnp.where(i >= n_lat_tiles, nb, b), 0, 0)
    return pl.BlockSpec((1, 1, n6), imap)


def _ev_inproj_kernel(hl_ref, hc_ref, mod_ref, g_ref, w_ref, cos_ref, sin_ref,
                      u_ref, qt_ref, k_ref, vt_ref, *, d, hy3, qk, qscale, n_lat_tiles):
    m = mod_ref[0]
    x = _stream_tile(hl_ref, hc_ref, n_lat_tiles)
    xn = _rms_mod(x, g_ref[...], m[:, 0:d], m[:, d:2 * d]).astype(BF16)
    cos = cos_ref[...]
    sin = sin_ref[...]
    lane = lax.broadcasted_iota(jnp.int32, cos.shape, 1)
    first = (lane % 32) < 16

    def rope(z):
        sw = jnp.where(first, pltpu.roll(z, LANES - 16, 1), pltpu.roll(z, 16, 1))
        return z * cos + sw * sin

    extra = vt_ref.shape[3] - LANES
    ones_row = jnp.where(lax.broadcasted_iota(jnp.int32, (extra, cos.shape[0]), 0) == 0,
                         1.0, 0.0).astype(BF16)
    yq = _dot(xn, w_ref[:, hy3:hy3 + qk])
    for c in range(qk // LANES):
        qt_ref[0, c] = (rope(yq[:, c * LANES:(c + 1) * LANES]) * qscale).T.astype(BF16)
    yk = _dot(xn, w_ref[:, hy3 + qk:hy3 + 2 * qk])
    for c in range(qk // LANES):
        k_ref[0, :, c * LANES:(c + 1) * LANES] = rope(yk[:, c * LANES:(c + 1) * LANES]).astype(BF16)
    yv = _dot(xn, w_ref[:, hy3 + 2 * qk:])
    for c in range(qk // LANES):
        vt_ref[0, c, 0, 0:LANES, :] = yv[:, c * LANES:(c + 1) * LANES].T.astype(BF16)
        vt_ref[0, c, 0, LANES:, :] = ones_row
    u_ref[0] = _dot(xn, w_ref[:, :hy3]).astype(u_ref.dtype)


def _ev_inproj(h, mods, rows, layer, g, w_in, cos_t, sin_t, *, n_lat_tiles, hy3, qk, head_dim, tk):
    h_specs, h_args = _stream_specs(h, n_lat_tiles)
    nb, _, d = h_args[0].shape
    s = (n_lat_tiles + 1) * TT
    n_in = w_in.shape[1]
    assert 2 * head_dim == LANES
    heads, per = qk // LANES, tk // TT
    kern = functools.partial(_ev_inproj_kernel, d=d, hy3=hy3, qk=qk, n_lat_tiles=n_lat_tiles,
                             qscale=head_dim ** -0.5 * math.log2(math.e))
    return pl.pallas_call(
        kern,
        grid=(nb, s // TT),
        in_specs=h_specs + [
            _mod_spec(layer, rows, nb, n_lat_tiles, mods.shape[-1]),
            pl.BlockSpec((1, d), lambda b, i: (0, 0)),
            pl.BlockSpec((d, n_in), lambda b, i: (0, 0)),
            pl.BlockSpec((TT, LANES), lambda b, i: (i, 0)),
            pl.BlockSpec((TT, LANES), lambda b, i: (i, 0)),
        ],
        out_specs=[
            pl.BlockSpec((1, TT, hy3), lambda b, i: (b, i, 0)),
            pl.BlockSpec((1, heads, LANES, TT), lambda b, i: (b, 0, 0, i)),
            pl.BlockSpec((1, TT, qk), lambda b, i: (b, i, 0)),
            pl.BlockSpec((1, heads, 1, VT_ROWS, TT), lambda b, i: (b, 0, i // per, 0, i % per)),
        ],
        out_shape=[
            jax.ShapeDtypeStruct((nb, s, hy3), BF16),
            jax.ShapeDtypeStruct((nb, heads, LANES, s), BF16),
            jax.ShapeDtypeStruct((nb, s, qk), BF16),
            jax.ShapeDtypeStruct((nb, heads, s // tk, VT_ROWS, tk), BF16),
        ],
        compiler_params=_cp("parallel", "parallel"),
        name="ev_inproj",
    )(*h_args, mods, g, w_in, cos_t, sin_t)


def _split_maps(qt, half):
    row = lax.broadcasted_iota(jnp.int32, qt.shape, 0)
    zero = jnp.zeros_like(qt)
    return jnp.where(row < half, qt, zero), jnp.where(row >= half, qt, zero)


def _attn_out(a0, a1, lam_refs, sg_ref, lam_init, dv):
    lq1_ref, lk1_ref, lq2_ref, lk2_ref = lam_refs
    lam = (jnp.exp(jnp.sum(lq1_ref[...] * lk1_ref[...], axis=-1, keepdims=True))
           - jnp.exp(jnp.sum(lq2_ref[...] * lk2_ref[...], axis=-1, keepdims=True)) + lam_init)
    o = (a0[:dv] / a0[dv:dv + 1] - lam * (a1[:dv] / a1[dv:dv + 1])).T
    on = o * lax.rsqrt(jnp.mean(o * o, axis=-1, keepdims=True) + EPS)
    return on * sg_ref[...] * (1.0 - lam_init)


def _col_max8(sc):
    part = sc[0:SUBLANES]
    for g in range(1, sc.shape[0] // SUBLANES):
        part = jnp.maximum(part, sc[g * SUBLANES:(g + 1) * SUBLANES])
    return part


def _attn_kernel(qt_ref, qnt_ref, k_ref, vt_ref, lq1_ref, lk1_ref, lq2_ref, lk2_ref, sg_ref, o_ref,
                 s_ref, mp_ref, p_ref, acc_ref, m_ref, alpha_ref, *, n_chunks, tk, lam_init, half):
    i = pl.program_id(2)
    dv = 2 * half
    lam_refs = (lq1_ref, lk1_ref, lq2_ref, lk2_ref)
    tq = qnt_ref.shape[-1]
    q_sub = (_split_maps(qt_ref[0, 0, :, 0:tq], half), _split_maps(qt_ref[0, 0, :, tq:2 * tq], half))
    q_next = _split_maps(qnt_ref[0, 0], half)

    def put_scores(slot, qts, c):
        kc = k_ref[0, pl.ds(pl.multiple_of(c * tk, tk), tk), :]
        for j in range(2):
            sc = _dot(kc, qts[j])
            s_ref[slot, j] = sc
            mp_ref[slot, j] = _col_max8(sc)

    def put_probs(slot, sub, first):
        for j in range(2):
            col_max = jnp.max(mp_ref[slot, j], axis=0, keepdims=True)
            if first:
                m_new = col_max
            else:
                m_old = m_ref[sub, j]
                m_new = jnp.maximum(m_old, col_max)
                alpha_ref[slot, j] = jnp.exp2(m_old - m_new)
            m_ref[sub, j] = m_new
            p_ref[slot, j] = jnp.exp2(s_ref[slot, j] - m_new).astype(BF16)

    def add_pv(slot, sub, c, first):
        for j in range(2):
            pv = _dot(vt_ref[0, 0, c], p_ref[slot, j])
            acc_ref[sub, j] = pv if first else alpha_ref[slot, j] * acc_ref[sub, j] + pv

    def sub_tile(sub, parity, qts, qts_after, sub_after):
        slot = lambda c: (c + parity) % 2
        put_scores(slot(0), qts, 2)
        put_probs(slot(1), sub, False)
        add_pv(slot(0), sub, 0, True)
        n_uniform = n_chunks - 3

        unroll = 2

        def group(u, carry):
            c = 1 + unroll * u
            for k in range(unroll):
                put_scores(slot(1 + k), qts, c + k + 2)
                put_probs(slot(k), sub, False)
                add_pv(slot(1 + k), sub, c + k, False)
            return carry

        lax.fori_loop(0, n_uniform // unroll, group, 0)
        c = n_chunks - 2
        put_scores(slot(c), qts_after, 0)
        put_probs(slot(c + 1), sub, False)
        add_pv(slot(c), sub, c, False)
        c = n_chunks - 1
        put_scores(slot(c), qts_after, 1)
        put_probs(slot(c + 1), sub_after, True)
        add_pv(slot(c), sub, c, False)
        o_ref[0, sub * tq:(sub + 1) * tq, :] = _attn_out(
            acc_ref[sub, 0], acc_ref[sub, 1], lam_refs, sg_ref, lam_init, dv).astype(o_ref.dtype)

    @pl.when(i == 0)
    def _():
        put_scores(0, q_sub[0], 0)
        put_probs(0, 0, True)
        put_scores(1, q_sub[0], 1)

    sub_tile(0, 0, q_sub[0], q_sub[1], 1)
    sub_tile(1, 1, q_sub[1], q_next, 0)


def _ctx_attn_kernel(qt_ref, k_ref, vt_ref, lq1_ref, lk1_ref, lq2_ref, lk2_ref, sg_ref, o_ref,
                     *, lam_init, half):
    acc = []
    for qtj in _split_maps(qt_ref[0, 0], half):
        sc = _dot(k_ref[0], qtj)
        col_max = jnp.max(_col_max8(sc), axis=0, keepdims=True)
        acc.append(_dot(vt_ref[0, 0, 0], jnp.exp2(sc - col_max).astype(BF16)))
    o_ref[0] = _attn_out(acc[0], acc[1], (lq1_ref, lk1_ref, lq2_ref, lk2_ref), sg_ref, lam_init,
                         2 * half).astype(o_ref.dtype)


def _attn_chunk(s):
    return next(t for t in (3 * TT, TT) if s % t == 0 and (s // t) % 2 == 1 and s // t >= 5)


def _diff_attention(qt, k, vt, lam_vecs, subln_g, *, n_lat_tiles, lam_init, head_dim):
    nb, s, qk = k.shape
    dv = 2 * head_dim
    heads = qk // dv
    l = n_lat_tiles * TT
    n_chunks, tk = vt.shape[2], vt.shape[4]
    tq = ATTN_SUB_Q if l % (2 * ATTN_SUB_Q) == 0 else TT
    assert l % (2 * tq) == 0 and n_chunks % 2 == 1 and n_chunks >= 5 and s - l == TT
    n_steps = l // (2 * tq)
    vec = lambda n: pl.BlockSpec((1, n), lambda b, h, i: (0, 0))
    vecs = [vec(head_dim)] * 4 + [vec(dv)]
    o_lat = pl.pallas_call(
        functools.partial(_attn_kernel, n_chunks=n_chunks, tk=tk, lam_init=lam_init, half=head_dim),
        grid=(nb, heads, n_steps),
        in_specs=[
            pl.BlockSpec((1, 1, dv, 2 * tq), lambda b, h, i: (b, h, 0, i)),
            pl.BlockSpec((1, 1, dv, tq),
                         lambda b, h, i: (b, h, 0, jnp.minimum(2 * i + 2, 2 * n_steps - 1))),
            pl.BlockSpec((1, s, dv), lambda b, h, i: (b, 0, h)),
            pl.BlockSpec((1, 1, n_chunks, VT_ROWS, tk), lambda b, h, i: (b, h, 0, 0, 0)),
        ] + vecs,
        out_specs=pl.BlockSpec((1, 2 * tq, dv), lambda b, h, i: (b, i, h)),
        out_shape=jax.ShapeDtypeStruct((nb, l, qk), BF16),
        scratch_shapes=[pltpu.VMEM((2, 2, tk, tq), F32), pltpu.VMEM((2, 2, SUBLANES, tq), F32),
                        pltpu.VMEM((2, 2, tk, tq), BF16), pltpu.VMEM((2, 2, VT_ROWS, tq), F32),
                        pltpu.VMEM((2, 2, 1, tq), F32), pltpu.VMEM((2, 2, 1, tq), F32)],
        compiler_params=_cp("parallel", "parallel", "arbitrary"),
        name="diff_attn",
    )(qt, qt, k, vt, *lam_vecs, subln_g)
    per = tk // TT
    vec2 = lambda n: pl.BlockSpec((1, n), lambda b, h: (0, 0))
    o_ctx = pl.pallas_call(
        functools.partial(_ctx_attn_kernel, lam_init=lam_init, half=head_dim),
        grid=(nb, heads),
        in_specs=[pl.BlockSpec((1, 1, dv, TT), lambda b, h: (b, h, 0, n_lat_tiles)),
                  pl.BlockSpec((1, TT, dv), lambda b, h: (b, n_lat_tiles, h)),
                  pl.BlockSpec((1, 1, 1, VT_ROWS, TT),
                               lambda b, h: (b, h, n_lat_tiles // per, 0, n_lat_tiles % per)),
                  ] + [vec2(head_dim)] * 4 + [vec2(dv)],
        out_specs=pl.BlockSpec((1, TT, dv), lambda b, h: (b, 0, h)),
        out_shape=jax.ShapeDtypeStruct((nb, TT, qk), BF16),
        compiler_params=_cp("parallel", "parallel"),
        name="ctx_attn",
    )(qt, k, vt, *lam_vecs, subln_g)
    return o_lat, o_ctx


def _halo_rows(dtype):
    return SUBLANES * 4 // jnp.dtype(dtype).itemsize


def _halo_fill(xp_ref, prev_ref, x_ref, next_ref, has_prev, has_next):
    hr = prev_ref.shape[2]
    zero = jnp.zeros((hr, xp_ref.shape[-1]), F32)
    xp_ref[hr:hr + TT, :] = x_ref[0].astype(F32)
    xp_ref[0:hr, :] = jnp.where(has_prev, prev_ref[0, 0].astype(F32), zero)
    xp_ref[hr + TT:2 * hr + TT, :] = jnp.where(has_next, next_ref[0, 0].astype(F32), zero)


def _conv_taps(xp, w, bias, left):
    rows = xp.shape[0]
    hr = (rows - TT) // 2
    before = None
    for j in range(left):
        z = w[j:j + 1, :] * xp
        before = pltpu.roll(z if before is None else before + z, 1, 0)
    after = None
    for j in range(w.shape[0] - 1, left, -1):
        z = w[j:j + 1, :] * xp
        after = pltpu.roll(z if after is None else after + z, rows - 1, 0)
    acc = w[left:left + 1, :] * xp + bias
    for part in (before, after):
        if part is not None:
            acc = acc + part
    return acc[hr:hr + TT, :]


def _shortconv_kernel(prev_ref, x_ref, next_ref, w_ref, b_ref, v_ref, x1_ref, x2_ref, xp_ref,
                      *, n_seg_tiles, hy):
    i = pl.program_id(1)
    _halo_fill(xp_ref, prev_ref, x_ref, next_ref, i > 0, i < n_seg_tiles - 1)
    y = _conv_taps(xp_ref[...], w_ref[...], b_ref[...], HY_SHORT_LEFT)
    v_ref[0] = y[:, :hy].astype(v_ref.dtype)
    x1_ref[0] = y[:, hy:2 * hy].astype(x1_ref.dtype)
    x2_ref[0] = y[:, 2 * hy:].astype(x2_ref.dtype)


def _shortconv(u, w, b, *, tile0, n_seg_tiles):
    nb, s, hy3 = u.shape
    hy = hy3 // 3
    hr = _halo_rows(u.dtype)
    nh, per = s // hr, TT // hr
    uh = u.reshape(nb, nh, hr, hy3)
    prev = pl.BlockSpec((1, 1, hr, hy3),
                        lambda b, i: (b, jnp.maximum((tile0 + i) * per - 1, 0), 0, 0))
    nxt = pl.BlockSpec((1, 1, hr, hy3),
                       lambda b, i: (b, jnp.minimum((tile0 + i + 1) * per, nh - 1), 0, 0))
    out = jax.ShapeDtypeStruct((nb, n_seg_tiles * TT, hy), BF16)
    ospec = pl.BlockSpec((1, TT, hy), lambda b, i: (b, i, 0))
    return pl.pallas_call(
        functools.partial(_shortconv_kernel, n_seg_tiles=n_seg_tiles, hy=hy),
        grid=(nb, n_seg_tiles),
        in_specs=[
            prev,
            pl.BlockSpec((1, TT, hy3), lambda b, i: (b, tile0 + i, 0)),
            nxt,
            pl.BlockSpec(w.shape, lambda b, i: (0, 0)),
            pl.BlockSpec((1, hy3), lambda b, i: (0, 0)),
        ],
        out_specs=[ospec, ospec, ospec],
        out_shape=[out, out, out],
        scratch_shapes=[pltpu.VMEM((TT + 2 * hr, hy3), F32)],
        compiler_params=_cp("parallel", "parallel"),
        name="hy_shortconv",
    )(uh, u, uh, w, b.reshape(1, hy3))


def _filter_feats(lh, n_cols):
    n = np.arange(2 * lh).reshape(-1, n_cols).T.reshape(-1)
    lag = np.where(n < lh, n, 2 * lh - n).astype(np.float64)
    t = (lag / lh).astype(np.float32).astype(np.float64)
    bands = np.arange(1, N_BANDS + 1, dtype=np.float64)
    ang = 2.0 * math.pi * t[:, None] * bands
    feats = np.concatenate([t[:, None], np.cos(ang), np.sin(ang)], axis=-1)
    pad = (-(feats.shape[1] + 1)) % SUBLANES
    return np.concatenate([feats, np.zeros((2 * lh, pad)), n[:, None].astype(np.float64)], axis=-1)


def _filt_kernel(ft_ref, w1_ref, b1_ref, w2_ref, b2_ref, w3_ref, b3_ref, fr_ref, dec_ref, o_ref,
                 *, lh, c):
    ft = ft_ref[...]
    freq = fr_ref[...]
    h = jnp.sin(freq * (_dot(ft, w1_ref[...], precision=HIGHEST) + b1_ref[...]))
    h = jnp.sin(freq * (_dot(h, w2_ref[...], precision=HIGHEST) + b2_ref[...])).astype(BF16)
    tr = o_ref.shape[1]
    half = tr // 2
    for q in range(ft.shape[0] // tr):
        for dr in range(2):
            rs = slice(q * tr + dr * half, q * tr + (dr + 1) * half)
            t = ft[rs, 0:1]
            f = (_dot(h[rs], w3_ref[dr]) + b3_ref[dr]) * jnp.exp(-t * jnp.abs(dec_ref[dr]))
            if dr == 1:
                f = jnp.where(ft[rs, ft.shape[1] - 1:] == lh, 0.0, f)
            for o in range(2):
                o_ref[o, dr * half:(dr + 1) * half, q * c:(q + 1) * c] = f[:, o * c:(o + 1) * c]


def _hyena_filter(lh, n_cols, w1, b1, w2, b2, w3, b3, freq, decay):
    c = decay.shape[-1]
    feats = jnp.asarray(_filter_feats(lh, n_cols), F32)
    fe = feats.shape[1]
    w1p = jnp.pad(w1, ((0, fe - w1.shape[0]), (0, 0)))
    hid = w1.shape[1]
    tr = 2 * lh // n_cols
    cb = math.gcd(n_cols, FILT_COL_BLOCKS)
    by_dir = lambda a, lead: jnp.moveaxis(a.reshape(lead, 2, 2, c), 2, 0).reshape(2, lead, 2 * c)
    full = lambda a: pl.BlockSpec(a.shape, lambda i: (0,) * a.ndim)
    args = (w1p, b1.reshape(1, hid), w2, b2.reshape(1, hid), by_dir(w3, hid).astype(BF16),
            by_dir(b3, 1), freq.reshape(1, hid), by_dir(decay, 1))
    return pl.pallas_call(
        functools.partial(_filt_kernel, lh=lh, c=c),
        grid=(n_cols // cb,),
        in_specs=[pl.BlockSpec((cb * tr, fe), lambda i: (i, 0))] + [full(a) for a in args],
        out_specs=pl.BlockSpec((2, tr, cb * c), lambda i: (0, 0, i)),
        out_shape=jax.ShapeDtypeStruct((2, tr, n_cols * c), F32),
        compiler_params=_cp("parallel"),
        name="hy_filter",
    )(feats, *args)


@functools.lru_cache(maxsize=None)
def _dft_tables(n):
    nn = n * n
    h = n // 2
    k = np.arange(n)
    th = 2.0 * math.pi * np.outer(k, k) / n
    c, s = np.cos(th), np.sin(th)
    f1_data = np.block([[c[:, :h], s[:, :h]], [-s[:, :h], c[:, :h]]])
    f1_real = np.concatenate([c, -s], axis=0)
    idx = (k[None, None, :] * (k[:, None, None] + n * k[None, :, None])) % nn
    phi = 2.0 * math.pi * idx / nn
    cp, sp = np.cos(phi), np.sin(phi)
    g = np.concatenate([np.concatenate([cp, sp], axis=2), np.concatenate([-sp, cp], axis=2)], axis=1)
    hmat = np.transpose(g, (0, 2, 1)) / nn
    ci, si = c[:h, :], s[:h, :]
    f3 = np.zeros((n, 2 * n))
    f3[:h, 0::2], f3[:h, 1::2] = ci, -si
    f3[h:, 0::2], f3[h:, 1::2] = si, ci
    return f1_data, f1_real, g, hmat, f3


def _fft_s1_kernel(x_ref, f_ref, o_ref):
    o_ref[0] = _dot(f_ref[...], x_ref[0].astype(BF16)).astype(o_ref.dtype)


def _fft_s1(x, f1, n, c):
    p = x.shape[0]
    tc = FFT_N2_TILE * c
    return pl.pallas_call(
        _fft_s1_kernel,
        grid=(p, n * c // tc),
        in_specs=[pl.BlockSpec((1, n, tc), lambda q, j: (q, 0, j)),
                  pl.BlockSpec(f1.shape, lambda q, j: (0, 0))],
        out_specs=pl.BlockSpec((1, 2 * n, tc), lambda q, j: (q, 0, j)),
        out_shape=jax.ShapeDtypeStruct((p, 2 * n, n * c), BF16),
        compiler_params=_cp("parallel", "parallel"),
        name="fft_s1",
    )(x, f1)


def _fft_spec_kernel(a_ref, g_ref, o_ref, *, tk):
    for j in range(tk):
        x = jnp.concatenate([a_ref[0, 0, j], a_ref[0, 1, j]], axis=0)
        o_ref[0, j] = _dot(g_ref[j], x).astype(o_ref.dtype)


def _fft_mid_kernel(a_ref, g_ref, h_ref, kh_ref, o_ref, *, tk, n):
    ts = [_dot(g_ref[j], jnp.concatenate([a_ref[0, 0, j], a_ref[0, 1, j]], axis=0))
          for j in range(tk)]
    ys = []
    for j, t in enumerate(ts):
        tr, ti = t[:n], t[n:]
        kr, ki = kh_ref[0, j, :n].astype(F32), kh_ref[0, j, n:].astype(F32)
        ys.append(jnp.concatenate([tr * kr - ti * ki, tr * ki + ti * kr], axis=0).astype(BF16))
    for j, y in enumerate(ys):
        o_ref[0, j] = _dot(h_ref[j], y).astype(o_ref.dtype)


def _fft_spectrum(a, g, n, c):
    p = a.shape[0]
    tk = FFT_K1_TILE
    a5 = a.reshape(p, 2, n, n, c)
    return pl.pallas_call(
        functools.partial(_fft_spec_kernel, tk=tk),
        grid=(n // tk, p),
        in_specs=[pl.BlockSpec((1, 2, tk, n, c), lambda j, q: (q, 0, j, 0, 0)),
                  pl.BlockSpec((tk, 2 * n, 2 * n), lambda j, q: (j, 0, 0))],
        out_specs=pl.BlockSpec((1, tk, 2 * n, c), lambda j, q: (q, j, 0, 0)),
        out_shape=jax.ShapeDtypeStruct((p, n, 2 * n, c), BF16),
        compiler_params=_cp("parallel", "parallel"),
        name="fft_spectrum",
    )(a5, g)


def _fft_mid(a, g, hm, khat, order, n, c):
    p = a.shape[0]
    tk = FFT_K1_TILE
    a5 = a.reshape(p, 2, n, n, c)
    return pl.pallas_call(
        functools.partial(_fft_mid_kernel, tk=tk, n=n),
        grid=(n // tk, p),
        in_specs=[pl.BlockSpec((1, 2, tk, n, c), lambda j, q: (q, 0, j, 0, 0)),
                  pl.BlockSpec((tk, 2 * n, 2 * n), lambda j, q: (j, 0, 0)),
                  pl.BlockSpec((tk, 2 * n, 2 * n), lambda j, q: (j, 0, 0)),
                  pl.BlockSpec((1, tk, 2 * n, c), lambda j, q: (order, j, 0, 0))],
        out_specs=pl.BlockSpec((1, tk, 2 * n, c), lambda j, q: (q, j, 0, 0)),
        out_shape=jax.ShapeDtypeStruct((p, n, 2 * n, c), BF16),
        compiler_params=_cp("parallel", "parallel"),
        name="fft_mid",
    )(a5, g, hm, khat)


def _fft_s3_kernel(*refs, gated, chained):
    c_ref, f_ref, v_ref, b_ref = refs[:4]
    rest = list(refs[4:])
    y = _dot(f_ref[...], c_ref[0]) + v_ref[0].astype(F32) * b_ref[...]
    if gated:
        y = rest.pop(0)[0].astype(F32) * y
    f1_ref = rest.pop(0) if chained else None
    o_ref = rest.pop(0)
    z = y.astype(o_ref.dtype)
    o_ref[0] = z
    if chained:
        a_ref = rest.pop(0)
        a_ref[0] = _dot(f1_ref[...], z).astype(a_ref.dtype)


def _fft_s3(cm, f3, vin, xg, bias, n, c, f1_next=None):
    p = cm.shape[0]
    tc = FFT_N2_TILE * c
    c2 = cm.reshape(p, 2 * n, n * c)
    bias_t = jnp.tile(bias.reshape(1, c), (1, FFT_N2_TILE))
    blk = pl.BlockSpec((1, n, tc), lambda q, j: (q, 0, j))
    full = lambda a: pl.BlockSpec(a.shape, lambda q, j: (0, 0))
    in_specs = [pl.BlockSpec((1, 2 * n, tc), lambda q, j: (q, 0, j)), full(f3), blk,
                pl.BlockSpec((1, tc), lambda q, j: (0, 0))]
    args = [c2, f3, vin, bias_t]
    out_specs, out_shape = [blk], [jax.ShapeDtypeStruct((p, n, n * c), BF16)]
    if xg is not None:
        in_specs.append(blk)
        args.append(xg)
    if f1_next is not None:
        in_specs.append(full(f1_next))
        args.append(f1_next)
        out_specs.append(pl.BlockSpec((1, 2 * n, tc), lambda q, j: (q, 0, j)))
        out_shape.append(jax.ShapeDtypeStruct((p, 2 * n, n * c), BF16))
    return pl.pallas_call(
        functools.partial(_fft_s3_kernel, gated=xg is not None, chained=f1_next is not None),
        grid=(p, n * c // tc),
        in_specs=in_specs,
        out_specs=out_specs,
        out_shape=out_shape,
        compiler_params=_cp("parallel", "parallel"),
        name="fft_s3",
    )(*args)


def _hyena_long(v, x1, kk, hy_bias):
    nb, l, c = v.shape
    n = math.isqrt(2 * l)
    assert n * n == 2 * l and nb % 2 == 0
    p = nb // 2
    f1d, f1r, g, hm, f3 = (jnp.asarray(t, F32).astype(BF16) for t in _dft_tables(n))
    pair = lambda a: a.reshape(p, n, n * c)
    khat = _fft_spectrum(_fft_s1(kk, f1r, n, c), g, n, c)
    v = pair(v)
    cm = _fft_mid(_fft_s1(v, f1d, n, c), g, hm, khat, 0, n, c)
    z1, a = _fft_s3(cm, f3, v, pair(x1), hy_bias[0], n, c, f1_next=f1d)
    cm = _fft_mid(a, g, hm, khat, 1, n, c)
    (w2,) = _fft_s3(cm, f3, z1, None, hy_bias[1], n, c)
    return w2.reshape(nb, l, c)


@functools.lru_cache(maxsize=None)
def _ctx_dft_tables(lc):
    m = 2 * lc
    k = np.arange(m)
    th = 2.0 * math.pi * np.outer(k, k) / m
    c, s = np.cos(th), np.sin(th)
    f_data = np.block([[c[:, :lc], s[:, :lc]], [-s[:, :lc], c[:, :lc]]])
    f_real = np.concatenate([c, -s], axis=0)
    ci, si = c[:lc, :], s[:lc, :]
    f_inv = np.block([[ci, -si], [si, ci]]) / m
    return f_data, f_real, f_inv


def _ctxconv_kernel(v_ref, x1_ref, x2_ref, kk_ref, fd_ref, fr_ref, fi_ref, b_ref, o_ref, *, m):
    def conv(u, order):
        kh = _dot(fr_ref[...], kk_ref[order].astype(BF16))
        t = _dot(fd_ref[...], u.astype(BF16))
        tr, ti, kr, ki = t[:m], t[m:], kh[:m], kh[m:]
        y = jnp.concatenate([tr * kr - ti * ki, tr * ki + ti * kr], axis=0).astype(BF16)
        return _dot(fi_ref[...], y)

    v = v_ref[0].astype(F32)
    z1 = x1_ref[0].astype(F32) * (conv(v, 0) + v * b_ref[0:1, :])
    o_ref[0] = (x2_ref[0].astype(F32) * (conv(z1, 1) + z1 * b_ref[1:2, :])).astype(o_ref.dtype)


def _hyena_ctx(v, x1, x2, kk, hy_bias):
    nb, lc, c = v.shape
    p, m = nb // 2, 2 * lc
    fd, fr, fi = (jnp.asarray(t, F32).astype(BF16) for t in _ctx_dft_tables(lc))
    pair = lambda a: a.reshape(p, m, c)
    blk = pl.BlockSpec((1, m, c), lambda q: (q, 0, 0))
    full = lambda a: pl.BlockSpec(a.shape, lambda q: (0,) * a.ndim)
    z = pl.pallas_call(
        functools.partial(_ctxconv_kernel, m=m),
        grid=(p,),
        in_specs=[blk, blk, blk, full(kk), full(fd), full(fr), full(fi), full(hy_bias)],
        out_specs=blk,
        out_shape=jax.ShapeDtypeStruct((p, m, c), BF16),
        compiler_params=_cp("parallel"),
        name="hy_ctx",
    )(pair(v), pair(x1), pair(x2), kk, fd, fr, fi, hy_bias)
    return z.reshape(nb, lc, c)


def _od_inproj_kernel(prev_ref, h_ref, next_ref, mod_ref, g_ref, w_ref, cw_ref, cb_ref,
                      gate_ref, xc_ref, *, d, r, n_lat_tiles, n_tiles):
    i = pl.program_id(1)
    seg_first = jnp.logical_or(i == 0, i == n_lat_tiles)
    seg_last = jnp.logical_or(i == n_lat_tiles - 1, i == n_tiles - 1)
    m = mod_ref[0]
    rows = jnp.concatenate([prev_ref[0, 0], h_ref[0], next_ref[0, 0]], axis=0)
    xn = _rms_mod(rows, g_ref[...], m[:, 0:d], m[:, d:2 * d]).astype(BF16)
    x = _dot(xn, w_ref[:, r:])
    gate_ref[0] = _dot(xn[SUBLANES:SUBLANES + TT], w_ref[:, :r]).astype(gate_ref.dtype)
    row = lax.broadcasted_iota(jnp.int32, (x.shape[0], 1), 0)
    outside = jnp.logical_or(jnp.logical_and(row < SUBLANES, seg_first),
                             jnp.logical_and(row >= SUBLANES + TT, seg_last))
    x = jnp.where(outside, 0.0, x)
    xc_ref[0] = _conv_taps(x, cw_ref[...], cb_ref[...], RG_CONV_LEFT)


def _od_inproj(h, mods, rows, layer, g, w_in, conv_w, conv_b, *, n_lat_tiles):
    nb, s, d = h.shape
    r = w_in.shape[1] // 2
    n_tiles = s // TT
    n8, per = s // SUBLANES, TT // SUBLANES
    h8 = h.reshape(nb, n8, SUBLANES, d)
    ospec = pl.BlockSpec((1, TT, r), lambda b, i: (b, i, 0))
    return pl.pallas_call(
        functools.partial(_od_inproj_kernel, d=d, r=r, n_lat_tiles=n_lat_tiles, n_tiles=n_tiles),
        grid=(nb, n_tiles),
        in_specs=[
            pl.BlockSpec((1, 1, SUBLANES, d), lambda b, i: (b, jnp.maximum(i * per - 1, 0), 0, 0)),
            pl.BlockSpec((1, TT, d), lambda b, i: (b, i, 0)),
            pl.BlockSpec((1, 1, SUBLANES, d),
                         lambda b, i: (b, jnp.minimum((i + 1) * per, n8 - 1), 0, 0)),
            _mod_spec(layer, rows, nb, n_lat_tiles, mods.shape[-1]),
            pl.BlockSpec((1, d), lambda b, i: (0, 0)),
            pl.BlockSpec(w_in.shape, lambda b, i: (0, 0)),
            pl.BlockSpec(conv_w.shape, lambda b, i: (0, 0)),
            pl.BlockSpec((1, r), lambda b, i: (0, 0)),
        ],
        out_specs=[ospec, ospec],
        out_shape=[jax.ShapeDtypeStruct((nb, s, r), BF16), jax.ShapeDtypeStruct((nb, s, r), F32)],
        compiler_params=_cp("parallel", "parallel"),
        name="od_inproj",
    )(h8, h, h8, mods, g, w_in, conv_w, conv_b.reshape(1, r))


def _rglru_kernel(x_ref, w_ref, b_ref, lam_ref, o_ref, a_ref, bb_ref, carry_ref, *, r, windows):
    dr = pl.program_id(1)
    i = pl.program_id(2)
    part = TT // RG_PARTS
    nl = -lam_ref[0]
    softplus = jnp.maximum(nl, 0.0) + jnp.log1p(jnp.exp(-jnp.abs(nl)))
    neg_rate = (0.5 * RG_C) * softplus
    exp2_rate = (-0.5 * RG_C * math.log2(math.e)) * softplus

    def gates(r0):
        rs = slice(r0, r0 + part)
        xc = x_ref[0, rs, :]
        xb = xc.astype(BF16)
        half_x = 0.5 * xc
        for c0, c1, k0, k1 in windows:
            xk = xb[:, k0:k1]
            tr = jnp.tanh(_dot(xk, w_ref[0, k0:k1, c0:c1]) + b_ref[0, :, c0:c1]) + 1.0
            ti = jnp.tanh(_dot(xk, w_ref[0, k0:k1, r + c0:r + c1]) + b_ref[0, :, r + c0:r + c1]) + 1.0
            a = jnp.exp2(exp2_rate[:, c0:c1] * tr)
            a_ref[rs, c0:c1] = a
            e = jnp.tanh(neg_rate[:, c0:c1] * tr) * (a * a + 1.0)
            root = jnp.where(e > 0.0, e * lax.rsqrt(e), 0.0)
            bb_ref[rs, c0:c1] = root * (ti * half_x[:, c0:c1])

    def scan_rows(rows, h):
        for row in rows:
            h = a_ref[row:row + 1, :] * h + bb_ref[row:row + 1, :]
            o_ref[0, 0, row:row + 1, :] = h
        return h

    def scan_loop(r0, reverse, h):
        n_blk = part // SUBLANES

        def block(g, h):
            base = pl.multiple_of(r0 + (n_blk - 1 - g if reverse else g) * SUBLANES, SUBLANES)
            for k in range(SUBLANES):
                row = base + (SUBLANES - 1 - k if reverse else k)
                h = a_ref[pl.ds(row, 1), :] * h + bb_ref[pl.ds(row, 1), :]
                o_ref[0, 0, pl.ds(row, 1), :] = h
            return h

        return lax.fori_loop(0, n_blk, block, h)

    @pl.when(i == 0)
    def _():
        carry_ref[...] = jnp.zeros_like(carry_ref)

    def run(reverse):
        starts = [k * part for k in (reversed(range(RG_PARTS)) if reverse else range(RG_PARTS))]
        gates(starts[0])
        h = carry_ref[0:1, :]
        for k, r0 in enumerate(starts[:-1]):
            gates(starts[k + 1])
            rows = range(r0 + part - 1, r0 - 1, -1) if reverse else range(r0, r0 + part)
            h = scan_rows(rows, h)
        carry_ref[0:1, :] = scan_loop(starts[-1], reverse, h)

    pl.when(dr == 0)(lambda: run(False))
    pl.when(dr == 1)(lambda: run(True))


def _scan_tile(dr, i, n_lat_tiles, n_tiles):
    fwd = jnp.where(i == 0, n_lat_tiles, i - 1)
    bwd = jnp.where(i == 0, n_lat_tiles, n_lat_tiles - i)
    return jnp.where(dr == 0, fwd, bwd)


def _gate_windows(r, bs):
    out = []
    for c0 in range(0, r, 2 * LANES):
        c1 = min(c0 + 2 * LANES, r)
        k0 = (c0 // bs) * bs // LANES * LANES
        k1 = min(-(-(((c1 - 1) // bs + 1) * bs) // LANES) * LANES, r)
        out.append((c0, c1, k0, k1))
    return tuple(out)


def _rglru(xc, wcat, bcat, lam, *, n_lat_tiles, block_size):
    nb, s, r = xc.shape
    n_tiles = s // TT
    assert n_tiles == n_lat_tiles + 1
    tile = lambda d, i: _scan_tile(d, i, n_lat_tiles, n_tiles)
    return pl.pallas_call(
        functools.partial(_rglru_kernel, r=r, windows=_gate_windows(r, block_size)),
        grid=(nb, 2, n_tiles),
        in_specs=[
            pl.BlockSpec((1, TT, r), lambda b, d, i: (b, tile(d, i), 0)),
            pl.BlockSpec((1, r, 2 * r), lambda b, d, i: (d, 0, 0)),
            pl.BlockSpec((1, 1, 2 * r), lambda b, d, i: (d, 0, 0)),
            pl.BlockSpec((1, 1, r), lambda b, d, i: (d, 0, 0)),
        ],
        out_specs=pl.BlockSpec((1, 1, TT, r), lambda b, d, i: (d, b, tile(d, i), 0)),
        out_shape=jax.ShapeDtypeStruct((2, nb, s, r), F32),
        scratch_shapes=[pltpu.VMEM((TT, r), F32), pltpu.VMEM((TT, r), F32),
                        pltpu.VMEM((SUBLANES, r), F32)],
        compiler_params=_cp("parallel", "parallel", "arbitrary"),
        name="rglru",
    )(xc, wcat, bcat, lam.reshape(2, 1, r))


def _block_diag(w):
    n, bs, _ = w.shape
    eye = jnp.eye(n, dtype=w.dtype)
    return (eye[:, None, :, None] * w[:, :, None, :]).reshape(n * bs, n * bs)


def _gelu_tanh(x):
    return 0.5 * x * (1.0 + jnp.tanh(math.sqrt(2.0 / math.pi) * (x + 0.044715 * (x * x * x))))


def _post_kernel(*refs, kind, final, d, n_lat_tiles):
    if kind == "even":
        (hl_ref, hc_ref, x2_ref, wl_ref, zc_ref, ol_ref, oc_ref, mod_ref, g2_ref, wo_ref, w1_ref,
         w2_ref) = refs[:12]
        rest = refs[12:]
        is_ctx = pl.program_id(1) >= n_lat_tiles
        z_lat = (x2_ref[0].astype(F32) * wl_ref[0].astype(F32)).astype(BF16)
        z = jnp.where(is_ctx, zc_ref[0], z_lat)
        o = jnp.where(is_ctx, oc_ref[0], ol_ref[0])
        y = _dot(jnp.concatenate([z, o], axis=-1), wo_ref[...])
    else:
        hl_ref, hc_ref, hd_ref, gate_ref, mod_ref, g2_ref, wo_ref, w1_ref, w2_ref = refs[:9]
        rest = refs[9:]
        mix = (hd_ref[0, 0] + hd_ref[1, 0]) * _gelu_tanh(gate_ref[0].astype(F32))
        y = _dot(mix.astype(BF16), wo_ref[...])
    out_ref = rest[-1]
    m = mod_ref[0]
    h1 = _stream_tile(hl_ref, hc_ref, n_lat_tiles) + m[:, 2 * d:3 * d] * y
    xn = _rms_mod(h1, g2_ref[...], m[:, 3 * d:4 * d], m[:, 4 * d:5 * d]).astype(BF16)
    acc = jnp.zeros_like(h1)
    dff = w1_ref.shape[1]
    for c0 in range(0, dff, FF_CHUNK):
        a = jnp.maximum(_dot(xn, w1_ref[:, c0:c0 + FF_CHUNK]), 0.0)
        acc = acc + _dot((a * a).astype(BF16), w2_ref[c0:c0 + FF_CHUNK, :])
    h2 = h1 + m[:, 5 * d:6 * d] * acc
    if final:
        fg_ref = rest[0]
        h2 = h2 * lax.rsqrt(jnp.mean(h2 * h2, axis=-1, keepdims=True) + EPS) * fg_ref[...]
    out_ref[0] = h2


def _post(kind, h, mix_args, mods, rows, layer, g2, w_out, w1, w2, final_g, *, n_lat_tiles, final):
    h_specs, h_args = _stream_specs(h, n_lat_tiles)
    nb, _, d = h_args[0].shape
    n_tiles = n_lat_tiles if final else n_lat_tiles + 1
    tok = lambda w: pl.BlockSpec((1, TT, w), lambda b, i: (b, i, 0))
    full = lambda a: pl.BlockSpec(a.shape, lambda b, i: (0,) * a.ndim)
    if kind == "even":
        lat = lambda w: pl.BlockSpec((1, TT, w), lambda b, i: (b, jnp.minimum(i, n_lat_tiles - 1), 0))
        ctx = lambda w: pl.BlockSpec((1, TT, w), lambda b, i: (b, 0, 0))
        x2_lat, w_lat, z_ctx, o_lat, o_ctx = mix_args
        mix_specs = [lat(x2_lat.shape[-1]), lat(w_lat.shape[-1]), ctx(z_ctx.shape[-1]),
                     lat(o_lat.shape[-1]), ctx(o_ctx.shape[-1])]
    else:
        hd, gate = mix_args
        r = gate.shape[-1]
        mix_specs = [pl.BlockSpec((2, 1, TT, r), lambda b, i: (0, b, i, 0)), tok(r)]
    in_specs = h_specs + mix_specs + [
        _mod_spec(layer, rows, nb, n_lat_tiles, mods.shape[-1]),
        pl.BlockSpec((1, d), lambda b, i: (0, 0)), full(w_out), full(w1), full(w2)]
    args = [*h_args, *mix_args, mods, g2, w_out, w1, w2]
    if final:
        in_specs.append(pl.BlockSpec((1, d), lambda b, i: (0, 0)))
        args.append(final_g)
    return pl.pallas_call(
        functools.partial(_post_kernel, kind=kind, final=final, d=d, n_lat_tiles=n_lat_tiles),
        grid=(nb, n_tiles),
        in_specs=in_specs,
        out_specs=tok(d),
        out_shape=jax.ShapeDtypeStruct((nb, n_tiles * TT, d), F32),
        compiler_params=_cp("parallel", "parallel"),
        name="post_" + kind,
    )(*args)


@functools.lru_cache(maxsize=None)
def _rope_tables(l, lc, head_dim):
    axis = head_dim // 2
    freqs = ROPE_BASE ** (-np.arange(0, axis, 2, dtype=np.float64) / axis)
    freqs = freqs.astype(np.float32).astype(np.float64)
    t = np.arange(l)
    ang_r = (t // GRID_W)[:, None] * freqs
    ang_c = (t % GRID_W)[:, None] * freqs
    cos = np.concatenate([np.cos(ang_r)] * 2 + [np.cos(ang_c)] * 2, axis=-1)
    sin = np.concatenate([-np.sin(ang_r), np.sin(ang_r), -np.sin(ang_c), np.sin(ang_c)], axis=-1)
    cos = np.concatenate([cos, np.ones((lc, head_dim))], axis=0)
    sin = np.concatenate([sin, np.zeros((lc, head_dim))], axis=0)
    rep = LANES // head_dim
    return np.tile(cos, (1, rep)), np.tile(sin, (1, rep))


def kernel(x, c, ctx, c_ctx, ada_w, ada_b, norm1_g, norm2_g, mlp_w1, mlp_w2, final_g, ev_w_in, ev_w_out, hy_short_w, hy_short_b, hy_f_w1, hy_f_b1, hy_f_w2, hy_f_b2, hy_f_w3, hy_f_b3, hy_f_freq, hy_f_decay, hy_bias, df_lq1, df_lk1, df_lq2, df_lk2, df_subln_g, od_w_in, od_w_out, rg_conv_w, rg_conv_b, rg_wa, rg_ba, rg_wx, rg_bx, rg_lam):
    nb, l, d = x.shape
    lc = ctx.shape[1]
    depth = ada_w.shape[0]
    hy = hy_bias.shape[-1]
    head_dim = df_lq1.shape[-1]
    qk = (ev_w_in.shape[-1] - 3 * hy) // 3
    r = rg_lam.shape[-1]
    assert l % TT == 0 and lc == TT and l % GRID_W == 0
    n_lat_tiles = l // TT

    rows = -(-(nb + 1) // SUBLANES) * SUBLANES
    cond = jnp.zeros((rows, d), F32).at[:nb].set(c).at[nb].set(c_ctx)
    mods = _ada_mods(cond, ada_w, ada_b).reshape(depth * rows, 1, ada_w.shape[-1])

    cos_t, sin_t = (jnp.asarray(t, F32) for t in _rope_tables(l, lc, head_dim))
    h = (x, ctx)
    w1_bf, w2_bf = mlp_w1.astype(BF16), mlp_w2.astype(BF16)

    for i in range(depth):
        j = i // 2
        final = i == depth - 1
        g1, g2 = norm1_g[i].reshape(1, d), norm2_g[i].reshape(1, d)
        if i % 2 == 0:
            lam_init = 0.8 - 0.6 * math.exp(-0.3 * i)
            u, qt, k, vt = _ev_inproj(h, mods, rows, i, g1, ev_w_in[j].astype(BF16), cos_t, sin_t,
                                      n_lat_tiles=n_lat_tiles, hy3=3 * hy, qk=qk, head_dim=head_dim,
                                      tk=_attn_chunk(l + lc))
            lam_vecs = [a[j].reshape(1, -1) for a in (df_lq1, df_lk1, df_lq2, df_lk2)]
            o_lat, o_ctx = _diff_attention(qt, k, vt, lam_vecs, df_subln_g[j].reshape(1, -1),
                                           n_lat_tiles=n_lat_tiles, lam_init=lam_init,
                                           head_dim=head_dim)
            fparams = (hy_f_w1[j], hy_f_b1[j], hy_f_w2[j], hy_f_b2[j], hy_f_w3[j], hy_f_b3[j],
                       hy_f_freq[j], hy_f_decay[j])
            vl, x1l, x2l = _shortconv(u, hy_short_w[j], hy_short_b[j], tile0=0,
                                      n_seg_tiles=n_lat_tiles)
            vc, x1c, x2c = _shortconv(u, hy_short_w[j], hy_short_b[j], tile0=n_lat_tiles,
                                      n_seg_tiles=lc // TT)
            w_lat = _hyena_long(vl, x1l, _hyena_filter(l, math.isqrt(2 * l), *fparams), hy_bias[j])
            z_ctx = _hyena_ctx(vc, x1c, x2c, _hyena_filter(lc, 1, *fparams), hy_bias[j])
            h = _post("even", h, (x2l, w_lat, z_ctx, o_lat, o_ctx), mods, rows, i, g2, ev_w_out[j].astype(BF16),
                      w1_bf[i], w2_bf[i], final_g.reshape(1, d), n_lat_tiles=n_lat_tiles,
                      final=final)
        else:
            gate, xc = _od_inproj(h, mods, rows, i, g1, od_w_in[j].astype(BF16), rg_conv_w[j],
                                  rg_conv_b[j], n_lat_tiles=n_lat_tiles)
            wcat = (0.5 * jnp.stack(
                [jnp.concatenate([_block_diag(rg_wa[j, dd]), _block_diag(rg_wx[j, dd])], axis=1)
                 for dd in range(2)])).astype(BF16)
            bcat = 0.5 * jnp.concatenate([rg_ba[j], rg_bx[j]], axis=-1).reshape(2, 1, 2 * r)
            hd = _rglru(xc, wcat, bcat, rg_lam[j], n_lat_tiles=n_lat_tiles,
                        block_size=rg_wa.shape[-1])
            h = _post("odd", h, (hd, gate), mods, rows, i, g2, od_w_out[j].astype(BF16),
                      w1_bf[i], w2_bf[i], final_g.reshape(1, d), n_lat_tiles=n_lat_tiles,
                      final=final)
    return h
```

```python
import functools
import math

import numpy as np
import jax
import jax.numpy as jnp
from jax import lax
from jax.experimental import pallas as pl
from jax.experimental.pallas import tpu as pltpu

F32 = jnp.float32
BF16 = jnp.bfloat16
HIGHEST = lax.Precision.HIGHEST

EPS = 1e-6
GRID_W = 64
ROPE_BASE = 10000.0
N_BANDS = 16
RG_C = 8.0
RG_CONV_LEFT = 2
HY_SHORT_LEFT = 1

TT = 256
LANES = 128
SUBLANES = 8
VMEM_LIMIT = 56 * 1024 * 1024
FF_CHUNK = 1024
FFT_N2_TILE = 16
FFT_K1_TILE = 16
ADA_K_TILE = 256
ATTN_SUB_Q = 512
RG_PARTS = 2
FILT_COL_BLOCKS = 4
OD_COL_BLOCKS = 2
VT_ROWS = LANES + 16


def _cp(*sem):
    return pltpu.CompilerParams(dimension_semantics=sem, vmem_limit_bytes=VMEM_LIMIT)


def _dot(a, b, **kw):
    return jnp.dot(a, b, preferred_element_type=F32, **kw)


def _sigmoid(x):
    return 1.0 / (1.0 + jnp.exp(-x))


def _rms_mod(x, g, shift, scale):
    y = x * lax.rsqrt(jnp.mean(x * x, axis=-1, keepdims=True) + EPS)
    return (y * g) * (1.0 + scale) + shift


def _ada_kernel(c_ref, w_ref, b_ref, o_ref):
    k = pl.program_id(1)
    c = c_ref[k]
    part = _dot((c * _sigmoid(c)).astype(BF16), w_ref[0].astype(BF16))

    @pl.when(k == 0)
    def _():
        o_ref[0] = part + b_ref[0]

    @pl.when(k > 0)
    def _():
        o_ref[0] += part


def _ada_mods(cond, ada_w, ada_b):
    depth, d, n = ada_w.shape
    rows = cond.shape[0]
    tk = min(d, ADA_K_TILE)
    cond_k = cond.reshape(rows, d // tk, tk).transpose(1, 0, 2)
    return pl.pallas_call(
        _ada_kernel,
        grid=(depth, d // tk),
        in_specs=[
            pl.BlockSpec(cond_k.shape, lambda l, k: (0, 0, 0)),
            pl.BlockSpec((1, tk, n), lambda l, k: (l, k, 0)),
            pl.BlockSpec((1, 1, n), lambda l, k: (l, 0, 0)),
        ],
        out_specs=pl.BlockSpec((1, rows, n), lambda l, k: (l, 0, 0)),
        out_shape=jax.ShapeDtypeStruct((depth, rows, n), F32),
        compiler_params=_cp("parallel", "arbitrary"),
        name="ada_mods",
    )(cond_k, ada_w, ada_b.reshape(depth, 1, n))


def _stream_specs(h, n_lat_tiles):
    separate = isinstance(h, tuple)
    lat_arr, ctx_arr = h if separate else (h, h)
    ctx_blk = 0 if separate else n_lat_tiles
    d = lat_arr.shape[-1]
    lat = pl.BlockSpec((1, TT, d), lambda b, i: (b, jnp.minimum(i, n_lat_tiles - 1), 0))
    ctx = pl.BlockSpec((1, TT, d), lambda b, i: (b, ctx_blk, 0))
    return [lat, ctx], [lat_arr, ctx_arr]


def _stream_tile(hl_ref, hc_ref, n_lat_tiles):
    return jnp.where(pl.program_id(1) >= n_lat_tiles, hc_ref[0], hl_ref[0])


def _mod_spec(layer, rows, nb, n_lat_tiles, n6):
    def imap(b, i):
        return (layer * rows + jnp.where(i >= n_lat_tiles, nb, b), 0, 0)
    return pl.BlockSpec((1, 1, n6), imap)


def _ev_inproj_kernel(hl_ref, hc_ref, mod_ref, g_ref, w_ref, cos_ref, sin_ref,
                      u_ref, qt_ref, k_ref, vt_ref, *, d, hy3, qk, qscale, n_lat_tiles):
    m = mod_ref[0]
    x = _stream_tile(hl_ref, hc_ref, n_lat_tiles)
    xn = _rms_mod(x, g_ref[...], m[:, 0:d], m[:, d:2 * d]).astype(BF16)
    cos = cos_ref[...]
    sin = sin_ref[...]
    lane = lax.broadcasted_iota(jnp.int32, cos.shape, 1)
    first = (lane % 32) < 16

    def rope(z):
        sw = jnp.where(first, pltpu.roll(z, LANES - 16, 1), pltpu.roll(z, 16, 1))
        return z * cos + sw * sin

    extra = vt_ref.shape[3] - LANES
    ones_row = jnp.where(lax.broadcasted_iota(jnp.int32, (extra, cos.shape[0]), 0) == 0,
                         1.0, 0.0).astype(BF16)
    yq = _dot(xn, w_ref[:, hy3:hy3 + qk])
    for c in range(qk // LANES):
        qt_ref[0, c] = (rope(yq[:, c * LANES:(c + 1) * LANES]) * qscale).T.astype(BF16)
    yk = _dot(xn, w_ref[:, hy3 + qk:hy3 + 2 * qk])
    for c in range(qk // LANES):
        k_ref[0, :, c * LANES:(c + 1) * LANES] = rope(yk[:, c * LANES:(c + 1) * LANES]).astype(BF16)
    yv = _dot(xn, w_ref[:, hy3 + 2 * qk:])
    for c in range(qk // LANES):
        vt_ref[0, c, 0, 0:LANES, :] = yv[:, c * LANES:(c + 1) * LANES].T.astype(BF16)
        vt_ref[0, c, 0, LANES:, :] = ones_row
    u_ref[0] = _dot(xn, w_ref[:, :hy3]).astype(u_ref.dtype)


def _ev_inproj(h, mods, rows, layer, g, w_in, cos_t, sin_t, *, n_lat_tiles, hy3, qk, head_dim, tk):
    h_specs, h_args = _stream_specs(h, n_lat_tiles)
    nb, _, d = h_args[0].shape
    s = (n_lat_tiles + 1) * TT
    n_in = w_in.shape[1]
    assert 2 * head_dim == LANES
    heads, per = qk // LANES, tk // TT
    kern = functools.partial(_ev_inproj_kernel, d=d, hy3=hy3, qk=qk, n_lat_tiles=n_lat_tiles,
                             qscale=head_dim ** -0.5 * math.log2(math.e))
    return pl.pallas_call(
        kern,
        grid=(nb, s // TT),
        in_specs=h_specs + [
            _mod_spec(layer, rows, nb, n_lat_tiles, mods.shape[-1]),
            pl.BlockSpec((1, d), lambda b, i: (0, 0)),
            pl.BlockSpec((d, n_in), lambda b, i: (0, 0)),
            pl.BlockSpec((TT, LANES), lambda b, i: (i, 0)),
            pl.BlockSpec((TT, LANES), lambda b, i: (i, 0)),
        ],
        out_specs=[
            pl.BlockSpec((1, TT, hy3), lambda b, i: (b, i, 0)),
            pl.BlockSpec((1, heads, LANES, TT), lambda b, i: (b, 0, 0, i)),
            pl.BlockSpec((1, TT, qk), lambda b, i: (b, i, 0)),
            pl.BlockSpec((1, heads, 1, VT_ROWS, TT), lambda b, i: (b, 0, i // per, 0, i % per)),
        ],
        out_shape=[
            jax.ShapeDtypeStruct((nb, s, hy3), BF16),
            jax.ShapeDtypeStruct((nb, heads, LANES, s), BF16),
            jax.ShapeDtypeStruct((nb, s, qk), BF16),
            jax.ShapeDtypeStruct((nb, heads, s // tk, VT_ROWS, tk), BF16),
        ],
        compiler_params=_cp("parallel", "parallel"),
        name="ev_inproj",
    )(*h_args, mods, g, w_in, cos_t, sin_t)


def _split_maps(qt, half):
    row = lax.broadcasted_iota(jnp.int32, qt.shape, 0)
    zero = jnp.zeros_like(qt)
    return jnp.where(row < half, qt, zero), jnp.where(row >= half, qt, zero)


def _attn_out(a0, a1, lam_refs, sg_ref, lam_init, dv):
    lq1_ref, lk1_ref, lq2_ref, lk2_ref = lam_refs
    lam = (jnp.exp(jnp.sum(lq1_ref[...] * lk1_ref[...], axis=-1, keepdims=True))
           - jnp.exp(jnp.sum(lq2_ref[...] * lk2_ref[...], axis=-1, keepdims=True)) + lam_init)
    o = (a0[:dv] / a0[dv:dv + 1] - lam * (a1[:dv] / a1[dv:dv + 1])).T
    on = o * lax.rsqrt(jnp.mean(o * o, axis=-1, keepdims=True) + EPS)
    return on * sg_ref[...] * (1.0 - lam_init)


def _col_max8(sc):
    part = sc[0:SUBLANES]
    for g in range(1, sc.shape[0] // SUBLANES):
        part = jnp.maximum(part, sc[g * SUBLANES:(g + 1) * SUBLANES])
    return part


def _attn_kernel(qt_ref, qnt_ref, k_ref, vt_ref, lq1_ref, lk1_ref, lq2_ref, lk2_ref, sg_ref, o_ref,
                 s_ref, mp_ref, p_ref, acc_ref, m_ref, alpha_ref, *, n_chunks, tk, lam_init, half):
    i = pl.program_id(2)
    dv = 2 * half
    lam_refs = (lq1_ref, lk1_ref, lq2_ref, lk2_ref)
    tq = qnt_ref.shape[-1]
    q_sub = (_split_maps(qt_ref[0, 0, :, 0:tq], half), _split_maps(qt_ref[0, 0, :, tq:2 * tq], half))
    q_next = _split_maps(qnt_ref[0, 0], half)

    def put_scores(slot, qts, c):
        kc = k_ref[0, pl.ds(pl.multiple_of(c * tk, tk), tk), :]
        for j in range(2):
            sc = _dot(kc, qts[j])
            s_ref[slot, j] = sc
            mp_ref[slot, j] = _col_max8(sc)

    def put_probs(slot, sub, first):
        for j in range(2):
            col_max = jnp.max(mp_ref[slot, j], axis=0, keepdims=True)
            if first:
                m_new = col_max
            else:
                m_old = m_ref[sub, j]
                m_new = jnp.maximum(m_old, col_max)
                alpha_ref[slot, j] = jnp.exp2(m_old - m_new)
            m_ref[sub, j] = m_new
            p_ref[slot, j] = jnp.exp2(s_ref[slot, j] - m_new).astype(BF16)

    def add_pv(slot, sub, c, first):
        for j in range(2):
            pv = _dot(vt_ref[0, 0, c], p_ref[slot, j])
            acc_ref[sub, j] = pv if first else alpha_ref[slot, j] * acc_ref[sub, j] + pv

    def sub_tile(sub, parity, qts, qts_after, sub_after):
        slot = lambda c: (c + parity) % 2
        put_scores(slot(0), qts, 2)
        put_probs(slot(1), sub, False)
        add_pv(slot(0), sub, 0, True)
        n_uniform = n_chunks - 3

        unroll = 2

        def group(u, carry):
            c = 1 + unroll * u
            for k in range(unroll):
                put_scores(slot(1 + k), qts, c + k + 2)
                put_probs(slot(k), sub, False)
                add_pv(slot(1 + k), sub, c + k, False)
            return carry

        lax.fori_loop(0, n_uniform // unroll, group, 0)
        c = n_chunks - 2
        put_scores(slot(c), qts_after, 0)
        put_probs(slot(c + 1), sub, False)
        add_pv(slot(c), sub, c, False)
        c = n_chunks - 1
        put_scores(slot(c), qts_after, 1)
        put_probs(slot(c + 1), sub_after, True)
        add_pv(slot(c), sub, c, False)
        o_ref[0, sub * tq:(sub + 1) * tq, :] = _attn_out(
            acc_ref[sub, 0], acc_ref[sub, 1], lam_refs, sg_ref, lam_init, dv).astype(o_ref.dtype)

    @pl.when(i == 0)
    def _():
        put_scores(0, q_sub[0], 0)
        put_probs(0, 0, True)
        put_scores(1, q_sub[0], 1)

    sub_tile(0, 0, q_sub[0], q_sub[1], 1)
    sub_tile(1, 1, q_sub[1], q_next, 0)


def _ctx_attn_kernel(qt_ref, k_ref, vt_ref, lq1_ref, lk1_ref, lq2_ref, lk2_ref, sg_ref, o_ref,
                     *, lam_init, half):
    acc = []
    for qtj in _split_maps(qt_ref[0, 0], half):
        sc = _dot(k_ref[0], qtj)
        col_max = jnp.max(_col_max8(sc), axis=0, keepdims=True)
        acc.append(_dot(vt_ref[0, 0, 0], jnp.exp2(sc - col_max).astype(BF16)))
    o_ref[0] = _attn_out(acc[0], acc[1], (lq1_ref, lk1_ref, lq2_ref, lk2_ref), sg_ref, lam_init,
                         2 * half).astype(o_ref.dtype)


def _attn_chunk(s):
    return next(t for t in (3 * TT, TT) if s % t == 0 and (s // t) % 2 == 1 and s // t >= 5)


def _diff_attention(qt, k, vt, lam_vecs, subln_g, *, n_lat_tiles, lam_init, head_dim):
    nb, s, qk = k.shape
    dv = 2 * head_dim
    heads = qk // dv
    l = n_lat_tiles * TT
    n_chunks, tk = vt.shape[2], vt.shape[4]
    tq = ATTN_SUB_Q if l % (2 * ATTN_SUB_Q) == 0 else TT
    assert l % (2 * tq) == 0 and n_chunks % 2 == 1 and n_chunks >= 5 and s - l == TT
    n_steps = l // (2 * tq)
    vec = lambda n: pl.BlockSpec((1, n), lambda b, h, i: (0, 0))
    vecs = [vec(head_dim)] * 4 + [vec(dv)]
    o_lat = pl.pallas_call(
        functools.partial(_attn_kernel, n_chunks=n_chunks, tk=tk, lam_init=lam_init, half=head_dim),
        grid=(nb, heads, n_steps),
        in_specs=[
            pl.BlockSpec((1, 1, dv, 2 * tq), lambda b, h, i: (b, h, 0, i)),
            pl.BlockSpec((1, 1, dv, tq),
                         lambda b, h, i: (b, h, 0, jnp.minimum(2 * i + 2, 2 * n_steps - 1))),
            pl.BlockSpec((1, s, dv), lambda b, h, i: (b, 0, h)),
            pl.BlockSpec((1, 1, n_chunks, VT_ROWS, tk), lambda b, h, i: (b, h, 0, 0, 0)),
        ] + vecs,
        out_specs=pl.BlockSpec((1, 2 * tq, dv), lambda b, h, i: (b, i, h)),
        out_shape=jax.ShapeDtypeStruct((nb, l, qk), BF16),
        scratch_shapes=[pltpu.VMEM((2, 2, tk, tq), F32), pltpu.VMEM((2, 2, SUBLANES, tq), F32),
                        pltpu.VMEM((2, 2, tk, tq), BF16), pltpu.VMEM((2, 2, VT_ROWS, tq), F32),
                        pltpu.VMEM((2, 2, 1, tq), F32), pltpu.VMEM((2, 2, 1, tq), F32)],
        compiler_params=_cp("parallel", "parallel", "arbitrary"),
        name="diff_attn",
    )(qt, qt, k, vt, *lam_vecs, subln_g)
    per = tk // TT
    vec2 = lambda n: pl.BlockSpec((1, n), lambda b, h: (0, 0))
    o_ctx = pl.pallas_call(
        functools.partial(_ctx_attn_kernel, lam_init=lam_init, half=head_dim),
        grid=(nb, heads),
        in_specs=[pl.BlockSpec((1, 1, dv, TT), lambda b, h: (b, h, 0, n_lat_tiles)),
                  pl.BlockSpec((1, TT, dv), lambda b, h: (b, n_lat_tiles, h)),
                  pl.BlockSpec((1, 1, 1, VT_ROWS, TT),
                               lambda b, h: (b, h, n_lat_tiles // per, 0, n_lat_tiles % per)),
                  ] + [vec2(head_dim)] * 4 + [vec2(dv)],
        out_specs=pl.BlockSpec((1, TT, dv), lambda b, h: (b, 0, h)),
        out_shape=jax.ShapeDtypeStruct((nb, TT, qk), BF16),
        compiler_params=_cp("parallel", "parallel"),
        name="ctx_attn",
    )(qt, k, vt, *lam_vecs, subln_g)
    return o_lat, o_ctx


def _halo_rows(dtype):
    return SUBLANES * 4 // jnp.dtype(dtype).itemsize


def _halo_fill(xp_ref, prev_ref, x_ref, next_ref, has_prev, has_next):
    hr = prev_ref.shape[2]
    zero = jnp.zeros((hr, xp_ref.shape[-1]), F32)
    xp_ref[hr:hr + TT, :] = x_ref[0].astype(F32)
    xp_ref[0:hr, :] = jnp.where(has_prev, prev_ref[0, 0].astype(F32), zero)
    xp_ref[hr + TT:2 * hr + TT, :] = jnp.where(has_next, next_ref[0, 0].astype(F32), zero)


def _conv_taps(xp, w, bias, left):
    rows = xp.shape[0]
    hr = (rows - TT) // 2
    before = None
    for j in range(left):
        z = w[j:j + 1, :] * xp
        before = pltpu.roll(z if before is None else before + z, 1, 0)
    after = None
    for j in range(w.shape[0] - 1, left, -1):
        z = w[j:j + 1, :] * xp
        after = pltpu.roll(z if after is None else after + z, rows - 1, 0)
    acc = w[left:left + 1, :] * xp + bias
    for part in (before, after):
        if part is not None:
            acc = acc + part
    return acc[hr:hr + TT, :]


def _shortconv_kernel(prev_ref, x_ref, next_ref, w_ref, b_ref, v_ref, x1_ref, x2_ref, xp_ref,
                      *, n_seg_tiles, hy):
    i = pl.program_id(1)
    _halo_fill(xp_ref, prev_ref, x_ref, next_ref, i > 0, i < n_seg_tiles - 1)
    y = _conv_taps(xp_ref[...], w_ref[...], b_ref[...], HY_SHORT_LEFT)
    v_ref[0] = y[:, :hy].astype(v_ref.dtype)
    x1_ref[0] = y[:, hy:2 * hy].astype(x1_ref.dtype)
    x2_ref[0] = y[:, 2 * hy:].astype(x2_ref.dtype)


def _shortconv(u, w, b, *, tile0, n_seg_tiles):
    nb, s, hy3 = u.shape
    hy = hy3 // 3
    hr = _halo_rows(u.dtype)
    nh, per = s // hr, TT // hr
    uh = u.reshape(nb, nh, hr, hy3)
    prev = pl.BlockSpec((1, 1, hr, hy3),
                        lambda b, i: (b, jnp.maximum((tile0 + i) * per - 1, 0), 0, 0))
    nxt = pl.BlockSpec((1, 1, hr, hy3),
                       lambda b, i: (b, jnp.minimum((tile0 + i + 1) * per, nh - 1), 0, 0))
    out = jax.ShapeDtypeStruct((nb, n_seg_tiles * TT, hy), BF16)
    ospec = pl.BlockSpec((1, TT, hy), lambda b, i: (b, i, 0))
    return pl.pallas_call(
        functools.partial(_shortconv_kernel, n_seg_tiles=n_seg_tiles, hy=hy),
        grid=(nb, n_seg_tiles),
        in_specs=[
            prev,
            pl.BlockSpec((1, TT, hy3), lambda b, i: (b, tile0 + i, 0)),
            nxt,
            pl.BlockSpec(w.shape, lambda b, i: (0, 0)),
            pl.BlockSpec((1, hy3), lambda b, i: (0, 0)),
        ],
        out_specs=[ospec, ospec, ospec],
        out_shape=[out, out, out],
        scratch_shapes=[pltpu.VMEM((TT + 2 * hr, hy3), F32)],
        compiler_params=_cp("parallel", "parallel"),
        name="hy_shortconv",
    )(uh, u, uh, w, b.reshape(1, hy3))


def _filter_feats(lh, n_cols):
    n = np.arange(2 * lh).reshape(-1, n_cols).T.reshape(-1)
    lag = np.where(n < lh, n, 2 * lh - n).astype(np.float64)
    t = (lag / lh).astype(np.float32).astype(np.float64)
    bands = np.arange(1, N_BANDS + 1, dtype=np.float64)
    ang = 2.0 * math.pi * t[:, None] * bands
    feats = np.concatenate([t[:, None], np.cos(ang), np.sin(ang)], axis=-1)
    pad = (-(feats.shape[1] + 1)) % SUBLANES
    return np.concatenate([feats, np.zeros((2 * lh, pad)), n[:, None].astype(np.float64)], axis=-1)


def _filt_kernel(ft_ref, w1_ref, b1_ref, w2_ref, b2_ref, w3_ref, b3_ref, fr_ref, dec_ref, o_ref,
                 *, lh, c):
    ft = ft_ref[...]
    freq = fr_ref[...]
    h = jnp.sin(freq * (_dot(ft, w1_ref[...], precision=HIGHEST) + b1_ref[...]))
    h = jnp.sin(freq * (_dot(h, w2_ref[...], precision=HIGHEST) + b2_ref[...])).astype(BF16)
    tr = o_ref.shape[1]
    half = tr // 2
    for q in range(ft.shape[0] // tr):
        for dr in range(2):
            rs = slice(q * tr + dr * half, q * tr + (dr + 1) * half)
            t = ft[rs, 0:1]
            f = (_dot(h[rs], w3_ref[dr]) + b3_ref[dr]) * jnp.exp(-t * jnp.abs(dec_ref[dr]))
            if dr == 1:
                f = jnp.where(ft[rs, ft.shape[1] - 1:] == lh, 0.0, f)
            for o in range(2):
                o_ref[o, dr * half:(dr + 1) * half, q * c:(q + 1) * c] = f[:, o * c:(o + 1) * c]


def _hyena_filter(lh, n_cols, w1, b1, w2, b2, w3, b3, freq, decay):
    c = decay.shape[-1]
    feats = jnp.asarray(_filter_feats(lh, n_cols), F32)
    fe = feats.shape[1]
    w1p = jnp.pad(w1, ((0, fe - w1.shape[0]), (0, 0)))
    hid = w1.shape[1]
    tr = 2 * lh // n_cols
    cb = math.gcd(n_cols, FILT_COL_BLOCKS)
    by_dir = lambda a, lead: jnp.moveaxis(a.reshape(lead, 2, 2, c), 2, 0).reshape(2, lead, 2 * c)
    full = lambda a: pl.BlockSpec(a.shape, lambda i: (0,) * a.ndim)
    args = (w1p, b1.reshape(1, hid), w2, b2.reshape(1, hid), by_dir(w3, hid).astype(BF16),
            by_dir(b3, 1), freq.reshape(1, hid), by_dir(decay, 1))
    return pl.pallas_call(
        functools.partial(_filt_kernel, lh=lh, c=c),
        grid=(n_cols // cb,),
        in_specs=[pl.BlockSpec((cb * tr, fe), lambda i: (i, 0))] + [full(a) for a in args],
        out_specs=pl.BlockSpec((2, tr, cb * c), lambda i: (0, 0, i)),
        out_shape=jax.ShapeDtypeStruct((2, tr, n_cols * c), F32),
        compiler_params=_cp("parallel"),
        name="hy_filter",
    )(feats, *args)


@functools.lru_cache(maxsize=None)
def _dft_tables(n):
    nn = n * n
    h = n // 2
    k = np.arange(n)
    th = 2.0 * math.pi * np.outer(k, k) / n
    c, s = np.cos(th), np.sin(th)
    f1_data = np.block([[c[:, :h], s[:, :h]], [-s[:, :h], c[:, :h]]])
    f1_real = np.concatenate([c, -s], axis=0)
    idx = (k[None, None, :] * (k[:, None, None] + n * k[None, :, None])) % nn
    phi = 2.0 * math.pi * idx / nn
    cp, sp = np.cos(phi), np.sin(phi)
    g = np.concatenate([np.concatenate([cp, sp], axis=2), np.concatenate([-sp, cp], axis=2)], axis=1)
    hmat = np.transpose(g, (0, 2, 1)) / nn
    ci, si = c[:h, :], s[:h, :]
    f3 = np.zeros((n, 2 * n))
    f3[:h, 0::2], f3[:h, 1::2] = ci, -si
    f3[h:, 0::2], f3[h:, 1::2] = si, ci
    return f1_data, f1_real, g, hmat, f3


def _fft_s1_kernel(x_ref, f_ref, o_ref):
    o_ref[0] = _dot(f_ref[...], x_ref[0].astype(BF16)).astype(o_ref.dtype)


def _fft_s1(x, f1, n, c):
    p = x.shape[0]
    tc = FFT_N2_TILE * c
    return pl.pallas_call(
        _fft_s1_kernel,
        grid=(p, n * c // tc),
        in_specs=[pl.BlockSpec((1, n, tc), lambda q, j: (q, 0, j)),
                  pl.BlockSpec(f1.shape, lambda q, j: (0, 0))],
        out_specs=pl.BlockSpec((1, 2 * n, tc), lambda q, j: (q, 0, j)),
        out_shape=jax.ShapeDtypeStruct((p, 2 * n, n * c), BF16),
        compiler_params=_cp("parallel", "parallel"),
        name="fft_s1",
    )(x, f1)


def _fft_spec_kernel(a_ref, g_ref, o_ref, *, tk):
    for j in range(tk):
        x = jnp.concatenate([a_ref[0, 0, j], a_ref[0, 1, j]], axis=0)
        o_ref[0, j] = _dot(g_ref[j], x).astype(o_ref.dtype)


def _fft_mid_kernel(a_ref, g_ref, h_ref, kh_ref, o_ref, *, tk, n):
    ts = [_dot(g_ref[j], jnp.concatenate([a_ref[0, 0, j], a_ref[0, 1, j]], axis=0))
          for j in range(tk)]
    ys = []
    for j, t in enumerate(ts):
        tr, ti = t[:n], t[n:]
        kr, ki = kh_ref[0, j, :n].astype(F32), kh_ref[0, j, n:].astype(F32)
        ys.append(jnp.concatenate([tr * kr - ti * ki, tr * ki + ti * kr], axis=0).astype(BF16))
    for j, y in enumerate(ys):
        o_ref[0, j] = _dot(h_ref[j], y).astype(o_ref.dtype)


def _fft_spectrum(a, g, n, c):
    p = a.shape[0]
    tk = FFT_K1_TILE
    a5 = a.reshape(p, 2, n, n, c)
    return pl.pallas_call(
        functools.partial(_fft_spec_kernel, tk=tk),
        grid=(n // tk, p),
        in_specs=[pl.BlockSpec((1, 2, tk, n, c), lambda j, q: (q, 0, j, 0, 0)),
                  pl.BlockSpec((tk, 2 * n, 2 * n), lambda j, q: (j, 0, 0))],
        out_specs=pl.BlockSpec((1, tk, 2 * n, c), lambda j, q: (q, j, 0, 0)),
        out_shape=jax.ShapeDtypeStruct((p, n, 2 * n, c), BF16),
        compiler_params=_cp("parallel", "parallel"),
        name="fft_spectrum",
    )(a5, g)


def _fft_mid(a, g, hm, khat, order, n, c):
    p = a.shape[0]
    tk = FFT_K1_TILE
    a5 = a.reshape(p, 2, n, n, c)
    return pl.pallas_call(
        functools.partial(_fft_mid_kernel, tk=tk, n=n),
        grid=(n // tk, p),
        in_specs=[pl.BlockSpec((1, 2, tk, n, c), lambda j, q: (q, 0, j, 0, 0)),
                  pl.BlockSpec((tk, 2 * n, 2 * n), lambda j, q: (j, 0, 0)),
                  pl.BlockSpec((tk, 2 * n, 2 * n), lambda j, q: (j, 0, 0)),
                  pl.BlockSpec((1, tk, 2 * n, c), lambda j, q: (order, j, 0, 0))],
        out_specs=pl.BlockSpec((1, tk, 2 * n, c), lambda j, q: (q, j, 0, 0)),
        out_shape=jax.ShapeDtypeStruct((p, n, 2 * n, c), BF16),
        compiler_params=_cp("parallel", "parallel"),
        name="fft_mid",
    )(a5, g, hm, khat)


def _fft_s3_kernel(*refs, gated, chained):
    c_ref, f_ref, v_ref, b_ref = refs[:4]
    rest = list(refs[4:])
    y = _dot(f_ref[...], c_ref[0]) + v_ref[0].astype(F32) * b_ref[...]
    if gated:
        y = rest.pop(0)[0].astype(F32) * y
    f1_ref = rest.pop(0) if chained else None
    o_ref = rest.pop(0)
    z = y.astype(o_ref.dtype)
    o_ref[0] = z
    if chained:
        a_ref = rest.pop(0)
        a_ref[0] = _dot(f1_ref[...], z).astype(a_ref.dtype)


def _fft_s3(cm, f3, vin, xg, bias, n, c, f1_next=None):
    p = cm.shape[0]
    tc = FFT_N2_TILE * c
    c2 = cm.reshape(p, 2 * n, n * c)
    bias_t = jnp.tile(bias.reshape(1, c), (1, FFT_N2_TILE))
    blk = pl.BlockSpec((1, n, tc), lambda q, j: (q, 0, j))
    full = lambda a: pl.BlockSpec(a.shape, lambda q, j: (0, 0))
    in_specs = [pl.BlockSpec((1, 2 * n, tc), lambda q, j: (q, 0, j)), full(f3), blk,
                pl.BlockSpec((1, tc), lambda q, j: (0, 0))]
    args = [c2, f3, vin, bias_t]
    out_specs, out_shape = [blk], [jax.ShapeDtypeStruct((p, n, n * c), BF16)]
    if xg is not None:
        in_specs.append(blk)
        args.append(xg)
    if f1_next is not None:
        in_specs.append(full(f1_next))
        args.append(f1_next)
        out_specs.append(pl.BlockSpec((1, 2 * n, tc), lambda q, j: (q, 0, j)))
        out_shape.append(jax.ShapeDtypeStruct((p, 2 * n, n * c), BF16))
    return pl.pallas_call(
        functools.partial(_fft_s3_kernel, gated=xg is not None, chained=f1_next is not None),
        grid=(p, n * c // tc),
        in_specs=in_specs,
        out_specs=out_specs,
        out_shape=out_shape,
        compiler_params=_cp("parallel", "parallel"),
        name="fft_s3",
    )(*args)


def _hyena_long(v, x1, kk, hy_bias):
    nb, l, c = v.shape
    n = math.isqrt(2 * l)
    assert n * n == 2 * l and nb % 2 == 0
    p = nb // 2
    f1d, f1r, g, hm, f3 = (jnp.asarray(t, F32).astype(BF16) for t in _dft_tables(n))
    pair = lambda a: a.reshape(p, n, n * c)
    khat = _fft_spectrum(_fft_s1(kk, f1r, n, c), g, n, c)
    v = pair(v)
    cm = _fft_mid(_fft_s1(v, f1d, n, c), g, hm, khat, 0, n, c)
    z1, a = _fft_s3(cm, f3, v, pair(x1), hy_bias[0], n, c, f1_next=f1d)
    cm = _fft_mid(a, g, hm, khat, 1, n, c)
    (w2,) = _fft_s3(cm, f3, z1, None, hy_bias[1], n, c)
    return w2.reshape(nb, l, c)


@functools.lru_cache(maxsize=None)
def _ctx_dft_tables(lc):
    m = 2 * lc
    k = np.arange(m)
    th = 2.0 * math.pi * np.outer(k, k) / m
    c, s = np.cos(th), np.sin(th)
    f_data = np.block([[c[:, :lc], s[:, :lc]], [-s[:, :lc], c[:, :lc]]])
    f_real = np.concatenate([c, -s], axis=0)
    ci, si = c[:lc, :], s[:lc, :]
    f_inv = np.block([[ci, -si], [si, ci]]) / m
    return f_data, f_real, f_inv


def _ctxconv_kernel(v_ref, x1_ref, x2_ref, kk_ref, fd_ref, fr_ref, fi_ref, b_ref, o_ref, *, m):
    def conv(u, order):
        kh = _dot(fr_ref[...], kk_ref[order].astype(BF16))
        t = _dot(fd_ref[...], u.astype(BF16))
        tr, ti, kr, ki = t[:m], t[m:], kh[:m], kh[m:]
        y = jnp.concatenate([tr * kr - ti * ki, tr * ki + ti * kr], axis=0).astype(BF16)
        return _dot(fi_ref[...], y)

    v = v_ref[0].astype(F32)
    z1 = x1_ref[0].astype(F32) * (conv(v, 0) + v * b_ref[0:1, :])
    o_ref[0] = (x2_ref[0].astype(F32) * (conv(z1, 1) + z1 * b_ref[1:2, :])).astype(o_ref.dtype)


def _hyena_ctx(v, x1, x2, kk, hy_bias):
    nb, lc, c = v.shape
    p, m = nb // 2, 2 * lc
    fd, fr, fi = (jnp.asarray(t, F32).astype(BF16) for t in _ctx_dft_tables(lc))
    pair = lambda a: a.reshape(p, m, c)
    blk = pl.BlockSpec((1, m, c), lambda q: (q, 0, 0))
    full = lambda a: pl.BlockSpec(a.shape, lambda q: (0,) * a.ndim)
    z = pl.pallas_call(
        functools.partial(_ctxconv_kernel, m=m),
        grid=(p,),
        in_specs=[blk, blk, blk, full(kk), full(fd), full(fr), full(fi), full(hy_bias)],
        out_specs=blk,
        out_shape=jax.ShapeDtypeStruct((p, m, c), BF16),
        compiler_params=_cp("parallel"),
        name="hy_ctx",
    )(pair(v), pair(x1), pair(x2), kk, fd, fr, fi, hy_bias)
    return z.reshape(nb, lc, c)


def _od_inproj_kernel(prev_ref, h_ref, next_ref, mod_ref, g_ref, w_ref, cw_ref, cb_ref,
                      gate_ref, xc_ref, *, d, r, n_lat_tiles, n_tiles):
    i = pl.program_id(1)
    seg_first = jnp.logical_or(i == 0, i == n_lat_tiles)
    seg_last = jnp.logical_or(i == n_lat_tiles - 1, i == n_tiles - 1)
    m = mod_ref[0]
    rows = jnp.concatenate([prev_ref[0, 0], h_ref[0], next_ref[0, 0]], axis=0)
    xn = _rms_mod(rows, g_ref[...], m[:, 0:d], m[:, d:2 * d]).astype(BF16)
    row = lax.broadcasted_iota(jnp.int32, (xn.shape[0], 1), 0)
    outside = jnp.logical_or(jnp.logical_and(row < SUBLANES, seg_first),
                             jnp.logical_and(row >= SUBLANES + TT, seg_last))
    wc = r // OD_COL_BLOCKS
    for c0 in range(0, r, wc):
        x = jnp.where(outside, 0.0, _dot(xn, w_ref[:, r + c0:r + c0 + wc]))
        xc_ref[0, :, c0:c0 + wc] = _conv_taps(x, cw_ref[:, c0:c0 + wc], cb_ref[:, c0:c0 + wc],
                                              RG_CONV_LEFT)
    gate_ref[0] = _dot(xn[SUBLANES:SUBLANES + TT], w_ref[:, :r]).astype(gate_ref.dtype)


def _od_inproj(h, mods, rows, layer, g, w_in, conv_w, conv_b, *, n_lat_tiles):
    nb, s, d = h.shape
    r = w_in.shape[1] // 2
    n_tiles = s // TT
    n8, per = s // SUBLANES, TT // SUBLANES
    h8 = h.reshape(nb, n8, SUBLANES, d)
    ospec = pl.BlockSpec((1, TT, r), lambda b, i: (b, i, 0))
    return pl.pallas_call(
        functools.partial(_od_inproj_kernel, d=d, r=r, n_lat_tiles=n_lat_tiles, n_tiles=n_tiles),
        grid=(nb, n_tiles),
        in_specs=[
            pl.BlockSpec((1, 1, SUBLANES, d), lambda b, i: (b, jnp.maximum(i * per - 1, 0), 0, 0)),
            pl.BlockSpec((1, TT, d), lambda b, i: (b, i, 0)),
            pl.BlockSpec((1, 1, SUBLANES, d),
                         lambda b, i: (b, jnp.minimum((i + 1) * per, n8 - 1), 0, 0)),
            _mod_spec(layer, rows, nb, n_lat_tiles, mods.shape[-1]),
            pl.BlockSpec((1, d), lambda b, i: (0, 0)),
            pl.BlockSpec(w_in.shape, lambda b, i: (0, 0)),
            pl.BlockSpec(conv_w.shape, lambda b, i: (0, 0)),
            pl.BlockSpec((1, r), lambda b, i: (0, 0)),
        ],
        out_specs=[ospec, ospec],
        out_shape=[jax.ShapeDtypeStruct((nb, s, r), BF16), jax.ShapeDtypeStruct((nb, s, r), F32)],
        compiler_params=_cp("parallel", "parallel"),
        name="od_inproj",
    )(h8, h, h8, mods, g, w_in, conv_w, conv_b.reshape(1, r))


def _rglru_kernel(x_ref, w_ref, b_ref, lam_ref, o_ref, a_ref, bb_ref, carry_ref, *, r, windows):
    dr = pl.program_id(1)
    i = pl.program_id(2)
    part = TT // RG_PARTS
    nl = -lam_ref[0]
    softplus = jnp.maximum(nl, 0.0) + jnp.log1p(jnp.exp(-jnp.abs(nl)))
    neg_rate = (0.5 * RG_C) * softplus
    exp2_rate = (-0.5 * RG_C * math.log2(math.e)) * softplus

    def gates(r0):
        rs = slice(r0, r0 + part)
        xc = x_ref[0, rs, :]
        xb = xc.astype(BF16)
        half_x = 0.5 * xc
        for c0, c1, k0, k1 in windows:
            xk = xb[:, k0:k1]
            tr = jnp.tanh(_dot(xk, w_ref[0, k0:k1, c0:c1]) + b_ref[0, :, c0:c1]) + 1.0
            ti = jnp.tanh(_dot(xk, w_ref[0, k0:k1, r + c0:r + c1]) + b_ref[0, :, r + c0:r + c1]) + 1.0
            a = jnp.exp2(exp2_rate[:, c0:c1] * tr)
            a_ref[rs, c0:c1] = a
            e = jnp.tanh(neg_rate[:, c0:c1] * tr) * (a * a + 1.0)
            root = jnp.where(e > 0.0, e * lax.rsqrt(e), 0.0)
            bb_ref[rs, c0:c1] = root * (ti * half_x[:, c0:c1])

    def scan_rows(rows, h):
        for row in rows:
            h = a_ref[row:row + 1, :] * h + bb_ref[row:row + 1, :]
            o_ref[0, 0, row:row + 1, :] = h
        return h

    def scan_loop(r0, reverse, h):
        n_blk = part // SUBLANES

        def block(g, h):
            base = pl.multiple_of(r0 + (n_blk - 1 - g if reverse else g) * SUBLANES, SUBLANES)
            for k in range(SUBLANES):
                row = base + (SUBLANES - 1 - k if reverse else k)
                h = a_ref[pl.ds(row, 1), :] * h + bb_ref[pl.ds(row, 1), :]
                o_ref[0, 0, pl.ds(row, 1), :] = h
            return h

        return lax.fori_loop(0, n_blk, block, h)

    @pl.when(i == 0)
    def _():
        carry_ref[...] = jnp.zeros_like(carry_ref)

    def run(reverse):
        starts = [k * part for k in (reversed(range(RG_PARTS)) if reverse else range(RG_PARTS))]
        gates(starts[0])
        h = carry_ref[0:1, :]
        for k, r0 in enumerate(starts[:-1]):
            gates(starts[k + 1])
            rows = range(r0 + part - 1, r0 - 1, -1) if reverse else range(r0, r0 + part)
            h = scan_rows(rows, h)
        carry_ref[0:1, :] = scan_loop(starts[-1], reverse, h)

    pl.when(dr == 0)(lambda: run(False))
    pl.when(dr == 1)(lambda: run(True))


def _scan_tile(dr, i, n_lat_tiles, n_tiles):
    fwd = jnp.where(i == 0, n_lat_tiles, i - 1)
    bwd = jnp.where(i == 0, n_lat_tiles, n_lat_tiles - i)
    return jnp.where(dr == 0, fwd, bwd)


def _gate_windows(r, bs):
    out = []
    for c0 in range(0, r, 2 * LANES):
        c1 = min(c0 + 2 * LANES, r)
        k0 = (c0 // bs) * bs // LANES * LANES
        k1 = min(-(-(((c1 - 1) // bs + 1) * bs) // LANES) * LANES, r)
        out.append((c0, c1, k0, k1))
    return tuple(out)


def _rglru(xc, wcat, bcat, lam, *, n_lat_tiles, block_size):
    nb, s, r = xc.shape
    n_tiles = s // TT
    assert n_tiles == n_lat_tiles + 1
    tile = lambda d, i: _scan_tile(d, i, n_lat_tiles, n_tiles)
    return pl.pallas_call(
        functools.partial(_rglru_kernel, r=r, windows=_gate_windows(r, block_size)),
        grid=(nb, 2, n_tiles),
        in_specs=[
            pl.BlockSpec((1, TT, r), lambda b, d, i: (b, tile(d, i), 0)),
            pl.BlockSpec((1, r, 2 * r), lambda b, d, i: (d, 0, 0)),
            pl.BlockSpec((1, 1, 2 * r), lambda b, d, i: (d, 0, 0)),
            pl.BlockSpec((1, 1, r), lambda b, d, i: (d, 0, 0)),
        ],
        out_specs=pl.BlockSpec((1, 1, TT, r), lambda b, d, i: (d, b, tile(d, i), 0)),
        out_shape=jax.ShapeDtypeStruct((2, nb, s, r), F32),
        scratch_shapes=[pltpu.VMEM((TT, r), F32), pltpu.VMEM((TT, r), F32),
                        pltpu.VMEM((SUBLANES, r), F32)],
        compiler_params=_cp("parallel", "parallel", "arbitrary"),
        name="rglru",
    )(xc, wcat, bcat, lam.reshape(2, 1, r))


def _block_diag(w):
    n, bs, _ = w.shape
    eye = jnp.eye(n, dtype=w.dtype)
    return (eye[:, None, :, None] * w[:, :, None, :]).reshape(n * bs, n * bs)


def _gelu_tanh(x):
    return 0.5 * x * (1.0 + jnp.tanh(math.sqrt(2.0 / math.pi) * (x + 0.044715 * (x * x * x))))


def _post_kernel(*refs, kind, final, d, n_lat_tiles):
    if kind == "even":
        (hl_ref, hc_ref, x2_ref, wl_ref, zc_ref, ol_ref, oc_ref, mod_ref, g2_ref, wo_ref, w1_ref,
         w2_ref) = refs[:12]
        rest = refs[12:]
        is_ctx = pl.program_id(1) >= n_lat_tiles
        z_lat = (x2_ref[0].astype(F32) * wl_ref[0].astype(F32)).astype(BF16)
        z = jnp.where(is_ctx, zc_ref[0], z_lat)
        o = jnp.where(is_ctx, oc_ref[0], ol_ref[0])
        y = _dot(jnp.concatenate([z, o], axis=-1), wo_ref[...])
    else:
        hl_ref, hc_ref, hd_ref, gate_ref, mod_ref, g2_ref, wo_ref, w1_ref, w2_ref = refs[:9]
        rest = refs[9:]
        mix = (hd_ref[0, 0] + hd_ref[1, 0]) * _gelu_tanh(gate_ref[0].astype(F32))
        y = _dot(mix.astype(BF16), wo_ref[...])
    out_ref = rest[-1]
    m = mod_ref[0]
    h1 = _stream_tile(hl_ref, hc_ref, n_lat_tiles) + m[:, 2 * d:3 * d] * y
    xn = _rms_mod(h1, g2_ref[...], m[:, 3 * d:4 * d], m[:, 4 * d:5 * d]).astype(BF16)
    acc = jnp.zeros_like(h1)
    dff = w1_ref.shape[1]
    for c0 in range(0, dff, FF_CHUNK):
        a = jnp.maximum(_dot(xn, w1_ref[:, c0:c0 + FF_CHUNK]), 0.0)
        acc = acc + _dot((a * a).astype(BF16), w2_ref[c0:c0 + FF_CHUNK, :])
    h2 = h1 + m[:, 5 * d:6 * d] * acc
    if final:
        fg_ref = rest[0]
        h2 = h2 * lax.rsqrt(jnp.mean(h2 * h2, axis=-1, keepdims=True) + EPS) * fg_ref[...]
    out_ref[0] = h2


def _post(kind, h, mix_args, mods, rows, layer, g2, w_out, w1, w2, final_g, *, n_lat_tiles, final):
    h_specs, h_args = _stream_specs(h, n_lat_tiles)
    nb, _, d = h_args[0].shape
    n_tiles = n_lat_tiles if final else n_lat_tiles + 1
    tok = lambda w: pl.BlockSpec((1, TT, w), lambda b, i: (b, i, 0))
    full = lambda a: pl.BlockSpec(a.shape, lambda b, i: (0,) * a.ndim)
    if kind == "even":
        lat = lambda w: pl.BlockSpec((1, TT, w), lambda b, i: (b, jnp.minimum(i, n_lat_tiles - 1), 0))
        ctx = lambda w: pl.BlockSpec((1, TT, w), lambda b, i: (b, 0, 0))
        x2_lat, w_lat, z_ctx, o_lat, o_ctx = mix_args
        mix_specs = [lat(x2_lat.shape[-1]), lat(w_lat.shape[-1]), ctx(z_ctx.shape[-1]),
                     lat(o_lat.shape[-1]), ctx(o_ctx.shape[-1])]
    else:
        hd, gate = mix_args
        r = gate.shape[-1]
        mix_specs = [pl.BlockSpec((2, 1, TT, r), lambda b, i: (0, b, i, 0)), tok(r)]
    in_specs = h_specs + mix_specs + [
        _mod_spec(layer, rows, nb, n_lat_tiles, mods.shape[-1]),
        pl.BlockSpec((1, d), lambda b, i: (0, 0)), full(w_out), full(w1), full(w2)]
    args = [*h_args, *mix_args, mods, g2, w_out, w1, w2]
    if final:
        in_specs.append(pl.BlockSpec((1, d), lambda b, i: (0, 0)))
        args.append(final_g)
    return pl.pallas_call(
        functools.partial(_post_kernel, kind=kind, final=final, d=d, n_lat_tiles=n_lat_tiles),
        grid=(nb, n_tiles),
        in_specs=in_specs,
        out_specs=tok(d),
        out_shape=jax.ShapeDtypeStruct((nb, n_tiles * TT, d), F32),
        compiler_params=_cp("parallel", "parallel"),
        name="post_" + kind,
    )(*args)


@functools.lru_cache(maxsize=None)
def _rope_tables(l, lc, head_dim):
    axis = head_dim // 2
    freqs = ROPE_BASE ** (-np.arange(0, axis, 2, dtype=np.float64) / axis)
    freqs = freqs.astype(np.float32).astype(np.float64)
    t = np.arange(l)
    ang_r = (t // GRID_W)[:, None] * freqs
    ang_c = (t % GRID_W)[:, None] * freqs
    cos = np.concatenate([np.cos(ang_r)] * 2 + [np.cos(ang_c)] * 2, axis=-1)
    sin = np.concatenate([-np.sin(ang_r), np.sin(ang_r), -np.sin(ang_c), np.sin(ang_c)], axis=-1)
    cos = np.concatenate([cos, np.ones((lc, head_dim))], axis=0)
    sin = np.concatenate([sin, np.zeros((lc, head_dim))], axis=0)
    rep = LANES // head_dim
    return np.tile(cos, (1, rep)), np.tile(sin, (1, rep))


def kernel(x, c, ctx, c_ctx, ada_w, ada_b, norm1_g, norm2_g, mlp_w1, mlp_w2, final_g, ev_w_in, ev_w_out, hy_short_w, hy_short_b, hy_f_w1, hy_f_b1, hy_f_w2, hy_f_b2, hy_f_w3, hy_f_b3, hy_f_freq, hy_f_decay, hy_bias, df_lq1, df_lk1, df_lq2, df_lk2, df_subln_g, od_w_in, od_w_out, rg_conv_w, rg_conv_b, rg_wa, rg_ba, rg_wx, rg_bx, rg_lam):
    nb, l, d = x.shape
    lc = ctx.shape[1]
    depth = ada_w.shape[0]
    hy = hy_bias.shape[-1]
    head_dim = df_lq1.shape[-1]
    qk = (ev_w_in.shape[-1] - 3 * hy) // 3
    r = rg_lam.shape[-1]
    assert l % TT == 0 and lc == TT and l % GRID_W == 0
    n_lat_tiles = l // TT

    rows = -(-(nb + 1) // SUBLANES) * SUBLANES
    cond = jnp.zeros((rows, d), F32).at[:nb].set(c).at[nb].set(c_ctx)
    mods = _ada_mods(cond, ada_w, ada_b).reshape(depth * rows, 1, ada_w.shape[-1])

    cos_t, sin_t = (jnp.asarray(t, F32) for t in _rope_tables(l, lc, head_dim))
    h = (x, ctx)
    w1_bf, w2_bf = mlp_w1.astype(BF16), mlp_w2.astype(BF16)

    for i in range(depth):
        j = i // 2
        final = i == depth - 1
        g1, g2 = norm1_g[i].reshape(1, d), norm2_g[i].reshape(1, d)
        if i % 2 == 0:
            lam_init = 0.8 - 0.6 * math.exp(-0.3 * i)
            u, qt, k, vt = _ev_inproj(h, mods, rows, i, g1, ev_w_in[j].astype(BF16), cos_t, sin_t,
                                      n_lat_tiles=n_lat_tiles, hy3=3 * hy, qk=qk, head_dim=head_dim,
                                      tk=_attn_chunk(l + lc))
            lam_vecs = [a[j].reshape(1, -1) for a in (df_lq1, df_lk1, df_lq2, df_lk2)]
            o_lat, o_ctx = _diff_attention(qt, k, vt, lam_vecs, df_subln_g[j].reshape(1, -1),
                                           n_lat_tiles=n_lat_tiles, lam_init=lam_init,
                                           head_dim=head_dim)
            fparams = (hy_f_w1[j], hy_f_b1[j], hy_f_w2[j], hy_f_b2[j], hy_f_w3[j], hy_f_b3[j],
                       hy_f_freq[j], hy_f_decay[j])
            vl, x1l, x2l = _shortconv(u, hy_short_w[j], hy_short_b[j], tile0=0,
                                      n_seg_tiles=n_lat_tiles)
            vc, x1c, x2c = _shortconv(u, hy_short_w[j], hy_short_b[j], tile0=n_lat_tiles,
                                      n_seg_tiles=lc // TT)
            w_lat = _hyena_long(vl, x1l, _hyena_filter(l, math.isqrt(2 * l), *fparams), hy_bias[j])
            z_ctx = _hyena_ctx(vc, x1c, x2c, _hyena_filter(lc, 1, *fparams), hy_bias[j])
            h = _post("even", h, (x2l, w_lat, z_ctx, o_lat, o_ctx), mods, rows, i, g2, ev_w_out[j].astype(BF16),
                      w1_bf[i], w2_bf[i], final_g.reshape(1, d), n_lat_tiles=n_lat_tiles,
                      final=final)
        else:
            gate, xc = _od_inproj(h, mods, rows, i, g1, od_w_in[j].astype(BF16), rg_conv_w[j],
                                  rg_conv_b[j], n_lat_tiles=n_lat_tiles)
            wcat = (0.5 * jnp.stack(
                [jnp.concatenate([_block_diag(rg_wa[j, dd]), _block_diag(rg_wx[j, dd])], axis=1)
                 for dd in range(2)])).astype(BF16)
            bcat = 0.5 * jnp.concatenate([rg_ba[j], rg_bx[j]], axis=-1).reshape(2, 1, 2 * r)
            hd = _rglru(xc, wcat, bcat, rg_lam[j], n_lat_tiles=n_lat_tiles,
                        block_size=rg_wa.shape[-1])
            h = _post("odd", h, (hd, gate), mods, rows, i, g2, od_w_out[j].astype(BF16),
                      w1_bf[i], w2_bf[i], final_g.reshape(1, d), n_lat_tiles=n_lat_tiles,
                      final=final)
    return h
```
